```python
import jax, jax.numpy as jnp
from jax import lax
import numpy as np

D_MODEL = 1024
BATCH = 2
SEQ = 8192
DEPTH = 4

N_MIXERS = 2
N_HGRN_LAYERS = (DEPTH + N_MIXERS - 1) // N_MIXERS
N_CONV_LAYERS = DEPTH // N_MIXERS
HG_DK = 128
HG_HEADS = D_MODEL // HG_DK
HG_DV = D_MODEL // HG_HEADS
HG_CHUNK = 64
HG_STREAMS = 5
CONV_WIDTH = 3
CONV_STREAMS = 3
MOE_GROUPS = 4
MOE_EXPERTS_PER_GROUP = 8
MOE_EXPERTS = MOE_GROUPS * MOE_EXPERTS_PER_GROUP
MOE_TOPK = 2
MOE_D_EXPERT = D_MODEL // 2
MOE_BLOCK = 256
DN_ALPHA = (2.0 * DEPTH) ** 0.25
DN_BETA = (8.0 * DEPTH) ** -0.25
LN_EPS = 1e-5
RMS_EPS = 1e-6

kernel_name = 'hybrid_hgrn2_shortconv_hmoe_deepnorm_encoder'


def _layer_norm(x, g, b):
    xf = x.astype(jnp.float32)
    mu = jnp.mean(xf, axis=-1, keepdims=True)
    var = jnp.mean(jnp.square(xf - mu), axis=-1, keepdims=True)
    return ((xf - mu) * lax.rsqrt(var + LN_EPS) * g + b).astype(x.dtype)


def _gla_chunked(q, k, v, logf):
    b, h, t, dk = q.shape
    dv = v.shape[-1]
    n = t // HG_CHUNK

    def chunks(a):
        return jnp.moveaxis(a.reshape(b, h, n, HG_CHUNK, a.shape[-1]), 2, 0)

    gc = jnp.cumsum(chunks(logf), axis=-2)
    incl = jnp.tril(jnp.ones((HG_CHUNK, HG_CHUNK), dtype=bool))[:, :, None]

    def step(state, inp):
        qc, kc, vc, g = inp
        o_inter = jnp.einsum('bhik,bhkv->bhiv', qc * jnp.exp(g), state)
        diff = g[:, :, :, None, :] - g[:, :, None, :, :]
        decay = jnp.exp(jnp.where(incl, diff, -jnp.inf))
        scores = jnp.einsum('bhik,bhjk,bhijk->bhij', qc, kc, decay)
        o = o_inter + jnp.einsum('bhij,bhjv->bhiv', scores, vc)
        g_last = g[:, :, -1:, :]
        state = (jnp.exp(g_last[:, :, 0, :, None]) * state
                 + jnp.einsum('bhjk,bhjv->bhkv', kc * jnp.exp(g_last - g), vc))
        return state, o

    s0 = jnp.zeros((b, h, dk, dv), jnp.float32)
    _, o = lax.scan(step, s0, (chunks(q), chunks(k), chunks(v), gc))
    return jnp.moveaxis(o, 0, 2).reshape(b, h, t, dv)


def _hgrn2_mixer(x, w_in, lb, norm_w, w_out):
    b, t, d = x.shape
    q, f_fwd, f_bwd, v, gate = jnp.split(x @ w_in, HG_STREAMS, axis=-1)

    def heads(a):
        return a.reshape(b, t, HG_HEADS, -1).transpose(0, 2, 1, 3).astype(jnp.float32)

    def log_forget(fx, lbd):
        lbd = lbd.astype(jnp.float32).reshape(HG_HEADS, 1, HG_DK)
        return jnp.logaddexp(jnp.log(lbd), jnp.log1p(-lbd) + jax.nn.log_sigmoid(heads(fx)))

    q = jax.nn.silu(heads(q))
    v = heads(v)
    lf_f = log_forget(f_fwd, lb[0])
    lf_b = jnp.flip(log_forget(f_bwd, lb[1]), axis=2)
    o_f = _gla_chunked(q, -jnp.expm1(lf_f), v, lf_f)
    o_b = jnp.flip(_gla_chunked(jnp.flip(q, axis=2), -jnp.expm1(lf_b),
                                jnp.flip(v, axis=2), lf_b), axis=2)
    o = (o_f + o_b).transpose(0, 2, 1, 3)
    o = o * lax.rsqrt(jnp.mean(o * o, axis=-1, keepdims=True) + RMS_EPS) * norm_w.astype(jnp.float32)
    o = o.reshape(b, t, d) * jax.nn.silu(gate.astype(jnp.float32))
    return o.astype(x.dtype) @ w_out


def _short_conv_mixer(x, w_in, conv_w, w_out):
    bg, cg, h = jnp.split(x @ w_in, CONV_STREAMS, axis=-1)
    half = (CONV_WIDTH - 1) // 2
    y = lax.conv_general_dilated(cg * h, conv_w[:, None, :], window_strides=(1,),
                                 padding=[(half, half)],
                                 dimension_numbers=('NWC', 'WIO', 'NWC'),
                                 feature_group_count=x.shape[-1])
    return (bg * y) @ w_out


def _hier_moe(x, w_group, b_group, w_expert, b_expert, w_up, w_down):
    b, t, d = x.shape
    n = b * t
    xf = x.reshape(n, d)
    xr = xf.astype(jnp.float32)
    g_prob = jax.nn.softmax(xr @ w_group.astype(jnp.float32) + b_group.astype(jnp.float32), axis=-1)
    p_group, g_sel = lax.top_k(g_prob, 1)
    e_logits = (xr @ w_expert.astype(jnp.float32) + b_expert.astype(jnp.float32)
                ).reshape(n, MOE_GROUPS, MOE_EXPERTS_PER_GROUP)
    e_logits = jnp.take_along_axis(e_logits, g_sel[:, :, None], axis=1)[:, 0]
    top_logit, top_local = lax.top_k(e_logits, MOE_TOPK)
    gates = p_group * jax.nn.softmax(top_logit, axis=-1)
    expert = g_sel * MOE_EXPERTS_PER_GROUP + top_local

    nk = n * MOE_TOPK
    e_flat = expert.reshape(nk).astype(jnp.int32)
    tok_flat = jnp.arange(nk, dtype=jnp.int32) // MOE_TOPK
    order = jnp.argsort(e_flat)
    e_sorted = e_flat[order]
    counts = jnp.zeros((MOE_EXPERTS,), jnp.int32).at[e_flat].add(1)
    padded = (counts + MOE_BLOCK - 1) // MOE_BLOCK * MOE_BLOCK
    start = jnp.cumsum(counts) - counts
    pad_end = jnp.cumsum(padded)
    pad_start = pad_end - padded
    dest = pad_start[e_sorted] + jnp.arange(nk, dtype=jnp.int32) - start[e_sorted]
    n_rows = -(-nk // MOE_BLOCK) * MOE_BLOCK + MOE_EXPERTS * MOE_BLOCK
    n_blocks = n_rows // MOE_BLOCK
    row_tok = jnp.full((n_rows,), n, jnp.int32).at[dest].set(tok_flat[order])
    row_gate = jnp.zeros((n_rows,), jnp.float32).at[dest].set(gates.reshape(nk)[order])
    block_start = jnp.arange(n_blocks, dtype=jnp.int32) * MOE_BLOCK
    block_expert = jnp.minimum(jnp.searchsorted(pad_end, block_start, side='right'),
                               MOE_EXPERTS - 1).astype(jnp.int32)
    x_pad = jnp.concatenate([xf, jnp.zeros((1, d), xf.dtype)], axis=0)
    xs = x_pad[row_tok].reshape(n_blocks, MOE_BLOCK, d)

    def run_block(args):
        xb, e = args
        hg, hu = jnp.split(xb @ w_up[e], 2, axis=-1)
        return (jax.nn.silu(hg) * hu) @ w_down[e]

    ys = lax.map(run_block, (xs, block_expert)).reshape(n_rows, d)
    out = jnp.zeros((n + 1, d), x.dtype).at[row_tok].add(ys * row_gate[:, None].astype(x.dtype))
    return out[:n].reshape(b, t, d)


def setup_inputs(seed: int = 0) -> dict:
    key = jax.random.key(seed)
    ks = jax.random.split(key, 18)
    d = D_MODEL

    def nrm(k, shape, scale):
        return jax.random.normal(k, shape, jnp.float32) * scale

    return {
        'x': nrm(ks[0], (BATCH, SEQ, d), 1.0),
        'hg_w_in': nrm(ks[1], (N_HGRN_LAYERS, d, HG_STREAMS * d), d ** -0.5),
        'hg_lb_logits': nrm(ks[2], (DEPTH, 2, HG_HEADS * HG_DK), 0.5),
        'hg_norm_w': 1.0 + nrm(ks[3], (N_HGRN_LAYERS, HG_DV), 0.02),
        'hg_w_out': nrm(ks[4], (N_HGRN_LAYERS, d, d), DN_BETA * d ** -0.5),
        'cv_w_in': nrm(ks[5], (N_CONV_LAYERS, d, CONV_STREAMS * d), d ** -0.5),
        'cv_w': nrm(ks[6], (N_CONV_LAYERS, CONV_WIDTH, d), CONV_WIDTH ** -0.5),
        'cv_w_out': nrm(ks[7], (N_CONV_LAYERS, d, d), DN_BETA * d ** -0.5),
        'ln_g': 1.0 + nrm(ks[8], (DEPTH, 2, d), 0.02),
        'ln_b': nrm(ks[9], (DEPTH, 2, d), 0.02),
        'moe_w_group': nrm(ks[10], (DEPTH, d, MOE_GROUPS), d ** -0.5),
        'moe_b_group': nrm(ks[11], (DEPTH, MOE_GROUPS), 0.01),
        'moe_w_expert': nrm(ks[12], (DEPTH, d, MOE_EXPERTS), d ** -0.5),
        'moe_b_expert': nrm(ks[13], (DEPTH, MOE_EXPERTS), 0.01),
        'moe_w_up': nrm(ks[14], (DEPTH, MOE_EXPERTS, d, 2 * MOE_D_EXPERT), d ** -0.5),
        'moe_w_down': nrm(ks[15], (DEPTH, MOE_EXPERTS, MOE_D_EXPERT, d), DN_BETA * MOE_D_EXPERT ** -0.5),
    }


def reference(x, hg_w_in, hg_lb_logits, hg_norm_w, hg_w_out, cv_w_in, cv_w, cv_w_out,
              ln_g, ln_b, moe_w_group, moe_b_group, moe_w_expert, moe_b_expert,
              moe_w_up, moe_w_down):
    lb_all = jnp.cumsum(jax.nn.softmax(hg_lb_logits.astype(jnp.float32), axis=0), axis=0)
    lb_all = lb_all - lb_all[0:1]
    for layer in range(DEPTH):
        j = layer // N_MIXERS
        if layer % N_MIXERS == 0:
            mix = _hgrn2_mixer(x, hg_w_in[j], lb_all[layer], hg_norm_w[j], hg_w_out[j])
        else:
            mix = _short_conv_mixer(x, cv_w_in[j], cv_w[j], cv_w_out[j])
        x = _layer_norm(DN_ALPHA * x + mix, ln_g[layer, 0], ln_b[layer, 0])
        ffn = _hier_moe(x, moe_w_group[layer], moe_b_group[layer], moe_w_expert[layer],
                        moe_b_expert[layer], moe_w_up[layer], moe_w_down[layer])
        x = _layer_norm(DN_ALPHA * x + ffn, ln_g[layer, 1], ln_b[layer, 1])
    return x
```

```python
import functools

import jax
import jax.numpy as jnp
from jax import lax
from jax.experimental import pallas as pl
from jax.experimental.pallas import tpu as pltpu

D_MODEL = 1024
DEPTH = 4
HG_DK = 128
HG_HEADS = D_MODEL // HG_DK
HG_STREAMS = 5
CONV_STREAMS = 3
MOE_GROUPS = 4
MOE_EXPERTS_PER_GROUP = 8
MOE_EXPERTS = MOE_GROUPS * MOE_EXPERTS_PER_GROUP
MOE_TOPK = 2
MOE_D_EXPERT = D_MODEL // 2
MOE_BLOCK = 256
DN_ALPHA = (2.0 * DEPTH) ** 0.25
LN_EPS = 1e-5
RMS_EPS = 1e-6

LANES = 128
SUBLANES = 8
VMEM_LIMIT = 48 * 1024 * 1024
GLA_CHUNK = 64
GLA_TBLOCK = 512
ROW_TILE = 256
NEG_BIG = -1e30

BF16 = jnp.bfloat16
F32 = jnp.float32


def _cparams(*sem):
    return pltpu.CompilerParams(dimension_semantics=sem, vmem_limit_bytes=VMEM_LIMIT)


def _mm_kernel(x_ref, w_ref, o_ref):
    o_ref[...] = jnp.dot(x_ref[...].astype(BF16), w_ref[...], preferred_element_type=F32)


def _matmul(x, w, tm=512, tn=1024):
    n, k = x.shape
    nn = w.shape[1]
    return pl.pallas_call(
        _mm_kernel,
        grid=(nn // tn, n // tm),
        in_specs=[pl.BlockSpec((tm, k), lambda j, i: (i, 0)),
                  pl.BlockSpec((k, tn), lambda j, i: (0, j))],
        out_specs=pl.BlockSpec((tm, tn), lambda j, i: (i, j)),
        out_shape=jax.ShapeDtypeStruct((n, nn), F32),
        compiler_params=_cparams("arbitrary", "arbitrary"),
        name="in_proj",
    )(x, w)


def _layer_norm_rows(y, g, b):
    mu = jnp.mean(y, axis=-1, keepdims=True)
    yc = y - mu
    var = jnp.mean(yc * yc, axis=-1, keepdims=True)
    return yc * lax.rsqrt(var + LN_EPS) * g + b


def _chunk_cumsum(lf, row_in_chunk, reverse):
    rows = lf.shape[0]
    p = lf
    s = 1
    while s < GLA_CHUNK:
        if reverse:
            sh = pltpu.roll(p, rows - s, 0)
            p = p + jnp.where(row_in_chunk + s < GLA_CHUNK, sh, 0.0)
        else:
            sh = pltpu.roll(p, s, 0)
            p = p + jnp.where(row_in_chunk >= s, sh, 0.0)
        s *= 2
    return p


def _gla_chunk(q, kk, v, g, st, reverse):
    c = GLA_CHUNK
    nt = c // SUBLANES
    ii = lax.broadcasted_iota(jnp.int32, (c, c), 0)
    jj = lax.broadcasted_iota(jnp.int32, (c, c), 1)
    nt_dims = (((1,), (1,)), ((), ()))

    o = lax.dot_general((q * jnp.exp(g)).astype(BF16), st.astype(BF16), nt_dims,
                        preferred_element_type=F32)

    p_off = jnp.zeros((c, c), F32)
    b = 2 * SUBLANES
    while b <= c:
        h = b // 2
        pieces = []
        for m in range(c // b):
            r = m * b + h if reverse else m * b + h - 1
            pieces.append(jnp.broadcast_to(g[r:r + 1, :], (b, LANES)))
        gref = pieces[0] if len(pieces) == 1 else jnp.concatenate(pieces, axis=0)
        qfac = jnp.exp(jnp.minimum(g - gref, 0.0))
        kfac = jnp.exp(jnp.minimum(gref - g, 0.0))
        pl_ = lax.dot_general((q * qfac).astype(BF16), (kk * kfac).astype(BF16), nt_dims,
                              preferred_element_type=F32)
        same = (ii // b) == (jj // b)
        if reverse:
            mask = same & ((ii % b) < h) & ((jj % b) >= h)
        else:
            mask = same & ((ii % b) >= h) & ((jj % b) < h)
        p_off = p_off + jnp.where(mask, pl_, 0.0)
        b *= 2
    o = o + jnp.dot(p_off.astype(BF16), v.astype(BF16), preferred_element_type=F32)

    q3 = q.reshape(nt, SUBLANES, LANES)
    k3 = kk.reshape(nt, SUBLANES, LANES)
    v3 = v.reshape(nt, SUBLANES, LANES)
    g3 = g.reshape(nt, SUBLANES, LANES)
    i3 = lax.broadcasted_iota(jnp.int32, (nt, SUBLANES, LANES), 1)
    o3 = jnp.zeros((nt, SUBLANES, LANES), F32)
    for r in range(SUBLANES):
        gr = g3[:, r:r + 1, :]
        valid = (i3 <= r) if reverse else (i3 >= r)
        dec = jnp.where(valid, jnp.exp(jnp.minimum(g3 - gr, 0.0)), 0.0)
        a = q3 * k3[:, r:r + 1, :] * dec
        s = jnp.sum(a, axis=-1, keepdims=True)
        o3 = o3 + s * v3[:, r:r + 1, :]
    o = o + o3.reshape(c, LANES)

    gl = g[0:1, :] if reverse else g[c - 1:c, :]
    kd = (kk * jnp.exp(gl - g)).astype(BF16)
    upd = lax.dot_general(v.astype(BF16), kd, (((0,), (0,)), ((), ())),
                          preferred_element_type=F32)
    st = st * jnp.exp(gl) + upd
    return o, st


def _gla_kernel(layer, lbl_ref, qf_ref, ff_ref, vf_ref, qb_ref, fb_ref, vb_ref,
                of_ref, ob_ref, st_ref, q_s, k_s, v_s, g_s):
    tb = GLA_TBLOCK
    nc = tb // GLA_CHUNK

    @pl.when(pl.program_id(2) == 0)
    def _():
        st_ref[...] = jnp.zeros_like(st_ref)

    if layer > 0:
        lg = lbl_ref[...]
        e = jnp.exp(lg - jnp.max(lg, axis=0, keepdims=True))
        sm = e / jnp.sum(e, axis=0, keepdims=True)
        lb = sm[1]
        for l in range(2, layer + 1):
            lb = lb + sm[l]

    row_in_chunk = lax.broadcasted_iota(jnp.int32, (tb, LANES), 0) % GLA_CHUNK

    for d, (q_ref, f_ref, v_ref) in enumerate(((qf_ref, ff_ref, vf_ref),
                                               (qb_ref, fb_ref, vb_ref))):
        qx = q_ref[...]
        fx = f_ref[...]
        sp = jnp.log1p(jnp.exp(-jnp.abs(fx)))
        logsig = jnp.minimum(fx, 0.0) - sp
        sig_neg = jnp.exp(jnp.minimum(-fx, 0.0) - sp)
        if layer > 0:
            lbd = lb[d:d + 1, :]
            a = jnp.log(lbd)
            bb = jnp.log1p(-lbd) + logsig
            lf = jnp.maximum(a, bb) + jnp.log1p(jnp.exp(-jnp.abs(a - bb)))
            kk = (1.0 - lbd) * sig_neg
        else:
            lf = logsig
            kk = sig_neg
        q_s[d] = qx * jax.nn.sigmoid(qx)
        k_s[d] = kk
        v_s[d] = v_ref[...]
        g_s[d] = _chunk_cumsum(lf, row_in_chunk, reverse=(d == 1))

    def body(ci, carry):
        sf = pl.multiple_of(ci * GLA_CHUNK, GLA_CHUNK)
        sb = pl.multiple_of((nc - 1 - ci) * GLA_CHUNK, GLA_CHUNK)
        for d, start, o_ref in ((0, sf, of_ref), (1, sb, ob_ref)):
            sl = pl.ds(start, GLA_CHUNK)
            o, st = _gla_chunk(q_s[d, sl, :], k_s[d, sl, :], v_s[d, sl, :], g_s[d, sl, :],
                               st_ref[d], reverse=(d == 1))
            o_ref[sl, :] = o
            st_ref[d] = st
        return carry

    lax.fori_loop(0, nc, body, 0)


def _gla(proj, lb_logits, layer, batch, seq):
    n = proj.shape[0]
    tb = GLA_TBLOCK
    nb = seq // tb
    h8 = HG_HEADS

    def spec(stream, rev):
        if rev:
            return pl.BlockSpec((tb, LANES), lambda b, h, c: (b * nb + nb - 1 - c, stream * h8 + h))
        return pl.BlockSpec((tb, LANES), lambda b, h, c: (b * nb + c, stream * h8 + h))

    o_f_spec = pl.BlockSpec((tb, LANES), lambda b, h, c: (b * nb + c, h))
    o_b_spec = pl.BlockSpec((tb, LANES), lambda b, h, c: (b * nb + nb - 1 - c, h))
    return pl.pallas_call(
        functools.partial(_gla_kernel, layer),
        grid=(batch, h8, nb),
        in_specs=[pl.BlockSpec((DEPTH, 2, LANES), lambda b, h, c: (0, 0, h)),
                  spec(0, False), spec(1, False), spec(3, False),
                  spec(0, True), spec(2, True), spec(3, True)],
        out_specs=[o_f_spec, o_b_spec],
        out_shape=[jax.ShapeDtypeStruct((n, D_MODEL), F32)] * 2,
        scratch_shapes=[pltpu.VMEM((2, HG_DK, HG_DK), F32)]
        + [pltpu.VMEM((2, tb, LANES), F32)] * 4,
        compiler_params=_cparams("arbitrary", "arbitrary", "arbitrary"),
        name="gla",
    )(lb_logits, proj, proj, proj, proj, proj, proj)


def _hgrn_out_kernel(of_ref, ob_ref, gate_ref, nw_ref, w_ref, x_ref, g_ref, b_ref, o_ref):
    o = of_ref[...] + ob_ref[...]
    parts = []
    for h in range(HG_HEADS):
        oh = o[:, h * LANES:(h + 1) * LANES]
        ms = jnp.mean(oh * oh, axis=-1, keepdims=True)
        parts.append(oh * lax.rsqrt(ms + RMS_EPS))
    gate = gate_ref[...]
    y = jnp.concatenate(parts, axis=-1) * nw_ref[...] * (gate * jax.nn.sigmoid(gate))
    mix = jnp.dot(y.astype(BF16), w_ref[...], preferred_element_type=F32)
    o_ref[...] = _layer_norm_rows(DN_ALPHA * x_ref[...] + mix, g_ref[...], b_ref[...])


def _hgrn_out(o_f, o_b, proj, norm_w, w_out, x, g, b):
    n = x.shape[0]
    tm = ROW_TILE
    row = pl.BlockSpec((tm, D_MODEL), lambda i: (i, 0))
    vec = pl.BlockSpec((1, D_MODEL), lambda i: (0, 0))
    return pl.pallas_call(
        _hgrn_out_kernel,
        grid=(n // tm,),
        in_specs=[row, row, pl.BlockSpec((tm, D_MODEL), lambda i: (i, HG_STREAMS - 1)), vec,
                  pl.BlockSpec((D_MODEL, D_MODEL), lambda i: (0, 0)), row, vec, vec],
        out_specs=row,
        out_shape=jax.ShapeDtypeStruct((n, D_MODEL), F32),
        compiler_params=_cparams("arbitrary"),
        name="hgrn_out",
    )(o_f, o_b, proj, jnp.tile(norm_w, HG_HEADS).reshape(1, D_MODEL), w_out, x,
      g.reshape(1, D_MODEL), b.reshape(1, D_MODEL))


def _conv_out_kernel(tiles_per_seq, bg_ref, cg_ref, h_ref, cgp_ref, hp_ref, cgn_ref, hn_ref,
                     cw_ref, w_ref, x_ref, g_ref, b_ref, o_ref):
    i = pl.program_id(0)
    tm = cg_ref.shape[0]
    u = cg_ref[...] * h_ref[...]
    first = (i % tiles_per_seq) == 0
    last = (i % tiles_per_seq) == tiles_per_seq - 1
    u_prev_row = jnp.where(first, 0.0, cgp_ref[SUBLANES - 1:SUBLANES, :] * hp_ref[SUBLANES - 1:SUBLANES, :])
    u_next_row = jnp.where(last, 0.0, cgn_ref[0:1, :] * hn_ref[0:1, :])
    rows = lax.broadcasted_iota(jnp.int32, u.shape, 0)
    u_prev = jnp.where(rows == 0, u_prev_row, pltpu.roll(u, 1, 0))
    u_next = jnp.where(rows == tm - 1, u_next_row, pltpu.roll(u, tm - 1, 0))
    cw = cw_ref[...]
    y = u_prev * cw[0:1, :] + u * cw[1:2, :] + u_next * cw[2:3, :]
    mix = jnp.dot((bg_ref[...] * y).astype(BF16), w_ref[...], preferred_element_type=F32)
    o_ref[...] = _layer_norm_rows(DN_ALPHA * x_ref[...] + mix, g_ref[...], b_ref[...])


def _conv_out(proj, conv_w, w_out, x, g, b, seq):
    n = x.shape[0]
    tm = ROW_TILE
    r8 = tm // SUBLANES
    nblk8 = n // SUBLANES
    row = pl.BlockSpec((tm, D_MODEL), lambda i: (i, 0))
    vec = pl.BlockSpec((1, D_MODEL), lambda i: (0, 0))

    def main(stream):
        return pl.BlockSpec((tm, D_MODEL), lambda i: (i, stream))

    def prev(stream):
        return pl.BlockSpec((SUBLANES, D_MODEL), lambda i: (jnp.maximum(i * r8 - 1, 0), stream))

    def nxt(stream):
        return pl.BlockSpec((SUBLANES, D_MODEL),
                            lambda i: (jnp.minimum((i + 1) * r8, nblk8 - 1), stream))

    return pl.pallas_call(
        functools.partial(_conv_out_kernel, seq // tm),
        grid=(n // tm,),
        in_specs=[main(0), main(1), main(2), prev(1), prev(2), nxt(1), nxt(2),
                  pl.BlockSpec((3, D_MODEL), lambda i: (0, 0)),
                  pl.BlockSpec((D_MODEL, D_MODEL), lambda i: (0, 0)), row, vec, vec],
        out_specs=row,
        out_shape=jax.ShapeDtypeStruct((n, D_MODEL), F32),
        compiler_params=_cparams("arbitrary"),
        name="conv_out",
    )(proj, proj, proj, proj, proj, proj, proj, conv_w, w_out, x,
      g.reshape(1, D_MODEL), b.reshape(1, D_MODEL))


def _router_kernel(x_ref, w_ref, b_ref, o_ref):
    logits = jnp.dot(x_ref[...], w_ref[...], precision=lax.Precision.HIGHEST,
                     preferred_element_type=F32) + b_ref[...]
    lane = lax.broadcasted_iota(jnp.int32, logits.shape, 1)
    lane_f = lane.astype(F32)
    gl = jnp.where(lane < MOE_GROUPS, logits, NEG_BIG)
    gmax = jnp.max(gl, axis=-1, keepdims=True)
    gsum = jnp.sum(jnp.exp(gl - gmax), axis=-1, keepdims=True)
    p_group = 1.0 / gsum
    g_sel = jnp.min(jnp.where(gl == gmax, lane_f, 1e9), axis=-1, keepdims=True)
    e_lane = lane - MOE_GROUPS
    in_grp = ((lane >= MOE_GROUPS) & (lane < MOE_GROUPS + MOE_EXPERTS)
              & ((e_lane // MOE_EXPERTS_PER_GROUP).astype(F32) == g_sel))
    el = jnp.where(in_grp, logits, NEG_BIG)
    t1 = jnp.max(el, axis=-1, keepdims=True)
    i1 = jnp.min(jnp.where(el == t1, lane_f, 1e9), axis=-1, keepdims=True)
    el2 = jnp.where(lane_f == i1, NEG_BIG, el)
    t2 = jnp.max(el2, axis=-1, keepdims=True)
    i2 = jnp.min(jnp.where(el2 == t2, lane_f, 1e9), axis=-1, keepdims=True)
    z = jnp.exp(t2 - t1)
    g1 = p_group / (1.0 + z)
    g2 = g1 * z
    out = jnp.where(lane == 0, g1, 0.0)
    out = jnp.where(lane == 1, g2, out)
    out = jnp.where(lane == 2, i1 - MOE_GROUPS, out)
    out = jnp.where(lane == 3, i2 - MOE_GROUPS, out)
    o_ref[...] = out


def _router(x, w_group, b_group, w_expert, b_expert):
    n = x.shape[0]
    tm = 512
    pad = LANES - MOE_GROUPS - MOE_EXPERTS
    w = jnp.concatenate([w_group, w_expert, jnp.zeros((D_MODEL, pad), F32)], axis=1)
    b = jnp.concatenate([b_group, b_expert, jnp.zeros((pad,), F32)]).reshape(1, LANES)
    return pl.pallas_call(
        _router_kernel,
        grid=(n // tm,),
        in_specs=[pl.BlockSpec((tm, D_MODEL), lambda i: (i, 0)),
                  pl.BlockSpec((D_MODEL, LANES), lambda i: (0, 0)),
                  pl.BlockSpec((1, LANES), lambda i: (0, 0))],
        out_specs=pl.BlockSpec((tm, LANES), lambda i: (i, 0)),
        out_shape=jax.ShapeDtypeStruct((n, LANES), F32),
        compiler_params=_cparams("arbitrary"),
        name="router",
    )(x, w, b)


def _moe_ffn_kernel(be_ref, nact_ref, cnt_ref, tok_ref, dst_ref, x_hbm, gate_ref, wup_ref,
                    wdn_ref, out_hbm, xbuf, ybuf, gsem, ssem):
    i = pl.program_id(0)
    nact = nact_ref[0]
    slot = i % 2
    blk = MOE_BLOCK

    def start_gather(block, sl):
        def body(r, c):
            tok = tok_ref[block * blk + r]
            pltpu.make_async_copy(x_hbm.at[pl.ds(tok, 1)], xbuf.at[sl, pl.ds(r, 1)],
                                  gsem.at[sl]).start()
            return c
        lax.fori_loop(0, blk, body, 0, unroll=8)

    def wait_gather(sl):
        pltpu.make_async_copy(x_hbm.at[pl.ds(0, blk)], xbuf.at[sl], gsem.at[sl]).wait()

    def start_scatter(block, sl):
        def body(r, c):
            dst = dst_ref[block * blk + r]
            pltpu.make_async_copy(ybuf.at[sl, pl.ds(r, 1)], out_hbm.at[pl.ds(dst, 1)],
                                  ssem.at[sl]).start()
            return c
        lax.fori_loop(0, cnt_ref[block], body, 0)

    def wait_scatter(block, sl):
        cnt = cnt_ref[block]
        whole = pl.multiple_of((cnt // SUBLANES) * SUBLANES, SUBLANES)

        @pl.when(whole > 0)
        def _():
            pltpu.make_async_copy(ybuf.at[sl, pl.ds(0, whole)], out_hbm.at[pl.ds(0, whole)],
                                  ssem.at[sl]).wait()

        def body(r, c):
            pltpu.make_async_copy(ybuf.at[sl, pl.ds(0, 1)], out_hbm.at[pl.ds(0, 1)],
                                  ssem.at[sl]).wait()
            return c
        lax.fori_loop(0, cnt - whole, body, 0)

    @pl.when((i == 0) & (nact > 0))
    def _():
        start_gather(0, 0)

    @pl.when(i < nact)
    def _():
        @pl.when(i + 1 < nact)
        def _():
            start_gather(i + 1, 1 - slot)

        wait_gather(slot)
        xb = xbuf[slot].astype(BF16)
        hcat = jnp.dot(xb, wup_ref[0], preferred_element_type=F32)
        hg = hcat[:, :MOE_D_EXPERT]
        hu = hcat[:, MOE_D_EXPERT:]
        act = (hg * jax.nn.sigmoid(hg) * hu).astype(BF16)
        y = jnp.dot(act, wdn_ref[0], preferred_element_type=F32) * gate_ref[...]

        @pl.when(i >= 2)
        def _():
            wait_scatter(i - 2, slot)

        ybuf[slot] = y
        start_scatter(i, slot)

        @pl.when(i == nact - 1)
        def _():
            wait_scatter(i, slot)

            @pl.when(i >= 1)
            def _():
                wait_scatter(i - 1, 1 - slot)


def _moe_ffn(x, block_expert, nact, block_cnt, row_tok, row_dst, row_gate, w_up, w_down):
    n_rows = row_tok.shape[0]
    n_blocks = n_rows // MOE_BLOCK
    grid_spec = pltpu.PrefetchScalarGridSpec(
        num_scalar_prefetch=5,
        grid=(n_blocks,),
        in_specs=[pl.BlockSpec(memory_space=pl.ANY),
                  pl.BlockSpec((MOE_BLOCK, 1), lambda i, be, *_: (i, 0)),
                  pl.BlockSpec((1, D_MODEL, 2 * MOE_D_EXPERT), lambda i, be, *_: (be[i], 0, 0)),
                  pl.BlockSpec((1, MOE_D_EXPERT, D_MODEL), lambda i, be, *_: (be[i], 0, 0))],
        out_specs=pl.BlockSpec(memory_space=pl.ANY),
        scratch_shapes=[pltpu.VMEM((2, MOE_BLOCK, D_MODEL), F32),
                        pltpu.VMEM((2, MOE_BLOCK, D_MODEL), F32),
                        pltpu.SemaphoreType.DMA((2,)),
                        pltpu.SemaphoreType.DMA((2,))],
    )
    return pl.pallas_call(
        _moe_ffn_kernel,
        grid_spec=grid_spec,
        out_shape=jax.ShapeDtypeStruct((x.shape[0] * MOE_TOPK, D_MODEL), F32),
        compiler_params=_cparams("arbitrary"),
        name="moe_ffn",
    )(block_expert, nact, block_cnt, row_tok, row_dst, x, row_gate, w_up, w_down)


def _moe_dispatch(route, n):
    nk = n * MOE_TOPK
    n_rows = -(-nk // MOE_BLOCK) * MOE_BLOCK + MOE_EXPERTS * MOE_BLOCK
    n_blocks = n_rows // MOE_BLOCK
    gates = route[:, 0:MOE_TOPK].reshape(nk)
    e_flat = route[:, MOE_TOPK:2 * MOE_TOPK].astype(jnp.int32).reshape(nk)
    onehot = (e_flat[:, None] == jnp.arange(MOE_EXPERTS, dtype=jnp.int32)[None, :]).astype(jnp.int32)
    csum = jnp.cumsum(onehot, axis=0)
    counts = csum[-1]
    pos = jnp.sum(onehot * csum, axis=1) - 1
    padded = (counts + MOE_BLOCK - 1) // MOE_BLOCK * MOE_BLOCK
    pad_end = jnp.cumsum(padded)
    pad_start = pad_end - padded
    dest = pad_start[e_flat] + pos
    slot = jnp.arange(nk, dtype=jnp.int32)
    tok = slot // MOE_TOPK
    row_tok = jnp.zeros((n_rows,), jnp.int32).at[dest].set(tok)
    row_dst = jnp.zeros((n_rows,), jnp.int32).at[dest].set((slot % MOE_TOPK) * n + tok)
    row_gate = jnp.zeros((n_rows,), F32).at[dest].set(gates).reshape(n_rows, 1)
    block_start = jnp.arange(n_blocks, dtype=jnp.int32) * MOE_BLOCK
    block_expert = jnp.minimum(jnp.searchsorted(pad_end, block_start, side='right'),
                               MOE_EXPERTS - 1).astype(jnp.int32)
    nact = (pad_end[-1:] // MOE_BLOCK).astype(jnp.int32)
    block_cnt = jnp.clip(counts[block_expert] - (block_start - pad_start[block_expert]),
                         0, MOE_BLOCK).astype(jnp.int32)
    return block_expert, nact, block_cnt, row_tok, row_dst, row_gate


def _combine_ln_kernel(x_ref, y0_ref, y1_ref, g_ref, b_ref, o_ref):
    o_ref[...] = _layer_norm_rows(DN_ALPHA * x_ref[...] + (y0_ref[...] + y1_ref[...]),
                                  g_ref[...], b_ref[...])


def _combine_ln(x, ys, g, b):
    n = x.shape[0]
    tm = 512
    nt = n // tm
    row = pl.BlockSpec((tm, D_MODEL), lambda i: (i, 0))
    vec = pl.BlockSpec((1, D_MODEL), lambda i: (0, 0))
    return pl.pallas_call(
        _combine_ln_kernel,
        grid=(nt,),
        in_specs=[row, row, pl.BlockSpec((tm, D_MODEL), lambda i: (i + nt, 0)), vec, vec],
        out_specs=row,
        out_shape=jax.ShapeDtypeStruct((n, D_MODEL), F32),
        compiler_params=_cparams("arbitrary"),
        name="combine_ln",
    )(x, ys, ys, g.reshape(1, D_MODEL), b.reshape(1, D_MODEL))


def kernel(x, hg_w_in, hg_lb_logits, hg_norm_w, hg_w_out, cv_w_in, cv_w, cv_w_out, ln_g, ln_b,
           moe_w_group, moe_b_group, moe_w_expert, moe_b_expert, moe_w_up, moe_w_down):
    batch, seq, d = x.shape
    n = batch * seq
    xf = x.reshape(n, d)
    for layer in range(DEPTH):
        j = layer // 2
        if layer % 2 == 0:
            proj = _matmul(xf, hg_w_in[j].astype(BF16))
            o_f, o_b = _gla(proj, hg_lb_logits, layer, batch, seq)
            xf = _hgrn_out(o_f, o_b, proj, hg_norm_w[j], hg_w_out[j].astype(BF16), xf,
                           ln_g[layer, 0], ln_b[layer, 0])
        else:
            proj = _matmul(xf, cv_w_in[j].astype(BF16))
            xf = _conv_out(proj, cv_w[j], cv_w_out[j].astype(BF16), xf,
                           ln_g[layer, 0], ln_b[layer, 0], seq)
        route = _router(xf, moe_w_group[layer], moe_b_group[layer],
                        moe_w_expert[layer], moe_b_expert[layer])
        block_expert, nact, block_cnt, row_tok, row_dst, row_gate = _moe_dispatch(route, n)
        ys = _moe_ffn(xf, block_expert, nact, block_cnt, row_tok, row_dst, row_gate,
                      moe_w_up[layer].astype(BF16), moe_w_down[layer].astype(BF16))
        xf = _combine_ln(xf, ys, ln_g[layer, 1], ln_b[layer, 1])
    return xf.reshape(batch, seq, d)
```

```python
import functools

import jax
import jax.numpy as jnp
from jax import lax
from jax.experimental import pallas as pl
from jax.experimental.pallas import tpu as pltpu

D_MODEL = 1024
DEPTH = 4
HG_DK = 128
HG_HEADS = D_MODEL // HG_DK
HG_STREAMS = 5
CONV_STREAMS = 3
MOE_GROUPS = 4
MOE_EXPERTS_PER_GROUP = 8
MOE_EXPERTS = MOE_GROUPS * MOE_EXPERTS_PER_GROUP
MOE_TOPK = 2
MOE_D_EXPERT = D_MODEL // 2
MOE_BLOCK = 256
DN_ALPHA = (2.0 * DEPTH) ** 0.25
LN_EPS = 1e-5
RMS_EPS = 1e-6

LANES = 128
SUBLANES = 8
VMEM_LIMIT = 48 * 1024 * 1024
GLA_CHUNK = 64
GLA_TBLOCK = 512
ROW_TILE = 256
NEG_BIG = -1e30

BF16 = jnp.bfloat16
F32 = jnp.float32


def _cparams(*sem):
    return pltpu.CompilerParams(dimension_semantics=sem, vmem_limit_bytes=VMEM_LIMIT)


def _mm_kernel(x_ref, w_ref, o_ref):
    o_ref[...] = jnp.dot(x_ref[...].astype(BF16), w_ref[...], preferred_element_type=F32)


def _matmul(x, w, tm=512, tn=1024):
    n, k = x.shape
    nn = w.shape[1]
    return pl.pallas_call(
        _mm_kernel,
        grid=(nn // tn, n // tm),
        in_specs=[pl.BlockSpec((tm, k), lambda j, i: (i, 0)),
                  pl.BlockSpec((k, tn), lambda j, i: (0, j))],
        out_specs=pl.BlockSpec((tm, tn), lambda j, i: (i, j)),
        out_shape=jax.ShapeDtypeStruct((n, nn), F32),
        compiler_params=_cparams("arbitrary", "arbitrary"),
        name="in_proj",
    )(x, w)


def _layer_norm_rows(y, g, b):
    mu = jnp.mean(y, axis=-1, keepdims=True)
    yc = y - mu
    var = jnp.mean(yc * yc, axis=-1, keepdims=True)
    return yc * lax.rsqrt(var + LN_EPS) * g + b


def _chunk_cumsum(lf, row_in_chunk, reverse):
    rows = lf.shape[0]
    p = lf
    s = 1
    while s < GLA_CHUNK:
        if reverse:
            sh = pltpu.roll(p, rows - s, 0)
            p = p + jnp.where(row_in_chunk + s < GLA_CHUNK, sh, 0.0)
        else:
            sh = pltpu.roll(p, s, 0)
            p = p + jnp.where(row_in_chunk >= s, sh, 0.0)
        s *= 2
    return p


def _gla_chunk(q, kk, v, g, st, reverse):
    c = GLA_CHUNK
    nt = c // SUBLANES
    ii = lax.broadcasted_iota(jnp.int32, (c, c), 0)
    jj = lax.broadcasted_iota(jnp.int32, (c, c), 1)
    nt_dims = (((1,), (1,)), ((), ()))

    o = lax.dot_general((q * jnp.exp(g)).astype(BF16), st.astype(BF16), nt_dims,
                        preferred_element_type=F32)

    p_off = jnp.zeros((c, c), F32)
    b = 2 * SUBLANES
    while b <= c:
        h = b // 2
        pieces = []
        for m in range(c // b):
            r = m * b + h if reverse else m * b + h - 1
            pieces.append(jnp.broadcast_to(g[r:r + 1, :], (b, LANES)))
        gref = pieces[0] if len(pieces) == 1 else jnp.concatenate(pieces, axis=0)
        qfac = jnp.exp(jnp.minimum(g - gref, 0.0))
        kfac = jnp.exp(jnp.minimum(gref - g, 0.0))
        pl_ = lax.dot_general((q * qfac).astype(BF16), (kk * kfac).astype(BF16), nt_dims,
                              preferred_element_type=F32)
        same = (ii // b) == (jj // b)
        if reverse:
            mask = same & ((ii % b) < h) & ((jj % b) >= h)
        else:
            mask = same & ((ii % b) >= h) & ((jj % b) < h)
        p_off = p_off + jnp.where(mask, pl_, 0.0)
        b *= 2
    o = o + jnp.dot(p_off.astype(BF16), v.astype(BF16), preferred_element_type=F32)

    q3 = q.reshape(nt, SUBLANES, LANES)
    k3 = kk.reshape(nt, SUBLANES, LANES)
    v3 = v.reshape(nt, SUBLANES, LANES)
    g3 = g.reshape(nt, SUBLANES, LANES)
    i3 = lax.broadcasted_iota(jnp.int32, (nt, SUBLANES, LANES), 1)
    o3 = jnp.zeros((nt, SUBLANES, LANES), F32)
    for r in range(SUBLANES):
        gr = g3[:, r:r + 1, :]
        valid = (i3 <= r) if reverse else (i3 >= r)
        dec = jnp.where(valid, jnp.exp(jnp.minimum(g3 - gr, 0.0)), 0.0)
        a = q3 * k3[:, r:r + 1, :] * dec
        s = jnp.sum(a, axis=-1, keepdims=True)
        o3 = o3 + s * v3[:, r:r + 1, :]
    o = o + o3.reshape(c, LANES)

    gl = g[0:1, :] if reverse else g[c - 1:c, :]
    kd = (kk * jnp.exp(gl - g)).astype(BF16)
    upd = lax.dot_general(v.astype(BF16), kd, (((0,), (0,)), ((), ())),
                          preferred_element_type=F32)
    st = st * jnp.exp(gl) + upd
    return o, st


def _gla_kernel(layer, lbl_ref, qf_ref, ff_ref, vf_ref, qb_ref, fb_ref, vb_ref,
                of_ref, ob_ref, st_ref, q_s, k_s, v_s, g_s):
    tb = GLA_TBLOCK
    nc = tb // GLA_CHUNK

    @pl.when(pl.program_id(2) == 0)
    def _():
        st_ref[...] = jnp.zeros_like(st_ref)

    if layer > 0:
        lg = lbl_ref[...]
        e = jnp.exp(lg - jnp.max(lg, axis=0, keepdims=True))
        sm = e / jnp.sum(e, axis=0, keepdims=True)
        lb = sm[1]
        for l in range(2, layer + 1):
            lb = lb + sm[l]

    row_in_chunk = lax.broadcasted_iota(jnp.int32, (tb, LANES), 0) % GLA_CHUNK

    for d, (q_ref, f_ref, v_ref) in enumerate(((qf_ref, ff_ref, vf_ref),
                                               (qb_ref, fb_ref, vb_ref))):
        qx = q_ref[...]
        fx = f_ref[...]
        sp = jnp.log1p(jnp.exp(-jnp.abs(fx)))
        logsig = jnp.minimum(fx, 0.0) - sp
        sig_neg = jnp.exp(jnp.minimum(-fx, 0.0) - sp)
        if layer > 0:
            lbd = lb[d:d + 1, :]
            a = jnp.log(lbd)
            bb = jnp.log1p(-lbd) + logsig
            lf = jnp.maximum(a, bb) + jnp.log1p(jnp.exp(-jnp.abs(a - bb)))
            kk = (1.0 - lbd) * sig_neg
        else:
            lf = logsig
            kk = sig_neg
        q_s[d] = qx * jax.nn.sigmoid(qx)
        k_s[d] = kk
        v_s[d] = v_ref[...]
        g_s[d] = _chunk_cumsum(lf, row_in_chunk, reverse=(d == 1))

    def body(ci, carry):
        sf = pl.multiple_of(ci * GLA_CHUNK, GLA_CHUNK)
        sb = pl.multiple_of((nc - 1 - ci) * GLA_CHUNK, GLA_CHUNK)
        for d, start, o_ref in ((0, sf, of_ref), (1, sb, ob_ref)):
            sl = pl.ds(start, GLA_CHUNK)
            o, st = _gla_chunk(q_s[d, sl, :], k_s[d, sl, :], v_s[d, sl, :], g_s[d, sl, :],
                               st_ref[d], reverse=(d == 1))
            o_ref[sl, :] = o
            st_ref[d] = st
        return carry

    lax.fori_loop(0, nc, body, 0)


def _gla(proj, lb_logits, layer, batch, seq):
    n = proj.shape[0]
    tb = GLA_TBLOCK
    nb = seq // tb
    h8 = HG_HEADS

    def spec(stream, rev):
        if rev:
            return pl.BlockSpec((tb, LANES), lambda b, h, c: (b * nb + nb - 1 - c, stream * h8 + h))
        return pl.BlockSpec((tb, LANES), lambda b, h, c: (b * nb + c, stream * h8 + h))

    o_f_spec = pl.BlockSpec((tb, LANES), lambda b, h, c: (b * nb + c, h))
    o_b_spec = pl.BlockSpec((tb, LANES), lambda b, h, c: (b * nb + nb - 1 - c, h))
    return pl.pallas_call(
        functools.partial(_gla_kernel, layer),
        grid=(batch, h8, nb),
        in_specs=[pl.BlockSpec((DEPTH, 2, LANES), lambda b, h, c: (0, 0, h)),
                  spec(0, False), spec(1, False), spec(3, False),
                  spec(0, True), spec(2, True), spec(3, True)],
        out_specs=[o_f_spec, o_b_spec],
        out_shape=[jax.ShapeDtypeStruct((n, D_MODEL), F32)] * 2,
        scratch_shapes=[pltpu.VMEM((2, HG_DK, HG_DK), F32)]
        + [pltpu.VMEM((2, tb, LANES), F32)] * 4,
        compiler_params=_cparams("arbitrary", "arbitrary", "arbitrary"),
        name="gla",
    )(lb_logits, proj, proj, proj, proj, proj, proj)


def _hgrn_out_kernel(of_ref, ob_ref, gate_ref, nw_ref, w_ref, x_ref, g_ref, b_ref, o_ref):
    o = of_ref[...] + ob_ref[...]
    parts = []
    for h in range(HG_HEADS):
        oh = o[:, h * LANES:(h + 1) * LANES]
        ms = jnp.mean(oh * oh, axis=-1, keepdims=True)
        parts.append(oh * lax.rsqrt(ms + RMS_EPS))
    gate = gate_ref[...]
    y = jnp.concatenate(parts, axis=-1) * nw_ref[...] * (gate * jax.nn.sigmoid(gate))
    mix = jnp.dot(y.astype(BF16), w_ref[...], preferred_element_type=F32)
    o_ref[...] = _layer_norm_rows(DN_ALPHA * x_ref[...] + mix, g_ref[...], b_ref[...])


def _hgrn_out(o_f, o_b, proj, norm_w, w_out, x, g, b):
    n = x.shape[0]
    tm = ROW_TILE
    row = pl.BlockSpec((tm, D_MODEL), lambda i: (i, 0))
    vec = pl.BlockSpec((1, D_MODEL), lambda i: (0, 0))
    return pl.pallas_call(
        _hgrn_out_kernel,
        grid=(n // tm,),
        in_specs=[row, row, pl.BlockSpec((tm, D_MODEL), lambda i: (i, HG_STREAMS - 1)), vec,
                  pl.BlockSpec((D_MODEL, D_MODEL), lambda i: (0, 0)), row, vec, vec],
        out_specs=row,
        out_shape=jax.ShapeDtypeStruct((n, D_MODEL), F32),
        compiler_params=_cparams("arbitrary"),
        name="hgrn_out",
    )(o_f, o_b, proj, jnp.tile(norm_w, HG_HEADS).reshape(1, D_MODEL), w_out, x,
      g.reshape(1, D_MODEL), b.reshape(1, D_MODEL))


def _conv_out_kernel(tiles_per_seq, bg_ref, cg_ref, h_ref, cgp_ref, hp_ref, cgn_ref, hn_ref,
                     cw_ref, w_ref, x_ref, g_ref, b_ref, o_ref):
    i = pl.program_id(0)
    tm = cg_ref.shape[0]
    u = cg_ref[...] * h_ref[...]
    first = (i % tiles_per_seq) == 0
    last = (i % tiles_per_seq) == tiles_per_seq - 1
    u_prev_row = jnp.where(first, 0.0, cgp_ref[SUBLANES - 1:SUBLANES, :] * hp_ref[SUBLANES - 1:SUBLANES, :])
    u_next_row = jnp.where(last, 0.0, cgn_ref[0:1, :] * hn_ref[0:1, :])
    rows = lax.broadcasted_iota(jnp.int32, u.shape, 0)
    u_prev = jnp.where(rows == 0, u_prev_row, pltpu.roll(u, 1, 0))
    u_next = jnp.where(rows == tm - 1, u_next_row, pltpu.roll(u, tm - 1, 0))
    cw = cw_ref[...]
    y = u_prev * cw[0:1, :] + u * cw[1:2, :] + u_next * cw[2:3, :]
    mix = jnp.dot((bg_ref[...] * y).astype(BF16), w_ref[...], preferred_element_type=F32)
    o_ref[...] = _layer_norm_rows(DN_ALPHA * x_ref[...] + mix, g_ref[...], b_ref[...])


def _conv_out(proj, conv_w, w_out, x, g, b, seq):
    n = x.shape[0]
    tm = ROW_TILE
    r8 = tm // SUBLANES
    nblk8 = n // SUBLANES
    row = pl.BlockSpec((tm, D_MODEL), lambda i: (i, 0))
    vec = pl.BlockSpec((1, D_MODEL), lambda i: (0, 0))

    def main(stream):
        return pl.BlockSpec((tm, D_MODEL), lambda i: (i, stream))

    def prev(stream):
        return pl.BlockSpec((SUBLANES, D_MODEL), lambda i: (jnp.maximum(i * r8 - 1, 0), stream))

    def nxt(stream):
        return pl.BlockSpec((SUBLANES, D_MODEL),
                            lambda i: (jnp.minimum((i + 1) * r8, nblk8 - 1), stream))

    return pl.pallas_call(
        functools.partial(_conv_out_kernel, seq // tm),
        grid=(n // tm,),
        in_specs=[main(0), main(1), main(2), prev(1), prev(2), nxt(1), nxt(2),
                  pl.BlockSpec((3, D_MODEL), lambda i: (0, 0)),
                  pl.BlockSpec((D_MODEL, D_MODEL), lambda i: (0, 0)), row, vec, vec],
        out_specs=row,
        out_shape=jax.ShapeDtypeStruct((n, D_MODEL), F32),
        compiler_params=_cparams("arbitrary"),
        name="conv_out",
    )(proj, proj, proj, proj, proj, proj, proj, conv_w, w_out, x,
      g.reshape(1, D_MODEL), b.reshape(1, D_MODEL))


def _router_kernel(x_ref, w_ref, b_ref, tri_ref, o_ref, cnt_ref, carry_ref):
    @pl.when(pl.program_id(0) == 0)
    def _():
        carry_ref[...] = jnp.zeros_like(carry_ref)

    logits = jnp.dot(x_ref[...], w_ref[...], precision=lax.Precision.HIGHEST,
                     preferred_element_type=F32) + b_ref[...]
    lane = lax.broadcasted_iota(jnp.int32, logits.shape, 1)
    lane_f = lane.astype(F32)
    gl = jnp.where(lane < MOE_GROUPS, logits, NEG_BIG)
    gmax = jnp.max(gl, axis=-1, keepdims=True)
    gsum = jnp.sum(jnp.exp(gl - gmax), axis=-1, keepdims=True)
    p_group = 1.0 / gsum
    g_sel = jnp.min(jnp.where(gl == gmax, lane_f, 1e9), axis=-1, keepdims=True)
    e_lane = lane - MOE_GROUPS
    in_grp = ((lane >= MOE_GROUPS) & (lane < MOE_GROUPS + MOE_EXPERTS)
              & ((e_lane // MOE_EXPERTS_PER_GROUP).astype(F32) == g_sel))
    el = jnp.where(in_grp, logits, NEG_BIG)
    t1 = jnp.max(el, axis=-1, keepdims=True)
    i1 = jnp.min(jnp.where(el == t1, lane_f, 1e9), axis=-1, keepdims=True)
    el2 = jnp.where(lane_f == i1, NEG_BIG, el)
    t2 = jnp.max(el2, axis=-1, keepdims=True)
    i2 = jnp.min(jnp.where(el2 == t2, lane_f, 1e9), axis=-1, keepdims=True)
    z = jnp.exp(t2 - t1)
    g1 = p_group / (1.0 + z)
    g2 = g1 * z
    out = jnp.where(lane == 0, g1, 0.0)
    out = jnp.where(lane == 1, g2, out)
    out = jnp.where(lane == 2, i1 - MOE_GROUPS, out)
    out = jnp.where(lane == 3, i2 - MOE_GROUPS, out)

    sel1 = lane_f == i1
    sel2 = lane_f == i2
    onehot = jnp.where(sel1 | sel2, 1.0, 0.0)
    prefix = jnp.dot(tri_ref[...], onehot.astype(BF16), preferred_element_type=F32)
    before = prefix + carry_ref[0:1, :]
    r1 = jnp.sum(jnp.where(sel1, before, 0.0), axis=-1, keepdims=True)
    r2 = jnp.sum(jnp.where(sel2, before, 0.0), axis=-1, keepdims=True)
    out = jnp.where(lane == 4, r1, out)
    out = jnp.where(lane == 5, r2, out)
    o_ref[...] = out
    carry_ref[...] = carry_ref[...] + jnp.sum(onehot, axis=0, keepdims=True)
    cnt_ref[...] = carry_ref[...]


def _router(x, w_group, b_group, w_expert, b_expert):
    n = x.shape[0]
    tm = 512
    pad = LANES - MOE_GROUPS - MOE_EXPERTS
    w = jnp.concatenate([w_group, w_expert, jnp.zeros((D_MODEL, pad), F32)], axis=1)
    b = jnp.concatenate([b_group, b_expert, jnp.zeros((pad,), F32)]).reshape(1, LANES)
    r = jnp.arange(tm, dtype=jnp.int32)
    tri = (r[None, :] < r[:, None]).astype(BF16)
    route, cnt = pl.pallas_call(
        _router_kernel,
        grid=(n // tm,),
        in_specs=[pl.BlockSpec((tm, D_MODEL), lambda i: (i, 0)),
                  pl.BlockSpec((D_MODEL, LANES), lambda i: (0, 0)),
                  pl.BlockSpec((1, LANES), lambda i: (0, 0)),
                  pl.BlockSpec((tm, tm), lambda i: (0, 0))],
        out_specs=[pl.BlockSpec((tm, LANES), lambda i: (i, 0)),
                   pl.BlockSpec((SUBLANES, LANES), lambda i: (0, 0))],
        out_shape=[jax.ShapeDtypeStruct((n, LANES), F32),
                   jax.ShapeDtypeStruct((SUBLANES, LANES), F32)],
        scratch_shapes=[pltpu.VMEM((SUBLANES, LANES), F32)],
        compiler_params=_cparams("arbitrary"),
        name="router",
    )(x, w, b, tri)
    return route, cnt[0, MOE_GROUPS:MOE_GROUPS + MOE_EXPERTS].astype(jnp.int32)


DISPATCH_TOKENS = 256


def _moe_plan(route, counts, n):
    nk = n * MOE_TOPK
    n_rows = -(-nk // MOE_BLOCK) * MOE_BLOCK + MOE_EXPERTS * MOE_BLOCK
    n_blocks = n_rows // MOE_BLOCK
    padded = (counts + MOE_BLOCK - 1) // MOE_BLOCK * MOE_BLOCK
    pad_end = jnp.cumsum(padded)
    pad_start = pad_end - padded
    e = route[:, MOE_TOPK:2 * MOE_TOPK].astype(jnp.int32)
    rank = route[:, 2 * MOE_TOPK:3 * MOE_TOPK].astype(jnp.int32)
    ids = jnp.arange(MOE_EXPERTS, dtype=jnp.int32)
    base = jnp.sum(jnp.where(e[:, :, None] == ids, pad_start, 0), axis=-1)
    dest = (base + rank).reshape(nk)
    block_start = jnp.arange(n_blocks, dtype=jnp.int32) * MOE_BLOCK
    block_expert = jnp.minimum(jnp.sum(block_start[:, None] >= pad_end[None, :], axis=1),
                               MOE_EXPERTS - 1).astype(jnp.int32)
    nact = (pad_end[-1:] // MOE_BLOCK).astype(jnp.int32)
    fill_start = (pad_start + counts).astype(jnp.int32)
    return dest, block_expert, nact, fill_start, pad_end.astype(jnp.int32), n_rows


def _dispatch_kernel(dest_ref, fs_ref, fe_ref, nact_ref, x_hbm, xs_hbm, zbuf, sem, zsem):
    i = pl.program_id(0)
    nsteps = pl.num_programs(0)
    slot = i % 2
    ct = DISPATCH_TOKENS
    n_blocks = xs_hbm.shape[0] // MOE_BLOCK

    def body(t, c):
        tok = i * ct + t
        for k in range(MOE_TOPK):
            d = dest_ref[tok * MOE_TOPK + k]
            pltpu.make_async_copy(x_hbm.at[pl.ds(tok, 1)], xs_hbm.at[pl.ds(d, 1)],
                                  sem.at[slot]).start()
        return c
    lax.fori_loop(0, ct, body, 0, unroll=4)

    def wait_step(sl):
        for _ in range(MOE_TOPK):
            pltpu.make_async_copy(x_hbm.at[pl.ds(0, ct)], xs_hbm.at[pl.ds(0, ct)],
                                  sem.at[sl]).wait()

    @pl.when(i >= 1)
    def _():
        wait_step(1 - slot)

    @pl.when(i == nsteps - 1)
    def _():
        wait_step(slot)
        zbuf[...] = jnp.zeros_like(zbuf)

        def fill_rows(wait):
            def per_expert(e, c):
                def per_row(r, c2):
                    cp = pltpu.make_async_copy(zbuf.at[pl.ds(0, 1)], xs_hbm.at[pl.ds(r, 1)],
                                               zsem.at[0])
                    cp.wait() if wait else cp.start()
                    return c2
                lax.fori_loop(fs_ref[e], fe_ref[e], per_row, 0)
                return c
            lax.fori_loop(0, MOE_EXPERTS, per_expert, 0)

        def fill_blocks(wait):
            def per_block(b, c):
                start = pl.multiple_of(b * MOE_BLOCK, MOE_BLOCK)
                cp = pltpu.make_async_copy(zbuf, xs_hbm.at[pl.ds(start, MOE_BLOCK)], zsem.at[1])
                cp.wait() if wait else cp.start()
                return c
            lax.fori_loop(nact_ref[0], n_blocks, per_block, 0)

        fill_rows(False)
        fill_blocks(False)
        fill_rows(True)
        fill_blocks(True)


def _dispatch(x, dest, fill_start, fill_end, nact, n_rows):
    n = x.shape[0]
    grid_spec = pltpu.PrefetchScalarGridSpec(
        num_scalar_prefetch=4,
        grid=(n // DISPATCH_TOKENS,),
        in_specs=[pl.BlockSpec(memory_space=pl.ANY)],
        out_specs=pl.BlockSpec(memory_space=pl.ANY),
        scratch_shapes=[pltpu.VMEM((MOE_BLOCK, D_MODEL), F32),
                        pltpu.SemaphoreType.DMA((2,)),
                        pltpu.SemaphoreType.DMA((2,))],
    )
    return pl.pallas_call(
        _dispatch_kernel,
        grid_spec=grid_spec,
        out_shape=jax.ShapeDtypeStruct((n_rows, D_MODEL), F32),
        compiler_params=_cparams("arbitrary"),
        name="dispatch",
    )(dest, fill_start, fill_end, nact, x)


def _moe_ffn_kernel(be_ref, nact_ref, xs_ref, wup_ref, wdn_ref, ys_ref, wup_bf, wdn_bf):
    i = pl.program_id(0)
    active = i < nact_ref[0]
    new_expert = (i == 0) | (be_ref[i] != be_ref[jnp.maximum(i - 1, 0)])

    @pl.when(active & new_expert)
    def _():
        wup_bf[...] = wup_ref[0].astype(BF16)
        wdn_bf[...] = wdn_ref[0].astype(BF16)

    @pl.when(active)
    def _():
        hcat = jnp.dot(xs_ref[...].astype(BF16), wup_bf[...], preferred_element_type=F32)
        hg = hcat[:, :MOE_D_EXPERT]
        hu = hcat[:, MOE_D_EXPERT:]
        act = (hg * jax.nn.sigmoid(hg) * hu).astype(BF16)
        ys_ref[...] = jnp.dot(act, wdn_bf[...], preferred_element_type=F32)

    @pl.when(jnp.logical_not(active))
    def _():
        ys_ref[...] = jnp.zeros_like(ys_ref)


def _moe_ffn(xs, block_expert, nact, w_up, w_down):
    n_rows = xs.shape[0]
    grid_spec = pltpu.PrefetchScalarGridSpec(
        num_scalar_prefetch=2,
        grid=(n_rows // MOE_BLOCK,),
        in_specs=[pl.BlockSpec((MOE_BLOCK, D_MODEL), lambda i, be, na: (i, 0)),
                  pl.BlockSpec((1, D_MODEL, 2 * MOE_D_EXPERT), lambda i, be, na: (be[i], 0, 0)),
                  pl.BlockSpec((1, MOE_D_EXPERT, D_MODEL), lambda i, be, na: (be[i], 0, 0))],
        out_specs=pl.BlockSpec((MOE_BLOCK, D_MODEL), lambda i, be, na: (i, 0)),
        scratch_shapes=[pltpu.VMEM((D_MODEL, 2 * MOE_D_EXPERT), BF16),
                        pltpu.VMEM((MOE_D_EXPERT, D_MODEL), BF16)],
    )
    return pl.pallas_call(
        _moe_ffn_kernel,
        grid_spec=grid_spec,
        out_shape=jax.ShapeDtypeStruct((n_rows, D_MODEL), F32),
        compiler_params=_cparams("arbitrary"),
        name="moe_ffn",
    )(block_expert, nact, xs, w_up, w_down)


def _combine_ln_kernel(dest_ref, x_ref, route_ref, ys_hbm, g_ref, b_ref, o_ref, ybuf, sem):
    i = pl.program_id(0)
    nsteps = pl.num_programs(0)
    slot = i % 2
    tm = x_ref.shape[0]

    def start_gather(step, sl):
        def body(t, c):
            for k in range(MOE_TOPK):
                d = dest_ref[(step * tm + t) * MOE_TOPK + k]
                pltpu.make_async_copy(ys_hbm.at[pl.ds(d, 1)], ybuf.at[sl, k, pl.ds(t, 1)],
                                      sem.at[sl]).start()
            return c
        lax.fori_loop(0, tm, body, 0, unroll=4)

    @pl.when(i == 0)
    def _():
        start_gather(0, 0)

    @pl.when(i + 1 < nsteps)
    def _():
        start_gather(i + 1, 1 - slot)

    for k in range(MOE_TOPK):
        pltpu.make_async_copy(ys_hbm.at[pl.ds(0, tm)], ybuf.at[slot, k], sem.at[slot]).wait()
    route = route_ref[...]
    ffn = route[:, 0:1] * ybuf[slot, 0] + route[:, 1:2] * ybuf[slot, 1]
    o_ref[...] = _layer_norm_rows(DN_ALPHA * x_ref[...] + ffn, g_ref[...], b_ref[...])


def _combine_ln(x, route, ys, dest, g, b):
    n = x.shape[0]
    tm = ROW_TILE
    vec = pl.BlockSpec((1, D_MODEL), lambda i, d: (0, 0))
    grid_spec = pltpu.PrefetchScalarGridSpec(
        num_scalar_prefetch=1,
        grid=(n // tm,),
        in_specs=[pl.BlockSpec((tm, D_MODEL), lambda i, d: (i, 0)),
                  pl.BlockSpec((tm, LANES), lambda i, d: (i, 0)),
                  pl.BlockSpec(memory_space=pl.ANY), vec, vec],
        out_specs=pl.BlockSpec((tm, D_MODEL), lambda i, d: (i, 0)),
        scratch_shapes=[pltpu.VMEM((2, MOE_TOPK, tm, D_MODEL), F32),
                        pltpu.SemaphoreType.DMA((2,))],
    )
    return pl.pallas_call(
        _combine_ln_kernel,
        grid_spec=grid_spec,
        out_shape=jax.ShapeDtypeStruct((n, D_MODEL), F32),
        compiler_params=_cparams("arbitrary"),
        name="combine_ln",
    )(dest, x, route, ys, g.reshape(1, D_MODEL), b.reshape(1, D_MODEL))


def kernel(x, hg_w_in, hg_lb_logits, hg_norm_w, hg_w_out, cv_w_in, cv_w, cv_w_out, ln_g, ln_b,
           moe_w_group, moe_b_group, moe_w_expert, moe_b_expert, moe_w_up, moe_w_down):
    batch, seq, d = x.shape
    n = batch * seq
    xf = x.reshape(n, d)
    for layer in range(DEPTH):
        j = layer // 2
        if layer % 2 == 0:
            proj = _matmul(xf, hg_w_in[j].astype(BF16))
            o_f, o_b = _gla(proj, hg_lb_logits, layer, batch, seq)
            xf = _hgrn_out(o_f, o_b, proj, hg_norm_w[j], hg_w_out[j].astype(BF16), xf,
                           ln_g[layer, 0], ln_b[layer, 0])
        else:
            proj = _matmul(xf, cv_w_in[j].astype(BF16))
            xf = _conv_out(proj, cv_w[j], cv_w_out[j].astype(BF16), xf,
                           ln_g[layer, 0], ln_b[layer, 0], seq)
        route, counts = _router(xf, moe_w_group[layer], moe_b_group[layer],
                                moe_w_expert[layer], moe_b_expert[layer])
        dest, block_expert, nact, fill_start, fill_end, n_rows = _moe_plan(route, counts, n)
        xs = _dispatch(xf, dest, fill_start, fill_end, nact, n_rows)
        ys = _moe_ffn(xs, block_expert, nact, moe_w_up[layer], moe_w_down[layer])
        xf = _combine_ln(xf, route, ys, dest, ln_g[layer, 1], ln_b[layer, 1])
    return xf.reshape(batch, seq, d)
```

```python
import functools

import jax
import jax.numpy as jnp
from jax import lax
from jax.experimental import pallas as pl
from jax.experimental.pallas import tpu as pltpu

D_MODEL = 1024
DEPTH = 4
HG_DK = 128
HG_HEADS = D_MODEL // HG_DK
HG_STREAMS = 5
CONV_STREAMS = 3
MOE_GROUPS = 4
MOE_EXPERTS_PER_GROUP = 8
MOE_EXPERTS = MOE_GROUPS * MOE_EXPERTS_PER_GROUP
MOE_TOPK = 2
MOE_D_EXPERT = D_MODEL // 2
MOE_BLOCK = 256
DN_ALPHA = (2.0 * DEPTH) ** 0.25
LN_EPS = 1e-5
RMS_EPS = 1e-6

LANES = 128
SUBLANES = 8
VMEM_LIMIT = 48 * 1024 * 1024
GLA_CHUNK = 64
GLA_TBLOCK = 512
ROW_TILE = 256
NEG_BIG = -1e30

BF16 = jnp.bfloat16
F32 = jnp.float32


def _cparams(*sem):
    return pltpu.CompilerParams(dimension_semantics=sem, vmem_limit_bytes=VMEM_LIMIT)


def _mm_kernel(x_ref, w_ref, o_ref):
    o_ref[...] = jnp.dot(x_ref[...].astype(BF16), w_ref[...], preferred_element_type=F32)


def _matmul(x, w, tm=512, tn=1024):
    n, k = x.shape
    nn = w.shape[1]
    return pl.pallas_call(
        _mm_kernel,
        grid=(nn // tn, n // tm),
        in_specs=[pl.BlockSpec((tm, k), lambda j, i: (i, 0)),
                  pl.BlockSpec((k, tn), lambda j, i: (0, j))],
        out_specs=pl.BlockSpec((tm, tn), lambda j, i: (i, j)),
        out_shape=jax.ShapeDtypeStruct((n, nn), F32),
        compiler_params=_cparams("arbitrary", "arbitrary"),
        name="in_proj",
    )(x, w)


def _layer_norm_rows(y, g, b):
    mu = jnp.mean(y, axis=-1, keepdims=True)
    yc = y - mu
    var = jnp.mean(yc * yc, axis=-1, keepdims=True)
    return yc * lax.rsqrt(var + LN_EPS) * g + b


def _chunk_cumsum(lf, row_in_chunk, reverse):
    rows = lf.shape[0]
    p = lf
    s = 1
    while s < GLA_CHUNK:
        if reverse:
            sh = pltpu.roll(p, rows - s, 0)
            p = p + jnp.where(row_in_chunk + s < GLA_CHUNK, sh, 0.0)
        else:
            sh = pltpu.roll(p, s, 0)
            p = p + jnp.where(row_in_chunk >= s, sh, 0.0)
        s *= 2
    return p


def _gla_chunk(q, kk, v, g, st, reverse):
    c = GLA_CHUNK
    nt = c // SUBLANES
    ii = lax.broadcasted_iota(jnp.int32, (c, c), 0)
    jj = lax.broadcasted_iota(jnp.int32, (c, c), 1)
    nt_dims = (((1,), (1,)), ((), ()))

    o = lax.dot_general((q * jnp.exp(g)).astype(BF16), st.astype(BF16), nt_dims,
                        preferred_element_type=F32)

    p_off = jnp.zeros((c, c), F32)
    b = 2 * SUBLANES
    while b <= c:
        h = b // 2
        pieces = []
        for m in range(c // b):
            r = m * b + h if reverse else m * b + h - 1
            pieces.append(jnp.broadcast_to(g[r:r + 1, :], (b, LANES)))
        gref = pieces[0] if len(pieces) == 1 else jnp.concatenate(pieces, axis=0)
        qfac = jnp.exp(jnp.minimum(g - gref, 0.0))
        kfac = jnp.exp(jnp.minimum(gref - g, 0.0))
        pl_ = lax.dot_general((q * qfac).astype(BF16), (kk * kfac).astype(BF16), nt_dims,
                              preferred_element_type=F32)
        same = (ii // b) == (jj // b)
        if reverse:
            mask = same & ((ii % b) < h) & ((jj % b) >= h)
        else:
            mask = same & ((ii % b) >= h) & ((jj % b) < h)
        p_off = p_off + jnp.where(mask, pl_, 0.0)
        b *= 2
    o = o + jnp.dot(p_off.astype(BF16), v.astype(BF16), preferred_element_type=F32)

    q3 = q.reshape(nt, SUBLANES, LANES)
    k3 = kk.reshape(nt, SUBLANES, LANES)
    v3 = v.reshape(nt, SUBLANES, LANES)
    g3 = g.reshape(nt, SUBLANES, LANES)
    i3 = lax.broadcasted_iota(jnp.int32, (nt, SUBLANES, LANES), 1)
    o3 = jnp.zeros((nt, SUBLANES, LANES), F32)
    for r in range(SUBLANES):
        gr = g3[:, r:r + 1, :]
        valid = (i3 <= r) if reverse else (i3 >= r)
        dec = jnp.where(valid, jnp.exp(jnp.minimum(g3 - gr, 0.0)), 0.0)
        a = q3 * k3[:, r:r + 1, :] * dec
        s = jnp.sum(a, axis=-1, keepdims=True)
        o3 = o3 + s * v3[:, r:r + 1, :]
    o = o + o3.reshape(c, LANES)

    gl = g[0:1, :] if reverse else g[c - 1:c, :]
    kd = (kk * jnp.exp(gl - g)).astype(BF16)
    upd = lax.dot_general(v.astype(BF16), kd, (((0,), (0,)), ((), ())),
                          preferred_element_type=F32)
    st = st * jnp.exp(gl) + upd
    return o, st


def _gla_kernel(layer, lbl_ref, qf_ref, ff_ref, vf_ref, qb_ref, fb_ref, vb_ref,
                of_ref, ob_ref, st_ref, q_s, k_s, v_s, g_s):
    tb = GLA_TBLOCK
    nc = tb // GLA_CHUNK

    @pl.when(pl.program_id(2) == 0)
    def _():
        st_ref[...] = jnp.zeros_like(st_ref)

    if layer > 0:
        lg = lbl_ref[...]
        e = jnp.exp(lg - jnp.max(lg, axis=0, keepdims=True))
        sm = e / jnp.sum(e, axis=0, keepdims=True)
        lb = sm[1]
        for l in range(2, layer + 1):
            lb = lb + sm[l]

    row_in_chunk = lax.broadcasted_iota(jnp.int32, (tb, LANES), 0) % GLA_CHUNK

    for d, (q_ref, f_ref, v_ref) in enumerate(((qf_ref, ff_ref, vf_ref),
                                               (qb_ref, fb_ref, vb_ref))):
        qx = q_ref[...]
        fx = f_ref[...]
        sp = jnp.log1p(jnp.exp(-jnp.abs(fx)))
        logsig = jnp.minimum(fx, 0.0) - sp
        sig_neg = jnp.exp(jnp.minimum(-fx, 0.0) - sp)
        if layer > 0:
            lbd = lb[d:d + 1, :]
            a = jnp.log(lbd)
            bb = jnp.log1p(-lbd) + logsig
            lf = jnp.maximum(a, bb) + jnp.log1p(jnp.exp(-jnp.abs(a - bb)))
            kk = (1.0 - lbd) * sig_neg
        else:
            lf = logsig
            kk = sig_neg
        q_s[d] = qx * jax.nn.sigmoid(qx)
        k_s[d] = kk
        v_s[d] = v_ref[...]
        g_s[d] = _chunk_cumsum(lf, row_in_chunk, reverse=(d == 1))

    def body(ci, carry):
        sf = pl.multiple_of(ci * GLA_CHUNK, GLA_CHUNK)
        sb = pl.multiple_of((nc - 1 - ci) * GLA_CHUNK, GLA_CHUNK)
        for d, start, o_ref in ((0, sf, of_ref), (1, sb, ob_ref)):
            sl = pl.ds(start, GLA_CHUNK)
            o, st = _gla_chunk(q_s[d, sl, :], k_s[d, sl, :], v_s[d, sl, :], g_s[d, sl, :],
                               st_ref[d], reverse=(d == 1))
            o_ref[sl, :] = o
            st_ref[d] = st
        return carry

    lax.fori_loop(0, nc, body, 0)


def _gla(proj, lb_logits, layer, batch, seq):
    n = proj.shape[0]
    tb = GLA_TBLOCK
    nb = seq // tb
    h8 = HG_HEADS

    def spec(stream, rev):
        if rev:
            return pl.BlockSpec((tb, LANES), lambda b, h, c: (b * nb + nb - 1 - c, stream * h8 + h))
        return pl.BlockSpec((tb, LANES), lambda b, h, c: (b * nb + c, stream * h8 + h))

    o_f_spec = pl.BlockSpec((tb, LANES), lambda b, h, c: (b * nb + c, h))
    o_b_spec = pl.BlockSpec((tb, LANES), lambda b, h, c: (b * nb + nb - 1 - c, h))
    return pl.pallas_call(
        functools.partial(_gla_kernel, layer),
        grid=(batch, h8, nb),
        in_specs=[pl.BlockSpec((DEPTH, 2, LANES), lambda b, h, c: (0, 0, h)),
                  spec(0, False), spec(1, False), spec(3, False),
                  spec(0, True), spec(2, True), spec(3, True)],
        out_specs=[o_f_spec, o_b_spec],
        out_shape=[jax.ShapeDtypeStruct((n, D_MODEL), F32)] * 2,
        scratch_shapes=[pltpu.VMEM((2, HG_DK, HG_DK), F32)]
        + [pltpu.VMEM((2, tb, LANES), F32)] * 4,
        compiler_params=_cparams("arbitrary", "arbitrary", "arbitrary"),
        name="gla",
    )(lb_logits, proj, proj, proj, proj, proj, proj)


def _hgrn_out_kernel(of_ref, ob_ref, gate_ref, nw_ref, w_ref, x_ref, g_ref, b_ref, o_ref):
    o = of_ref[...] + ob_ref[...]
    parts = []
    for h in range(HG_HEADS):
        oh = o[:, h * LANES:(h + 1) * LANES]
        ms = jnp.mean(oh * oh, axis=-1, keepdims=True)
        parts.append(oh * lax.rsqrt(ms + RMS_EPS))
    gate = gate_ref[...]
    y = jnp.concatenate(parts, axis=-1) * nw_ref[...] * (gate * jax.nn.sigmoid(gate))
    mix = jnp.dot(y.astype(BF16), w_ref[...], preferred_element_type=F32)
    o_ref[...] = _layer_norm_rows(DN_ALPHA * x_ref[...] + mix, g_ref[...], b_ref[...])


def _hgrn_out(o_f, o_b, proj, norm_w, w_out, x, g, b):
    n = x.shape[0]
    tm = ROW_TILE
    row = pl.BlockSpec((tm, D_MODEL), lambda i: (i, 0))
    vec = pl.BlockSpec((1, D_MODEL), lambda i: (0, 0))
    return pl.pallas_call(
        _hgrn_out_kernel,
        grid=(n // tm,),
        in_specs=[row, row, pl.BlockSpec((tm, D_MODEL), lambda i: (i, HG_STREAMS - 1)), vec,
                  pl.BlockSpec((D_MODEL, D_MODEL), lambda i: (0, 0)), row, vec, vec],
        out_specs=row,
        out_shape=jax.ShapeDtypeStruct((n, D_MODEL), F32),
        compiler_params=_cparams("arbitrary"),
        name="hgrn_out",
    )(o_f, o_b, proj, jnp.tile(norm_w, HG_HEADS).reshape(1, D_MODEL), w_out, x,
      g.reshape(1, D_MODEL), b.reshape(1, D_MODEL))


def _conv_out_kernel(tiles_per_seq, bg_ref, cg_ref, h_ref, cgp_ref, hp_ref, cgn_ref, hn_ref,
                     cw_ref, w_ref, x_ref, g_ref, b_ref, o_ref):
    i = pl.program_id(0)
    tm = cg_ref.shape[0]
    u = cg_ref[...] * h_ref[...]
    first = (i % tiles_per_seq) == 0
    last = (i % tiles_per_seq) == tiles_per_seq - 1
    u_prev_row = jnp.where(first, 0.0, cgp_ref[SUBLANES - 1:SUBLANES, :] * hp_ref[SUBLANES - 1:SUBLANES, :])
    u_next_row = jnp.where(last, 0.0, cgn_ref[0:1, :] * hn_ref[0:1, :])
    rows = lax.broadcasted_iota(jnp.int32, u.shape, 0)
    u_prev = jnp.where(rows == 0, u_prev_row, pltpu.roll(u, 1, 0))
    u_next = jnp.where(rows == tm - 1, u_next_row, pltpu.roll(u, tm - 1, 0))
    cw = cw_ref[...]
    y = u_prev * cw[0:1, :] + u * cw[1:2, :] + u_next * cw[2:3, :]
    mix = jnp.dot((bg_ref[...] * y).astype(BF16), w_ref[...], preferred_element_type=F32)
    o_ref[...] = _layer_norm_rows(DN_ALPHA * x_ref[...] + mix, g_ref[...], b_ref[...])


def _conv_out(proj, conv_w, w_out, x, g, b, seq):
    n = x.shape[0]
    tm = ROW_TILE
    r8 = tm // SUBLANES
    nblk8 = n // SUBLANES
    row = pl.BlockSpec((tm, D_MODEL), lambda i: (i, 0))
    vec = pl.BlockSpec((1, D_MODEL), lambda i: (0, 0))

    def main(stream):
        return pl.BlockSpec((tm, D_MODEL), lambda i: (i, stream))

    def prev(stream):
        return pl.BlockSpec((SUBLANES, D_MODEL), lambda i: (jnp.maximum(i * r8 - 1, 0), stream))

    def nxt(stream):
        return pl.BlockSpec((SUBLANES, D_MODEL),
                            lambda i: (jnp.minimum((i + 1) * r8, nblk8 - 1), stream))

    return pl.pallas_call(
        functools.partial(_conv_out_kernel, seq // tm),
        grid=(n // tm,),
        in_specs=[main(0), main(1), main(2), prev(1), prev(2), nxt(1), nxt(2),
                  pl.BlockSpec((3, D_MODEL), lambda i: (0, 0)),
                  pl.BlockSpec((D_MODEL, D_MODEL), lambda i: (0, 0)), row, vec, vec],
        out_specs=row,
        out_shape=jax.ShapeDtypeStruct((n, D_MODEL), F32),
        compiler_params=_cparams("arbitrary"),
        name="conv_out",
    )(proj, proj, proj, proj, proj, proj, proj, conv_w, w_out, x,
      g.reshape(1, D_MODEL), b.reshape(1, D_MODEL))


def _router_kernel(x_ref, w_ref, b_ref, tri_ref, o_ref, cnt_ref, carry_ref):
    @pl.when(pl.program_id(0) == 0)
    def _():
        carry_ref[...] = jnp.zeros_like(carry_ref)

    logits = jnp.dot(x_ref[...], w_ref[...], precision=lax.Precision.HIGHEST,
                     preferred_element_type=F32) + b_ref[...]
    lane = lax.broadcasted_iota(jnp.int32, logits.shape, 1)
    lane_f = lane.astype(F32)
    gl = jnp.where(lane < MOE_GROUPS, logits, NEG_BIG)
    gmax = jnp.max(gl, axis=-1, keepdims=True)
    gsum = jnp.sum(jnp.exp(gl - gmax), axis=-1, keepdims=True)
    p_group = 1.0 / gsum
    g_sel = jnp.min(jnp.where(gl == gmax, lane_f, 1e9), axis=-1, keepdims=True)
    e_lane = lane - MOE_GROUPS
    in_grp = ((lane >= MOE_GROUPS) & (lane < MOE_GROUPS + MOE_EXPERTS)
              & ((e_lane // MOE_EXPERTS_PER_GROUP).astype(F32) == g_sel))
    el = jnp.where(in_grp, logits, NEG_BIG)
    t1 = jnp.max(el, axis=-1, keepdims=True)
    i1 = jnp.min(jnp.where(el == t1, lane_f, 1e9), axis=-1, keepdims=True)
    el2 = jnp.where(lane_f == i1, NEG_BIG, el)
    t2 = jnp.max(el2, axis=-1, keepdims=True)
    i2 = jnp.min(jnp.where(el2 == t2, lane_f, 1e9), axis=-1, keepdims=True)
    z = jnp.exp(t2 - t1)
    g1 = p_group / (1.0 + z)
    g2 = g1 * z
    out = jnp.where(lane == 0, g1, 0.0)
    out = jnp.where(lane == 1, g2, out)
    out = jnp.where(lane == 2, i1 - MOE_GROUPS, out)
    out = jnp.where(lane == 3, i2 - MOE_GROUPS, out)

    sel1 = lane_f == i1
    sel2 = lane_f == i2
    onehot = jnp.where(sel1 | sel2, 1.0, 0.0)
    prefix = jnp.dot(tri_ref[...], onehot.astype(BF16), preferred_element_type=F32)
    before = prefix + carry_ref[0:1, :]
    r1 = jnp.sum(jnp.where(sel1, before, 0.0), axis=-1, keepdims=True)
    r2 = jnp.sum(jnp.where(sel2, before, 0.0), axis=-1, keepdims=True)
    out = jnp.where(lane == 4, r1, out)
    out = jnp.where(lane == 5, r2, out)
    o_ref[...] = out
    carry_ref[...] = carry_ref[...] + jnp.sum(onehot, axis=0, keepdims=True)
    cnt_ref[...] = carry_ref[...]


def _router(x, w_group, b_group, w_expert, b_expert):
    n = x.shape[0]
    tm = 512
    pad = LANES - MOE_GROUPS - MOE_EXPERTS
    w = jnp.concatenate([w_group, w_expert, jnp.zeros((D_MODEL, pad), F32)], axis=1)
    b = jnp.concatenate([b_group, b_expert, jnp.zeros((pad,), F32)]).reshape(1, LANES)
    r = jnp.arange(tm, dtype=jnp.int32)
    tri = (r[None, :] < r[:, None]).astype(BF16)
    route, cnt = pl.pallas_call(
        _router_kernel,
        grid=(n // tm,),
        in_specs=[pl.BlockSpec((tm, D_MODEL), lambda i: (i, 0)),
                  pl.BlockSpec((D_MODEL, LANES), lambda i: (0, 0)),
                  pl.BlockSpec((1, LANES), lambda i: (0, 0)),
                  pl.BlockSpec((tm, tm), lambda i: (0, 0))],
        out_specs=[pl.BlockSpec((tm, LANES), lambda i: (i, 0)),
                   pl.BlockSpec((SUBLANES, LANES), lambda i: (0, 0))],
        out_shape=[jax.ShapeDtypeStruct((n, LANES), F32),
                   jax.ShapeDtypeStruct((SUBLANES, LANES), F32)],
        scratch_shapes=[pltpu.VMEM((SUBLANES, LANES), F32)],
        compiler_params=_cparams("arbitrary"),
        name="router",
    )(x, w, b, tri)
    return route, cnt[0, MOE_GROUPS:MOE_GROUPS + MOE_EXPERTS].astype(jnp.int32)


DISPATCH_TOKENS = 256
DISPATCH_BUFS = 3


def _moe_plan(route, counts, n):
    nk = n * MOE_TOPK
    n_rows = -(-nk // MOE_BLOCK) * MOE_BLOCK + MOE_EXPERTS * MOE_BLOCK
    n_blocks = n_rows // MOE_BLOCK
    padded = (counts + MOE_BLOCK - 1) // MOE_BLOCK * MOE_BLOCK
    pad_end = jnp.cumsum(padded)
    pad_start = pad_end - padded
    e = route[:, MOE_TOPK:2 * MOE_TOPK].astype(jnp.int32)
    rank = route[:, 2 * MOE_TOPK:3 * MOE_TOPK].astype(jnp.int32)
    ids = jnp.arange(MOE_EXPERTS, dtype=jnp.int32)
    base = jnp.sum(jnp.where(e[:, :, None] == ids, pad_start, 0), axis=-1)
    dest = (base + rank).reshape(nk)
    block_start = jnp.arange(n_blocks, dtype=jnp.int32) * MOE_BLOCK
    block_expert = jnp.minimum(jnp.sum(block_start[:, None] >= pad_end[None, :], axis=1),
                               MOE_EXPERTS - 1).astype(jnp.int32)
    nact = (pad_end[-1:] // MOE_BLOCK).astype(jnp.int32)
    fill_start = (pad_start + counts).astype(jnp.int32)
    return dest, block_expert, nact, fill_start, pad_end.astype(jnp.int32), n_rows


def _dispatch_kernel(dest_ref, fs_ref, fe_ref, nact_ref, x_hbm, xs_hbm, xbuf, zbuf, lsem, sem,
                     zsem):
    i = pl.program_id(0)
    nsteps = pl.num_programs(0)
    slot = i % DISPATCH_BUFS
    ct = DISPATCH_TOKENS
    n_blocks = xs_hbm.shape[0] // MOE_BLOCK

    def load(step, sl):
        start = pl.multiple_of(step * ct, ct)
        return pltpu.make_async_copy(x_hbm.at[pl.ds(start, ct)], xbuf.at[sl], lsem.at[sl])

    def wait_step(sl):
        for _ in range(MOE_TOPK):
            pltpu.make_async_copy(xbuf.at[sl], xs_hbm.at[pl.ds(0, ct)], sem.at[sl]).wait()

    @pl.when(i == 0)
    def _():
        for s in range(DISPATCH_BUFS - 1):
            load(s, s).start()

    load(i, slot).wait()

    def body(t, c):
        for k in range(MOE_TOPK):
            d = dest_ref[(i * ct + t) * MOE_TOPK + k]
            pltpu.make_async_copy(xbuf.at[slot, pl.ds(t, 1)], xs_hbm.at[pl.ds(d, 1)],
                                  sem.at[slot]).start()
        return c
    lax.fori_loop(0, ct, body, 0, unroll=4)

    @pl.when(i >= 1)
    def _():
        wait_step((i - 1) % DISPATCH_BUFS)

    @pl.when(i + DISPATCH_BUFS - 1 < nsteps)
    def _():
        load(i + DISPATCH_BUFS - 1, (i + DISPATCH_BUFS - 1) % DISPATCH_BUFS).start()

    @pl.when(i == nsteps - 1)
    def _():
        wait_step(slot)
        zbuf[...] = jnp.zeros_like(zbuf)

        def fill_rows(wait):
            def per_expert(e, c):
                def per_row(r, c2):
                    cp = pltpu.make_async_copy(zbuf.at[pl.ds(0, 1)], xs_hbm.at[pl.ds(r, 1)],
                                               zsem.at[0])
                    cp.wait() if wait else cp.start()
                    return c2
                lax.fori_loop(fs_ref[e], fe_ref[e], per_row, 0)
                return c
            lax.fori_loop(0, MOE_EXPERTS, per_expert, 0)

        def fill_blocks(wait):
            def per_block(b, c):
                start = pl.multiple_of(b * MOE_BLOCK, MOE_BLOCK)
                cp = pltpu.make_async_copy(zbuf, xs_hbm.at[pl.ds(start, MOE_BLOCK)], zsem.at[1])
                cp.wait() if wait else cp.start()
                return c
            lax.fori_loop(nact_ref[0], n_blocks, per_block, 0)

        fill_rows(False)
        fill_blocks(False)
        fill_rows(True)
        fill_blocks(True)


def _dispatch(x, dest, fill_start, fill_end, nact, n_rows):
    n = x.shape[0]
    grid_spec = pltpu.PrefetchScalarGridSpec(
        num_scalar_prefetch=4,
        grid=(n // DISPATCH_TOKENS,),
        in_specs=[pl.BlockSpec(memory_space=pl.ANY)],
        out_specs=pl.BlockSpec(memory_space=pl.ANY),
        scratch_shapes=[pltpu.VMEM((DISPATCH_BUFS, DISPATCH_TOKENS, D_MODEL), F32),
                        pltpu.VMEM((MOE_BLOCK, D_MODEL), F32),
                        pltpu.SemaphoreType.DMA((DISPATCH_BUFS,)),
                        pltpu.SemaphoreType.DMA((DISPATCH_BUFS,)),
                        pltpu.SemaphoreType.DMA((2,))],
    )
    return pl.pallas_call(
        _dispatch_kernel,
        grid_spec=grid_spec,
        out_shape=jax.ShapeDtypeStruct((n_rows, D_MODEL), F32),
        compiler_params=_cparams("arbitrary"),
        name="dispatch",
    )(dest, fill_start, fill_end, nact, x)


def _moe_ffn_kernel(be_ref, nact_ref, xs_ref, wup_ref, wdn_ref, ys_ref, wup_bf, wdn_bf):
    i = pl.program_id(0)
    active = i < nact_ref[0]
    new_expert = (i == 0) | (be_ref[i] != be_ref[jnp.maximum(i - 1, 0)])

    @pl.when(active & new_expert)
    def _():
        wup_bf[...] = wup_ref[0].astype(BF16)
        wdn_bf[...] = wdn_ref[0].astype(BF16)

    @pl.when(active)
    def _():
        hcat = jnp.dot(xs_ref[...].astype(BF16), wup_bf[...], preferred_element_type=F32)
        hg = hcat[:, :MOE_D_EXPERT]
        hu = hcat[:, MOE_D_EXPERT:]
        act = (hg * jax.nn.sigmoid(hg) * hu).astype(BF16)
        ys_ref[...] = jnp.dot(act, wdn_bf[...], preferred_element_type=F32)

    @pl.when(jnp.logical_not(active))
    def _():
        ys_ref[...] = jnp.zeros_like(ys_ref)


def _moe_ffn(xs, block_expert, nact, w_up, w_down, layer):
    n_rows = xs.shape[0]
    e0 = layer * MOE_EXPERTS
    grid_spec = pltpu.PrefetchScalarGridSpec(
        num_scalar_prefetch=2,
        grid=(n_rows // MOE_BLOCK,),
        in_specs=[pl.BlockSpec((MOE_BLOCK, D_MODEL), lambda i, be, na: (i, 0)),
                  pl.BlockSpec((1, D_MODEL, 2 * MOE_D_EXPERT),
                               lambda i, be, na: (e0 + be[i], 0, 0)),
                  pl.BlockSpec((1, MOE_D_EXPERT, D_MODEL), lambda i, be, na: (e0 + be[i], 0, 0))],
        out_specs=pl.BlockSpec((MOE_BLOCK, D_MODEL), lambda i, be, na: (i, 0)),
        scratch_shapes=[pltpu.VMEM((D_MODEL, 2 * MOE_D_EXPERT), BF16),
                        pltpu.VMEM((MOE_D_EXPERT, D_MODEL), BF16)],
    )
    return pl.pallas_call(
        _moe_ffn_kernel,
        grid_spec=grid_spec,
        out_shape=jax.ShapeDtypeStruct((n_rows, D_MODEL), F32),
        compiler_params=_cparams("arbitrary"),
        name="moe_ffn",
    )(block_expert, nact, xs, w_up, w_down)


def _combine_ln_kernel(dest_ref, x_ref, route_ref, ys_hbm, g_ref, b_ref, o_ref, ybuf, sem):
    i = pl.program_id(0)
    nsteps = pl.num_programs(0)
    slot = i % 2
    tm = x_ref.shape[0]

    def start_gather(step, sl):
        def body(t, c):
            for k in range(MOE_TOPK):
                d = dest_ref[(step * tm + t) * MOE_TOPK + k]
                pltpu.make_async_copy(ys_hbm.at[pl.ds(d, 1)], ybuf.at[sl, k, pl.ds(t, 1)],
                                      sem.at[sl]).start()
            return c
        lax.fori_loop(0, tm, body, 0, unroll=4)

    @pl.when(i == 0)
    def _():
        start_gather(0, 0)

    @pl.when(i + 1 < nsteps)
    def _():
        start_gather(i + 1, 1 - slot)

    for k in range(MOE_TOPK):
        pltpu.make_async_copy(ys_hbm.at[pl.ds(0, tm)], ybuf.at[slot, k], sem.at[slot]).wait()
    route = route_ref[...]
    ffn = route[:, 0:1] * ybuf[slot, 0] + route[:, 1:2] * ybuf[slot, 1]
    o_ref[...] = _layer_norm_rows(DN_ALPHA * x_ref[...] + ffn, g_ref[...], b_ref[...])


def _combine_ln(x, route, ys, dest, g, b):
    n = x.shape[0]
    tm = ROW_TILE
    vec = pl.BlockSpec((1, D_MODEL), lambda i, d: (0, 0))
    grid_spec = pltpu.PrefetchScalarGridSpec(
        num_scalar_prefetch=1,
        grid=(n // tm,),
        in_specs=[pl.BlockSpec((tm, D_MODEL), lambda i, d: (i, 0)),
                  pl.BlockSpec((tm, LANES), lambda i, d: (i, 0)),
                  pl.BlockSpec(memory_space=pl.ANY), vec, vec],
        out_specs=pl.BlockSpec((tm, D_MODEL), lambda i, d: (i, 0)),
        scratch_shapes=[pltpu.VMEM((2, MOE_TOPK, tm, D_MODEL), F32),
                        pltpu.SemaphoreType.DMA((2,))],
    )
    return pl.pallas_call(
        _combine_ln_kernel,
        grid_spec=grid_spec,
        out_shape=jax.ShapeDtypeStruct((n, D_MODEL), F32),
        compiler_params=_cparams("arbitrary"),
        name="combine_ln",
    )(dest, x, route, ys, g.reshape(1, D_MODEL), b.reshape(1, D_MODEL))


def kernel(x, hg_w_in, hg_lb_logits, hg_norm_w, hg_w_out, cv_w_in, cv_w, cv_w_out, ln_g, ln_b,
           moe_w_group, moe_b_group, moe_w_expert, moe_b_expert, moe_w_up, moe_w_down):
    batch, seq, d = x.shape
    n = batch * seq
    xf = x.reshape(n, d)
    w_up_all = moe_w_up.reshape(DEPTH * MOE_EXPERTS, D_MODEL, 2 * MOE_D_EXPERT)
    w_down_all = moe_w_down.reshape(DEPTH * MOE_EXPERTS, MOE_D_EXPERT, D_MODEL)
    for layer in range(DEPTH):
        j = layer // 2
        if layer % 2 == 0:
            proj = _matmul(xf, hg_w_in[j].astype(BF16))
            o_f, o_b = _gla(proj, hg_lb_logits, layer, batch, seq)
            xf = _hgrn_out(o_f, o_b, proj, hg_norm_w[j], hg_w_out[j].astype(BF16), xf,
                           ln_g[layer, 0], ln_b[layer, 0])
        else:
            proj = _matmul(xf, cv_w_in[j].astype(BF16))
            xf = _conv_out(proj, cv_w[j], cv_w_out[j].astype(BF16), xf,
                           ln_g[layer, 0], ln_b[layer, 0], seq)
        route, counts = _router(xf, moe_w_group[layer], moe_b_group[layer],
                                moe_w_expert[layer], moe_b_expert[layer])
        dest, block_expert, nact, fill_start, fill_end, n_rows = _moe_plan(route, counts, n)
        xs = _dispatch(xf, dest, fill_start, fill_end, nact, n_rows)
        ys = _moe_ffn(xs, block_expert, nact, w_up_all, w_down_all, layer)
        xf = _combine_ln(xf, route, ys, dest, ln_g[layer, 1], ln_b[layer, 1])
    return xf.reshape(batch, seq, d)
```

```python
import functools

import jax
import jax.numpy as jnp
from jax import lax
from jax.experimental import pallas as pl
from jax.experimental.pallas import tpu as pltpu

D_MODEL = 1024
DEPTH = 4
HG_DK = 128
HG_HEADS = D_MODEL // HG_DK
HG_STREAMS = 5
CONV_STREAMS = 3
MOE_GROUPS = 4
MOE_EXPERTS_PER_GROUP = 8
MOE_EXPERTS = MOE_GROUPS * MOE_EXPERTS_PER_GROUP
MOE_TOPK = 2
MOE_D_EXPERT = D_MODEL // 2
MOE_BLOCK = 256
DN_ALPHA = (2.0 * DEPTH) ** 0.25
LN_EPS = 1e-5
RMS_EPS = 1e-6

LANES = 128
SUBLANES = 8
VMEM_LIMIT = 48 * 1024 * 1024
GLA_CHUNK = 64
GLA_TBLOCK = 512
ROW_TILE = 256
NEG_BIG = -1e30

BF16 = jnp.bfloat16
F32 = jnp.float32


def _cparams(*sem):
    return pltpu.CompilerParams(dimension_semantics=sem, vmem_limit_bytes=VMEM_LIMIT)


def _mm_kernel(x_ref, w_ref, o_ref):
    o_ref[...] = jnp.dot(x_ref[...].astype(BF16), w_ref[...], preferred_element_type=F32)


def _matmul(x, w, tm=512, tn=1024):
    n, k = x.shape
    nn = w.shape[1]
    return pl.pallas_call(
        _mm_kernel,
        grid=(nn // tn, n // tm),
        in_specs=[pl.BlockSpec((tm, k), lambda j, i: (i, 0)),
                  pl.BlockSpec((k, tn), lambda j, i: (0, j))],
        out_specs=pl.BlockSpec((tm, tn), lambda j, i: (i, j)),
        out_shape=jax.ShapeDtypeStruct((n, nn), F32),
        compiler_params=_cparams("arbitrary", "arbitrary"),
        name="in_proj",
    )(x, w)


def _layer_norm_rows(y, g, b):
    mu = jnp.mean(y, axis=-1, keepdims=True)
    yc = y - mu
    var = jnp.mean(yc * yc, axis=-1, keepdims=True)
    return yc * lax.rsqrt(var + LN_EPS) * g + b


def _rows(x, blocks, size):
    parts = [x[b * size:(b + 1) * size, :] for b in blocks]
    return parts[0] if len(parts) == 1 else jnp.concatenate(parts, axis=0)


def _gla_chunk(q_s, k_s, lf_s, g_s, acc_s, v_ref, o_ref, st_ref, d, start, reverse):
    c = GLA_CHUNK
    t8 = SUBLANES
    nt_dims = (((1,), (1,)), ((), ()))

    def tt(ref, i, lead=True):
        idx = pl.ds(start + i, t8, stride=t8)
        return ref[d, idx, :] if lead else ref[idx, :]

    lf = [tt(lf_s, i) for i in range(t8)]
    qt = [tt(q_s, i) for i in range(t8)]
    kt = [tt(k_s, i) for i in range(t8)]
    vt = [tt(v_ref, i, lead=False) for i in range(t8)]

    gi = [None] * t8
    prev = None
    for i in (range(t8 - 1, -1, -1) if reverse else range(t8)):
        gi[i] = lf[i] if prev is None else prev + lf[i]
        prev = gi[i]
    tot = prev
    sub = lax.broadcasted_iota(jnp.int32, (t8, LANES), 0)
    incl = tot
    s = 1
    while s < t8:
        if reverse:
            incl = incl + jnp.where(sub + s < t8, pltpu.roll(incl, t8 - s, 0), 0.0)
        else:
            incl = incl + jnp.where(sub >= s, pltpu.roll(incl, s, 0), 0.0)
        s *= 2
    excl = incl - tot
    for i in range(t8):
        g_s[d, pl.ds(start + i, t8, stride=t8), :] = gi[i] + excl

    acc = []
    for i in range(t8):
        a_i = jnp.sum(qt[i] * kt[i], axis=-1, keepdims=True) * vt[i]
        for r in (range(i + 1, t8) if reverse else range(i)):
            w = qt[i] * kt[r] * jnp.exp(gi[i] - gi[r])
            a_i = a_i + jnp.sum(w, axis=-1, keepdims=True) * vt[r]
        acc.append(a_i)

    sl = pl.ds(start, c)
    q = q_s[d, sl, :]
    kk = k_s[d, sl, :]
    v = v_ref[sl, :]
    g = g_s[d, sl, :]
    vb = v.astype(BF16)
    st = st_ref[d]

    o = lax.dot_general((q * jnp.exp(g)).astype(BF16), st.astype(BF16), nt_dims,
                        preferred_element_type=F32)

    half = c // 2
    qd_l, kd_l, v_l, q_tiles = [], [], [], []
    b = 2 * t8
    while b <= c:
        h = b // 2
        nblk = c // b
        lo = [2 * m for m in range(nblk)]
        hi = [2 * m + 1 for m in range(nblk)]
        q_half, k_half = (lo, hi) if reverse else (hi, lo)
        refs = [g[m * b + h:m * b + h + 1, :] if reverse else g[m * b + h - 1:m * b + h, :]
                for m in range(nblk)]
        gref = jnp.concatenate([jnp.broadcast_to(r, (h, LANES)) for r in refs], axis=0) \
            if nblk > 1 else jnp.broadcast_to(refs[0], (h, LANES))
        qd_l.append(_rows(q, q_half, h) * jnp.exp(_rows(g, q_half, h) - gref))
        kd_l.append(_rows(kk, k_half, h) * jnp.exp(gref - _rows(g, k_half, h)))
        v_l.append(_rows(v, k_half, h))
        q_tiles.append([hb * (h // t8) + t for hb in q_half for t in range(h // t8)])
        b *= 2
    nlev = len(qd_l)
    p = lax.dot_general(jnp.concatenate(qd_l, axis=0).astype(BF16),
                        jnp.concatenate(kd_l, axis=0).astype(BF16), nt_dims,
                        preferred_element_type=F32)
    ii = lax.broadcasted_iota(jnp.int32, p.shape, 0)
    jj = lax.broadcasted_iota(jnp.int32, p.shape, 1)
    keep = None
    for lv in range(nlev):
        h = t8 << lv
        m = (ii // half == lv) & (jj // half == lv) & ((ii // h) == (jj // h))
        keep = m if keep is None else keep | m
    p = jnp.where(keep, p, 0.0)
    res = jnp.dot(p.astype(BF16), jnp.concatenate(v_l, axis=0).astype(BF16),
                  preferred_element_type=F32)
    contrib = [None] * (c // t8)
    for lv in range(nlev):
        for n_, tile in enumerate(q_tiles[lv]):
            piece = res[lv * half + n_ * t8:lv * half + (n_ + 1) * t8, :]
            contrib[tile] = piece if contrib[tile] is None else contrib[tile] + piece
    zero = jnp.zeros((t8, LANES), F32)
    o = o + jnp.concatenate([zero if p_ is None else p_ for p_ in contrib], axis=0)

    gl = g[0:1, :] if reverse else g[c - 1:c, :]
    kd = (kk * jnp.exp(gl - g)).astype(BF16)
    upd = lax.dot_general(vb, kd, (((0,), (0,)), ((), ())), preferred_element_type=F32)
    st_ref[d] = st * jnp.exp(gl) + upd

    for i in range(t8):
        acc_s[d, pl.ds(start + i, t8, stride=t8), :] = acc[i]
    o_ref[sl, :] = o + acc_s[d, sl, :]


def _gla_kernel(layer, lbl_ref, qf_ref, ff_ref, vf_ref, qb_ref, fb_ref, vb_ref,
                of_ref, ob_ref, st_ref, q_s, k_s, lf_s, g_s, acc_s):
    tb = GLA_TBLOCK
    nc = tb // GLA_CHUNK

    @pl.when(pl.program_id(2) == 0)
    def _():
        st_ref[...] = jnp.zeros_like(st_ref)

    if layer > 0:
        lg = lbl_ref[...]
        e = jnp.exp(lg - jnp.max(lg, axis=0, keepdims=True))
        sm = e / jnp.sum(e, axis=0, keepdims=True)
        lb = sm[1]
        for l in range(2, layer + 1):
            lb = lb + sm[l]

    for d, (q_ref, f_ref) in enumerate(((qf_ref, ff_ref), (qb_ref, fb_ref))):
        qx = q_ref[...]
        fx = f_ref[...]
        t = jnp.exp(-jnp.abs(fx))
        r = 1.0 / (1.0 + t)
        logsig = jnp.minimum(fx, 0.0) - jnp.log(1.0 + t)
        sig_neg = jnp.where(fx >= 0.0, t * r, r)
        if layer > 0:
            lbd = lb[d:d + 1, :]
            a = jnp.log(lbd)
            bb = jnp.log1p(-lbd) + logsig
            lf = jnp.maximum(a, bb) + jnp.log(1.0 + jnp.exp(-jnp.abs(a - bb)))
            kk = (1.0 - lbd) * sig_neg
        else:
            lf = logsig
            kk = sig_neg
        q_s[d] = qx * jax.nn.sigmoid(qx)
        k_s[d] = kk
        lf_s[d] = lf

    def body(ci, carry):
        sf = pl.multiple_of(ci * GLA_CHUNK, GLA_CHUNK)
        sb = pl.multiple_of((nc - 1 - ci) * GLA_CHUNK, GLA_CHUNK)
        _gla_chunk(q_s, k_s, lf_s, g_s, acc_s, vf_ref, of_ref, st_ref, 0, sf, False)
        _gla_chunk(q_s, k_s, lf_s, g_s, acc_s, vb_ref, ob_ref, st_ref, 1, sb, True)
        return carry

    lax.fori_loop(0, nc, body, 0, unroll=2)


def _gla(proj, lb_logits, layer, batch, seq):
    n = proj.shape[0]
    tb = GLA_TBLOCK
    nb = seq // tb
    h8 = HG_HEADS

    def spec(stream, rev):
        if rev:
            return pl.BlockSpec((tb, LANES), lambda b, h, c: (b * nb + nb - 1 - c, stream * h8 + h))
        return pl.BlockSpec((tb, LANES), lambda b, h, c: (b * nb + c, stream * h8 + h))

    o_f_spec = pl.BlockSpec((tb, LANES), lambda b, h, c: (b * nb + c, h))
    o_b_spec = pl.BlockSpec((tb, LANES), lambda b, h, c: (b * nb + nb - 1 - c, h))
    return pl.pallas_call(
        functools.partial(_gla_kernel, layer),
        grid=(batch, h8, nb),
        in_specs=[pl.BlockSpec((DEPTH, 2, LANES), lambda b, h, c: (0, 0, h)),
                  spec(0, False), spec(1, False), spec(3, False),
                  spec(0, True), spec(2, True), spec(3, True)],
        out_specs=[o_f_spec, o_b_spec],
        out_shape=[jax.ShapeDtypeStruct((n, D_MODEL), F32)] * 2,
        scratch_shapes=[pltpu.VMEM((2, HG_DK, HG_DK), F32)]
        + [pltpu.VMEM((2, tb, LANES), F32)] * 5,
        compiler_params=_cparams("arbitrary", "arbitrary", "arbitrary"),
        name="gla",
    )(lb_logits, proj, proj, proj, proj, proj, proj)


def _hgrn_out_kernel(of_ref, ob_ref, gate_ref, nw_ref, w_ref, x_ref, g_ref, b_ref, o_ref):
    o = of_ref[...] + ob_ref[...]
    parts = []
    for h in range(HG_HEADS):
        oh = o[:, h * LANES:(h + 1) * LANES]
        ms = jnp.mean(oh * oh, axis=-1, keepdims=True)
        parts.append(oh * lax.rsqrt(ms + RMS_EPS))
    gate = gate_ref[...]
    y = jnp.concatenate(parts, axis=-1) * nw_ref[...] * (gate * jax.nn.sigmoid(gate))
    mix = jnp.dot(y.astype(BF16), w_ref[...], preferred_element_type=F32)
    o_ref[...] = _layer_norm_rows(DN_ALPHA * x_ref[...] + mix, g_ref[...], b_ref[...])


def _hgrn_out(o_f, o_b, proj, norm_w, w_out, x, g, b):
    n = x.shape[0]
    tm = ROW_TILE
    row = pl.BlockSpec((tm, D_MODEL), lambda i: (i, 0))
    vec = pl.BlockSpec((1, D_MODEL), lambda i: (0, 0))
    return pl.pallas_call(
        _hgrn_out_kernel,
        grid=(n // tm,),
        in_specs=[row, row, pl.BlockSpec((tm, D_MODEL), lambda i: (i, HG_STREAMS - 1)), vec,
                  pl.BlockSpec((D_MODEL, D_MODEL), lambda i: (0, 0)), row, vec, vec],
        out_specs=row,
        out_shape=jax.ShapeDtypeStruct((n, D_MODEL), F32),
        compiler_params=_cparams("arbitrary"),
        name="hgrn_out",
    )(o_f, o_b, proj, jnp.tile(norm_w, HG_HEADS).reshape(1, D_MODEL), w_out, x,
      g.reshape(1, D_MODEL), b.reshape(1, D_MODEL))


def _conv_out_kernel(tiles_per_seq, bg_ref, cg_ref, h_ref, cgp_ref, hp_ref, cgn_ref, hn_ref,
                     cw_ref, w_ref, x_ref, g_ref, b_ref, o_ref):
    i = pl.program_id(0)
    tm = cg_ref.shape[0]
    u = cg_ref[...] * h_ref[...]
    first = (i % tiles_per_seq) == 0
    last = (i % tiles_per_seq) == tiles_per_seq - 1
    u_prev_row = jnp.where(first, 0.0, cgp_ref[SUBLANES - 1:SUBLANES, :] * hp_ref[SUBLANES - 1:SUBLANES, :])
    u_next_row = jnp.where(last, 0.0, cgn_ref[0:1, :] * hn_ref[0:1, :])
    rows = lax.broadcasted_iota(jnp.int32, u.shape, 0)
    u_prev = jnp.where(rows == 0, u_prev_row, pltpu.roll(u, 1, 0))
    u_next = jnp.where(rows == tm - 1, u_next_row, pltpu.roll(u, tm - 1, 0))
    cw = cw_ref[...]
    y = u_prev * cw[0:1, :] + u * cw[1:2, :] + u_next * cw[2:3, :]
    mix = jnp.dot((bg_ref[...] * y).astype(BF16), w_ref[...], preferred_element_type=F32)
    o_ref[...] = _layer_norm_rows(DN_ALPHA * x_ref[...] + mix, g_ref[...], b_ref[...])


def _conv_out(proj, conv_w, w_out, x, g, b, seq):
    n = x.shape[0]
    tm = ROW_TILE
    r8 = tm // SUBLANES
    nblk8 = n // SUBLANES
    row = pl.BlockSpec((tm, D_MODEL), lambda i: (i, 0))
    vec = pl.BlockSpec((1, D_MODEL), lambda i: (0, 0))

    def main(stream):
        return pl.BlockSpec((tm, D_MODEL), lambda i: (i, stream))

    def prev(stream):
        return pl.BlockSpec((SUBLANES, D_MODEL), lambda i: (jnp.maximum(i * r8 - 1, 0), stream))

    def nxt(stream):
        return pl.BlockSpec((SUBLANES, D_MODEL),
                            lambda i: (jnp.minimum((i + 1) * r8, nblk8 - 1), stream))

    return pl.pallas_call(
        functools.partial(_conv_out_kernel, seq // tm),
        grid=(n // tm,),
        in_specs=[main(0), main(1), main(2), prev(1), prev(2), nxt(1), nxt(2),
                  pl.BlockSpec((3, D_MODEL), lambda i: (0, 0)),
                  pl.BlockSpec((D_MODEL, D_MODEL), lambda i: (0, 0)), row, vec, vec],
        out_specs=row,
        out_shape=jax.ShapeDtypeStruct((n, D_MODEL), F32),
        compiler_params=_cparams("arbitrary"),
        name="conv_out",
    )(proj, proj, proj, proj, proj, proj, proj, conv_w, w_out, x,
      g.reshape(1, D_MODEL), b.reshape(1, D_MODEL))


def _router_kernel(x_ref, w_ref, b_ref, tri_ref, o_ref, cnt_ref, carry_ref):
    @pl.when(pl.program_id(0) == 0)
    def _():
        carry_ref[...] = jnp.zeros_like(carry_ref)

    logits = jnp.dot(x_ref[...], w_ref[...], precision=lax.Precision.HIGHEST,
                     preferred_element_type=F32) + b_ref[...]
    lane = lax.broadcasted_iota(jnp.int32, logits.shape, 1)
    lane_f = lane.astype(F32)
    gl = jnp.where(lane < MOE_GROUPS, logits, NEG_BIG)
    gmax = jnp.max(gl, axis=-1, keepdims=True)
    gsum = jnp.sum(jnp.exp(gl - gmax), axis=-1, keepdims=True)
    p_group = 1.0 / gsum
    g_sel = jnp.min(jnp.where(gl == gmax, lane_f, 1e9), axis=-1, keepdims=True)
    e_lane = lane - MOE_GROUPS
    in_grp = ((lane >= MOE_GROUPS) & (lane < MOE_GROUPS + MOE_EXPERTS)
              & ((e_lane // MOE_EXPERTS_PER_GROUP).astype(F32) == g_sel))
    el = jnp.where(in_grp, logits, NEG_BIG)
    t1 = jnp.max(el, axis=-1, keepdims=True)
    i1 = jnp.min(jnp.where(el == t1, lane_f, 1e9), axis=-1, keepdims=True)
    el2 = jnp.where(lane_f == i1, NEG_BIG, el)
    t2 = jnp.max(el2, axis=-1, keepdims=True)
    i2 = jnp.min(jnp.where(el2 == t2, lane_f, 1e9), axis=-1, keepdims=True)
    z = jnp.exp(t2 - t1)
    g1 = p_group / (1.0 + z)
    g2 = g1 * z
    out = jnp.where(lane == 0, g1, 0.0)
    out = jnp.where(lane == 1, g2, out)
    out = jnp.where(lane == 2, i1 - MOE_GROUPS, out)
    out = jnp.where(lane == 3, i2 - MOE_GROUPS, out)

    sel1 = lane_f == i1
    sel2 = lane_f == i2
    onehot = jnp.where(sel1 | sel2, 1.0, 0.0)
    prefix = jnp.dot(tri_ref[...], onehot.astype(BF16), preferred_element_type=F32)
    before = prefix + carry_ref[0:1, :]
    r1 = jnp.sum(jnp.where(sel1, before, 0.0), axis=-1, keepdims=True)
    r2 = jnp.sum(jnp.where(sel2, before, 0.0), axis=-1, keepdims=True)
    out = jnp.where(lane == 4, r1, out)
    out = jnp.where(lane == 5, r2, out)
    o_ref[...] = out
    carry_ref[...] = carry_ref[...] + jnp.sum(onehot, axis=0, keepdims=True)
    cnt_ref[...] = carry_ref[...]


def _router(x, w_group, b_group, w_expert, b_expert):
    n = x.shape[0]
    tm = 512
    pad = LANES - MOE_GROUPS - MOE_EXPERTS
    w = jnp.concatenate([w_group, w_expert, jnp.zeros((D_MODEL, pad), F32)], axis=1)
    b = jnp.concatenate([b_group, b_expert, jnp.zeros((pad,), F32)]).reshape(1, LANES)
    r = jnp.arange(tm, dtype=jnp.int32)
    tri = (r[None, :] < r[:, None]).astype(BF16)
    route, cnt = pl.pallas_call(
        _router_kernel,
        grid=(n // tm,),
        in_specs=[pl.BlockSpec((tm, D_MODEL), lambda i: (i, 0)),
                  pl.BlockSpec((D_MODEL, LANES), lambda i: (0, 0)),
                  pl.BlockSpec((1, LANES), lambda i: (0, 0)),
                  pl.BlockSpec((tm, tm), lambda i: (0, 0))],
        out_specs=[pl.BlockSpec((tm, LANES), lambda i: (i, 0)),
                   pl.BlockSpec((SUBLANES, LANES), lambda i: (0, 0))],
        out_shape=[jax.ShapeDtypeStruct((n, LANES), F32),
                   jax.ShapeDtypeStruct((SUBLANES, LANES), F32)],
        scratch_shapes=[pltpu.VMEM((SUBLANES, LANES), F32)],
        compiler_params=_cparams("arbitrary"),
        name="router",
    )(x, w, b, tri)
    return route, cnt[0, MOE_GROUPS:MOE_GROUPS + MOE_EXPERTS].astype(jnp.int32)


DISPATCH_TOKENS = 256
DISPATCH_BUFS = 3


def _moe_plan(route, counts, n):
    nk = n * MOE_TOPK
    n_rows = -(-nk // MOE_BLOCK) * MOE_BLOCK + MOE_EXPERTS * MOE_BLOCK
    n_blocks = n_rows // MOE_BLOCK
    padded = (counts + MOE_BLOCK - 1) // MOE_BLOCK * MOE_BLOCK
    pad_end = jnp.cumsum(padded)
    pad_start = pad_end - padded
    e = route[:, MOE_TOPK:2 * MOE_TOPK].astype(jnp.int32)
    rank = route[:, 2 * MOE_TOPK:3 * MOE_TOPK].astype(jnp.int32)
    ids = jnp.arange(MOE_EXPERTS, dtype=jnp.int32)
    base = jnp.sum(jnp.where(e[:, :, None] == ids, pad_start, 0), axis=-1)
    dest = (base + rank).reshape(nk)
    block_start = jnp.arange(n_blocks, dtype=jnp.int32) * MOE_BLOCK
    block_expert = jnp.minimum(jnp.sum(block_start[:, None] >= pad_end[None, :], axis=1),
                               MOE_EXPERTS - 1).astype(jnp.int32)
    nact = (pad_end[-1:] // MOE_BLOCK).astype(jnp.int32)
    fill_start = (pad_start + counts).astype(jnp.int32)
    return dest, block_expert, nact, fill_start, pad_end.astype(jnp.int32), n_rows


def _dispatch_kernel(dest_ref, fs_ref, fe_ref, nact_ref, x_hbm, xs_hbm, xbuf, zbuf, lsem, sem,
                     zsem):
    i = pl.program_id(0)
    nsteps = pl.num_programs(0)
    slot = i % DISPATCH_BUFS
    ct = DISPATCH_TOKENS
    n_blocks = xs_hbm.shape[0] // MOE_BLOCK

    def load(step, sl):
        start = pl.multiple_of(step * ct, ct)
        return pltpu.make_async_copy(x_hbm.at[pl.ds(start, ct)], xbuf.at[sl], lsem.at[sl])

    def wait_step(sl):
        for _ in range(MOE_TOPK):
            pltpu.make_async_copy(xbuf.at[sl], xs_hbm.at[pl.ds(0, ct)], sem.at[sl]).wait()

    @pl.when(i == 0)
    def _():
        for s in range(DISPATCH_BUFS - 1):
            load(s, s).start()

    load(i, slot).wait()

    def body(t, c):
        for k in range(MOE_TOPK):
            d = dest_ref[(i * ct + t) * MOE_TOPK + k]
            pltpu.make_async_copy(xbuf.at[slot, pl.ds(t, 1)], xs_hbm.at[pl.ds(d, 1)],
                                  sem.at[slot]).start()
        return c
    lax.fori_loop(0, ct, body, 0, unroll=4)

    @pl.when(i >= 1)
    def _():
        wait_step((i - 1) % DISPATCH_BUFS)

    @pl.when(i + DISPATCH_BUFS - 1 < nsteps)
    def _():
        load(i + DISPATCH_BUFS - 1, (i + DISPATCH_BUFS - 1) % DISPATCH_BUFS).start()

    @pl.when(i == nsteps - 1)
    def _():
        wait_step(slot)
        zbuf[...] = jnp.zeros_like(zbuf)

        def fill_rows(wait):
            def per_expert(e, c):
                def per_row(r, c2):
                    cp = pltpu.make_async_copy(zbuf.at[pl.ds(0, 1)], xs_hbm.at[pl.ds(r, 1)],
                                               zsem.at[0])
                    cp.wait() if wait else cp.start()
                    return c2
                lax.fori_loop(fs_ref[e], fe_ref[e], per_row, 0)
                return c
            lax.fori_loop(0, MOE_EXPERTS, per_expert, 0)

        def fill_blocks(wait):
            def per_block(b, c):
                start = pl.multiple_of(b * MOE_BLOCK, MOE_BLOCK)
                cp = pltpu.make_async_copy(zbuf, xs_hbm.at[pl.ds(start, MOE_BLOCK)], zsem.at[1])
                cp.wait() if wait else cp.start()
                return c
            lax.fori_loop(nact_ref[0], n_blocks, per_block, 0)

        fill_rows(False)
        fill_blocks(False)
        fill_rows(True)
        fill_blocks(True)


def _dispatch(x, dest, fill_start, fill_end, nact, n_rows):
    n = x.shape[0]
    grid_spec = pltpu.PrefetchScalarGridSpec(
        num_scalar_prefetch=4,
        grid=(n // DISPATCH_TOKENS,),
        in_specs=[pl.BlockSpec(memory_space=pl.ANY)],
        out_specs=pl.BlockSpec(memory_space=pl.ANY),
        scratch_shapes=[pltpu.VMEM((DISPATCH_BUFS, DISPATCH_TOKENS, D_MODEL), F32),
                        pltpu.VMEM((MOE_BLOCK, D_MODEL), F32),
                        pltpu.SemaphoreType.DMA((DISPATCH_BUFS,)),
                        pltpu.SemaphoreType.DMA((DISPATCH_BUFS,)),
                        pltpu.SemaphoreType.DMA((2,))],
    )
    return pl.pallas_call(
        _dispatch_kernel,
        grid_spec=grid_spec,
        out_shape=jax.ShapeDtypeStruct((n_rows, D_MODEL), F32),
        compiler_params=_cparams("arbitrary"),
        name="dispatch",
    )(dest, fill_start, fill_end, nact, x)


def _moe_ffn_kernel(be_ref, nact_ref, xs_ref, wup_ref, wdn_ref, ys_ref, wup_bf, wdn_bf):
    i = pl.program_id(0)
    active = i < nact_ref[0]
    new_expert = (i == 0) | (be_ref[i] != be_ref[jnp.maximum(i - 1, 0)])

    @pl.when(active & new_expert)
    def _():
        wup_bf[...] = wup_ref[0].astype(BF16)
        wdn_bf[...] = wdn_ref[0].astype(BF16)

    @pl.when(active)
    def _():
        hcat = jnp.dot(xs_ref[...].astype(BF16), wup_bf[...], preferred_element_type=F32)
        hg = hcat[:, :MOE_D_EXPERT]
        hu = hcat[:, MOE_D_EXPERT:]
        act = (hg * jax.nn.sigmoid(hg) * hu).astype(BF16)
        ys_ref[...] = jnp.dot(act, wdn_bf[...], preferred_element_type=F32)

    @pl.when(jnp.logical_not(active))
    def _():
        ys_ref[...] = jnp.zeros_like(ys_ref)


def _moe_ffn(xs, block_expert, nact, w_up, w_down, layer):
    n_rows = xs.shape[0]
    e0 = layer * MOE_EXPERTS
    grid_spec = pltpu.PrefetchScalarGridSpec(
        num_scalar_prefetch=2,
        grid=(n_rows // MOE_BLOCK,),
        in_specs=[pl.BlockSpec((MOE_BLOCK, D_MODEL), lambda i, be, na: (i, 0)),
                  pl.BlockSpec((1, D_MODEL, 2 * MOE_D_EXPERT),
                               lambda i, be, na: (e0 + be[i], 0, 0)),
                  pl.BlockSpec((1, MOE_D_EXPERT, D_MODEL), lambda i, be, na: (e0 + be[i], 0, 0))],
        out_specs=pl.BlockSpec((MOE_BLOCK, D_MODEL), lambda i, be, na: (i, 0)),
        scratch_shapes=[pltpu.VMEM((D_MODEL, 2 * MOE_D_EXPERT), BF16),
                        pltpu.VMEM((MOE_D_EXPERT, D_MODEL), BF16)],
    )
    return pl.pallas_call(
        _moe_ffn_kernel,
        grid_spec=grid_spec,
        out_shape=jax.ShapeDtypeStruct((n_rows, D_MODEL), F32),
        compiler_params=_cparams("arbitrary"),
        name="moe_ffn",
    )(block_expert, nact, xs, w_up, w_down)


def _combine_ln_kernel(dest_ref, x_ref, route_ref, ys_hbm, g_ref, b_ref, o_ref, ybuf, sem):
    i = pl.program_id(0)
    nsteps = pl.num_programs(0)
    slot = i % 2
    tm = x_ref.shape[0]

    def start_gather(step, sl):
        def body(t, c):
            for k in range(MOE_TOPK):
                d = dest_ref[(step * tm + t) * MOE_TOPK + k]
                pltpu.make_async_copy(ys_hbm.at[pl.ds(d, 1)], ybuf.at[sl, k, pl.ds(t, 1)],
                                      sem.at[sl]).start()
            return c
        lax.fori_loop(0, tm, body, 0, unroll=4)

    @pl.when(i == 0)
    def _():
        start_gather(0, 0)

    @pl.when(i + 1 < nsteps)
    def _():
        start_gather(i + 1, 1 - slot)

    for k in range(MOE_TOPK):
        pltpu.make_async_copy(ys_hbm.at[pl.ds(0, tm)], ybuf.at[slot, k], sem.at[slot]).wait()
    route = route_ref[...]
    ffn = route[:, 0:1] * ybuf[slot, 0] + route[:, 1:2] * ybuf[slot, 1]
    o_ref[...] = _layer_norm_rows(DN_ALPHA * x_ref[...] + ffn, g_ref[...], b_ref[...])


def _combine_ln(x, route, ys, dest, g, b):
    n = x.shape[0]
    tm = ROW_TILE
    vec = pl.BlockSpec((1, D_MODEL), lambda i, d: (0, 0))
    grid_spec = pltpu.PrefetchScalarGridSpec(
        num_scalar_prefetch=1,
        grid=(n // tm,),
        in_specs=[pl.BlockSpec((tm, D_MODEL), lambda i, d: (i, 0)),
                  pl.BlockSpec((tm, LANES), lambda i, d: (i, 0)),
                  pl.BlockSpec(memory_space=pl.ANY), vec, vec],
        out_specs=pl.BlockSpec((tm, D_MODEL), lambda i, d: (i, 0)),
        scratch_shapes=[pltpu.VMEM((2, MOE_TOPK, tm, D_MODEL), F32),
                        pltpu.SemaphoreType.DMA((2,))],
    )
    return pl.pallas_call(
        _combine_ln_kernel,
        grid_spec=grid_spec,
        out_shape=jax.ShapeDtypeStruct((n, D_MODEL), F32),
        compiler_params=_cparams("arbitrary"),
        name="combine_ln",
    )(dest, x, route, ys, g.reshape(1, D_MODEL), b.reshape(1, D_MODEL))


def kernel(x, hg_w_in, hg_lb_logits, hg_norm_w, hg_w_out, cv_w_in, cv_w, cv_w_out, ln_g, ln_b,
           moe_w_group, moe_b_group, moe_w_expert, moe_b_expert, moe_w_up, moe_w_down):
    batch, seq, d = x.shape
    n = batch * seq
    xf = x.reshape(n, d)
    w_up_all = moe_w_up.reshape(DEPTH * MOE_EXPERTS, D_MODEL, 2 * MOE_D_EXPERT)
    w_down_all = moe_w_down.reshape(DEPTH * MOE_EXPERTS, MOE_D_EXPERT, D_MODEL)
    for layer in range(DEPTH):
        j = layer // 2
        if layer % 2 == 0:
            proj = _matmul(xf, hg_w_in[j].astype(BF16))
            o_f, o_b = _gla(proj, hg_lb_logits, layer, batch, seq)
            xf = _hgrn_out(o_f, o_b, proj, hg_norm_w[j], hg_w_out[j].astype(BF16), xf,
                           ln_g[layer, 0], ln_b[layer, 0])
        else:
            proj = _matmul(xf, cv_w_in[j].astype(BF16))
            xf = _conv_out(proj, cv_w[j], cv_w_out[j].astype(BF16), xf,
                           ln_g[layer, 0], ln_b[layer, 0], seq)
        route, counts = _router(xf, moe_w_group[layer], moe_b_group[layer],
                                moe_w_expert[layer], moe_b_expert[layer])
        dest, block_expert, nact, fill_start, fill_end, n_rows = _moe_plan(route, counts, n)
        xs = _dispatch(xf, dest, fill_start, fill_end, nact, n_rows)
        ys = _moe_ffn(xs, block_expert, nact, w_up_all, w_down_all, layer)
        xf = _combine_ln(xf, route, ys, dest, ln_g[layer, 1], ln_b[layer, 1])
    return xf.reshape(batch, seq, d)
```

```python
import functools

import jax
import jax.numpy as jnp
from jax import lax
from jax.experimental import pallas as pl
from jax.experimental.pallas import tpu as pltpu

D_MODEL = 1024
DEPTH = 4
HG_DK = 128
HG_HEADS = D_MODEL // HG_DK
HG_STREAMS = 5
CONV_STREAMS = 3
MOE_GROUPS = 4
MOE_EXPERTS_PER_GROUP = 8
MOE_EXPERTS = MOE_GROUPS * MOE_EXPERTS_PER_GROUP
MOE_TOPK = 2
MOE_D_EXPERT = D_MODEL // 2
MOE_BLOCK = 256
DN_ALPHA = (2.0 * DEPTH) ** 0.25
LN_EPS = 1e-5
RMS_EPS = 1e-6

LANES = 128
SUBLANES = 8
VMEM_LIMIT = 48 * 1024 * 1024
GLA_CHUNK = 64
GLA_TBLOCK = 512
ROW_TILE = 256
NEG_BIG = -1e30

BF16 = jnp.bfloat16
F32 = jnp.float32


def _cparams(*sem):
    return pltpu.CompilerParams(dimension_semantics=sem, vmem_limit_bytes=VMEM_LIMIT)


def _mm_kernel(x_ref, w_ref, o_ref):
    o_ref[...] = jnp.dot(x_ref[...].astype(BF16), w_ref[...], preferred_element_type=F32)


def _matmul(x, w, tm=512, tn=1024):
    n, k = x.shape
    nn = w.shape[1]
    return pl.pallas_call(
        _mm_kernel,
        grid=(nn // tn, n // tm),
        in_specs=[pl.BlockSpec((tm, k), lambda j, i: (i, 0)),
                  pl.BlockSpec((k, tn), lambda j, i: (0, j))],
        out_specs=pl.BlockSpec((tm, tn), lambda j, i: (i, j)),
        out_shape=jax.ShapeDtypeStruct((n, nn), F32),
        compiler_params=_cparams("arbitrary", "arbitrary"),
        name="in_proj",
    )(x, w)


def _layer_norm_rows(y, g, b):
    mu = jnp.mean(y, axis=-1, keepdims=True)
    yc = y - mu
    var = jnp.mean(yc * yc, axis=-1, keepdims=True)
    return yc * lax.rsqrt(var + LN_EPS) * g + b


def _rows(x, blocks, size):
    parts = [x[b * size:(b + 1) * size, :] for b in blocks]
    return parts[0] if len(parts) == 1 else jnp.concatenate(parts, axis=0)


def _gla_chunk(q_s, k_s, lf_s, g_s, acc_s, v_ref, o_ref, st_ref, d, start, reverse):
    c = GLA_CHUNK
    t8 = SUBLANES
    nt_dims = (((1,), (1,)), ((), ()))

    def tt(ref, i, lead=True):
        idx = pl.ds(start + i, t8, stride=t8)
        return ref[d, idx, :] if lead else ref[idx, :]

    lf = [tt(lf_s, i) for i in range(t8)]
    qt = [tt(q_s, i) for i in range(t8)]
    kt = [tt(k_s, i) for i in range(t8)]
    vt = [tt(v_ref, i, lead=False) for i in range(t8)]

    gi = [None] * t8
    prev = None
    for i in (range(t8 - 1, -1, -1) if reverse else range(t8)):
        gi[i] = lf[i] if prev is None else prev + lf[i]
        prev = gi[i]
    tot = prev
    sub = lax.broadcasted_iota(jnp.int32, (t8, LANES), 0)
    incl = tot
    s = 1
    while s < t8:
        if reverse:
            incl = incl + jnp.where(sub + s < t8, pltpu.roll(incl, t8 - s, 0), 0.0)
        else:
            incl = incl + jnp.where(sub >= s, pltpu.roll(incl, s, 0), 0.0)
        s *= 2
    excl = incl - tot
    for i in range(t8):
        g_s[d, pl.ds(start + i, t8, stride=t8), :] = gi[i] + excl

    acc = []
    for i in range(t8):
        a_i = jnp.sum(qt[i] * kt[i], axis=-1, keepdims=True) * vt[i]
        for r in (range(i + 1, t8) if reverse else range(i)):
            w = qt[i] * kt[r] * jnp.exp(gi[i] - gi[r])
            a_i = a_i + jnp.sum(w, axis=-1, keepdims=True) * vt[r]
        acc.append(a_i)

    sl = pl.ds(start, c)
    q = q_s[d, sl, :]
    kk = k_s[d, sl, :]
    v = v_ref[sl, :]
    g = g_s[d, sl, :]
    vb = v.astype(BF16)
    st = st_ref[d]

    o = lax.dot_general((q * jnp.exp(g)).astype(BF16), st.astype(BF16), nt_dims,
                        preferred_element_type=F32)

    half = c // 2
    qd_l, kd_l, v_l, q_tiles = [], [], [], []
    b = 2 * t8
    while b <= c:
        h = b // 2
        nblk = c // b
        lo = [2 * m for m in range(nblk)]
        hi = [2 * m + 1 for m in range(nblk)]
        q_half, k_half = (lo, hi) if reverse else (hi, lo)
        refs = [g[m * b + h:m * b + h + 1, :] if reverse else g[m * b + h - 1:m * b + h, :]
                for m in range(nblk)]
        gref = jnp.concatenate([jnp.broadcast_to(r, (h, LANES)) for r in refs], axis=0) \
            if nblk > 1 else jnp.broadcast_to(refs[0], (h, LANES))
        qd_l.append(_rows(q, q_half, h) * jnp.exp(_rows(g, q_half, h) - gref))
        kd_l.append(_rows(kk, k_half, h) * jnp.exp(gref - _rows(g, k_half, h)))
        v_l.append(_rows(v, k_half, h))
        q_tiles.append([hb * (h // t8) + t for hb in q_half for t in range(h // t8)])
        b *= 2
    nlev = len(qd_l)
    p = lax.dot_general(jnp.concatenate(qd_l, axis=0).astype(BF16),
                        jnp.concatenate(kd_l, axis=0).astype(BF16), nt_dims,
                        preferred_element_type=F32)
    ii = lax.broadcasted_iota(jnp.int32, p.shape, 0)
    jj = lax.broadcasted_iota(jnp.int32, p.shape, 1)
    keep = None
    for lv in range(nlev):
        h = t8 << lv
        m = (ii // half == lv) & (jj // half == lv) & ((ii // h) == (jj // h))
        keep = m if keep is None else keep | m
    p = jnp.where(keep, p, 0.0)
    res = jnp.dot(p.astype(BF16), jnp.concatenate(v_l, axis=0).astype(BF16),
                  preferred_element_type=F32)
    contrib = [None] * (c // t8)
    for lv in range(nlev):
        for n_, tile in enumerate(q_tiles[lv]):
            piece = res[lv * half + n_ * t8:lv * half + (n_ + 1) * t8, :]
            contrib[tile] = piece if contrib[tile] is None else contrib[tile] + piece
    zero = jnp.zeros((t8, LANES), F32)
    o = o + jnp.concatenate([zero if p_ is None else p_ for p_ in contrib], axis=0)

    gl = g[0:1, :] if reverse else g[c - 1:c, :]
    kd = (kk * jnp.exp(gl - g)).astype(BF16)
    upd = lax.dot_general(vb, kd, (((0,), (0,)), ((), ())), preferred_element_type=F32)
    st_ref[d] = st * jnp.exp(gl) + upd

    for i in range(t8):
        acc_s[d, pl.ds(start + i, t8, stride=t8), :] = acc[i]
    o_ref[sl, :] = o + acc_s[d, sl, :]


def _gla_kernel(layer, lbl_ref, qf_ref, ff_ref, vf_ref, qb_ref, fb_ref, vb_ref,
                of_ref, ob_ref, st_ref, q_s, k_s, lf_s, g_s, acc_s):
    tb = GLA_TBLOCK
    nc = tb // GLA_CHUNK

    @pl.when(pl.program_id(2) == 0)
    def _():
        st_ref[...] = jnp.zeros_like(st_ref)

    if layer > 0:
        lg = lbl_ref[...]
        e = jnp.exp(lg - jnp.max(lg, axis=0, keepdims=True))
        sm = e / jnp.sum(e, axis=0, keepdims=True)
        lb = sm[1]
        for l in range(2, layer + 1):
            lb = lb + sm[l]

    for d, (q_ref, f_ref) in enumerate(((qf_ref, ff_ref), (qb_ref, fb_ref))):
        qx = q_ref[...]
        fx = f_ref[...]
        t = jnp.exp(-jnp.abs(fx))
        r = 1.0 / (1.0 + t)
        logsig = jnp.minimum(fx, 0.0) - jnp.log(1.0 + t)
        sig_neg = jnp.where(fx >= 0.0, t * r, r)
        if layer > 0:
            lbd = lb[d:d + 1, :]
            a = jnp.log(lbd)
            bb = jnp.log1p(-lbd) + logsig
            lf = jnp.maximum(a, bb) + jnp.log(1.0 + jnp.exp(-jnp.abs(a - bb)))
            kk = (1.0 - lbd) * sig_neg
        else:
            lf = logsig
            kk = sig_neg
        q_s[d] = qx * jax.nn.sigmoid(qx)
        k_s[d] = kk
        lf_s[d] = lf

    def body(ci, carry):
        sf = pl.multiple_of(ci * GLA_CHUNK, GLA_CHUNK)
        sb = pl.multiple_of((nc - 1 - ci) * GLA_CHUNK, GLA_CHUNK)
        _gla_chunk(q_s, k_s, lf_s, g_s, acc_s, vf_ref, of_ref, st_ref, 0, sf, False)
        _gla_chunk(q_s, k_s, lf_s, g_s, acc_s, vb_ref, ob_ref, st_ref, 1, sb, True)
        return carry

    lax.fori_loop(0, nc, body, 0, unroll=2)


def _gla(proj, lb_logits, layer, batch, seq):
    n = proj.shape[0]
    tb = GLA_TBLOCK
    nb = seq // tb
    h8 = HG_HEADS

    def spec(stream, rev):
        if rev:
            return pl.BlockSpec((tb, LANES), lambda b, h, c: (b * nb + nb - 1 - c, stream * h8 + h))
        return pl.BlockSpec((tb, LANES), lambda b, h, c: (b * nb + c, stream * h8 + h))

    o_f_spec = pl.BlockSpec((tb, LANES), lambda b, h, c: (b * nb + c, h))
    o_b_spec = pl.BlockSpec((tb, LANES), lambda b, h, c: (b * nb + nb - 1 - c, h))
    return pl.pallas_call(
        functools.partial(_gla_kernel, layer),
        grid=(batch, h8, nb),
        in_specs=[pl.BlockSpec((DEPTH, 2, LANES), lambda b, h, c: (0, 0, h)),
                  spec(0, False), spec(1, False), spec(3, False),
                  spec(0, True), spec(2, True), spec(3, True)],
        out_specs=[o_f_spec, o_b_spec],
        out_shape=[jax.ShapeDtypeStruct((n, D_MODEL), F32)] * 2,
        scratch_shapes=[pltpu.VMEM((2, HG_DK, HG_DK), F32)]
        + [pltpu.VMEM((2, tb, LANES), F32)] * 5,
        compiler_params=_cparams("arbitrary", "arbitrary", "arbitrary"),
        name="gla",
    )(lb_logits, proj, proj, proj, proj, proj, proj)


def _hgrn_out_kernel(of_ref, ob_ref, gate_ref, nw_ref, w_ref, x_ref, g_ref, b_ref, o_ref):
    o = of_ref[...] + ob_ref[...]
    parts = []
    for h in range(HG_HEADS):
        oh = o[:, h * LANES:(h + 1) * LANES]
        ms = jnp.mean(oh * oh, axis=-1, keepdims=True)
        parts.append(oh * lax.rsqrt(ms + RMS_EPS))
    gate = gate_ref[...]
    y = jnp.concatenate(parts, axis=-1) * nw_ref[...] * (gate * jax.nn.sigmoid(gate))
    mix = jnp.dot(y.astype(BF16), w_ref[...], preferred_element_type=F32)
    o_ref[...] = _layer_norm_rows(DN_ALPHA * x_ref[...] + mix, g_ref[...], b_ref[...])


def _hgrn_out(o_f, o_b, proj, norm_w, w_out, x, g, b):
    n = x.shape[0]
    tm = ROW_TILE
    row = pl.BlockSpec((tm, D_MODEL), lambda i: (i, 0))
    vec = pl.BlockSpec((1, D_MODEL), lambda i: (0, 0))
    return pl.pallas_call(
        _hgrn_out_kernel,
        grid=(n // tm,),
        in_specs=[row, row, pl.BlockSpec((tm, D_MODEL), lambda i: (i, HG_STREAMS - 1)), vec,
                  pl.BlockSpec((D_MODEL, D_MODEL), lambda i: (0, 0)), row, vec, vec],
        out_specs=row,
        out_shape=jax.ShapeDtypeStruct((n, D_MODEL), F32),
        compiler_params=_cparams("arbitrary"),
        name="hgrn_out",
    )(o_f, o_b, proj, jnp.tile(norm_w, HG_HEADS).reshape(1, D_MODEL), w_out, x,
      g.reshape(1, D_MODEL), b.reshape(1, D_MODEL))


def _conv_out_kernel(tiles_per_seq, bg_ref, cg_ref, h_ref, cgp_ref, hp_ref, cgn_ref, hn_ref,
                     cw_ref, w_ref, x_ref, g_ref, b_ref, o_ref):
    i = pl.program_id(0)
    tm = cg_ref.shape[0]
    u = cg_ref[...] * h_ref[...]
    first = (i % tiles_per_seq) == 0
    last = (i % tiles_per_seq) == tiles_per_seq - 1
    u_prev_row = jnp.where(first, 0.0, cgp_ref[SUBLANES - 1:SUBLANES, :] * hp_ref[SUBLANES - 1:SUBLANES, :])
    u_next_row = jnp.where(last, 0.0, cgn_ref[0:1, :] * hn_ref[0:1, :])
    rows = lax.broadcasted_iota(jnp.int32, u.shape, 0)
    u_prev = jnp.where(rows == 0, u_prev_row, pltpu.roll(u, 1, 0))
    u_next = jnp.where(rows == tm - 1, u_next_row, pltpu.roll(u, tm - 1, 0))
    cw = cw_ref[...]
    y = u_prev * cw[0:1, :] + u * cw[1:2, :] + u_next * cw[2:3, :]
    mix = jnp.dot((bg_ref[...] * y).astype(BF16), w_ref[...], preferred_element_type=F32)
    o_ref[...] = _layer_norm_rows(DN_ALPHA * x_ref[...] + mix, g_ref[...], b_ref[...])


def _conv_out(proj, conv_w, w_out, x, g, b, seq):
    n = x.shape[0]
    tm = ROW_TILE
    r8 = tm // SUBLANES
    nblk8 = n // SUBLANES
    row = pl.BlockSpec((tm, D_MODEL), lambda i: (i, 0))
    vec = pl.BlockSpec((1, D_MODEL), lambda i: (0, 0))

    def main(stream):
        return pl.BlockSpec((tm, D_MODEL), lambda i: (i, stream))

    def prev(stream):
        return pl.BlockSpec((SUBLANES, D_MODEL), lambda i: (jnp.maximum(i * r8 - 1, 0), stream))

    def nxt(stream):
        return pl.BlockSpec((SUBLANES, D_MODEL),
                            lambda i: (jnp.minimum((i + 1) * r8, nblk8 - 1), stream))

    return pl.pallas_call(
        functools.partial(_conv_out_kernel, seq // tm),
        grid=(n // tm,),
        in_specs=[main(0), main(1), main(2), prev(1), prev(2), nxt(1), nxt(2),
                  pl.BlockSpec((3, D_MODEL), lambda i: (0, 0)),
                  pl.BlockSpec((D_MODEL, D_MODEL), lambda i: (0, 0)), row, vec, vec],
        out_specs=row,
        out_shape=jax.ShapeDtypeStruct((n, D_MODEL), F32),
        compiler_params=_cparams("arbitrary"),
        name="conv_out",
    )(proj, proj, proj, proj, proj, proj, proj, conv_w, w_out, x,
      g.reshape(1, D_MODEL), b.reshape(1, D_MODEL))


def _router_kernel(x_ref, w_ref, b_ref, tri_ref, o_ref, cnt_ref, carry_ref):
    @pl.when(pl.program_id(0) == 0)
    def _():
        carry_ref[...] = jnp.zeros_like(carry_ref)

    logits = jnp.dot(x_ref[...], w_ref[...], precision=lax.Precision.HIGHEST,
                     preferred_element_type=F32) + b_ref[...]
    lane = lax.broadcasted_iota(jnp.int32, logits.shape, 1)
    lane_f = lane.astype(F32)
    gl = jnp.where(lane < MOE_GROUPS, logits, NEG_BIG)
    gmax = jnp.max(gl, axis=-1, keepdims=True)
    gsum = jnp.sum(jnp.exp(gl - gmax), axis=-1, keepdims=True)
    p_group = 1.0 / gsum
    g_sel = jnp.min(jnp.where(gl == gmax, lane_f, 1e9), axis=-1, keepdims=True)
    e_lane = lane - MOE_GROUPS
    in_grp = ((lane >= MOE_GROUPS) & (lane < MOE_GROUPS + MOE_EXPERTS)
              & ((e_lane // MOE_EXPERTS_PER_GROUP).astype(F32) == g_sel))
    el = jnp.where(in_grp, logits, NEG_BIG)
    t1 = jnp.max(el, axis=-1, keepdims=True)
    i1 = jnp.min(jnp.where(el == t1, lane_f, 1e9), axis=-1, keepdims=True)
    el2 = jnp.where(lane_f == i1, NEG_BIG, el)
    t2 = jnp.max(el2, axis=-1, keepdims=True)
    i2 = jnp.min(jnp.where(el2 == t2, lane_f, 1e9), axis=-1, keepdims=True)
    z = jnp.exp(t2 - t1)
    g1 = p_group / (1.0 + z)
    g2 = g1 * z
    out = jnp.where(lane == 0, g1, 0.0)
    out = jnp.where(lane == 1, g2, out)
    out = jnp.where(lane == 2, i1 - MOE_GROUPS, out)
    out = jnp.where(lane == 3, i2 - MOE_GROUPS, out)

    sel1 = lane_f == i1
    sel2 = lane_f == i2
    onehot = jnp.where(sel1 | sel2, 1.0, 0.0)
    prefix = jnp.dot(tri_ref[...], onehot.astype(BF16), preferred_element_type=F32)
    before = prefix + carry_ref[0:1, :]
    r1 = jnp.sum(jnp.where(sel1, before, 0.0), axis=-1, keepdims=True)
    r2 = jnp.sum(jnp.where(sel2, before, 0.0), axis=-1, keepdims=True)
    out = jnp.where(lane == 4, r1, out)
    out = jnp.where(lane == 5, r2, out)
    o_ref[...] = out
    carry_ref[...] = carry_ref[...] + jnp.sum(onehot, axis=0, keepdims=True)
    cnt_ref[...] = carry_ref[...]


def _router(x, w_group, b_group, w_expert, b_expert):
    n = x.shape[0]
    tm = 512
    pad = LANES - MOE_GROUPS - MOE_EXPERTS
    w = jnp.concatenate([w_group, w_expert, jnp.zeros((D_MODEL, pad), F32)], axis=1)
    b = jnp.concatenate([b_group, b_expert, jnp.zeros((pad,), F32)]).reshape(1, LANES)
    r = jnp.arange(tm, dtype=jnp.int32)
    tri = (r[None, :] < r[:, None]).astype(BF16)
    route, cnt = pl.pallas_call(
        _router_kernel,
        grid=(n // tm,),
        in_specs=[pl.BlockSpec((tm, D_MODEL), lambda i: (i, 0)),
                  pl.BlockSpec((D_MODEL, LANES), lambda i: (0, 0)),
                  pl.BlockSpec((1, LANES), lambda i: (0, 0)),
                  pl.BlockSpec((tm, tm), lambda i: (0, 0))],
        out_specs=[pl.BlockSpec((tm, LANES), lambda i: (i, 0)),
                   pl.BlockSpec((SUBLANES, LANES), lambda i: (0, 0))],
        out_shape=[jax.ShapeDtypeStruct((n, LANES), F32),
                   jax.ShapeDtypeStruct((SUBLANES, LANES), F32)],
        scratch_shapes=[pltpu.VMEM((SUBLANES, LANES), F32)],
        compiler_params=_cparams("arbitrary"),
        name="router",
    )(x, w, b, tri)
    return route, cnt[0, MOE_GROUPS:MOE_GROUPS + MOE_EXPERTS].astype(jnp.int32)


DISPATCH_TOKENS = 256
DISPATCH_BUFS = 3


def _moe_plan(route, counts, n):
    nk = n * MOE_TOPK
    n_rows = -(-nk // MOE_BLOCK) * MOE_BLOCK + MOE_EXPERTS * MOE_BLOCK
    n_blocks = n_rows // MOE_BLOCK
    padded = (counts + MOE_BLOCK - 1) // MOE_BLOCK * MOE_BLOCK
    pad_end = jnp.cumsum(padded)
    pad_start = pad_end - padded
    e = route[:, MOE_TOPK:2 * MOE_TOPK].astype(jnp.int32)
    rank = route[:, 2 * MOE_TOPK:3 * MOE_TOPK].astype(jnp.int32)
    ids = jnp.arange(MOE_EXPERTS, dtype=jnp.int32)
    base = jnp.sum(jnp.where(e[:, :, None] == ids, pad_start, 0), axis=-1)
    dest = (base + rank).reshape(nk)
    block_start = jnp.arange(n_blocks, dtype=jnp.int32) * MOE_BLOCK
    block_expert = jnp.minimum(jnp.sum(block_start[:, None] >= pad_end[None, :], axis=1),
                               MOE_EXPERTS - 1).astype(jnp.int32)
    nact = (pad_end[-1:] // MOE_BLOCK).astype(jnp.int32)
    fill_start = (pad_start + counts).astype(jnp.int32)
    return dest, block_expert, nact, fill_start, pad_end.astype(jnp.int32), n_rows


def _dispatch_kernel(dest_ref, fs_ref, fe_ref, nact_ref, x_hbm, xs_hbm, xbuf, zbuf, lsem, sem,
                     zsem):
    i = pl.program_id(0)
    nsteps = pl.num_programs(0)
    slot = i % DISPATCH_BUFS
    ct = DISPATCH_TOKENS
    gt = ct // SUBLANES
    n_blocks = xs_hbm.shape[0] // MOE_BLOCK

    def load(step, sl):
        start = pl.multiple_of(step * gt, gt)
        return pltpu.make_async_copy(x_hbm.at[pl.ds(start, gt)], xbuf.at[sl], lsem.at[sl])

    def wait_step(sl):
        for _ in range(MOE_TOPK):
            pltpu.make_async_copy(x_hbm.at[pl.ds(0, gt)], xbuf.at[sl], sem.at[sl]).wait()

    @pl.when(i == 0)
    def _():
        for s in range(DISPATCH_BUFS - 1):
            load(s, s).start()

    load(i, slot).wait()

    def body(g, c):
        base = (i * ct + g * SUBLANES) * MOE_TOPK
        for t in range(SUBLANES):
            for k in range(MOE_TOPK):
                d = dest_ref[base + t * MOE_TOPK + k]
                pltpu.make_async_copy(xbuf.at[slot, g, pl.ds(t, 1)], xs_hbm.at[pl.ds(d, 1)],
                                      sem.at[slot]).start()
        return c
    lax.fori_loop(0, gt, body, 0)

    @pl.when(i >= 1)
    def _():
        wait_step((i - 1) % DISPATCH_BUFS)

    @pl.when(i + DISPATCH_BUFS - 1 < nsteps)
    def _():
        load(i + DISPATCH_BUFS - 1, (i + DISPATCH_BUFS - 1) % DISPATCH_BUFS).start()

    @pl.when(i == nsteps - 1)
    def _():
        wait_step(slot)
        zbuf[...] = jnp.zeros_like(zbuf)

        def fill_rows(wait):
            def per_expert(e, c):
                def per_row(r, c2):
                    cp = pltpu.make_async_copy(zbuf.at[pl.ds(0, 1)], xs_hbm.at[pl.ds(r, 1)],
                                               zsem.at[0])
                    cp.wait() if wait else cp.start()
                    return c2
                lax.fori_loop(fs_ref[e], fe_ref[e], per_row, 0)
                return c
            lax.fori_loop(0, MOE_EXPERTS, per_expert, 0)

        def fill_blocks(wait):
            def per_block(b, c):
                start = pl.multiple_of(b * MOE_BLOCK, MOE_BLOCK)
                cp = pltpu.make_async_copy(zbuf, xs_hbm.at[pl.ds(start, MOE_BLOCK)], zsem.at[1])
                cp.wait() if wait else cp.start()
                return c
            lax.fori_loop(nact_ref[0], n_blocks, per_block, 0)

        fill_rows(False)
        fill_blocks(False)
        fill_rows(True)
        fill_blocks(True)


def _dispatch(x, dest, fill_start, fill_end, nact, n_rows):
    n = x.shape[0]
    grid_spec = pltpu.PrefetchScalarGridSpec(
        num_scalar_prefetch=4,
        grid=(n // DISPATCH_TOKENS,),
        in_specs=[pl.BlockSpec(memory_space=pl.ANY)],
        out_specs=pl.BlockSpec(memory_space=pl.ANY),
        scratch_shapes=[pltpu.VMEM((DISPATCH_BUFS, DISPATCH_TOKENS // SUBLANES, SUBLANES, D_MODEL),
                                   F32),
                        pltpu.VMEM((MOE_BLOCK, D_MODEL), F32),
                        pltpu.SemaphoreType.DMA((DISPATCH_BUFS,)),
                        pltpu.SemaphoreType.DMA((DISPATCH_BUFS,)),
                        pltpu.SemaphoreType.DMA((2,))],
    )
    return pl.pallas_call(
        _dispatch_kernel,
        grid_spec=grid_spec,
        out_shape=jax.ShapeDtypeStruct((n_rows, D_MODEL), F32),
        compiler_params=_cparams("arbitrary"),
        name="dispatch",
    )(dest, fill_start, fill_end, nact, x.reshape(n // SUBLANES, SUBLANES, D_MODEL))


def _moe_ffn_kernel(be_ref, nact_ref, xs_ref, wup_ref, wdn_ref, ys_ref, wup_bf, wdn_bf):
    i = pl.program_id(0)
    active = i < nact_ref[0]
    new_expert = (i == 0) | (be_ref[i] != be_ref[jnp.maximum(i - 1, 0)])

    @pl.when(active & new_expert)
    def _():
        wup_bf[...] = wup_ref[0].astype(BF16)
        wdn_bf[...] = wdn_ref[0].astype(BF16)

    @pl.when(active)
    def _():
        hcat = jnp.dot(xs_ref[...].astype(BF16), wup_bf[...], preferred_element_type=F32)
        hg = hcat[:, :MOE_D_EXPERT]
        hu = hcat[:, MOE_D_EXPERT:]
        act = (hg * jax.nn.sigmoid(hg) * hu).astype(BF16)
        ys_ref[...] = jnp.dot(act, wdn_bf[...], preferred_element_type=F32)

    @pl.when(jnp.logical_not(active))
    def _():
        ys_ref[...] = jnp.zeros_like(ys_ref)


def _moe_ffn(xs, block_expert, nact, w_up, w_down, layer):
    n_rows = xs.shape[0]
    e0 = layer * MOE_EXPERTS
    grid_spec = pltpu.PrefetchScalarGridSpec(
        num_scalar_prefetch=2,
        grid=(n_rows // MOE_BLOCK,),
        in_specs=[pl.BlockSpec((MOE_BLOCK, D_MODEL), lambda i, be, na: (i, 0)),
                  pl.BlockSpec((1, D_MODEL, 2 * MOE_D_EXPERT),
                               lambda i, be, na: (e0 + be[i], 0, 0)),
                  pl.BlockSpec((1, MOE_D_EXPERT, D_MODEL), lambda i, be, na: (e0 + be[i], 0, 0))],
        out_specs=pl.BlockSpec((MOE_BLOCK, D_MODEL), lambda i, be, na: (i, 0)),
        scratch_shapes=[pltpu.VMEM((D_MODEL, 2 * MOE_D_EXPERT), BF16),
                        pltpu.VMEM((MOE_D_EXPERT, D_MODEL), BF16)],
    )
    return pl.pallas_call(
        _moe_ffn_kernel,
        grid_spec=grid_spec,
        out_shape=jax.ShapeDtypeStruct((n_rows, D_MODEL), F32),
        compiler_params=_cparams("arbitrary"),
        name="moe_ffn",
    )(block_expert, nact, xs, w_up, w_down)


def _combine_ln_kernel(dest_ref, x_ref, route_ref, ys_hbm, g_ref, b_ref, o_ref, ybuf, sem):
    i = pl.program_id(0)
    nsteps = pl.num_programs(0)
    slot = i % 2
    tm = x_ref.shape[0]

    def start_gather(step, sl):
        def body(g, c):
            base = (step * tm + g * SUBLANES) * MOE_TOPK
            for t in range(SUBLANES):
                for k in range(MOE_TOPK):
                    d = dest_ref[base + t * MOE_TOPK + k]
                    pltpu.make_async_copy(ys_hbm.at[pl.ds(d, 1)], ybuf.at[sl, k, g, pl.ds(t, 1)],
                                          sem.at[sl]).start()
            return c
        lax.fori_loop(0, tm // SUBLANES, body, 0)

    @pl.when(i == 0)
    def _():
        start_gather(0, 0)

    @pl.when(i + 1 < nsteps)
    def _():
        start_gather(i + 1, 1 - slot)

    for k in range(MOE_TOPK):
        pltpu.make_async_copy(ybuf.at[1 - slot, k], ybuf.at[slot, k], sem.at[slot]).wait()
    route = route_ref[...]
    y0 = ybuf[slot, 0].reshape(tm, D_MODEL)
    y1 = ybuf[slot, 1].reshape(tm, D_MODEL)
    ffn = route[:, 0:1] * y0 + route[:, 1:2] * y1
    o_ref[...] = _layer_norm_rows(DN_ALPHA * x_ref[...] + ffn, g_ref[...], b_ref[...])


def _combine_ln(x, route, ys, dest, g, b):
    n = x.shape[0]
    tm = ROW_TILE
    vec = pl.BlockSpec((1, D_MODEL), lambda i, d: (0, 0))
    grid_spec = pltpu.PrefetchScalarGridSpec(
        num_scalar_prefetch=1,
        grid=(n // tm,),
        in_specs=[pl.BlockSpec((tm, D_MODEL), lambda i, d: (i, 0)),
                  pl.BlockSpec((tm, LANES), lambda i, d: (i, 0)),
                  pl.BlockSpec(memory_space=pl.ANY), vec, vec],
        out_specs=pl.BlockSpec((tm, D_MODEL), lambda i, d: (i, 0)),
        scratch_shapes=[pltpu.VMEM((2, MOE_TOPK, tm // SUBLANES, SUBLANES, D_MODEL), F32),
                        pltpu.SemaphoreType.DMA((2,))],
    )
    return pl.pallas_call(
        _combine_ln_kernel,
        grid_spec=grid_spec,
        out_shape=jax.ShapeDtypeStruct((n, D_MODEL), F32),
        compiler_params=_cparams("arbitrary"),
        name="combine_ln",
    )(dest, x, route, ys, g.reshape(1, D_MODEL), b.reshape(1, D_MODEL))


def kernel(x, hg_w_in, hg_lb_logits, hg_norm_w, hg_w_out, cv_w_in, cv_w, cv_w_out, ln_g, ln_b,
           moe_w_group, moe_b_group, moe_w_expert, moe_b_expert, moe_w_up, moe_w_down):
    batch, seq, d = x.shape
    n = batch * seq
    xf = x.reshape(n, d)
    w_up_all = moe_w_up.reshape(DEPTH * MOE_EXPERTS, D_MODEL, 2 * MOE_D_EXPERT)
    w_down_all = moe_w_down.reshape(DEPTH * MOE_EXPERTS, MOE_D_EXPERT, D_MODEL)
    for layer in range(DEPTH):
        j = layer // 2
        if layer % 2 == 0:
            proj = _matmul(xf, hg_w_in[j].astype(BF16))
            o_f, o_b = _gla(proj, hg_lb_logits, layer, batch, seq)
            xf = _hgrn_out(o_f, o_b, proj, hg_norm_w[j], hg_w_out[j].astype(BF16), xf,
                           ln_g[layer, 0], ln_b[layer, 0])
        else:
            proj = _matmul(xf, cv_w_in[j].astype(BF16))
            xf = _conv_out(proj, cv_w[j], cv_w_out[j].astype(BF16), xf,
                           ln_g[layer, 0], ln_b[layer, 0], seq)
        route, counts = _router(xf, moe_w_group[layer], moe_b_group[layer],
                                moe_w_expert[layer], moe_b_expert[layer])
        dest, block_expert, nact, fill_start, fill_end, n_rows = _moe_plan(route, counts, n)
        xs = _dispatch(xf, dest, fill_start, fill_end, nact, n_rows)
        ys = _moe_ffn(xs, block_expert, nact, w_up_all, w_down_all, layer)
        xf = _combine_ln(xf, route, ys, dest, ln_g[layer, 1], ln_b[layer, 1])
    return xf.reshape(batch, seq, d)
```

```python
import functools

import jax
import jax.numpy as jnp
from jax import lax
from jax.experimental import pallas as pl
from jax.experimental.pallas import tpu as pltpu

D_MODEL = 1024
DEPTH = 4
HG_DK = 128
HG_HEADS = D_MODEL // HG_DK
HG_STREAMS = 5
CONV_STREAMS = 3
MOE_GROUPS = 4
MOE_EXPERTS_PER_GROUP = 8
MOE_EXPERTS = MOE_GROUPS * MOE_EXPERTS_PER_GROUP
MOE_TOPK = 2
MOE_D_EXPERT = D_MODEL // 2
MOE_BLOCK = 256
DN_ALPHA = (2.0 * DEPTH) ** 0.25
LN_EPS = 1e-5
RMS_EPS = 1e-6

LANES = 128
SUBLANES = 8
BF16_SUBLANES = 16
VMEM_LIMIT = 48 * 1024 * 1024
GLA_CHUNK = 64
GLA_TBLOCK = 512
ROW_TILE = 256
NEG_BIG = -1e30

BF16 = jnp.bfloat16
F32 = jnp.float32


def _cparams(*sem):
    return pltpu.CompilerParams(dimension_semantics=sem, vmem_limit_bytes=VMEM_LIMIT)


def _mm_kernel(x_ref, w_ref, o_ref):
    o_ref[...] = jnp.dot(x_ref[...].astype(BF16), w_ref[...],
                         preferred_element_type=F32).astype(o_ref.dtype)


def _matmul(x, w, out_dtype, tm=512, tn=1024):
    n, k = x.shape
    nn = w.shape[1]
    return pl.pallas_call(
        _mm_kernel,
        grid=(nn // tn, n // tm),
        in_specs=[pl.BlockSpec((tm, k), lambda j, i: (i, 0)),
                  pl.BlockSpec((k, tn), lambda j, i: (0, j))],
        out_specs=pl.BlockSpec((tm, tn), lambda j, i: (i, j)),
        out_shape=jax.ShapeDtypeStruct((n, nn), out_dtype),
        compiler_params=_cparams("arbitrary", "arbitrary"),
        name="in_proj",
    )(x, w)


def _layer_norm_rows(y, g, b):
    mu = jnp.mean(y, axis=-1, keepdims=True)
    yc = y - mu
    var = jnp.mean(yc * yc, axis=-1, keepdims=True)
    return yc * lax.rsqrt(var + LN_EPS) * g + b


def _rows(x, blocks, size):
    parts = [x[b * size:(b + 1) * size, :] for b in blocks]
    return parts[0] if len(parts) == 1 else jnp.concatenate(parts, axis=0)


def _gla_chunk(q_s, k_s, v_s, lf_s, g_s, acc_s, o_ref, st_ref, d, start, reverse):
    c = GLA_CHUNK
    t8 = SUBLANES
    nt_dims = (((1,), (1,)), ((), ()))

    def tt(ref, i):
        return ref[d, pl.ds(start + i, t8, stride=t8), :]

    lf = [tt(lf_s, i) for i in range(t8)]
    qt = [tt(q_s, i) for i in range(t8)]
    kt = [tt(k_s, i) for i in range(t8)]
    vt = [tt(v_s, i) for i in range(t8)]

    gi = [None] * t8
    prev = None
    for i in (range(t8 - 1, -1, -1) if reverse else range(t8)):
        gi[i] = lf[i] if prev is None else prev + lf[i]
        prev = gi[i]
    tot = prev
    sub = lax.broadcasted_iota(jnp.int32, (t8, LANES), 0)
    incl = tot
    s = 1
    while s < t8:
        if reverse:
            incl = incl + jnp.where(sub + s < t8, pltpu.roll(incl, t8 - s, 0), 0.0)
        else:
            incl = incl + jnp.where(sub >= s, pltpu.roll(incl, s, 0), 0.0)
        s *= 2
    excl = incl - tot
    for i in range(t8):
        g_s[d, pl.ds(start + i, t8, stride=t8), :] = gi[i] + excl

    acc = []
    for i in range(t8):
        a_i = jnp.sum(qt[i] * kt[i], axis=-1, keepdims=True) * vt[i]
        for r in (range(i + 1, t8) if reverse else range(i)):
            w = qt[i] * kt[r] * jnp.exp(gi[i] - gi[r])
            a_i = a_i + jnp.sum(w, axis=-1, keepdims=True) * vt[r]
        acc.append(a_i)

    sl = pl.ds(start, c)
    q = q_s[d, sl, :]
    kk = k_s[d, sl, :]
    v = v_s[d, sl, :]
    g = g_s[d, sl, :]
    vb = v.astype(BF16)
    st = st_ref[d]

    o = lax.dot_general((q * jnp.exp(g)).astype(BF16), st.astype(BF16), nt_dims,
                        preferred_element_type=F32)

    half = c // 2
    qd_l, kd_l, v_l, q_tiles = [], [], [], []
    b = 2 * t8
    while b <= c:
        h = b // 2
        nblk = c // b
        lo = [2 * m for m in range(nblk)]
        hi = [2 * m + 1 for m in range(nblk)]
        q_half, k_half = (lo, hi) if reverse else (hi, lo)
        refs = [g[m * b + h:m * b + h + 1, :] if reverse else g[m * b + h - 1:m * b + h, :]
                for m in range(nblk)]
        gref = jnp.concatenate([jnp.broadcast_to(r, (h, LANES)) for r in refs], axis=0) \
            if nblk > 1 else jnp.broadcast_to(refs[0], (h, LANES))
        qd_l.append(_rows(q, q_half, h) * jnp.exp(_rows(g, q_half, h) - gref))
        kd_l.append(_rows(kk, k_half, h) * jnp.exp(gref - _rows(g, k_half, h)))
        v_l.append(_rows(v, k_half, h))
        q_tiles.append([hb * (h // t8) + t for hb in q_half for t in range(h // t8)])
        b *= 2
    nlev = len(qd_l)
    p = lax.dot_general(jnp.concatenate(qd_l, axis=0).astype(BF16),
                        jnp.concatenate(kd_l, axis=0).astype(BF16), nt_dims,
                        preferred_element_type=F32)
    ii = lax.broadcasted_iota(jnp.int32, p.shape, 0)
    jj = lax.broadcasted_iota(jnp.int32, p.shape, 1)
    keep = None
    for lv in range(nlev):
        h = t8 << lv
        m = (ii // half == lv) & (jj // half == lv) & ((ii // h) == (jj // h))
        keep = m if keep is None else keep | m
    p = jnp.where(keep, p, 0.0)
    res = jnp.dot(p.astype(BF16), jnp.concatenate(v_l, axis=0).astype(BF16),
                  preferred_element_type=F32)
    contrib = [None] * (c // t8)
    for lv in range(nlev):
        for n_, tile in enumerate(q_tiles[lv]):
            piece = res[lv * half + n_ * t8:lv * half + (n_ + 1) * t8, :]
            contrib[tile] = piece if contrib[tile] is None else contrib[tile] + piece
    zero = jnp.zeros((t8, LANES), F32)
    o = o + jnp.concatenate([zero if p_ is None else p_ for p_ in contrib], axis=0)

    gl = g[0:1, :] if reverse else g[c - 1:c, :]
    kd = (kk * jnp.exp(gl - g)).astype(BF16)
    upd = lax.dot_general(vb, kd, (((0,), (0,)), ((), ())), preferred_element_type=F32)
    st_ref[d] = st * jnp.exp(gl) + upd

    for i in range(t8):
        acc_s[d, pl.ds(start + i, t8, stride=t8), :] = acc[i]
    o_ref[sl, :] = (o + acc_s[d, sl, :]).astype(o_ref.dtype)


def _gla_kernel(layer, lbl_ref, qf_ref, ff_ref, vf_ref, qb_ref, fb_ref, vb_ref,
                of_ref, ob_ref, st_ref, q_s, k_s, v_s, lf_s, g_s, acc_s):
    tb = GLA_TBLOCK
    nc = tb // GLA_CHUNK

    @pl.when(pl.program_id(2) == 0)
    def _():
        st_ref[...] = jnp.zeros_like(st_ref)

    if layer > 0:
        lg = lbl_ref[...]
        e = jnp.exp(lg - jnp.max(lg, axis=0, keepdims=True))
        sm = e / jnp.sum(e, axis=0, keepdims=True)
        lb = sm[1]
        for l in range(2, layer + 1):
            lb = lb + sm[l]

    for d, (q_ref, f_ref, v_ref) in enumerate(((qf_ref, ff_ref, vf_ref),
                                               (qb_ref, fb_ref, vb_ref))):
        qx = q_ref[...].astype(F32)
        fx = f_ref[...]
        v_s[d] = v_ref[...].astype(F32)
        t = jnp.exp(-jnp.abs(fx))
        r = 1.0 / (1.0 + t)
        logsig = jnp.minimum(fx, 0.0) - jnp.log(1.0 + t)
        sig_neg = jnp.where(fx >= 0.0, t * r, r)
        if layer > 0:
            lbd = lb[d:d + 1, :]
            a = jnp.log(lbd)
            bb = jnp.log1p(-lbd) + logsig
            lf = jnp.maximum(a, bb) + jnp.log(1.0 + jnp.exp(-jnp.abs(a - bb)))
            kk = (1.0 - lbd) * sig_neg
        else:
            lf = logsig
            kk = sig_neg
        q_s[d] = qx * jax.nn.sigmoid(qx)
        k_s[d] = kk
        lf_s[d] = lf

    def body(ci, carry):
        sf = pl.multiple_of(ci * GLA_CHUNK, GLA_CHUNK)
        sb = pl.multiple_of((nc - 1 - ci) * GLA_CHUNK, GLA_CHUNK)
        _gla_chunk(q_s, k_s, v_s, lf_s, g_s, acc_s, of_ref, st_ref, 0, sf, False)
        _gla_chunk(q_s, k_s, v_s, lf_s, g_s, acc_s, ob_ref, st_ref, 1, sb, True)
        return carry

    lax.fori_loop(0, nc, body, 0, unroll=2)


def _gla(proj_a, proj_f, lb_logits, layer, batch, seq):
    n = proj_a.shape[0]
    tb = GLA_TBLOCK
    nb = seq // tb
    h8 = HG_HEADS

    def spec(stream, rev):
        if rev:
            return pl.BlockSpec((tb, LANES), lambda b, h, c: (b * nb + nb - 1 - c, stream * h8 + h))
        return pl.BlockSpec((tb, LANES), lambda b, h, c: (b * nb + c, stream * h8 + h))

    o_f_spec = pl.BlockSpec((tb, LANES), lambda b, h, c: (b * nb + c, h))
    o_b_spec = pl.BlockSpec((tb, LANES), lambda b, h, c: (b * nb + nb - 1 - c, h))
    return pl.pallas_call(
        functools.partial(_gla_kernel, layer),
        grid=(batch, h8, nb),
        in_specs=[pl.BlockSpec((DEPTH, 2, LANES), lambda b, h, c: (0, 0, h)),
                  spec(0, False), spec(0, False), spec(1, False),
                  spec(0, True), spec(1, True), spec(1, True)],
        out_specs=[o_f_spec, o_b_spec],
        out_shape=[jax.ShapeDtypeStruct((n, D_MODEL), BF16)] * 2,
        scratch_shapes=[pltpu.VMEM((2, HG_DK, HG_DK), F32)]
        + [pltpu.VMEM((2, tb, LANES), F32)] * 6,
        compiler_params=_cparams("arbitrary", "arbitrary", "arbitrary"),
        name="gla",
    )(lb_logits, proj_a, proj_f, proj_a, proj_a, proj_f, proj_a)


def _hgrn_out_kernel(of_ref, ob_ref, gate_ref, nw_ref, w_ref, x_ref, g_ref, b_ref, o_ref):
    o = of_ref[...].astype(F32) + ob_ref[...].astype(F32)
    parts = []
    for h in range(HG_HEADS):
        oh = o[:, h * LANES:(h + 1) * LANES]
        ms = jnp.mean(oh * oh, axis=-1, keepdims=True)
        parts.append(oh * lax.rsqrt(ms + RMS_EPS))
    gate = gate_ref[...].astype(F32)
    y = jnp.concatenate(parts, axis=-1) * nw_ref[...] * (gate * jax.nn.sigmoid(gate))
    mix = jnp.dot(y.astype(BF16), w_ref[...], preferred_element_type=F32)
    o_ref[...] = _layer_norm_rows(DN_ALPHA * x_ref[...] + mix, g_ref[...], b_ref[...])


def _hgrn_out(o_f, o_b, proj_a, norm_w, w_out, x, g, b):
    n = x.shape[0]
    tm = ROW_TILE
    row = pl.BlockSpec((tm, D_MODEL), lambda i: (i, 0))
    vec = pl.BlockSpec((1, D_MODEL), lambda i: (0, 0))
    return pl.pallas_call(
        _hgrn_out_kernel,
        grid=(n // tm,),
        in_specs=[row, row, pl.BlockSpec((tm, D_MODEL), lambda i: (i, 2)), vec,
                  pl.BlockSpec((D_MODEL, D_MODEL), lambda i: (0, 0)), row, vec, vec],
        out_specs=row,
        out_shape=jax.ShapeDtypeStruct((n, D_MODEL), F32),
        compiler_params=_cparams("arbitrary"),
        name="hgrn_out",
    )(o_f, o_b, proj_a, jnp.tile(norm_w, HG_HEADS).reshape(1, D_MODEL), w_out, x,
      g.reshape(1, D_MODEL), b.reshape(1, D_MODEL))


def _conv_out_kernel(tiles_per_seq, bg_ref, cg_ref, h_ref, cgp_ref, hp_ref, cgn_ref, hn_ref,
                     cw_ref, w_ref, x_ref, g_ref, b_ref, o_ref):
    i = pl.program_id(0)
    tm = cg_ref.shape[0]
    u = cg_ref[...].astype(F32) * h_ref[...].astype(F32)
    first = (i % tiles_per_seq) == 0
    last = (i % tiles_per_seq) == tiles_per_seq - 1
    hr = cgp_ref.shape[0]
    u_halo_prev = cgp_ref[...].astype(F32) * hp_ref[...].astype(F32)
    u_halo_next = cgn_ref[...].astype(F32) * hn_ref[...].astype(F32)
    u_prev_row = jnp.where(first, 0.0, u_halo_prev[hr - 1:hr, :])
    u_next_row = jnp.where(last, 0.0, u_halo_next[0:1, :])
    rows = lax.broadcasted_iota(jnp.int32, u.shape, 0)
    u_prev = jnp.where(rows == 0, u_prev_row, pltpu.roll(u, 1, 0))
    u_next = jnp.where(rows == tm - 1, u_next_row, pltpu.roll(u, tm - 1, 0))
    cw = cw_ref[...]
    y = u_prev * cw[0:1, :] + u * cw[1:2, :] + u_next * cw[2:3, :]
    mix = jnp.dot((bg_ref[...].astype(F32) * y).astype(BF16), w_ref[...],
                  preferred_element_type=F32)
    o_ref[...] = _layer_norm_rows(DN_ALPHA * x_ref[...] + mix, g_ref[...], b_ref[...])


def _conv_out(proj, conv_w, w_out, x, g, b, seq):
    n = x.shape[0]
    tm = ROW_TILE
    halo = BF16_SUBLANES
    rh = tm // halo
    nblk = n // halo
    row = pl.BlockSpec((tm, D_MODEL), lambda i: (i, 0))
    vec = pl.BlockSpec((1, D_MODEL), lambda i: (0, 0))

    def main(stream):
        return pl.BlockSpec((tm, D_MODEL), lambda i: (i, stream))

    def prev(stream):
        return pl.BlockSpec((halo, D_MODEL), lambda i: (jnp.maximum(i * rh - 1, 0), stream))

    def nxt(stream):
        return pl.BlockSpec((halo, D_MODEL),
                            lambda i: (jnp.minimum((i + 1) * rh, nblk - 1), stream))

    return pl.pallas_call(
        functools.partial(_conv_out_kernel, seq // tm),
        grid=(n // tm,),
        in_specs=[main(0), main(1), main(2), prev(1), prev(2), nxt(1), nxt(2),
                  pl.BlockSpec((3, D_MODEL), lambda i: (0, 0)),
                  pl.BlockSpec((D_MODEL, D_MODEL), lambda i: (0, 0)), row, vec, vec],
        out_specs=row,
        out_shape=jax.ShapeDtypeStruct((n, D_MODEL), F32),
        compiler_params=_cparams("arbitrary"),
        name="conv_out",
    )(proj, proj, proj, proj, proj, proj, proj, conv_w, w_out, x,
      g.reshape(1, D_MODEL), b.reshape(1, D_MODEL))


def _router_kernel(x_ref, w_ref, b_ref, tri_ref, o_ref, cnt_ref, carry_ref):
    @pl.when(pl.program_id(0) == 0)
    def _():
        carry_ref[...] = jnp.zeros_like(carry_ref)

    logits = jnp.dot(x_ref[...], w_ref[...], precision=lax.Precision.HIGHEST,
                     preferred_element_type=F32) + b_ref[...]
    lane = lax.broadcasted_iota(jnp.int32, logits.shape, 1)
    lane_f = lane.astype(F32)
    gl = jnp.where(lane < MOE_GROUPS, logits, NEG_BIG)
    gmax = jnp.max(gl, axis=-1, keepdims=True)
    gsum = jnp.sum(jnp.exp(gl - gmax), axis=-1, keepdims=True)
    p_group = 1.0 / gsum
    g_sel = jnp.min(jnp.where(gl == gmax, lane_f, 1e9), axis=-1, keepdims=True)
    e_lane = lane - MOE_GROUPS
    in_grp = ((lane >= MOE_GROUPS) & (lane < MOE_GROUPS + MOE_EXPERTS)
              & ((e_lane // MOE_EXPERTS_PER_GROUP).astype(F32) == g_sel))
    el = jnp.where(in_grp, logits, NEG_BIG)
    t1 = jnp.max(el, axis=-1, keepdims=True)
    i1 = jnp.min(jnp.where(el == t1, lane_f, 1e9), axis=-1, keepdims=True)
    el2 = jnp.where(lane_f == i1, NEG_BIG, el)
    t2 = jnp.max(el2, axis=-1, keepdims=True)
    i2 = jnp.min(jnp.where(el2 == t2, lane_f, 1e9), axis=-1, keepdims=True)
    z = jnp.exp(t2 - t1)
    g1 = p_group / (1.0 + z)
    g2 = g1 * z
    out = jnp.where(lane == 0, g1, 0.0)
    out = jnp.where(lane == 1, g2, out)
    out = jnp.where(lane == 2, i1 - MOE_GROUPS, out)
    out = jnp.where(lane == 3, i2 - MOE_GROUPS, out)

    sel1 = lane_f == i1
    sel2 = lane_f == i2
    onehot = jnp.where(sel1 | sel2, 1.0, 0.0)
    prefix = jnp.dot(tri_ref[...], onehot.astype(BF16), preferred_element_type=F32)
    before = prefix + carry_ref[0:1, :]
    r1 = jnp.sum(jnp.where(sel1, before, 0.0), axis=-1, keepdims=True)
    r2 = jnp.sum(jnp.where(sel2, before, 0.0), axis=-1, keepdims=True)
    out = jnp.where(lane == 4, r1, out)
    out = jnp.where(lane == 5, r2, out)
    o_ref[...] = out
    carry_ref[...] = carry_ref[...] + jnp.sum(onehot, axis=0, keepdims=True)
    cnt_ref[...] = carry_ref[...]


def _router(x, w_group, b_group, w_expert, b_expert):
    n = x.shape[0]
    tm = 512
    pad = LANES - MOE_GROUPS - MOE_EXPERTS
    w = jnp.concatenate([w_group, w_expert, jnp.zeros((D_MODEL, pad), F32)], axis=1)
    b = jnp.concatenate([b_group, b_expert, jnp.zeros((pad,), F32)]).reshape(1, LANES)
    r = jnp.arange(tm, dtype=jnp.int32)
    tri = (r[None, :] < r[:, None]).astype(BF16)
    route, cnt = pl.pallas_call(
        _router_kernel,
        grid=(n // tm,),
        in_specs=[pl.BlockSpec((tm, D_MODEL), lambda i: (i, 0)),
                  pl.BlockSpec((D_MODEL, LANES), lambda i: (0, 0)),
                  pl.BlockSpec((1, LANES), lambda i: (0, 0)),
                  pl.BlockSpec((tm, tm), lambda i: (0, 0))],
        out_specs=[pl.BlockSpec((tm, LANES), lambda i: (i, 0)),
                   pl.BlockSpec((SUBLANES, LANES), lambda i: (0, 0))],
        out_shape=[jax.ShapeDtypeStruct((n, LANES), F32),
                   jax.ShapeDtypeStruct((SUBLANES, LANES), F32)],
        scratch_shapes=[pltpu.VMEM((SUBLANES, LANES), F32)],
        compiler_params=_cparams("arbitrary"),
        name="router",
    )(x, w, b, tri)
    return route, cnt[0, MOE_GROUPS:MOE_GROUPS + MOE_EXPERTS].astype(jnp.int32)


DISPATCH_TOKENS = 256
DISPATCH_BUFS = 3


def _moe_plan(route, counts, n):
    nk = n * MOE_TOPK
    n_rows = -(-nk // MOE_BLOCK) * MOE_BLOCK + MOE_EXPERTS * MOE_BLOCK
    n_blocks = n_rows // MOE_BLOCK
    padded = (counts + MOE_BLOCK - 1) // MOE_BLOCK * MOE_BLOCK
    pad_end = jnp.cumsum(padded)
    pad_start = pad_end - padded
    e = route[:, MOE_TOPK:2 * MOE_TOPK].astype(jnp.int32)
    rank = route[:, 2 * MOE_TOPK:3 * MOE_TOPK].astype(jnp.int32)
    ids = jnp.arange(MOE_EXPERTS, dtype=jnp.int32)
    base = jnp.sum(jnp.where(e[:, :, None] == ids, pad_start, 0), axis=-1)
    dest = (base + rank).reshape(nk)
    block_start = jnp.arange(n_blocks, dtype=jnp.int32) * MOE_BLOCK
    block_expert = jnp.minimum(jnp.sum(block_start[:, None] >= pad_end[None, :], axis=1),
                               MOE_EXPERTS - 1).astype(jnp.int32)
    nact = (pad_end[-1:] // MOE_BLOCK).astype(jnp.int32)
    fill_start = (pad_start + counts).astype(jnp.int32)
    return dest, block_expert, nact, fill_start, pad_end.astype(jnp.int32), n_rows


def _dispatch_kernel(dest_ref, fs_ref, fe_ref, nact_ref, x_hbm, xs_hbm, xbuf, zbuf, lsem, sem,
                     zsem):
    i = pl.program_id(0)
    nsteps = pl.num_programs(0)
    slot = i % DISPATCH_BUFS
    ct = DISPATCH_TOKENS
    gt = ct // SUBLANES
    n_blocks = xs_hbm.shape[0] // MOE_BLOCK

    def load(step, sl):
        start = pl.multiple_of(step * gt, gt)
        return pltpu.make_async_copy(x_hbm.at[pl.ds(start, gt)], xbuf.at[sl], lsem.at[sl])

    def wait_step(sl):
        for _ in range(MOE_TOPK):
            pltpu.make_async_copy(x_hbm.at[pl.ds(0, gt)], xbuf.at[sl], sem.at[sl]).wait()

    @pl.when(i == 0)
    def _():
        for s in range(DISPATCH_BUFS - 1):
            load(s, s).start()

    load(i, slot).wait()

    def body(g, c):
        base = (i * ct + g * SUBLANES) * MOE_TOPK
        for t in range(SUBLANES):
            for k in range(MOE_TOPK):
                d = dest_ref[base + t * MOE_TOPK + k]
                pltpu.make_async_copy(xbuf.at[slot, g, pl.ds(t, 1)], xs_hbm.at[pl.ds(d, 1)],
                                      sem.at[slot]).start()
        return c
    lax.fori_loop(0, gt, body, 0)

    @pl.when(i >= 1)
    def _():
        wait_step((i - 1) % DISPATCH_BUFS)

    @pl.when(i + DISPATCH_BUFS - 1 < nsteps)
    def _():
        load(i + DISPATCH_BUFS - 1, (i + DISPATCH_BUFS - 1) % DISPATCH_BUFS).start()

    @pl.when(i == nsteps - 1)
    def _():
        wait_step(slot)
        zbuf[...] = jnp.zeros_like(zbuf)

        def fill_rows(wait):
            def per_expert(e, c):
                def per_row(r, c2):
                    cp = pltpu.make_async_copy(zbuf.at[pl.ds(0, 1)], xs_hbm.at[pl.ds(r, 1)],
                                               zsem.at[0])
                    cp.wait() if wait else cp.start()
                    return c2
                lax.fori_loop(fs_ref[e], fe_ref[e], per_row, 0)
                return c
            lax.fori_loop(0, MOE_EXPERTS, per_expert, 0)

        def fill_blocks(wait):
            def per_block(b, c):
                start = pl.multiple_of(b * MOE_BLOCK, MOE_BLOCK)
                cp = pltpu.make_async_copy(zbuf, xs_hbm.at[pl.ds(start, MOE_BLOCK)], zsem.at[1])
                cp.wait() if wait else cp.start()
                return c
            lax.fori_loop(nact_ref[0], n_blocks, per_block, 0)

        fill_rows(False)
        fill_blocks(False)
        fill_rows(True)
        fill_blocks(True)


def _dispatch(x, dest, fill_start, fill_end, nact, n_rows):
    n = x.shape[0]
    grid_spec = pltpu.PrefetchScalarGridSpec(
        num_scalar_prefetch=4,
        grid=(n // DISPATCH_TOKENS,),
        in_specs=[pl.BlockSpec(memory_space=pl.ANY)],
        out_specs=pl.BlockSpec(memory_space=pl.ANY),
        scratch_shapes=[pltpu.VMEM((DISPATCH_BUFS, DISPATCH_TOKENS // SUBLANES, SUBLANES, D_MODEL),
                                   F32),
                        pltpu.VMEM((MOE_BLOCK, D_MODEL), F32),
                        pltpu.SemaphoreType.DMA((DISPATCH_BUFS,)),
                        pltpu.SemaphoreType.DMA((DISPATCH_BUFS,)),
                        pltpu.SemaphoreType.DMA((2,))],
    )
    return pl.pallas_call(
        _dispatch_kernel,
        grid_spec=grid_spec,
        out_shape=jax.ShapeDtypeStruct((n_rows, D_MODEL), F32),
        compiler_params=_cparams("arbitrary"),
        name="dispatch",
    )(dest, fill_start, fill_end, nact, x.reshape(n // SUBLANES, SUBLANES, D_MODEL))


def _moe_ffn_kernel(be_ref, nact_ref, xs_ref, wup_ref, wdn_ref, ys_ref, wup_bf, wdn_bf):
    i = pl.program_id(0)
    active = i < nact_ref[0]
    new_expert = (i == 0) | (be_ref[i] != be_ref[jnp.maximum(i - 1, 0)])

    @pl.when(active & new_expert)
    def _():
        wup_bf[...] = wup_ref[0].astype(BF16)
        wdn_bf[...] = wdn_ref[0].astype(BF16)

    @pl.when(active)
    def _():
        hcat = jnp.dot(xs_ref[...].astype(BF16), wup_bf[...], preferred_element_type=F32)
        hg = hcat[:, :MOE_D_EXPERT]
        hu = hcat[:, MOE_D_EXPERT:]
        act = (hg * jax.nn.sigmoid(hg) * hu).astype(BF16)
        ys_ref[...] = jnp.dot(act, wdn_bf[...], preferred_element_type=F32)

    @pl.when(jnp.logical_not(active))
    def _():
        ys_ref[...] = jnp.zeros_like(ys_ref)


def _moe_ffn(xs, block_expert, nact, w_up, w_down, layer):
    n_rows = xs.shape[0]
    e0 = layer * MOE_EXPERTS
    grid_spec = pltpu.PrefetchScalarGridSpec(
        num_scalar_prefetch=2,
        grid=(n_rows // MOE_BLOCK,),
        in_specs=[pl.BlockSpec((MOE_BLOCK, D_MODEL), lambda i, be, na: (i, 0)),
                  pl.BlockSpec((1, D_MODEL, 2 * MOE_D_EXPERT),
                               lambda i, be, na: (e0 + be[i], 0, 0)),
                  pl.BlockSpec((1, MOE_D_EXPERT, D_MODEL), lambda i, be, na: (e0 + be[i], 0, 0))],
        out_specs=pl.BlockSpec((MOE_BLOCK, D_MODEL), lambda i, be, na: (i, 0)),
        scratch_shapes=[pltpu.VMEM((D_MODEL, 2 * MOE_D_EXPERT), BF16),
                        pltpu.VMEM((MOE_D_EXPERT, D_MODEL), BF16)],
    )
    return pl.pallas_call(
        _moe_ffn_kernel,
        grid_spec=grid_spec,
        out_shape=jax.ShapeDtypeStruct((n_rows, D_MODEL), F32),
        compiler_params=_cparams("arbitrary"),
        name="moe_ffn",
    )(block_expert, nact, xs, w_up, w_down)


def _combine_ln_kernel(dest_ref, x_ref, route_ref, ys_hbm, g_ref, b_ref, o_ref, ob_ref, ybuf,
                       sem):
    i = pl.program_id(0)
    nsteps = pl.num_programs(0)
    slot = i % 2
    tm = x_ref.shape[0]

    def start_gather(step, sl):
        def body(g, c):
            base = (step * tm + g * SUBLANES) * MOE_TOPK
            for t in range(SUBLANES):
                for k in range(MOE_TOPK):
                    d = dest_ref[base + t * MOE_TOPK + k]
                    pltpu.make_async_copy(ys_hbm.at[pl.ds(d, 1)], ybuf.at[sl, k, g, pl.ds(t, 1)],
                                          sem.at[sl]).start()
            return c
        lax.fori_loop(0, tm // SUBLANES, body, 0)

    @pl.when(i == 0)
    def _():
        start_gather(0, 0)

    @pl.when(i + 1 < nsteps)
    def _():
        start_gather(i + 1, 1 - slot)

    for k in range(MOE_TOPK):
        pltpu.make_async_copy(ybuf.at[1 - slot, k], ybuf.at[slot, k], sem.at[slot]).wait()
    route = route_ref[...]
    y0 = ybuf[slot, 0].reshape(tm, D_MODEL)
    y1 = ybuf[slot, 1].reshape(tm, D_MODEL)
    ffn = route[:, 0:1] * y0 + route[:, 1:2] * y1
    out = _layer_norm_rows(DN_ALPHA * x_ref[...] + ffn, g_ref[...], b_ref[...])
    o_ref[...] = out
    ob_ref[...] = out.astype(BF16)


def _combine_ln(x, route, ys, dest, g, b):
    n = x.shape[0]
    tm = ROW_TILE
    vec = pl.BlockSpec((1, D_MODEL), lambda i, d: (0, 0))
    row = pl.BlockSpec((tm, D_MODEL), lambda i, d: (i, 0))
    grid_spec = pltpu.PrefetchScalarGridSpec(
        num_scalar_prefetch=1,
        grid=(n // tm,),
        in_specs=[pl.BlockSpec((tm, D_MODEL), lambda i, d: (i, 0)),
                  pl.BlockSpec((tm, LANES), lambda i, d: (i, 0)),
                  pl.BlockSpec(memory_space=pl.ANY), vec, vec],
        out_specs=[row, row],
        scratch_shapes=[pltpu.VMEM((2, MOE_TOPK, tm // SUBLANES, SUBLANES, D_MODEL), F32),
                        pltpu.SemaphoreType.DMA((2,))],
    )
    return pl.pallas_call(
        _combine_ln_kernel,
        grid_spec=grid_spec,
        out_shape=[jax.ShapeDtypeStruct((n, D_MODEL), F32),
                   jax.ShapeDtypeStruct((n, D_MODEL), BF16)],
        compiler_params=_cparams("arbitrary"),
        name="combine_ln",
    )(dest, x, route, ys, g.reshape(1, D_MODEL), b.reshape(1, D_MODEL))


def kernel(x, hg_w_in, hg_lb_logits, hg_norm_w, hg_w_out, cv_w_in, cv_w, cv_w_out, ln_g, ln_b,
           moe_w_group, moe_b_group, moe_w_expert, moe_b_expert, moe_w_up, moe_w_down):
    batch, seq, d = x.shape
    n = batch * seq
    xf = x.reshape(n, d)
    w_up_all = moe_w_up.reshape(DEPTH * MOE_EXPERTS, D_MODEL, 2 * MOE_D_EXPERT)
    w_down_all = moe_w_down.reshape(DEPTH * MOE_EXPERTS, MOE_D_EXPERT, D_MODEL)
    xin = xf
    for layer in range(DEPTH):
        j = layer // 2
        if layer % 2 == 0:
            w = hg_w_in[j]
            w_a = jnp.concatenate([w[:, :D_MODEL], w[:, 3 * D_MODEL:]], axis=1).astype(BF16)
            w_f = w[:, D_MODEL:3 * D_MODEL].astype(BF16)
            proj_a = _matmul(xin, w_a, BF16)
            proj_f = _matmul(xin, w_f, F32)
            o_f, o_b = _gla(proj_a, proj_f, hg_lb_logits, layer, batch, seq)
            xf = _hgrn_out(o_f, o_b, proj_a, hg_norm_w[j], hg_w_out[j].astype(BF16), xf,
                           ln_g[layer, 0], ln_b[layer, 0])
        else:
            proj = _matmul(xin, cv_w_in[j].astype(BF16), BF16)
            xf = _conv_out(proj, cv_w[j], cv_w_out[j].astype(BF16), xf,
                           ln_g[layer, 0], ln_b[layer, 0], seq)
        route, counts = _router(xf, moe_w_group[layer], moe_b_group[layer],
                                moe_w_expert[layer], moe_b_expert[layer])
        dest, block_expert, nact, fill_start, fill_end, n_rows = _moe_plan(route, counts, n)
        xs = _dispatch(xf, dest, fill_start, fill_end, nact, n_rows)
        ys = _moe_ffn(xs, block_expert, nact, w_up_all, w_down_all, layer)
        xf, xin = _combine_ln(xf, route, ys, dest, ln_g[layer, 1], ln_b[layer, 1])
    return xf.reshape(batch, seq, d)
```

```python
import functools

import jax
import jax.numpy as jnp
from jax import lax
from jax.experimental import pallas as pl
from jax.experimental.pallas import tpu as pltpu

D_MODEL = 1024
DEPTH = 4
HG_DK = 128
HG_HEADS = D_MODEL // HG_DK
HG_STREAMS = 5
CONV_STREAMS = 3
MOE_GROUPS = 4
MOE_EXPERTS_PER_GROUP = 8
MOE_EXPERTS = MOE_GROUPS * MOE_EXPERTS_PER_GROUP
MOE_TOPK = 2
MOE_D_EXPERT = D_MODEL // 2
MOE_BLOCK = 256
DN_ALPHA = (2.0 * DEPTH) ** 0.25
LN_EPS = 1e-5
RMS_EPS = 1e-6

LANES = 128
SUBLANES = 8
BF16_SUBLANES = 16
VMEM_LIMIT = 48 * 1024 * 1024
GLA_CHUNK = 64
GLA_TBLOCK = 512
GLA_UNROLL = 2
ROW_TILE = 256
NEG_BIG = -1e30

BF16 = jnp.bfloat16
F32 = jnp.float32


def _cparams(*sem):
    return pltpu.CompilerParams(dimension_semantics=sem, vmem_limit_bytes=VMEM_LIMIT)


def _mm_kernel(x_ref, w_ref, o_ref):
    o_ref[...] = jnp.dot(x_ref[...].astype(BF16), w_ref[...],
                         preferred_element_type=F32).astype(o_ref.dtype)


def _matmul(x, w, out_dtype, tm=512, tn=1024):
    n, k = x.shape
    nn = w.shape[1]
    return pl.pallas_call(
        _mm_kernel,
        grid=(nn // tn, n // tm),
        in_specs=[pl.BlockSpec((tm, k), lambda j, i: (i, 0)),
                  pl.BlockSpec((k, tn), lambda j, i: (0, j))],
        out_specs=pl.BlockSpec((tm, tn), lambda j, i: (i, j)),
        out_shape=jax.ShapeDtypeStruct((n, nn), out_dtype),
        compiler_params=_cparams("arbitrary", "arbitrary"),
        name="in_proj",
    )(x, w)


PACKED = D_MODEL // 2


def _pack_rows(x):
    lo = pltpu.bitcast(x[:, :PACKED].astype(BF16).astype(F32), jnp.int32)
    hi = pltpu.bitcast(x[:, PACKED:].astype(BF16).astype(F32), jnp.int32)
    return lax.shift_right_logical(lo, 16) | hi


def _unpack_rows(w):
    lo = pltpu.bitcast(lax.shift_left(w, 16), F32)
    hi = pltpu.bitcast(w & jnp.int32(-65536), F32)
    return lo, hi


def _layer_norm_rows(y, g, b):
    mu = jnp.mean(y, axis=-1, keepdims=True)
    yc = y - mu
    var = jnp.mean(yc * yc, axis=-1, keepdims=True)
    return yc * lax.rsqrt(var + LN_EPS) * g + b


def _rows(x, blocks, size):
    parts = [x[b * size:(b + 1) * size, :] for b in blocks]
    return parts[0] if len(parts) == 1 else jnp.concatenate(parts, axis=0)


def _gla_chunk(q_s, k_s, v_s, lf_s, g_s, acc_s, o_ref, st_ref, d, start, slot, reverse):
    c = GLA_CHUNK
    t8 = SUBLANES
    nt_dims = (((1,), (1,)), ((), ()))

    def tt(ref, i):
        return ref[d, pl.ds(start + i, t8, stride=t8), :]

    lf = [tt(lf_s, i) for i in range(t8)]
    qt = [tt(q_s, i) for i in range(t8)]
    kt = [tt(k_s, i) for i in range(t8)]
    vt = [tt(v_s, i) for i in range(t8)]

    gi = [None] * t8
    prev = None
    for i in (range(t8 - 1, -1, -1) if reverse else range(t8)):
        gi[i] = lf[i] if prev is None else prev + lf[i]
        prev = gi[i]
    tot = prev
    sub = lax.broadcasted_iota(jnp.int32, (t8, LANES), 0)
    incl = tot
    s = 1
    while s < t8:
        if reverse:
            incl = incl + jnp.where(sub + s < t8, pltpu.roll(incl, t8 - s, 0), 0.0)
        else:
            incl = incl + jnp.where(sub >= s, pltpu.roll(incl, s, 0), 0.0)
        s *= 2
    excl = incl - tot
    for i in range(t8):
        g_s[slot, pl.ds(i, t8, stride=t8), :] = gi[i] + excl

    acc = []
    for i in range(t8):
        a_i = jnp.sum(qt[i] * kt[i], axis=-1, keepdims=True) * vt[i]
        for r in (range(i + 1, t8) if reverse else range(i)):
            w = qt[i] * kt[r] * jnp.exp(gi[i] - gi[r])
            a_i = a_i + jnp.sum(w, axis=-1, keepdims=True) * vt[r]
        acc.append(a_i)

    sl = pl.ds(start, c)
    q = q_s[d, sl, :]
    kk = k_s[d, sl, :]
    v = v_s[d, sl, :]
    g = g_s[slot]
    vb = v.astype(BF16)
    st = st_ref[d]

    o = lax.dot_general((q * jnp.exp(g)).astype(BF16), st.astype(BF16), nt_dims,
                        preferred_element_type=F32)

    half = c // 2
    qd_l, kd_l, v_l, q_tiles = [], [], [], []
    b = 2 * t8
    while b <= c:
        h = b // 2
        nblk = c // b
        lo = [2 * m for m in range(nblk)]
        hi = [2 * m + 1 for m in range(nblk)]
        q_half, k_half = (lo, hi) if reverse else (hi, lo)
        refs = [g[m * b + h:m * b + h + 1, :] if reverse else g[m * b + h - 1:m * b + h, :]
                for m in range(nblk)]
        gref = jnp.concatenate([jnp.broadcast_to(r, (h, LANES)) for r in refs], axis=0) \
            if nblk > 1 else jnp.broadcast_to(refs[0], (h, LANES))
        qd_l.append(_rows(q, q_half, h) * jnp.exp(_rows(g, q_half, h) - gref))
        kd_l.append(_rows(kk, k_half, h) * jnp.exp(gref - _rows(g, k_half, h)))
        v_l.append(_rows(v, k_half, h))
        q_tiles.append([hb * (h // t8) + t for hb in q_half for t in range(h // t8)])
        b *= 2
    nlev = len(qd_l)
    p = lax.dot_general(jnp.concatenate(qd_l, axis=0).astype(BF16),
                        jnp.concatenate(kd_l, axis=0).astype(BF16), nt_dims,
                        preferred_element_type=F32)
    ii = lax.broadcasted_iota(jnp.int32, p.shape, 0)
    jj = lax.broadcasted_iota(jnp.int32, p.shape, 1)
    keep = None
    for lv in range(nlev):
        h = t8 << lv
        m = (ii // half == lv) & (jj // half == lv) & ((ii // h) == (jj // h))
        keep = m if keep is None else keep | m
    p = jnp.where(keep, p, 0.0)
    res = jnp.dot(p.astype(BF16), jnp.concatenate(v_l, axis=0).astype(BF16),
                  preferred_element_type=F32)
    contrib = [None] * (c // t8)
    for lv in range(nlev):
        for n_, tile in enumerate(q_tiles[lv]):
            piece = res[lv * half + n_ * t8:lv * half + (n_ + 1) * t8, :]
            contrib[tile] = piece if contrib[tile] is None else contrib[tile] + piece
    zero = jnp.zeros((t8, LANES), F32)
    o = o + jnp.concatenate([zero if p_ is None else p_ for p_ in contrib], axis=0)

    gl = g[0:1, :] if reverse else g[c - 1:c, :]
    kd = (kk * jnp.exp(gl - g)).astype(BF16)
    upd = lax.dot_general(vb, kd, (((0,), (0,)), ((), ())), preferred_element_type=F32)
    st_ref[d] = st * jnp.exp(gl) + upd

    for i in range(t8):
        acc_s[slot, pl.ds(i, t8, stride=t8), :] = acc[i]
    o_ref[sl, :] = (o + acc_s[slot]).astype(o_ref.dtype)


def _gla_kernel(layer, lbl_ref, qf_ref, ff_ref, vf_ref, qb_ref, fb_ref, vb_ref,
                of_ref, ob_ref, st_ref, q_s, k_s, v_s, lf_s, g_s, acc_s):
    tb = GLA_TBLOCK
    nc = tb // GLA_CHUNK

    @pl.when(pl.program_id(2) == 0)
    def _():
        st_ref[...] = jnp.zeros_like(st_ref)

    if layer > 0:
        lg = lbl_ref[...]
        e = jnp.exp(lg - jnp.max(lg, axis=0, keepdims=True))
        sm = e / jnp.sum(e, axis=0, keepdims=True)
        lb = sm[1]
        for l in range(2, layer + 1):
            lb = lb + sm[l]

    for d, (q_ref, f_ref, v_ref) in enumerate(((qf_ref, ff_ref, vf_ref),
                                               (qb_ref, fb_ref, vb_ref))):
        qx = q_ref[...].astype(F32)
        fx = f_ref[...]
        v_s[d] = v_ref[...].astype(F32)
        t = jnp.exp(-jnp.abs(fx))
        r = 1.0 / (1.0 + t)
        logsig = jnp.minimum(fx, 0.0) - jnp.log(1.0 + t)
        sig_neg = jnp.where(fx >= 0.0, t * r, r)
        if layer > 0:
            lbd = lb[d:d + 1, :]
            a = jnp.log(lbd)
            bb = jnp.log1p(-lbd) + logsig
            lf = jnp.maximum(a, bb) + jnp.log(1.0 + jnp.exp(-jnp.abs(a - bb)))
            kk = (1.0 - lbd) * sig_neg
        else:
            lf = logsig
            kk = sig_neg
        q_s[d] = qx * jax.nn.sigmoid(qx)
        k_s[d] = kk
        lf_s[d] = lf

    def body(ci, carry):
        for u in range(GLA_UNROLL):
            cf = ci * GLA_UNROLL + u
            sf = pl.multiple_of(cf * GLA_CHUNK, GLA_CHUNK)
            sb = pl.multiple_of((nc - 1 - cf) * GLA_CHUNK, GLA_CHUNK)
            _gla_chunk(q_s, k_s, v_s, lf_s, g_s, acc_s, of_ref, st_ref, 0, sf, 2 * u, False)
            _gla_chunk(q_s, k_s, v_s, lf_s, g_s, acc_s, ob_ref, st_ref, 1, sb, 2 * u + 1, True)
        return carry

    lax.fori_loop(0, nc // GLA_UNROLL, body, 0)


def _gla(proj_a, proj_f, lb_logits, layer, batch, seq):
    n = proj_a.shape[0]
    tb = GLA_TBLOCK
    nb = seq // tb
    h8 = HG_HEADS

    def spec(stream, rev):
        if rev:
            return pl.BlockSpec((tb, LANES), lambda b, h, c: (b * nb + nb - 1 - c, stream * h8 + h))
        return pl.BlockSpec((tb, LANES), lambda b, h, c: (b * nb + c, stream * h8 + h))

    o_f_spec = pl.BlockSpec((tb, LANES), lambda b, h, c: (b * nb + c, h))
    o_b_spec = pl.BlockSpec((tb, LANES), lambda b, h, c: (b * nb + nb - 1 - c, h))
    return pl.pallas_call(
        functools.partial(_gla_kernel, layer),
        grid=(batch, h8, nb),
        in_specs=[pl.BlockSpec((DEPTH, 2, LANES), lambda b, h, c: (0, 0, h)),
                  spec(0, False), spec(0, False), spec(1, False),
                  spec(0, True), spec(1, True), spec(1, True)],
        out_specs=[o_f_spec, o_b_spec],
        out_shape=[jax.ShapeDtypeStruct((n, D_MODEL), BF16)] * 2,
        scratch_shapes=[pltpu.VMEM((2, HG_DK, HG_DK), F32)]
        + [pltpu.VMEM((2, tb, LANES), F32)] * 4
        + [pltpu.VMEM((2 * GLA_UNROLL, GLA_CHUNK, LANES), F32)] * 2,
        compiler_params=_cparams("arbitrary", "arbitrary", "arbitrary"),
        name="gla",
    )(lb_logits, proj_a, proj_f, proj_a, proj_a, proj_f, proj_a)


def _hgrn_out_kernel(of_ref, ob_ref, gate_ref, nw_ref, w_ref, x_ref, g_ref, b_ref, o_ref,
                     op_ref):
    o = of_ref[...].astype(F32) + ob_ref[...].astype(F32)
    parts = []
    for h in range(HG_HEADS):
        oh = o[:, h * LANES:(h + 1) * LANES]
        ms = jnp.mean(oh * oh, axis=-1, keepdims=True)
        parts.append(oh * lax.rsqrt(ms + RMS_EPS))
    gate = gate_ref[...].astype(F32)
    y = jnp.concatenate(parts, axis=-1) * nw_ref[...] * (gate * jax.nn.sigmoid(gate))
    mix = jnp.dot(y.astype(BF16), w_ref[...], preferred_element_type=F32)
    out = _layer_norm_rows(DN_ALPHA * x_ref[...] + mix, g_ref[...], b_ref[...])
    o_ref[...] = out
    op_ref[...] = _pack_rows(out)


def _hgrn_out(o_f, o_b, proj_a, norm_w, w_out, x, g, b):
    n = x.shape[0]
    tm = ROW_TILE
    row = pl.BlockSpec((tm, D_MODEL), lambda i: (i, 0))
    vec = pl.BlockSpec((1, D_MODEL), lambda i: (0, 0))
    return pl.pallas_call(
        _hgrn_out_kernel,
        grid=(n // tm,),
        in_specs=[row, row, pl.BlockSpec((tm, D_MODEL), lambda i: (i, 2)), vec,
                  pl.BlockSpec((D_MODEL, D_MODEL), lambda i: (0, 0)), row, vec, vec],
        out_specs=[row, pl.BlockSpec((tm, PACKED), lambda i: (i, 0))],
        out_shape=[jax.ShapeDtypeStruct((n, D_MODEL), F32),
                   jax.ShapeDtypeStruct((n, PACKED), jnp.int32)],
        compiler_params=_cparams("arbitrary"),
        name="hgrn_out",
    )(o_f, o_b, proj_a, jnp.tile(norm_w, HG_HEADS).reshape(1, D_MODEL), w_out, x,
      g.reshape(1, D_MODEL), b.reshape(1, D_MODEL))


def _conv_out_kernel(tiles_per_seq, bg_ref, cg_ref, h_ref, cgp_ref, hp_ref, cgn_ref, hn_ref,
                     cw_ref, w_ref, x_ref, g_ref, b_ref, o_ref, op_ref):
    i = pl.program_id(0)
    tm = cg_ref.shape[0]
    u = cg_ref[...].astype(F32) * h_ref[...].astype(F32)
    first = (i % tiles_per_seq) == 0
    last = (i % tiles_per_seq) == tiles_per_seq - 1
    hr = cgp_ref.shape[0]
    u_halo_prev = cgp_ref[...].astype(F32) * hp_ref[...].astype(F32)
    u_halo_next = cgn_ref[...].astype(F32) * hn_ref[...].astype(F32)
    u_prev_row = jnp.where(first, 0.0, u_halo_prev[hr - 1:hr, :])
    u_next_row = jnp.where(last, 0.0, u_halo_next[0:1, :])
    rows = lax.broadcasted_iota(jnp.int32, u.shape, 0)
    u_prev = jnp.where(rows == 0, u_prev_row, pltpu.roll(u, 1, 0))
    u_next = jnp.where(rows == tm - 1, u_next_row, pltpu.roll(u, tm - 1, 0))
    cw = cw_ref[...]
    y = u_prev * cw[0:1, :] + u * cw[1:2, :] + u_next * cw[2:3, :]
    mix = jnp.dot((bg_ref[...].astype(F32) * y).astype(BF16), w_ref[...],
                  preferred_element_type=F32)
    out = _layer_norm_rows(DN_ALPHA * x_ref[...] + mix, g_ref[...], b_ref[...])
    o_ref[...] = out
    op_ref[...] = _pack_rows(out)


def _conv_out(proj, conv_w, w_out, x, g, b, seq):
    n = x.shape[0]
    tm = ROW_TILE
    halo = BF16_SUBLANES
    rh = tm // halo
    nblk = n // halo
    row = pl.BlockSpec((tm, D_MODEL), lambda i: (i, 0))
    vec = pl.BlockSpec((1, D_MODEL), lambda i: (0, 0))

    def main(stream):
        return pl.BlockSpec((tm, D_MODEL), lambda i: (i, stream))

    def prev(stream):
        return pl.BlockSpec((halo, D_MODEL), lambda i: (jnp.maximum(i * rh - 1, 0), stream))

    def nxt(stream):
        return pl.BlockSpec((halo, D_MODEL),
                            lambda i: (jnp.minimum((i + 1) * rh, nblk - 1), stream))

    return pl.pallas_call(
        functools.partial(_conv_out_kernel, seq // tm),
        grid=(n // tm,),
        in_specs=[main(0), main(1), main(2), prev(1), prev(2), nxt(1), nxt(2),
                  pl.BlockSpec((3, D_MODEL), lambda i: (0, 0)),
                  pl.BlockSpec((D_MODEL, D_MODEL), lambda i: (0, 0)), row, vec, vec],
        out_specs=[row, pl.BlockSpec((tm, PACKED), lambda i: (i, 0))],
        out_shape=[jax.ShapeDtypeStruct((n, D_MODEL), F32),
                   jax.ShapeDtypeStruct((n, PACKED), jnp.int32)],
        compiler_params=_cparams("arbitrary"),
        name="conv_out",
    )(proj, proj, proj, proj, proj, proj, proj, conv_w, w_out, x,
      g.reshape(1, D_MODEL), b.reshape(1, D_MODEL))


def _router_kernel(x_ref, w_ref, b_ref, tri_ref, o_ref, cnt_ref, carry_ref):
    @pl.when(pl.program_id(0) == 0)
    def _():
        carry_ref[...] = jnp.zeros_like(carry_ref)

    logits = jnp.dot(x_ref[...], w_ref[...], precision=lax.Precision.HIGHEST,
                     preferred_element_type=F32) + b_ref[...]
    lane = lax.broadcasted_iota(jnp.int32, logits.shape, 1)
    lane_f = lane.astype(F32)
    gl = jnp.where(lane < MOE_GROUPS, logits, NEG_BIG)
    gmax = jnp.max(gl, axis=-1, keepdims=True)
    gsum = jnp.sum(jnp.exp(gl - gmax), axis=-1, keepdims=True)
    p_group = 1.0 / gsum
    g_sel = jnp.min(jnp.where(gl == gmax, lane_f, 1e9), axis=-1, keepdims=True)
    e_lane = lane - MOE_GROUPS
    in_grp = ((lane >= MOE_GROUPS) & (lane < MOE_GROUPS + MOE_EXPERTS)
              & ((e_lane // MOE_EXPERTS_PER_GROUP).astype(F32) == g_sel))
    el = jnp.where(in_grp, logits, NEG_BIG)
    t1 = jnp.max(el, axis=-1, keepdims=True)
    i1 = jnp.min(jnp.where(el == t1, lane_f, 1e9), axis=-1, keepdims=True)
    el2 = jnp.where(lane_f == i1, NEG_BIG, el)
    t2 = jnp.max(el2, axis=-1, keepdims=True)
    i2 = jnp.min(jnp.where(el2 == t2, lane_f, 1e9), axis=-1, keepdims=True)
    z = jnp.exp(t2 - t1)
    g1 = p_group / (1.0 + z)
    g2 = g1 * z
    out = jnp.where(lane == 0, g1, 0.0)
    out = jnp.where(lane == 1, g2, out)
    out = jnp.where(lane == 2, i1 - MOE_GROUPS, out)
    out = jnp.where(lane == 3, i2 - MOE_GROUPS, out)

    sel1 = lane_f == i1
    sel2 = lane_f == i2
    onehot = jnp.where(sel1 | sel2, 1.0, 0.0)
    prefix = jnp.dot(tri_ref[...], onehot.astype(BF16), preferred_element_type=F32)
    before = prefix + carry_ref[0:1, :]
    r1 = jnp.sum(jnp.where(sel1, before, 0.0), axis=-1, keepdims=True)
    r2 = jnp.sum(jnp.where(sel2, before, 0.0), axis=-1, keepdims=True)
    out = jnp.where(lane == 4, r1, out)
    out = jnp.where(lane == 5, r2, out)
    o_ref[...] = out
    carry_ref[...] = carry_ref[...] + jnp.sum(onehot, axis=0, keepdims=True)
    cnt_ref[...] = carry_ref[...]


def _router(x, w_group, b_group, w_expert, b_expert):
    n = x.shape[0]
    tm = 512
    pad = LANES - MOE_GROUPS - MOE_EXPERTS
    w = jnp.concatenate([w_group, w_expert, jnp.zeros((D_MODEL, pad), F32)], axis=1)
    b = jnp.concatenate([b_group, b_expert, jnp.zeros((pad,), F32)]).reshape(1, LANES)
    r = jnp.arange(tm, dtype=jnp.int32)
    tri = (r[None, :] < r[:, None]).astype(BF16)
    route, cnt = pl.pallas_call(
        _router_kernel,
        grid=(n // tm,),
        in_specs=[pl.BlockSpec((tm, D_MODEL), lambda i: (i, 0)),
                  pl.BlockSpec((D_MODEL, LANES), lambda i: (0, 0)),
                  pl.BlockSpec((1, LANES), lambda i: (0, 0)),
                  pl.BlockSpec((tm, tm), lambda i: (0, 0))],
        out_specs=[pl.BlockSpec((tm, LANES), lambda i: (i, 0)),
                   pl.BlockSpec((SUBLANES, LANES), lambda i: (0, 0))],
        out_shape=[jax.ShapeDtypeStruct((n, LANES), F32),
                   jax.ShapeDtypeStruct((SUBLANES, LANES), F32)],
        scratch_shapes=[pltpu.VMEM((SUBLANES, LANES), F32)],
        compiler_params=_cparams("arbitrary"),
        name="router",
    )(x, w, b, tri)
    return route, cnt[0, MOE_GROUPS:MOE_GROUPS + MOE_EXPERTS].astype(jnp.int32)


DISPATCH_TOKENS = 256
DISPATCH_BUFS = 3


def _moe_plan(route, counts, n):
    nk = n * MOE_TOPK
    n_rows = -(-nk // MOE_BLOCK) * MOE_BLOCK + MOE_EXPERTS * MOE_BLOCK
    n_blocks = n_rows // MOE_BLOCK
    padded = (counts + MOE_BLOCK - 1) // MOE_BLOCK * MOE_BLOCK
    pad_end = jnp.cumsum(padded)
    pad_start = pad_end - padded
    e = route[:, MOE_TOPK:2 * MOE_TOPK].astype(jnp.int32)
    rank = route[:, 2 * MOE_TOPK:3 * MOE_TOPK].astype(jnp.int32)
    ids = jnp.arange(MOE_EXPERTS, dtype=jnp.int32)
    base = jnp.sum(jnp.where(e[:, :, None] == ids, pad_start, 0), axis=-1)
    dest = (base + rank).reshape(nk)
    block_start = jnp.arange(n_blocks, dtype=jnp.int32) * MOE_BLOCK
    block_expert = jnp.minimum(jnp.sum(block_start[:, None] >= pad_end[None, :], axis=1),
                               MOE_EXPERTS - 1).astype(jnp.int32)
    nact = (pad_end[-1:] // MOE_BLOCK).astype(jnp.int32)
    fill_start = (pad_start + counts).astype(jnp.int32)
    return dest, block_expert, nact, fill_start, pad_end.astype(jnp.int32), n_rows


def _dispatch_kernel(dest_ref, fs_ref, fe_ref, nact_ref, x_hbm, xs_hbm, xbuf, zbuf, lsem, sem,
                     zsem):
    i = pl.program_id(0)
    nsteps = pl.num_programs(0)
    slot = i % DISPATCH_BUFS
    ct = DISPATCH_TOKENS
    gt = ct // SUBLANES
    n_blocks = xs_hbm.shape[0] // MOE_BLOCK

    def load(step, sl):
        start = pl.multiple_of(step * gt, gt)
        return pltpu.make_async_copy(x_hbm.at[pl.ds(start, gt)], xbuf.at[sl], lsem.at[sl])

    def wait_step(sl):
        for _ in range(MOE_TOPK):
            pltpu.make_async_copy(x_hbm.at[pl.ds(0, gt)], xbuf.at[sl], sem.at[sl]).wait()

    @pl.when(i == 0)
    def _():
        for s in range(DISPATCH_BUFS - 1):
            load(s, s).start()

    load(i, slot).wait()

    def body(g, c):
        base = (i * ct + g * SUBLANES) * MOE_TOPK
        for t in range(SUBLANES):
            for k in range(MOE_TOPK):
                d = dest_ref[base + t * MOE_TOPK + k]
                pltpu.make_async_copy(xbuf.at[slot, g, pl.ds(t, 1)], xs_hbm.at[pl.ds(d, 1)],
                                      sem.at[slot]).start()
        return c
    lax.fori_loop(0, gt, body, 0)

    @pl.when(i >= 1)
    def _():
        wait_step((i - 1) % DISPATCH_BUFS)

    @pl.when(i + DISPATCH_BUFS - 1 < nsteps)
    def _():
        load(i + DISPATCH_BUFS - 1, (i + DISPATCH_BUFS - 1) % DISPATCH_BUFS).start()

    @pl.when(i == nsteps - 1)
    def _():
        wait_step(slot)
        zbuf[...] = jnp.zeros_like(zbuf)

        def fill_rows(wait):
            def per_expert(e, c):
                def per_row(r, c2):
                    cp = pltpu.make_async_copy(zbuf.at[pl.ds(0, 1)], xs_hbm.at[pl.ds(r, 1)],
                                               zsem.at[0])
                    cp.wait() if wait else cp.start()
                    return c2
                lax.fori_loop(fs_ref[e], fe_ref[e], per_row, 0)
                return c
            lax.fori_loop(0, MOE_EXPERTS, per_expert, 0)

        def fill_blocks(wait):
            def per_block(b, c):
                start = pl.multiple_of(b * MOE_BLOCK, MOE_BLOCK)
                cp = pltpu.make_async_copy(zbuf, xs_hbm.at[pl.ds(start, MOE_BLOCK)], zsem.at[1])
                cp.wait() if wait else cp.start()
                return c
            lax.fori_loop(nact_ref[0], n_blocks, per_block, 0)

        fill_rows(False)
        fill_blocks(False)
        fill_rows(True)
        fill_blocks(True)


def _dispatch(x, dest, fill_start, fill_end, nact, n_rows):
    n, width = x.shape
    grid_spec = pltpu.PrefetchScalarGridSpec(
        num_scalar_prefetch=4,
        grid=(n // DISPATCH_TOKENS,),
        in_specs=[pl.BlockSpec(memory_space=pl.ANY)],
        out_specs=pl.BlockSpec(memory_space=pl.ANY),
        scratch_shapes=[pltpu.VMEM((DISPATCH_BUFS, DISPATCH_TOKENS // SUBLANES, SUBLANES, width),
                                   x.dtype),
                        pltpu.VMEM((MOE_BLOCK, width), x.dtype),
                        pltpu.SemaphoreType.DMA((DISPATCH_BUFS,)),
                        pltpu.SemaphoreType.DMA((DISPATCH_BUFS,)),
                        pltpu.SemaphoreType.DMA((2,))],
    )
    return pl.pallas_call(
        _dispatch_kernel,
        grid_spec=grid_spec,
        out_shape=jax.ShapeDtypeStruct((n_rows, width), x.dtype),
        compiler_params=_cparams("arbitrary"),
        name="dispatch",
    )(dest, fill_start, fill_end, nact, x.reshape(n // SUBLANES, SUBLANES, width))


def _moe_ffn_kernel(be_ref, nact_ref, xs_ref, wup_ref, wdn_ref, ys_ref, wup_bf, wdn_bf):
    i = pl.program_id(0)
    active = i < nact_ref[0]
    new_expert = (i == 0) | (be_ref[i] != be_ref[jnp.maximum(i - 1, 0)])

    @pl.when(active & new_expert)
    def _():
        wup_bf[...] = wup_ref[0].astype(BF16)
        wdn_bf[...] = wdn_ref[0].astype(BF16)

    @pl.when(active)
    def _():
        x_lo, x_hi = _unpack_rows(xs_ref[...])
        xb = jnp.concatenate([x_lo.astype(BF16), x_hi.astype(BF16)], axis=1)
        hcat = jnp.dot(xb, wup_bf[...], preferred_element_type=F32)
        hg = hcat[:, :MOE_D_EXPERT]
        hu = hcat[:, MOE_D_EXPERT:]
        act = (hg * jax.nn.sigmoid(hg) * hu).astype(BF16)
        ys_ref[...] = _pack_rows(jnp.dot(act, wdn_bf[...], preferred_element_type=F32))

    @pl.when(jnp.logical_not(active))
    def _():
        ys_ref[...] = jnp.zeros_like(ys_ref)


def _moe_ffn(xs, block_expert, nact, w_up, w_down, layer):
    n_rows = xs.shape[0]
    e0 = layer * MOE_EXPERTS
    grid_spec = pltpu.PrefetchScalarGridSpec(
        num_scalar_prefetch=2,
        grid=(n_rows // MOE_BLOCK,),
        in_specs=[pl.BlockSpec((MOE_BLOCK, PACKED), lambda i, be, na: (i, 0)),
                  pl.BlockSpec((1, D_MODEL, 2 * MOE_D_EXPERT),
                               lambda i, be, na: (e0 + be[i], 0, 0)),
                  pl.BlockSpec((1, MOE_D_EXPERT, D_MODEL), lambda i, be, na: (e0 + be[i], 0, 0))],
        out_specs=pl.BlockSpec((MOE_BLOCK, PACKED), lambda i, be, na: (i, 0)),
        scratch_shapes=[pltpu.VMEM((D_MODEL, 2 * MOE_D_EXPERT), BF16),
                        pltpu.VMEM((MOE_D_EXPERT, D_MODEL), BF16)],
    )
    return pl.pallas_call(
        _moe_ffn_kernel,
        grid_spec=grid_spec,
        out_shape=jax.ShapeDtypeStruct((n_rows, PACKED), jnp.int32),
        compiler_params=_cparams("arbitrary"),
        name="moe_ffn",
    )(block_expert, nact, xs, w_up, w_down)


def _combine_ln_kernel(dest_ref, x_ref, route_ref, ys_hbm, g_ref, b_ref, o_ref, ob_ref, ybuf,
                       sem):
    i = pl.program_id(0)
    nsteps = pl.num_programs(0)
    slot = i % 2
    tm = x_ref.shape[0]

    def start_gather(step, sl):
        def body(g, c):
            base = (step * tm + g * SUBLANES) * MOE_TOPK
            for t in range(SUBLANES):
                for k in range(MOE_TOPK):
                    d = dest_ref[base + t * MOE_TOPK + k]
                    pltpu.make_async_copy(ys_hbm.at[pl.ds(d, 1)], ybuf.at[sl, k, g, pl.ds(t, 1)],
                                          sem.at[sl]).start()
            return c
        lax.fori_loop(0, tm // SUBLANES, body, 0)

    @pl.when(i == 0)
    def _():
        start_gather(0, 0)

    @pl.when(i + 1 < nsteps)
    def _():
        start_gather(i + 1, 1 - slot)

    for k in range(MOE_TOPK):
        pltpu.make_async_copy(ybuf.at[1 - slot, k], ybuf.at[slot, k], sem.at[slot]).wait()
    route = route_ref[...]
    y0_lo, y0_hi = _unpack_rows(ybuf[slot, 0].reshape(tm, PACKED))
    y1_lo, y1_hi = _unpack_rows(ybuf[slot, 1].reshape(tm, PACKED))
    g0 = route[:, 0:1]
    g1 = route[:, 1:2]
    ffn = jnp.concatenate([g0 * y0_lo + g1 * y1_lo, g0 * y0_hi + g1 * y1_hi], axis=1)
    out = _layer_norm_rows(DN_ALPHA * x_ref[...] + ffn, g_ref[...], b_ref[...])
    o_ref[...] = out
    ob_ref[...] = out.astype(BF16)


def _combine_ln(x, route, ys, dest, g, b):
    n = x.shape[0]
    tm = ROW_TILE
    vec = pl.BlockSpec((1, D_MODEL), lambda i, d: (0, 0))
    row = pl.BlockSpec((tm, D_MODEL), lambda i, d: (i, 0))
    grid_spec = pltpu.PrefetchScalarGridSpec(
        num_scalar_prefetch=1,
        grid=(n // tm,),
        in_specs=[pl.BlockSpec((tm, D_MODEL), lambda i, d: (i, 0)),
                  pl.BlockSpec((tm, LANES), lambda i, d: (i, 0)),
                  pl.BlockSpec(memory_space=pl.ANY), vec, vec],
        out_specs=[row, row],
        scratch_shapes=[pltpu.VMEM((2, MOE_TOPK, tm // SUBLANES, SUBLANES, PACKED), jnp.int32),
                        pltpu.SemaphoreType.DMA((2,))],
    )
    return pl.pallas_call(
        _combine_ln_kernel,
        grid_spec=grid_spec,
        out_shape=[jax.ShapeDtypeStruct((n, D_MODEL), F32),
                   jax.ShapeDtypeStruct((n, D_MODEL), BF16)],
        compiler_params=_cparams("arbitrary"),
        name="combine_ln",
    )(dest, x, route, ys, g.reshape(1, D_MODEL), b.reshape(1, D_MODEL))


def kernel(x, hg_w_in, hg_lb_logits, hg_norm_w, hg_w_out, cv_w_in, cv_w, cv_w_out, ln_g, ln_b,
           moe_w_group, moe_b_group, moe_w_expert, moe_b_expert, moe_w_up, moe_w_down):
    batch, seq, d = x.shape
    n = batch * seq
    xf = x.reshape(n, d)
    w_up_all = moe_w_up.reshape(DEPTH * MOE_EXPERTS, D_MODEL, 2 * MOE_D_EXPERT)
    w_down_all = moe_w_down.reshape(DEPTH * MOE_EXPERTS, MOE_D_EXPERT, D_MODEL)
    xin = xf
    for layer in range(DEPTH):
        j = layer // 2
        if layer % 2 == 0:
            w = hg_w_in[j]
            w_a = jnp.concatenate([w[:, :D_MODEL], w[:, 3 * D_MODEL:]], axis=1).astype(BF16)
            w_f = w[:, D_MODEL:3 * D_MODEL].astype(BF16)
            proj_a = _matmul(xin, w_a, BF16)
            proj_f = _matmul(xin, w_f, F32)
            o_f, o_b = _gla(proj_a, proj_f, hg_lb_logits, layer, batch, seq)
            xf, xpk = _hgrn_out(o_f, o_b, proj_a, hg_norm_w[j], hg_w_out[j].astype(BF16), xf,
                                ln_g[layer, 0], ln_b[layer, 0])
        else:
            proj = _matmul(xin, cv_w_in[j].astype(BF16), BF16)
            xf, xpk = _conv_out(proj, cv_w[j], cv_w_out[j].astype(BF16), xf,
                                ln_g[layer, 0], ln_b[layer, 0], seq)
        route, counts = _router(xf, moe_w_group[layer], moe_b_group[layer],
                                moe_w_expert[layer], moe_b_expert[layer])
        dest, block_expert, nact, fill_start, fill_end, n_rows = _moe_plan(route, counts, n)
        xs = _dispatch(xpk, dest, fill_start, fill_end, nact, n_rows)
        ys = _moe_ffn(xs, block_expert, nact, w_up_all, w_down_all, layer)
        xf, xin = _combine_ln(xf, route, ys, dest, ln_g[layer, 1], ln_b[layer, 1])
    return xf.reshape(batch, seq, d)
```

```python
import functools

import jax
import jax.numpy as jnp
from jax import lax
from jax.experimental import pallas as pl
from jax.experimental.pallas import tpu as pltpu

D_MODEL = 1024
DEPTH = 4
HG_DK = 128
HG_HEADS = D_MODEL // HG_DK
HG_STREAMS = 5
CONV_STREAMS = 3
MOE_GROUPS = 4
MOE_EXPERTS_PER_GROUP = 8
MOE_EXPERTS = MOE_GROUPS * MOE_EXPERTS_PER_GROUP
MOE_TOPK = 2
MOE_D_EXPERT = D_MODEL // 2
MOE_BLOCK = 256
DN_ALPHA = (2.0 * DEPTH) ** 0.25
LN_EPS = 1e-5
RMS_EPS = 1e-6

LANES = 128
SUBLANES = 8
BF16_SUBLANES = 16
VMEM_LIMIT = 48 * 1024 * 1024
GLA_CHUNK = 64
GLA_TBLOCK = 512
GLA_UNROLL = 2
ROW_TILE = 256
NEG_BIG = -1e30

BF16 = jnp.bfloat16
F32 = jnp.float32


def _cparams(*sem):
    return pltpu.CompilerParams(dimension_semantics=sem, vmem_limit_bytes=VMEM_LIMIT)


def _mm_kernel(x_ref, w_ref, o_ref):
    o_ref[...] = jnp.dot(x_ref[...].astype(BF16), w_ref[...],
                         preferred_element_type=F32).astype(o_ref.dtype)


def _matmul(x, w, out_dtype, tm=512, tn=1024):
    n, k = x.shape
    nn = w.shape[1]
    return pl.pallas_call(
        _mm_kernel,
        grid=(nn // tn, n // tm),
        in_specs=[pl.BlockSpec((tm, k), lambda j, i: (i, 0)),
                  pl.BlockSpec((k, tn), lambda j, i: (0, j))],
        out_specs=pl.BlockSpec((tm, tn), lambda j, i: (i, j)),
        out_shape=jax.ShapeDtypeStruct((n, nn), out_dtype),
        compiler_params=_cparams("arbitrary", "arbitrary"),
        name="in_proj",
    )(x, w)


PACKED = D_MODEL // 2


def _pack_rows(x):
    lo = pltpu.bitcast(x[:, :PACKED].astype(BF16).astype(F32), jnp.int32)
    hi = pltpu.bitcast(x[:, PACKED:].astype(BF16).astype(F32), jnp.int32)
    return lax.shift_right_logical(lo, 16) | hi


def _unpack_rows(w):
    lo = pltpu.bitcast(lax.shift_left(w, 16), F32)
    hi = pltpu.bitcast(w & jnp.int32(-65536), F32)
    return lo, hi


def _layer_norm_rows(y, g, b):
    mu = jnp.mean(y, axis=-1, keepdims=True)
    yc = y - mu
    var = jnp.mean(yc * yc, axis=-1, keepdims=True)
    return yc * lax.rsqrt(var + LN_EPS) * g + b


def _rows(x, blocks, size):
    parts = [x[b * size:(b + 1) * size, :] for b in blocks]
    return parts[0] if len(parts) == 1 else jnp.concatenate(parts, axis=0)


def _gla_chunk(q_s, k_s, v_s, lf_s, g_s, acc_s, o_ref, st_ref, d, start, slot, reverse):
    c = GLA_CHUNK
    t8 = SUBLANES
    nt_dims = (((1,), (1,)), ((), ()))

    def tt(ref, i):
        return ref[d, pl.ds(start + i, t8, stride=t8), :]

    lf = [tt(lf_s, i) for i in range(t8)]
    qt = [tt(q_s, i) for i in range(t8)]
    kt = [tt(k_s, i) for i in range(t8)]
    vt = [tt(v_s, i) for i in range(t8)]

    gi = [None] * t8
    prev = None
    for i in (range(t8 - 1, -1, -1) if reverse else range(t8)):
        gi[i] = lf[i] if prev is None else prev + lf[i]
        prev = gi[i]
    tot = prev
    sub = lax.broadcasted_iota(jnp.int32, (t8, LANES), 0)
    incl = tot
    s = 1
    while s < t8:
        if reverse:
            incl = incl + jnp.where(sub + s < t8, pltpu.roll(incl, t8 - s, 0), 0.0)
        else:
            incl = incl + jnp.where(sub >= s, pltpu.roll(incl, s, 0), 0.0)
        s *= 2
    excl = incl - tot
    for i in range(t8):
        g_s[slot, pl.ds(i, t8, stride=t8), :] = gi[i] + excl

    acc = []
    for i in range(t8):
        a_i = jnp.sum(qt[i] * kt[i], axis=-1, keepdims=True) * vt[i]
        for r in (range(i + 1, t8) if reverse else range(i)):
            w = qt[i] * kt[r] * jnp.exp(gi[i] - gi[r])
            a_i = a_i + jnp.sum(w, axis=-1, keepdims=True) * vt[r]
        acc.append(a_i)

    sl = pl.ds(start, c)
    q = q_s[d, sl, :]
    kk = k_s[d, sl, :]
    v = v_s[d, sl, :]
    g = g_s[slot]
    vb = v.astype(BF16)
    st = st_ref[d]

    o = lax.dot_general((q * jnp.exp(g)).astype(BF16), st.astype(BF16), nt_dims,
                        preferred_element_type=F32)

    half = c // 2
    qd_l, kd_l, v_l, q_tiles = [], [], [], []
    b = 2 * t8
    while b <= c:
        h = b // 2
        nblk = c // b
        lo = [2 * m for m in range(nblk)]
        hi = [2 * m + 1 for m in range(nblk)]
        q_half, k_half = (lo, hi) if reverse else (hi, lo)
        refs = [g[m * b + h:m * b + h + 1, :] if reverse else g[m * b + h - 1:m * b + h, :]
                for m in range(nblk)]
        gref = jnp.concatenate([jnp.broadcast_to(r, (h, LANES)) for r in refs], axis=0) \
            if nblk > 1 else jnp.broadcast_to(refs[0], (h, LANES))
        qd_l.append(_rows(q, q_half, h) * jnp.exp(_rows(g, q_half, h) - gref))
        kd_l.append(_rows(kk, k_half, h) * jnp.exp(gref - _rows(g, k_half, h)))
        v_l.append(_rows(v, k_half, h))
        q_tiles.append([hb * (h // t8) + t for hb in q_half for t in range(h // t8)])
        b *= 2
    nlev = len(qd_l)
    p = lax.dot_general(jnp.concatenate(qd_l, axis=0).astype(BF16),
                        jnp.concatenate(kd_l, axis=0).astype(BF16), nt_dims,
                        preferred_element_type=F32)
    ii = lax.broadcasted_iota(jnp.int32, p.shape, 0)
    jj = lax.broadcasted_iota(jnp.int32, p.shape, 1)
    keep = None
    for lv in range(nlev):
        h = t8 << lv
        m = (ii // half == lv) & (jj // half == lv) & ((ii // h) == (jj // h))
        keep = m if keep is None else keep | m
    p = jnp.where(keep, p, 0.0)
    res = jnp.dot(p.astype(BF16), jnp.concatenate(v_l, axis=0).astype(BF16),
                  preferred_element_type=F32)
    contrib = [None] * (c // t8)
    for lv in range(nlev):
        for n_, tile in enumerate(q_tiles[lv]):
            piece = res[lv * half + n_ * t8:lv * half + (n_ + 1) * t8, :]
            contrib[tile] = piece if contrib[tile] is None else contrib[tile] + piece
    zero = jnp.zeros((t8, LANES), F32)
    o = o + jnp.concatenate([zero if p_ is None else p_ for p_ in contrib], axis=0)

    gl = g[0:1, :] if reverse else g[c - 1:c, :]
    kd = (kk * jnp.exp(gl - g)).astype(BF16)
    upd = lax.dot_general(vb, kd, (((0,), (0,)), ((), ())), preferred_element_type=F32)
    st_ref[d] = st * jnp.exp(gl) + upd

    for i in range(t8):
        acc_s[slot, pl.ds(i, t8, stride=t8), :] = acc[i]
    o_ref[sl, :] = (o + acc_s[slot]).astype(o_ref.dtype)


def _gla_kernel(layer, lbl_ref, qf_ref, ff_ref, vf_ref, qb_ref, fb_ref, vb_ref,
                of_ref, ob_ref, st_ref, q_s, k_s, v_s, lf_s, g_s, acc_s):
    tb = GLA_TBLOCK
    nc = tb // GLA_CHUNK

    @pl.when(pl.program_id(2) == 0)
    def _():
        st_ref[...] = jnp.zeros_like(st_ref)

    if layer > 0:
        lg = lbl_ref[...]
        e = jnp.exp(lg - jnp.max(lg, axis=0, keepdims=True))
        sm = e / jnp.sum(e, axis=0, keepdims=True)
        lb = sm[1]
        for l in range(2, layer + 1):
            lb = lb + sm[l]

    for d, (q_ref, f_ref, v_ref) in enumerate(((qf_ref, ff_ref, vf_ref),
                                               (qb_ref, fb_ref, vb_ref))):
        qx = q_ref[...].astype(F32)
        fx = f_ref[...]
        v_s[d] = v_ref[...].astype(F32)
        t = jnp.exp(-jnp.abs(fx))
        r = 1.0 / (1.0 + t)
        logsig = jnp.minimum(fx, 0.0) - jnp.log(1.0 + t)
        sig_neg = jnp.where(fx >= 0.0, t * r, r)
        if layer > 0:
            lbd = lb[d:d + 1, :]
            a = jnp.log(lbd)
            bb = jnp.log1p(-lbd) + logsig
            lf = jnp.maximum(a, bb) + jnp.log(1.0 + jnp.exp(-jnp.abs(a - bb)))
            kk = (1.0 - lbd) * sig_neg
        else:
            lf = logsig
            kk = sig_neg
        q_s[d] = qx * jax.nn.sigmoid(qx)
        k_s[d] = kk
        lf_s[d] = lf

    def body(ci, carry):
        for u in range(GLA_UNROLL):
            cf = ci * GLA_UNROLL + u
            sf = pl.multiple_of(cf * GLA_CHUNK, GLA_CHUNK)
            sb = pl.multiple_of((nc - 1 - cf) * GLA_CHUNK, GLA_CHUNK)
            _gla_chunk(q_s, k_s, v_s, lf_s, g_s, acc_s, of_ref, st_ref, 0, sf, 2 * u, False)
            _gla_chunk(q_s, k_s, v_s, lf_s, g_s, acc_s, ob_ref, st_ref, 1, sb, 2 * u + 1, True)
        return carry

    lax.fori_loop(0, nc // GLA_UNROLL, body, 0)


def _gla(proj_a, proj_f, lb_logits, layer, batch, seq):
    n = proj_a.shape[0]
    tb = GLA_TBLOCK
    nb = seq // tb
    h8 = HG_HEADS

    def spec(stream, rev):
        if rev:
            return pl.BlockSpec((tb, LANES), lambda b, h, c: (b * nb + nb - 1 - c, stream * h8 + h))
        return pl.BlockSpec((tb, LANES), lambda b, h, c: (b * nb + c, stream * h8 + h))

    o_f_spec = pl.BlockSpec((tb, LANES), lambda b, h, c: (b * nb + c, h))
    o_b_spec = pl.BlockSpec((tb, LANES), lambda b, h, c: (b * nb + nb - 1 - c, h))
    return pl.pallas_call(
        functools.partial(_gla_kernel, layer),
        grid=(batch, h8, nb),
        in_specs=[pl.BlockSpec((DEPTH, 2, LANES), lambda b, h, c: (0, 0, h)),
                  spec(0, False), spec(0, False), spec(1, False),
                  spec(0, True), spec(1, True), spec(1, True)],
        out_specs=[o_f_spec, o_b_spec],
        out_shape=[jax.ShapeDtypeStruct((n, D_MODEL), BF16)] * 2,
        scratch_shapes=[pltpu.VMEM((2, HG_DK, HG_DK), F32)]
        + [pltpu.VMEM((2, tb, LANES), F32)] * 4
        + [pltpu.VMEM((2 * GLA_UNROLL, GLA_CHUNK, LANES), F32)] * 2,
        compiler_params=_cparams("arbitrary", "arbitrary", "arbitrary"),
        name="gla",
    )(lb_logits, proj_a, proj_f, proj_a, proj_a, proj_f, proj_a)


def _hgrn_out_kernel(of_ref, ob_ref, gate_ref, nw_ref, w_ref, x_ref, g_ref, b_ref, o_ref,
                     op_ref):
    o = of_ref[...].astype(F32) + ob_ref[...].astype(F32)
    parts = []
    for h in range(HG_HEADS):
        oh = o[:, h * LANES:(h + 1) * LANES]
        ms = jnp.mean(oh * oh, axis=-1, keepdims=True)
        parts.append(oh * lax.rsqrt(ms + RMS_EPS))
    gate = gate_ref[...].astype(F32)
    y = jnp.concatenate(parts, axis=-1) * nw_ref[...] * (gate * jax.nn.sigmoid(gate))
    mix = jnp.dot(y.astype(BF16), w_ref[...], preferred_element_type=F32)
    out = _layer_norm_rows(DN_ALPHA * x_ref[...] + mix, g_ref[...], b_ref[...])
    o_ref[...] = out
    op_ref[...] = _pack_rows(out)


def _hgrn_out(o_f, o_b, proj_a, norm_w, w_out, x, g, b):
    n = x.shape[0]
    tm = ROW_TILE
    row = pl.BlockSpec((tm, D_MODEL), lambda i: (i, 0))
    vec = pl.BlockSpec((1, D_MODEL), lambda i: (0, 0))
    return pl.pallas_call(
        _hgrn_out_kernel,
        grid=(n // tm,),
        in_specs=[row, row, pl.BlockSpec((tm, D_MODEL), lambda i: (i, 2)), vec,
                  pl.BlockSpec((D_MODEL, D_MODEL), lambda i: (0, 0)), row, vec, vec],
        out_specs=[row, pl.BlockSpec((tm, PACKED), lambda i: (i, 0))],
        out_shape=[jax.ShapeDtypeStruct((n, D_MODEL), F32),
                   jax.ShapeDtypeStruct((n, PACKED), jnp.int32)],
        compiler_params=_cparams("arbitrary"),
        name="hgrn_out",
    )(o_f, o_b, proj_a, jnp.tile(norm_w, HG_HEADS).reshape(1, D_MODEL), w_out, x,
      g.reshape(1, D_MODEL), b.reshape(1, D_MODEL))


def _conv_out_kernel(tiles_per_seq, bg_ref, cg_ref, h_ref, cgp_ref, hp_ref, cgn_ref, hn_ref,
                     cw_ref, w_ref, x_ref, g_ref, b_ref, o_ref, op_ref):
    i = pl.program_id(0)
    tm = cg_ref.shape[0]
    u = cg_ref[...].astype(F32) * h_ref[...].astype(F32)
    first = (i % tiles_per_seq) == 0
    last = (i % tiles_per_seq) == tiles_per_seq - 1
    hr = cgp_ref.shape[0]
    u_halo_prev = cgp_ref[...].astype(F32) * hp_ref[...].astype(F32)
    u_halo_next = cgn_ref[...].astype(F32) * hn_ref[...].astype(F32)
    u_prev_row = jnp.where(first, 0.0, u_halo_prev[hr - 1:hr, :])
    u_next_row = jnp.where(last, 0.0, u_halo_next[0:1, :])
    rows = lax.broadcasted_iota(jnp.int32, u.shape, 0)
    u_prev = jnp.where(rows == 0, u_prev_row, pltpu.roll(u, 1, 0))
    u_next = jnp.where(rows == tm - 1, u_next_row, pltpu.roll(u, tm - 1, 0))
    cw = cw_ref[...]
    y = u_prev * cw[0:1, :] + u * cw[1:2, :] + u_next * cw[2:3, :]
    mix = jnp.dot((bg_ref[...].astype(F32) * y).astype(BF16), w_ref[...],
                  preferred_element_type=F32)
    out = _layer_norm_rows(DN_ALPHA * x_ref[...] + mix, g_ref[...], b_ref[...])
    o_ref[...] = out
    op_ref[...] = _pack_rows(out)


def _conv_out(proj, conv_w, w_out, x, g, b, seq):
    n = x.shape[0]
    tm = ROW_TILE
    halo = BF16_SUBLANES
    rh = tm // halo
    nblk = n // halo
    row = pl.BlockSpec((tm, D_MODEL), lambda i: (i, 0))
    vec = pl.BlockSpec((1, D_MODEL), lambda i: (0, 0))

    def main(stream):
        return pl.BlockSpec((tm, D_MODEL), lambda i: (i, stream))

    def prev(stream):
        return pl.BlockSpec((halo, D_MODEL), lambda i: (jnp.maximum(i * rh - 1, 0), stream))

    def nxt(stream):
        return pl.BlockSpec((halo, D_MODEL),
                            lambda i: (jnp.minimum((i + 1) * rh, nblk - 1), stream))

    return pl.pallas_call(
        functools.partial(_conv_out_kernel, seq // tm),
        grid=(n // tm,),
        in_specs=[main(0), main(1), main(2), prev(1), prev(2), nxt(1), nxt(2),
                  pl.BlockSpec((3, D_MODEL), lambda i: (0, 0)),
                  pl.BlockSpec((D_MODEL, D_MODEL), lambda i: (0, 0)), row, vec, vec],
        out_specs=[row, pl.BlockSpec((tm, PACKED), lambda i: (i, 0))],
        out_shape=[jax.ShapeDtypeStruct((n, D_MODEL), F32),
                   jax.ShapeDtypeStruct((n, PACKED), jnp.int32)],
        compiler_params=_cparams("arbitrary"),
        name="conv_out",
    )(proj, proj, proj, proj, proj, proj, proj, conv_w, w_out, x,
      g.reshape(1, D_MODEL), b.reshape(1, D_MODEL))


def _router_kernel(x_ref, w_ref, b_ref, tri_ref, o_ref, cnt_ref, carry_ref):
    @pl.when(pl.program_id(0) == 0)
    def _():
        carry_ref[...] = jnp.zeros_like(carry_ref)

    logits = jnp.dot(x_ref[...], w_ref[...], precision=lax.Precision.HIGHEST,
                     preferred_element_type=F32) + b_ref[...]
    lane = lax.broadcasted_iota(jnp.int32, logits.shape, 1)
    lane_f = lane.astype(F32)
    gl = jnp.where(lane < MOE_GROUPS, logits, NEG_BIG)
    gmax = jnp.max(gl, axis=-1, keepdims=True)
    gsum = jnp.sum(jnp.exp(gl - gmax), axis=-1, keepdims=True)
    p_group = 1.0 / gsum
    g_sel = jnp.min(jnp.where(gl == gmax, lane_f, 1e9), axis=-1, keepdims=True)
    e_lane = lane - MOE_GROUPS
    in_grp = ((lane >= MOE_GROUPS) & (lane < MOE_GROUPS + MOE_EXPERTS)
              & ((e_lane // MOE_EXPERTS_PER_GROUP).astype(F32) == g_sel))
    el = jnp.where(in_grp, logits, NEG_BIG)
    t1 = jnp.max(el, axis=-1, keepdims=True)
    i1 = jnp.min(jnp.where(el == t1, lane_f, 1e9), axis=-1, keepdims=True)
    el2 = jnp.where(lane_f == i1, NEG_BIG, el)
    t2 = jnp.max(el2, axis=-1, keepdims=True)
    i2 = jnp.min(jnp.where(el2 == t2, lane_f, 1e9), axis=-1, keepdims=True)
    z = jnp.exp(t2 - t1)
    g1 = p_group / (1.0 + z)
    g2 = g1 * z
    out = jnp.where(lane == 0, g1, 0.0)
    out = jnp.where(lane == 1, g2, out)
    out = jnp.where(lane == 2, i1 - MOE_GROUPS, out)
    out = jnp.where(lane == 3, i2 - MOE_GROUPS, out)

    sel1 = lane_f == i1
    sel2 = lane_f == i2
    onehot = jnp.where(sel1 | sel2, 1.0, 0.0)
    prefix = jnp.dot(tri_ref[...], onehot.astype(BF16), preferred_element_type=F32)
    before = prefix + carry_ref[0:1, :]
    r1 = jnp.sum(jnp.where(sel1, before, 0.0), axis=-1, keepdims=True)
    r2 = jnp.sum(jnp.where(sel2, before, 0.0), axis=-1, keepdims=True)
    out = jnp.where(lane == 4, r1, out)
    out = jnp.where(lane == 5, r2, out)
    o_ref[...] = out
    carry_ref[...] = carry_ref[...] + jnp.sum(onehot, axis=0, keepdims=True)
    cnt_ref[...] = carry_ref[...]


def _router(x, w_group, b_group, w_expert, b_expert):
    n = x.shape[0]
    tm = 512
    pad = LANES - MOE_GROUPS - MOE_EXPERTS
    w = jnp.concatenate([w_group, w_expert, jnp.zeros((D_MODEL, pad), F32)], axis=1)
    b = jnp.concatenate([b_group, b_expert, jnp.zeros((pad,), F32)]).reshape(1, LANES)
    r = jnp.arange(tm, dtype=jnp.int32)
    tri = (r[None, :] < r[:, None]).astype(BF16)
    route, cnt = pl.pallas_call(
        _router_kernel,
        grid=(n // tm,),
        in_specs=[pl.BlockSpec((tm, D_MODEL), lambda i: (i, 0)),
                  pl.BlockSpec((D_MODEL, LANES), lambda i: (0, 0)),
                  pl.BlockSpec((1, LANES), lambda i: (0, 0)),
                  pl.BlockSpec((tm, tm), lambda i: (0, 0))],
        out_specs=[pl.BlockSpec((tm, LANES), lambda i: (i, 0)),
                   pl.BlockSpec((SUBLANES, LANES), lambda i: (0, 0))],
        out_shape=[jax.ShapeDtypeStruct((n, LANES), F32),
                   jax.ShapeDtypeStruct((SUBLANES, LANES), F32)],
        scratch_shapes=[pltpu.VMEM((SUBLANES, LANES), F32)],
        compiler_params=_cparams("arbitrary"),
        name="router",
    )(x, w, b, tri)
    return route, cnt[0, MOE_GROUPS:MOE_GROUPS + MOE_EXPERTS].astype(jnp.int32)


DISPATCH_TOKENS = 256
DISPATCH_BUFS = 3


def _moe_plan(route, counts, n):
    nk = n * MOE_TOPK
    n_rows = -(-nk // MOE_BLOCK) * MOE_BLOCK + MOE_EXPERTS * MOE_BLOCK
    n_blocks = n_rows // MOE_BLOCK
    padded = (counts + MOE_BLOCK - 1) // MOE_BLOCK * MOE_BLOCK
    pad_end = jnp.cumsum(padded)
    pad_start = pad_end - padded
    e = route[:, MOE_TOPK:2 * MOE_TOPK].astype(jnp.int32)
    rank = route[:, 2 * MOE_TOPK:3 * MOE_TOPK].astype(jnp.int32)
    ids = jnp.arange(MOE_EXPERTS, dtype=jnp.int32)
    base = jnp.sum(jnp.where(e[:, :, None] == ids, pad_start, 0), axis=-1)
    dest = (base + rank).reshape(nk)
    block_start = jnp.arange(n_blocks, dtype=jnp.int32) * MOE_BLOCK
    block_expert = jnp.minimum(jnp.sum(block_start[:, None] >= pad_end[None, :], axis=1),
                               MOE_EXPERTS - 1).astype(jnp.int32)
    nact = (pad_end[-1:] // MOE_BLOCK).astype(jnp.int32)
    fill_start = (pad_start + counts).astype(jnp.int32)
    return dest, block_expert, nact, fill_start, pad_end.astype(jnp.int32), n_rows


def _dispatch_kernel(dest_ref, fs_ref, fe_ref, nact_ref, x_hbm, xs_hbm, xbuf, zbuf, lsem, sem,
                     zsem):
    i = pl.program_id(0)
    nsteps = pl.num_programs(0)
    slot = i % DISPATCH_BUFS
    ct = DISPATCH_TOKENS
    gt = ct // SUBLANES
    n_blocks = xs_hbm.shape[0] // MOE_BLOCK

    def load(step, sl):
        start = pl.multiple_of(step * gt, gt)
        return pltpu.make_async_copy(x_hbm.at[pl.ds(start, gt)], xbuf.at[sl], lsem.at[sl])

    def wait_step(sl):
        for _ in range(MOE_TOPK):
            pltpu.make_async_copy(x_hbm.at[pl.ds(0, gt)], xbuf.at[sl], sem.at[sl]).wait()

    @pl.when(i == 0)
    def _():
        for s in range(DISPATCH_BUFS - 1):
            load(s, s).start()

    load(i, slot).wait()

    def body(g, c):
        base = (i * ct + g * SUBLANES) * MOE_TOPK
        for t in range(SUBLANES):
            for k in range(MOE_TOPK):
                d = dest_ref[base + t * MOE_TOPK + k]
                pltpu.make_async_copy(xbuf.at[slot, g, pl.ds(t, 1)], xs_hbm.at[pl.ds(d, 1)],
                                      sem.at[slot]).start()
        return c
    lax.fori_loop(0, gt, body, 0)

    @pl.when(i >= 1)
    def _():
        wait_step((i - 1) % DISPATCH_BUFS)

    @pl.when(i + DISPATCH_BUFS - 1 < nsteps)
    def _():
        load(i + DISPATCH_BUFS - 1, (i + DISPATCH_BUFS - 1) % DISPATCH_BUFS).start()

    @pl.when(i == nsteps - 1)
    def _():
        wait_step(slot)
        zbuf[...] = jnp.zeros_like(zbuf)

        def fill_rows(wait):
            def per_expert(e, c):
                def per_row(r, c2):
                    cp = pltpu.make_async_copy(zbuf.at[pl.ds(0, 1)], xs_hbm.at[pl.ds(r, 1)],
                                               zsem.at[0])
                    cp.wait() if wait else cp.start()
                    return c2
                lax.fori_loop(fs_ref[e], fe_ref[e], per_row, 0)
                return c
            lax.fori_loop(0, MOE_EXPERTS, per_expert, 0)

        def fill_blocks(wait):
            def per_block(b, c):
                start = pl.multiple_of(b * MOE_BLOCK, MOE_BLOCK)
                cp = pltpu.make_async_copy(zbuf, xs_hbm.at[pl.ds(start, MOE_BLOCK)], zsem.at[1])
                cp.wait() if wait else cp.start()
                return c
            lax.fori_loop(nact_ref[0], n_blocks, per_block, 0)

        fill_rows(False)
        fill_blocks(False)
        fill_rows(True)
        fill_blocks(True)


def _dispatch(x, dest, fill_start, fill_end, nact, n_rows):
    n, width = x.shape
    grid_spec = pltpu.PrefetchScalarGridSpec(
        num_scalar_prefetch=4,
        grid=(n // DISPATCH_TOKENS,),
        in_specs=[pl.BlockSpec(memory_space=pl.ANY)],
        out_specs=pl.BlockSpec(memory_space=pl.ANY),
        scratch_shapes=[pltpu.VMEM((DISPATCH_BUFS, DISPATCH_TOKENS // SUBLANES, SUBLANES, width),
                                   x.dtype),
                        pltpu.VMEM((MOE_BLOCK, width), x.dtype),
                        pltpu.SemaphoreType.DMA((DISPATCH_BUFS,)),
                        pltpu.SemaphoreType.DMA((DISPATCH_BUFS,)),
                        pltpu.SemaphoreType.DMA((2,))],
    )
    return pl.pallas_call(
        _dispatch_kernel,
        grid_spec=grid_spec,
        out_shape=jax.ShapeDtypeStruct((n_rows, width), x.dtype),
        compiler_params=_cparams("arbitrary"),
        name="dispatch",
    )(dest, fill_start, fill_end, nact, x.reshape(n // SUBLANES, SUBLANES, width))


def _moe_ffn_kernel(e0, be_ref, nact_ref, ord_ref, nxt_ref, xs_ref, wup_hbm, wdn_hbm, ys_ref,
                    wup_f, wdn_f, wup_bf, wdn_bf, wsem):
    i = pl.program_id(0)
    active = i < nact_ref[0]
    new_expert = (i == 0) | (be_ref[i] != be_ref[jnp.maximum(i - 1, 0)])

    def fetch(e, sl):
        return (pltpu.make_async_copy(wup_hbm.at[e0 + e], wup_f.at[sl], wsem.at[0, sl]),
                pltpu.make_async_copy(wdn_hbm.at[e0 + e], wdn_f.at[sl], wsem.at[1, sl]))

    @pl.when(active & (i == 0))
    def _():
        for cp in fetch(be_ref[0], 0):
            cp.start()

    @pl.when(active & new_expert)
    def _():
        sl = ord_ref[i] % 2
        for cp in fetch(be_ref[i], sl):
            cp.wait()

        @pl.when(nxt_ref[i] >= 0)
        def _():
            for cp in fetch(nxt_ref[i], 1 - sl):
                cp.start()

        wup_bf[...] = wup_f[sl].astype(BF16)
        wdn_bf[...] = wdn_f[sl].astype(BF16)

    @pl.when(active)
    def _():
        x_lo, x_hi = _unpack_rows(xs_ref[...])
        xb = jnp.concatenate([x_lo.astype(BF16), x_hi.astype(BF16)], axis=1)
        hcat = jnp.dot(xb, wup_bf[...], preferred_element_type=F32)
        hg = hcat[:, :MOE_D_EXPERT]
        hu = hcat[:, MOE_D_EXPERT:]
        act = (hg * jax.nn.sigmoid(hg) * hu).astype(BF16)
        ys_ref[...] = _pack_rows(jnp.dot(act, wdn_bf[...], preferred_element_type=F32))

    @pl.when(jnp.logical_not(active))
    def _():
        ys_ref[...] = jnp.zeros_like(ys_ref)


def _moe_ffn(xs, block_expert, nact, counts, w_up, w_down, layer):
    n_rows = xs.shape[0]
    n_blocks = n_rows // MOE_BLOCK
    first = jnp.concatenate([jnp.ones((1,), jnp.int32),
                             (block_expert[1:] != block_expert[:-1]).astype(jnp.int32)])
    ordinal = (jnp.cumsum(first) - 1).astype(jnp.int32)
    ids = jnp.arange(MOE_EXPERTS, dtype=jnp.int32)
    later = (ids[None, :] > ids[:, None]) & (counts[None, :] > 0)
    nxt = jnp.min(jnp.where(later, ids[None, :], MOE_EXPERTS), axis=1)
    nxt = jnp.where(nxt == MOE_EXPERTS, -1, nxt).astype(jnp.int32)
    next_expert = nxt[block_expert]
    row = pl.BlockSpec((MOE_BLOCK, PACKED), lambda i, *_: (i, 0))
    grid_spec = pltpu.PrefetchScalarGridSpec(
        num_scalar_prefetch=4,
        grid=(n_blocks,),
        in_specs=[row, pl.BlockSpec(memory_space=pl.ANY), pl.BlockSpec(memory_space=pl.ANY)],
        out_specs=row,
        scratch_shapes=[pltpu.VMEM((2, D_MODEL, 2 * MOE_D_EXPERT), F32),
                        pltpu.VMEM((2, MOE_D_EXPERT, D_MODEL), F32),
                        pltpu.VMEM((D_MODEL, 2 * MOE_D_EXPERT), BF16),
                        pltpu.VMEM((MOE_D_EXPERT, D_MODEL), BF16),
                        pltpu.SemaphoreType.DMA((2, 2))],
    )
    return pl.pallas_call(
        functools.partial(_moe_ffn_kernel, layer * MOE_EXPERTS),
        grid_spec=grid_spec,
        out_shape=jax.ShapeDtypeStruct((n_rows, PACKED), jnp.int32),
        compiler_params=_cparams("arbitrary"),
        name="moe_ffn",
    )(block_expert, nact, ordinal, next_expert, xs, w_up, w_down)


def _combine_ln_kernel(dest_ref, x_ref, route_ref, ys_hbm, g_ref, b_ref, o_ref, ob_ref, ybuf,
                       sem):
    i = pl.program_id(0)
    nsteps = pl.num_programs(0)
    slot = i % 2
    tm = x_ref.shape[0]

    def start_gather(step, sl):
        def body(g, c):
            base = (step * tm + g * SUBLANES) * MOE_TOPK
            for t in range(SUBLANES):
                for k in range(MOE_TOPK):
                    d = dest_ref[base + t * MOE_TOPK + k]
                    pltpu.make_async_copy(ys_hbm.at[pl.ds(d, 1)], ybuf.at[sl, k, g, pl.ds(t, 1)],
                                          sem.at[sl]).start()
            return c
        lax.fori_loop(0, tm // SUBLANES, body, 0)

    @pl.when(i == 0)
    def _():
        start_gather(0, 0)

    @pl.when(i + 1 < nsteps)
    def _():
        start_gather(i + 1, 1 - slot)

    for k in range(MOE_TOPK):
        pltpu.make_async_copy(ybuf.at[1 - slot, k], ybuf.at[slot, k], sem.at[slot]).wait()
    route = route_ref[...]
    y0_lo, y0_hi = _unpack_rows(ybuf[slot, 0].reshape(tm, PACKED))
    y1_lo, y1_hi = _unpack_rows(ybuf[slot, 1].reshape(tm, PACKED))
    g0 = route[:, 0:1]
    g1 = route[:, 1:2]
    ffn = jnp.concatenate([g0 * y0_lo + g1 * y1_lo, g0 * y0_hi + g1 * y1_hi], axis=1)
    out = _layer_norm_rows(DN_ALPHA * x_ref[...] + ffn, g_ref[...], b_ref[...])
    o_ref[...] = out
    ob_ref[...] = out.astype(BF16)


def _combine_ln(x, route, ys, dest, g, b):
    n = x.shape[0]
    tm = ROW_TILE
    vec = pl.BlockSpec((1, D_MODEL), lambda i, d: (0, 0))
    row = pl.BlockSpec((tm, D_MODEL), lambda i, d: (i, 0))
    grid_spec = pltpu.PrefetchScalarGridSpec(
        num_scalar_prefetch=1,
        grid=(n // tm,),
        in_specs=[pl.BlockSpec((tm, D_MODEL), lambda i, d: (i, 0)),
                  pl.BlockSpec((tm, LANES), lambda i, d: (i, 0)),
                  pl.BlockSpec(memory_space=pl.ANY), vec, vec],
        out_specs=[row, row],
        scratch_shapes=[pltpu.VMEM((2, MOE_TOPK, tm // SUBLANES, SUBLANES, PACKED), jnp.int32),
                        pltpu.SemaphoreType.DMA((2,))],
    )
    return pl.pallas_call(
        _combine_ln_kernel,
        grid_spec=grid_spec,
        out_shape=[jax.ShapeDtypeStruct((n, D_MODEL), F32),
                   jax.ShapeDtypeStruct((n, D_MODEL), BF16)],
        compiler_params=_cparams("arbitrary"),
        name="combine_ln",
    )(dest, x, route, ys, g.reshape(1, D_MODEL), b.reshape(1, D_MODEL))


def kernel(x, hg_w_in, hg_lb_logits, hg_norm_w, hg_w_out, cv_w_in, cv_w, cv_w_out, ln_g, ln_b,
           moe_w_group, moe_b_group, moe_w_expert, moe_b_expert, moe_w_up, moe_w_down):
    batch, seq, d = x.shape
    n = batch * seq
    xf = x.reshape(n, d)
    w_up_all = moe_w_up.reshape(DEPTH * MOE_EXPERTS, D_MODEL, 2 * MOE_D_EXPERT)
    w_down_all = moe_w_down.reshape(DEPTH * MOE_EXPERTS, MOE_D_EXPERT, D_MODEL)
    xin = xf
    for layer in range(DEPTH):
        j = layer // 2
        if layer % 2 == 0:
            w = hg_w_in[j]
            w_a = jnp.concatenate([w[:, :D_MODEL], w[:, 3 * D_MODEL:]], axis=1).astype(BF16)
            w_f = w[:, D_MODEL:3 * D_MODEL].astype(BF16)
            proj_a = _matmul(xin, w_a, BF16)
            proj_f = _matmul(xin, w_f, F32)
            o_f, o_b = _gla(proj_a, proj_f, hg_lb_logits, layer, batch, seq)
            xf, xpk = _hgrn_out(o_f, o_b, proj_a, hg_norm_w[j], hg_w_out[j].astype(BF16), xf,
                                ln_g[layer, 0], ln_b[layer, 0])
        else:
            proj = _matmul(xin, cv_w_in[j].astype(BF16), BF16)
            xf, xpk = _conv_out(proj, cv_w[j], cv_w_out[j].astype(BF16), xf,
                                ln_g[layer, 0], ln_b[layer, 0], seq)
        route, counts = _router(xf, moe_w_group[layer], moe_b_group[layer],
                                moe_w_expert[layer], moe_b_expert[layer])
        dest, block_expert, nact, fill_start, fill_end, n_rows = _moe_plan(route, counts, n)
        xs = _dispatch(xpk, dest, fill_start, fill_end, nact, n_rows)
        ys = _moe_ffn(xs, block_expert, nact, counts, w_up_all, w_down_all, layer)
        xf, xin = _combine_ln(xf, route, ys, dest, ln_g[layer, 1], ln_b[layer, 1])
    return xf.reshape(batch, seq, d)
```

```python
import functools

import jax
import jax.numpy as jnp
from jax import lax
from jax.experimental import pallas as pl
from jax.experimental.pallas import tpu as pltpu

D_MODEL = 1024
DEPTH = 4
HG_DK = 128
HG_HEADS = D_MODEL // HG_DK
HG_STREAMS = 5
CONV_STREAMS = 3
MOE_GROUPS = 4
MOE_EXPERTS_PER_GROUP = 8
MOE_EXPERTS = MOE_GROUPS * MOE_EXPERTS_PER_GROUP
MOE_TOPK = 2
MOE_D_EXPERT = D_MODEL // 2
MOE_BLOCK = 256
DN_ALPHA = (2.0 * DEPTH) ** 0.25
LN_EPS = 1e-5
RMS_EPS = 1e-6

LANES = 128
SUBLANES = 8
BF16_SUBLANES = 16
VMEM_LIMIT = 48 * 1024 * 1024
GLA_CHUNK = 64
GLA_TBLOCK = 1024
GLA_UNROLL = 2
ROW_TILE = 256
NEG_BIG = -1e30

BF16 = jnp.bfloat16
F32 = jnp.float32


def _cparams(*sem):
    return pltpu.CompilerParams(dimension_semantics=sem, vmem_limit_bytes=VMEM_LIMIT)


def _mm_kernel(x_ref, w_ref, o_ref):
    o_ref[...] = jnp.dot(x_ref[...].astype(BF16), w_ref[...],
                         preferred_element_type=F32).astype(o_ref.dtype)


def _matmul(x, w, out_dtype, tm=512, tn=1024):
    n, k = x.shape
    nn = w.shape[1]
    return pl.pallas_call(
        _mm_kernel,
        grid=(nn // tn, n // tm),
        in_specs=[pl.BlockSpec((tm, k), lambda j, i: (i, 0)),
                  pl.BlockSpec((k, tn), lambda j, i: (0, j))],
        out_specs=pl.BlockSpec((tm, tn), lambda j, i: (i, j)),
        out_shape=jax.ShapeDtypeStruct((n, nn), out_dtype),
        compiler_params=_cparams("arbitrary", "arbitrary"),
        name="in_proj",
    )(x, w)


PACKED = D_MODEL // 2


def _pack_rows(x):
    lo = pltpu.bitcast(x[:, :PACKED].astype(BF16).astype(F32), jnp.int32)
    hi = pltpu.bitcast(x[:, PACKED:].astype(BF16).astype(F32), jnp.int32)
    return lax.shift_right_logical(lo, 16) | hi


def _unpack_rows(w):
    lo = pltpu.bitcast(lax.shift_left(w, 16), F32)
    hi = pltpu.bitcast(w & jnp.int32(-65536), F32)
    return lo, hi


def _layer_norm_rows(y, g, b):
    mu = jnp.mean(y, axis=-1, keepdims=True)
    yc = y - mu
    var = jnp.mean(yc * yc, axis=-1, keepdims=True)
    return yc * lax.rsqrt(var + LN_EPS) * g + b


def _rows(x, blocks, size):
    parts = [x[b * size:(b + 1) * size, :] for b in blocks]
    return parts[0] if len(parts) == 1 else jnp.concatenate(parts, axis=0)


def _gla_chunk(q_s, k_s, v_s, lf_s, g_s, acc_s, o_ref, st_ref, d, start, slot, reverse):
    c = GLA_CHUNK
    t8 = SUBLANES
    nt_dims = (((1,), (1,)), ((), ()))

    def tt(ref, i):
        return ref[d, pl.ds(start + i, t8, stride=t8), :]

    lf = [tt(lf_s, i) for i in range(t8)]
    qt = [tt(q_s, i) for i in range(t8)]
    kt = [tt(k_s, i) for i in range(t8)]
    vt = [tt(v_s, i) for i in range(t8)]

    gi = [None] * t8
    prev = None
    for i in (range(t8 - 1, -1, -1) if reverse else range(t8)):
        gi[i] = lf[i] if prev is None else prev + lf[i]
        prev = gi[i]
    tot = prev
    sub = lax.broadcasted_iota(jnp.int32, (t8, LANES), 0)
    incl = tot
    s = 1
    while s < t8:
        if reverse:
            incl = incl + jnp.where(sub + s < t8, pltpu.roll(incl, t8 - s, 0), 0.0)
        else:
            incl = incl + jnp.where(sub >= s, pltpu.roll(incl, s, 0), 0.0)
        s *= 2
    excl = incl - tot
    for i in range(t8):
        g_s[slot, pl.ds(i, t8, stride=t8), :] = gi[i] + excl

    acc = []
    for i in range(t8):
        a_i = jnp.sum(qt[i] * kt[i], axis=-1, keepdims=True) * vt[i]
        for r in (range(i + 1, t8) if reverse else range(i)):
            w = qt[i] * kt[r] * jnp.exp(gi[i] - gi[r])
            a_i = a_i + jnp.sum(w, axis=-1, keepdims=True) * vt[r]
        acc.append(a_i)

    sl = pl.ds(start, c)
    q = q_s[d, sl, :]
    kk = k_s[d, sl, :]
    v = v_s[d, sl, :]
    g = g_s[slot]
    vb = v.astype(BF16)
    st = st_ref[d]

    o = lax.dot_general((q * jnp.exp(g)).astype(BF16), st.astype(BF16), nt_dims,
                        preferred_element_type=F32)

    half = c // 2
    qd_l, kd_l, v_l, q_tiles = [], [], [], []
    b = 2 * t8
    while b <= c:
        h = b // 2
        nblk = c // b
        lo = [2 * m for m in range(nblk)]
        hi = [2 * m + 1 for m in range(nblk)]
        q_half, k_half = (lo, hi) if reverse else (hi, lo)
        refs = [g[m * b + h:m * b + h + 1, :] if reverse else g[m * b + h - 1:m * b + h, :]
                for m in range(nblk)]
        gref = jnp.concatenate([jnp.broadcast_to(r, (h, LANES)) for r in refs], axis=0) \
            if nblk > 1 else jnp.broadcast_to(refs[0], (h, LANES))
        qd_l.append(_rows(q, q_half, h) * jnp.exp(_rows(g, q_half, h) - gref))
        kd_l.append(_rows(kk, k_half, h) * jnp.exp(gref - _rows(g, k_half, h)))
        v_l.append(_rows(v, k_half, h))
        q_tiles.append([hb * (h // t8) + t for hb in q_half for t in range(h // t8)])
        b *= 2
    nlev = len(qd_l)
    p = lax.dot_general(jnp.concatenate(qd_l, axis=0).astype(BF16),
                        jnp.concatenate(kd_l, axis=0).astype(BF16), nt_dims,
                        preferred_element_type=F32)
    ii = lax.broadcasted_iota(jnp.int32, p.shape, 0)
    jj = lax.broadcasted_iota(jnp.int32, p.shape, 1)
    keep = None
    for lv in range(nlev):
        h = t8 << lv
        m = (ii // half == lv) & (jj // half == lv) & ((ii // h) == (jj // h))
        keep = m if keep is None else keep | m
    p = jnp.where(keep, p, 0.0)
    res = jnp.dot(p.astype(BF16), jnp.concatenate(v_l, axis=0).astype(BF16),
                  preferred_element_type=F32)
    contrib = [None] * (c // t8)
    for lv in range(nlev):
        for n_, tile in enumerate(q_tiles[lv]):
            piece = res[lv * half + n_ * t8:lv * half + (n_ + 1) * t8, :]
            contrib[tile] = piece if contrib[tile] is None else contrib[tile] + piece
    zero = jnp.zeros((t8, LANES), F32)
    o = o + jnp.concatenate([zero if p_ is None else p_ for p_ in contrib], axis=0)

    gl = g[0:1, :] if reverse else g[c - 1:c, :]
    kd = (kk * jnp.exp(gl - g)).astype(BF16)
    upd = lax.dot_general(vb, kd, (((0,), (0,)), ((), ())), preferred_element_type=F32)
    st_ref[d] = st * jnp.exp(gl) + upd

    for i in range(t8):
        acc_s[slot, pl.ds(i, t8, stride=t8), :] = acc[i]
    o_ref[sl, :] = (o + acc_s[slot]).astype(o_ref.dtype)


def _gla_kernel(layer, lbl_ref, qf_ref, ff_ref, vf_ref, qb_ref, fb_ref, vb_ref,
                of_ref, ob_ref, st_ref, q_s, k_s, v_s, lf_s, g_s, acc_s):
    tb = GLA_TBLOCK
    nc = tb // GLA_CHUNK

    @pl.when(pl.program_id(2) == 0)
    def _():
        st_ref[...] = jnp.zeros_like(st_ref)

    if layer > 0:
        lg = lbl_ref[...]
        e = jnp.exp(lg - jnp.max(lg, axis=0, keepdims=True))
        sm = e / jnp.sum(e, axis=0, keepdims=True)
        lb = sm[1]
        for l in range(2, layer + 1):
            lb = lb + sm[l]

    for d, (q_ref, f_ref, v_ref) in enumerate(((qf_ref, ff_ref, vf_ref),
                                               (qb_ref, fb_ref, vb_ref))):
        qx = q_ref[...].astype(F32)
        fx = f_ref[...]
        v_s[d] = v_ref[...].astype(F32)
        t = jnp.exp(-jnp.abs(fx))
        r = 1.0 / (1.0 + t)
        logsig = jnp.minimum(fx, 0.0) - jnp.log(1.0 + t)
        sig_neg = jnp.where(fx >= 0.0, t * r, r)
        if layer > 0:
            lbd = lb[d:d + 1, :]
            a = jnp.log(lbd)
            bb = jnp.log1p(-lbd) + logsig
            lf = jnp.maximum(a, bb) + jnp.log(1.0 + jnp.exp(-jnp.abs(a - bb)))
            kk = (1.0 - lbd) * sig_neg
        else:
            lf = logsig
            kk = sig_neg
        q_s[d] = qx * (0.5 + 0.5 * jnp.tanh(0.5 * qx))
        k_s[d] = kk
        lf_s[d] = lf

    def body(ci, carry):
        for u in range(GLA_UNROLL):
            cf = ci * GLA_UNROLL + u
            sf = pl.multiple_of(cf * GLA_CHUNK, GLA_CHUNK)
            sb = pl.multiple_of((nc - 1 - cf) * GLA_CHUNK, GLA_CHUNK)
            _gla_chunk(q_s, k_s, v_s, lf_s, g_s, acc_s, of_ref, st_ref, 0, sf, 2 * u, False)
            _gla_chunk(q_s, k_s, v_s, lf_s, g_s, acc_s, ob_ref, st_ref, 1, sb, 2 * u + 1, True)
        return carry

    lax.fori_loop(0, nc // GLA_UNROLL, body, 0)


def _gla(proj_a, proj_f, lb_logits, layer, batch, seq):
    n = proj_a.shape[0]
    tb = GLA_TBLOCK
    nb = seq // tb
    h8 = HG_HEADS

    def spec(stream, rev):
        if rev:
            return pl.BlockSpec((tb, LANES), lambda b, h, c: (b * nb + nb - 1 - c, stream * h8 + h))
        return pl.BlockSpec((tb, LANES), lambda b, h, c: (b * nb + c, stream * h8 + h))

    o_f_spec = pl.BlockSpec((tb, LANES), lambda b, h, c: (b * nb + c, h))
    o_b_spec = pl.BlockSpec((tb, LANES), lambda b, h, c: (b * nb + nb - 1 - c, h))
    return pl.pallas_call(
        functools.partial(_gla_kernel, layer),
        grid=(batch, h8, nb),
        in_specs=[pl.BlockSpec((DEPTH, 2, LANES), lambda b, h, c: (0, 0, h)),
                  spec(0, False), spec(0, False), spec(1, False),
                  spec(0, True), spec(1, True), spec(1, True)],
        out_specs=[o_f_spec, o_b_spec],
        out_shape=[jax.ShapeDtypeStruct((n, D_MODEL), BF16)] * 2,
        scratch_shapes=[pltpu.VMEM((2, HG_DK, HG_DK), F32)]
        + [pltpu.VMEM((2, tb, LANES), F32)] * 4
        + [pltpu.VMEM((2 * GLA_UNROLL, GLA_CHUNK, LANES), F32)] * 2,
        compiler_params=_cparams("arbitrary", "arbitrary", "arbitrary"),
        name="gla",
    )(lb_logits, proj_a, proj_f, proj_a, proj_a, proj_f, proj_a)


def _hgrn_out_kernel(of_ref, ob_ref, gate_ref, nw_ref, w_ref, x_ref, g_ref, b_ref,
                     rwh_ref, rwl_ref, rb_ref, tri_ref,
                     o_ref, op_ref, route_ref, cnt_ref, carry_ref):
    o = of_ref[...].astype(F32) + ob_ref[...].astype(F32)
    parts = []
    for h in range(HG_HEADS):
        oh = o[:, h * LANES:(h + 1) * LANES]
        ms = jnp.mean(oh * oh, axis=-1, keepdims=True)
        parts.append(oh * lax.rsqrt(ms + RMS_EPS))
    gate = gate_ref[...].astype(F32)
    y = jnp.concatenate(parts, axis=-1) * nw_ref[...] * (gate * jax.nn.sigmoid(gate))
    mix = jnp.dot(y.astype(BF16), w_ref[...], preferred_element_type=F32)
    out = _layer_norm_rows(DN_ALPHA * x_ref[...] + mix, g_ref[...], b_ref[...])
    o_ref[...] = out
    op_ref[...] = _pack_rows(out)
    _route_tile(out, rwh_ref, rwl_ref, rb_ref, tri_ref, route_ref, cnt_ref, carry_ref)


def _hgrn_out(o_f, o_b, proj_a, norm_w, w_out, x, g, b, router_ops):
    n = x.shape[0]
    tm = ROW_TILE
    row = pl.BlockSpec((tm, D_MODEL), lambda i: (i, 0))
    vec = pl.BlockSpec((1, D_MODEL), lambda i: (0, 0))
    r_in, r_out, r_scratch = _router_specs()
    xo, xpk, route, cnt = pl.pallas_call(
        _hgrn_out_kernel,
        grid=(n // tm,),
        in_specs=[row, row, pl.BlockSpec((tm, D_MODEL), lambda i: (i, 2)), vec,
                  pl.BlockSpec((D_MODEL, D_MODEL), lambda i: (0, 0)), row, vec, vec] + r_in,
        out_specs=[row, pl.BlockSpec((tm, PACKED), lambda i: (i, 0))] + r_out,
        out_shape=[jax.ShapeDtypeStruct((n, D_MODEL), F32),
                   jax.ShapeDtypeStruct((n, PACKED), jnp.int32)] + _router_out_shapes(n),
        scratch_shapes=r_scratch,
        compiler_params=_cparams("arbitrary"),
        name="hgrn_out",
    )(o_f, o_b, proj_a, jnp.tile(norm_w, HG_HEADS).reshape(1, D_MODEL), w_out, x,
      g.reshape(1, D_MODEL), b.reshape(1, D_MODEL), *router_ops)
    return xo, xpk, route, _expert_counts(cnt)


def _conv_out_kernel(tiles_per_seq, bg_ref, cg_ref, h_ref, cgp_ref, hp_ref, cgn_ref, hn_ref,
                     cw_ref, w_ref, x_ref, g_ref, b_ref, rwh_ref, rwl_ref, rb_ref, tri_ref,
                     o_ref, op_ref, route_ref, cnt_ref, carry_ref):
    i = pl.program_id(0)
    tm = cg_ref.shape[0]
    u = cg_ref[...].astype(F32) * h_ref[...].astype(F32)
    first = (i % tiles_per_seq) == 0
    last = (i % tiles_per_seq) == tiles_per_seq - 1
    hr = cgp_ref.shape[0]
    u_halo_prev = cgp_ref[...].astype(F32) * hp_ref[...].astype(F32)
    u_halo_next = cgn_ref[...].astype(F32) * hn_ref[...].astype(F32)
    u_prev_row = jnp.where(first, 0.0, u_halo_prev[hr - 1:hr, :])
    u_next_row = jnp.where(last, 0.0, u_halo_next[0:1, :])
    rows = lax.broadcasted_iota(jnp.int32, u.shape, 0)
    u_prev = jnp.where(rows == 0, u_prev_row, pltpu.roll(u, 1, 0))
    u_next = jnp.where(rows == tm - 1, u_next_row, pltpu.roll(u, tm - 1, 0))
    cw = cw_ref[...]
    y = u_prev * cw[0:1, :] + u * cw[1:2, :] + u_next * cw[2:3, :]
    mix = jnp.dot((bg_ref[...].astype(F32) * y).astype(BF16), w_ref[...],
                  preferred_element_type=F32)
    out = _layer_norm_rows(DN_ALPHA * x_ref[...] + mix, g_ref[...], b_ref[...])
    o_ref[...] = out
    op_ref[...] = _pack_rows(out)
    _route_tile(out, rwh_ref, rwl_ref, rb_ref, tri_ref, route_ref, cnt_ref, carry_ref)


def _conv_out(proj, conv_w, w_out, x, g, b, seq, router_ops):
    n = x.shape[0]
    tm = ROW_TILE
    halo = BF16_SUBLANES
    rh = tm // halo
    nblk = n // halo
    row = pl.BlockSpec((tm, D_MODEL), lambda i: (i, 0))
    vec = pl.BlockSpec((1, D_MODEL), lambda i: (0, 0))

    def main(stream):
        return pl.BlockSpec((tm, D_MODEL), lambda i: (i, stream))

    def prev(stream):
        return pl.BlockSpec((halo, D_MODEL), lambda i: (jnp.maximum(i * rh - 1, 0), stream))

    def nxt(stream):
        return pl.BlockSpec((halo, D_MODEL),
                            lambda i: (jnp.minimum((i + 1) * rh, nblk - 1), stream))

    r_in, r_out, r_scratch = _router_specs()
    xo, xpk, route, cnt = pl.pallas_call(
        functools.partial(_conv_out_kernel, seq // tm),
        grid=(n // tm,),
        in_specs=[main(0), main(1), main(2), prev(1), prev(2), nxt(1), nxt(2),
                  pl.BlockSpec((3, D_MODEL), lambda i: (0, 0)),
                  pl.BlockSpec((D_MODEL, D_MODEL), lambda i: (0, 0)), row, vec, vec] + r_in,
        out_specs=[row, pl.BlockSpec((tm, PACKED), lambda i: (i, 0))] + r_out,
        out_shape=[jax.ShapeDtypeStruct((n, D_MODEL), F32),
                   jax.ShapeDtypeStruct((n, PACKED), jnp.int32)] + _router_out_shapes(n),
        scratch_shapes=r_scratch,
        compiler_params=_cparams("arbitrary"),
        name="conv_out",
    )(proj, proj, proj, proj, proj, proj, proj, conv_w, w_out, x,
      g.reshape(1, D_MODEL), b.reshape(1, D_MODEL), *router_ops)
    return xo, xpk, route, _expert_counts(cnt)


def _route_tile(x, wh_ref, wl_ref, b_ref, tri_ref, o_ref, cnt_ref, carry_ref):
    @pl.when(pl.program_id(0) == 0)
    def _():
        carry_ref[...] = jnp.zeros_like(carry_ref)

    xh = x.astype(BF16)
    xl = (x - xh.astype(F32)).astype(BF16)
    wh = wh_ref[...]
    logits = (jnp.dot(xh, wh, preferred_element_type=F32)
              + jnp.dot(xl, wh, preferred_element_type=F32)
              + jnp.dot(xh, wl_ref[...], preferred_element_type=F32)) + b_ref[...]
    lane = lax.broadcasted_iota(jnp.int32, logits.shape, 1)
    lane_f = lane.astype(F32)
    gl = jnp.where(lane < MOE_GROUPS, logits, NEG_BIG)
    gmax = jnp.max(gl, axis=-1, keepdims=True)
    gsum = jnp.sum(jnp.exp(gl - gmax), axis=-1, keepdims=True)
    p_group = 1.0 / gsum
    g_sel = jnp.min(jnp.where(gl == gmax, lane_f, 1e9), axis=-1, keepdims=True)
    e_lane = lane - MOE_GROUPS
    in_grp = ((lane >= MOE_GROUPS) & (lane < MOE_GROUPS + MOE_EXPERTS)
              & ((e_lane // MOE_EXPERTS_PER_GROUP).astype(F32) == g_sel))
    el = jnp.where(in_grp, logits, NEG_BIG)
    t1 = jnp.max(el, axis=-1, keepdims=True)
    i1 = jnp.min(jnp.where(el == t1, lane_f, 1e9), axis=-1, keepdims=True)
    el2 = jnp.where(lane_f == i1, NEG_BIG, el)
    t2 = jnp.max(el2, axis=-1, keepdims=True)
    i2 = jnp.min(jnp.where(el2 == t2, lane_f, 1e9), axis=-1, keepdims=True)
    z = jnp.exp(t2 - t1)
    g1 = p_group / (1.0 + z)
    g2 = g1 * z
    out = jnp.where(lane == 0, g1, 0.0)
    out = jnp.where(lane == 1, g2, out)
    out = jnp.where(lane == 2, i1 - MOE_GROUPS, out)
    out = jnp.where(lane == 3, i2 - MOE_GROUPS, out)

    sel1 = lane_f == i1
    sel2 = lane_f == i2
    onehot = jnp.where(sel1 | sel2, 1.0, 0.0)
    prefix = jnp.dot(tri_ref[...], onehot.astype(BF16), preferred_element_type=F32)
    before = prefix + carry_ref[0:1, :]
    r1 = jnp.sum(jnp.where(sel1, before, 0.0), axis=-1, keepdims=True)
    r2 = jnp.sum(jnp.where(sel2, before, 0.0), axis=-1, keepdims=True)
    out = jnp.where(lane == 4, r1, out)
    out = jnp.where(lane == 5, r2, out)
    o_ref[...] = out
    carry_ref[...] = carry_ref[...] + jnp.sum(onehot, axis=0, keepdims=True)
    cnt_ref[...] = carry_ref[...]


def _router_operands(w_group, b_group, w_expert, b_expert):
    pad = LANES - MOE_GROUPS - MOE_EXPERTS
    w = jnp.concatenate([w_group, w_expert, jnp.zeros((D_MODEL, pad), F32)], axis=1)
    wh = w.astype(BF16)
    wl = (w - wh.astype(F32)).astype(BF16)
    b = jnp.concatenate([b_group, b_expert, jnp.zeros((pad,), F32)]).reshape(1, LANES)
    r = jnp.arange(ROW_TILE, dtype=jnp.int32)
    tri = (r[None, :] < r[:, None]).astype(BF16)
    return wh, wl, b, tri


def _router_specs():
    in_specs = [pl.BlockSpec((D_MODEL, LANES), lambda i: (0, 0)),
                pl.BlockSpec((D_MODEL, LANES), lambda i: (0, 0)),
                pl.BlockSpec((1, LANES), lambda i: (0, 0)),
                pl.BlockSpec((ROW_TILE, ROW_TILE), lambda i: (0, 0))]
    out_specs = [pl.BlockSpec((ROW_TILE, LANES), lambda i: (i, 0)),
                 pl.BlockSpec((SUBLANES, LANES), lambda i: (0, 0))]
    scratch = [pltpu.VMEM((SUBLANES, LANES), F32)]
    return in_specs, out_specs, scratch


def _router_out_shapes(n):
    return [jax.ShapeDtypeStruct((n, LANES), F32), jax.ShapeDtypeStruct((SUBLANES, LANES), F32)]


def _expert_counts(cnt):
    return cnt[0, MOE_GROUPS:MOE_GROUPS + MOE_EXPERTS].astype(jnp.int32)


DISPATCH_TOKENS = 256
DISPATCH_BUFS = 3
WEIGHT_SLOTS = 3


def _moe_plan(route, counts, n):
    nk = n * MOE_TOPK
    n_rows = -(-nk // MOE_BLOCK) * MOE_BLOCK + MOE_EXPERTS * MOE_BLOCK
    n_blocks = n_rows // MOE_BLOCK
    padded = (counts + MOE_BLOCK - 1) // MOE_BLOCK * MOE_BLOCK
    pad_end = jnp.cumsum(padded)
    pad_start = pad_end - padded
    e = route[:, MOE_TOPK:2 * MOE_TOPK].astype(jnp.int32)
    rank = route[:, 2 * MOE_TOPK:3 * MOE_TOPK].astype(jnp.int32)
    ids = jnp.arange(MOE_EXPERTS, dtype=jnp.int32)
    base = jnp.sum(jnp.where(e[:, :, None] == ids, pad_start, 0), axis=-1)
    dest = (base + rank).reshape(nk)
    block_start = jnp.arange(n_blocks, dtype=jnp.int32) * MOE_BLOCK
    block_expert = jnp.minimum(jnp.sum(block_start[:, None] >= pad_end[None, :], axis=1),
                               MOE_EXPERTS - 1).astype(jnp.int32)
    nact = (pad_end[-1:] // MOE_BLOCK).astype(jnp.int32)
    fill_start = (pad_start + counts).astype(jnp.int32)
    return dest, block_expert, nact, fill_start, pad_end.astype(jnp.int32), n_rows


def _dispatch_kernel(dest_ref, fs_ref, fe_ref, nact_ref, x_hbm, xs_hbm, xbuf, zbuf, lsem, sem,
                     zsem):
    i = pl.program_id(0)
    nsteps = pl.num_programs(0)
    slot = i % DISPATCH_BUFS
    ct = DISPATCH_TOKENS
    gt = ct // SUBLANES
    n_blocks = xs_hbm.shape[0] // MOE_BLOCK

    def load(step, sl):
        start = pl.multiple_of(step * gt, gt)
        return pltpu.make_async_copy(x_hbm.at[pl.ds(start, gt)], xbuf.at[sl], lsem.at[sl])

    def wait_step(sl):
        for _ in range(MOE_TOPK):
            pltpu.make_async_copy(x_hbm.at[pl.ds(0, gt)], xbuf.at[sl], sem.at[sl]).wait()

    @pl.when(i == 0)
    def _():
        for s in range(DISPATCH_BUFS - 1):
            load(s, s).start()

    load(i, slot).wait()

    def body(g, c):
        base = (i * ct + g * SUBLANES) * MOE_TOPK
        for t in range(SUBLANES):
            for k in range(MOE_TOPK):
                d = dest_ref[base + t * MOE_TOPK + k]
                pltpu.make_async_copy(xbuf.at[slot, g, pl.ds(t, 1)], xs_hbm.at[pl.ds(d, 1)],
                                      sem.at[slot]).start()
        return c
    lax.fori_loop(0, gt, body, 0)

    @pl.when(i >= 1)
    def _():
        wait_step((i - 1) % DISPATCH_BUFS)

    @pl.when(i + DISPATCH_BUFS - 1 < nsteps)
    def _():
        load(i + DISPATCH_BUFS - 1, (i + DISPATCH_BUFS - 1) % DISPATCH_BUFS).start()

    @pl.when(i == nsteps - 1)
    def _():
        wait_step(slot)
        zbuf[...] = jnp.zeros_like(zbuf)

        def fill_rows(wait):
            def per_expert(e, c):
                def per_row(r, c2):
                    cp = pltpu.make_async_copy(zbuf.at[pl.ds(0, 1)], xs_hbm.at[pl.ds(r, 1)],
                                               zsem.at[0])
                    cp.wait() if wait else cp.start()
                    return c2
                lax.fori_loop(fs_ref[e], fe_ref[e], per_row, 0)
                return c
            lax.fori_loop(0, MOE_EXPERTS, per_expert, 0)

        def fill_blocks(wait):
            def per_block(b, c):
                start = pl.multiple_of(b * MOE_BLOCK, MOE_BLOCK)
                cp = pltpu.make_async_copy(zbuf, xs_hbm.at[pl.ds(start, MOE_BLOCK)], zsem.at[1])
                cp.wait() if wait else cp.start()
                return c
            lax.fori_loop(nact_ref[0], n_blocks, per_block, 0)

        fill_rows(False)
        fill_blocks(False)
        fill_rows(True)
        fill_blocks(True)


def _dispatch(x, dest, fill_start, fill_end, nact, n_rows):
    n, width = x.shape
    grid_spec = pltpu.PrefetchScalarGridSpec(
        num_scalar_prefetch=4,
        grid=(n // DISPATCH_TOKENS,),
        in_specs=[pl.BlockSpec(memory_space=pl.ANY)],
        out_specs=pl.BlockSpec(memory_space=pl.ANY),
        scratch_shapes=[pltpu.VMEM((DISPATCH_BUFS, DISPATCH_TOKENS // SUBLANES, SUBLANES, width),
                                   x.dtype),
                        pltpu.VMEM((MOE_BLOCK, width), x.dtype),
                        pltpu.SemaphoreType.DMA((DISPATCH_BUFS,)),
                        pltpu.SemaphoreType.DMA((DISPATCH_BUFS,)),
                        pltpu.SemaphoreType.DMA((2,))],
    )
    return pl.pallas_call(
        _dispatch_kernel,
        grid_spec=grid_spec,
        out_shape=jax.ShapeDtypeStruct((n_rows, width), x.dtype),
        compiler_params=_cparams("arbitrary"),
        name="dispatch",
    )(dest, fill_start, fill_end, nact, x.reshape(n // SUBLANES, SUBLANES, width))


def _moe_ffn_kernel(e0, be_ref, nact_ref, ord_ref, nxt_ref, nxt2_ref, xs_ref, wup_hbm, wdn_hbm,
                    ys_ref, wup_f, wdn_f, wup_bf, wdn_bf, wsem):
    i = pl.program_id(0)
    active = i < nact_ref[0]
    new_expert = (i == 0) | (be_ref[i] != be_ref[jnp.maximum(i - 1, 0)])

    def fetch(e, sl):
        return (pltpu.make_async_copy(wup_hbm.at[e0 + e], wup_f.at[sl], wsem.at[0, sl]),
                pltpu.make_async_copy(wdn_hbm.at[e0 + e], wdn_f.at[sl], wsem.at[1, sl]))

    @pl.when(active & (i == 0))
    def _():
        for cp in fetch(be_ref[0], 0):
            cp.start()

        @pl.when(nxt_ref[0] >= 0)
        def _():
            for cp in fetch(nxt_ref[0], 1):
                cp.start()

    @pl.when(active & new_expert)
    def _():
        k = ord_ref[i]
        sl = k % WEIGHT_SLOTS
        for cp in fetch(be_ref[i], sl):
            cp.wait()

        @pl.when(nxt2_ref[i] >= 0)
        def _():
            for cp in fetch(nxt2_ref[i], (k + 2) % WEIGHT_SLOTS):
                cp.start()

        wup_bf[...] = wup_f[sl].astype(BF16)
        wdn_bf[...] = wdn_f[sl].astype(BF16)

    @pl.when(active)
    def _():
        x_lo, x_hi = _unpack_rows(xs_ref[...])
        xb = jnp.concatenate([x_lo.astype(BF16), x_hi.astype(BF16)], axis=1)
        hcat = jnp.dot(xb, wup_bf[...], preferred_element_type=F32)
        hg = hcat[:, :MOE_D_EXPERT]
        hu = hcat[:, MOE_D_EXPERT:]
        act = (hg * jax.nn.sigmoid(hg) * hu).astype(BF16)
        ys_ref[...] = _pack_rows(jnp.dot(act, wdn_bf[...], preferred_element_type=F32))

    @pl.when(jnp.logical_not(active))
    def _():
        ys_ref[...] = jnp.zeros_like(ys_ref)


def _moe_ffn(xs, block_expert, nact, counts, w_up, w_down, layer):
    n_rows = xs.shape[0]
    n_blocks = n_rows // MOE_BLOCK
    first = jnp.concatenate([jnp.ones((1,), jnp.int32),
                             (block_expert[1:] != block_expert[:-1]).astype(jnp.int32)])
    ordinal = (jnp.cumsum(first) - 1).astype(jnp.int32)
    ids = jnp.arange(MOE_EXPERTS, dtype=jnp.int32)
    later = (ids[None, :] > ids[:, None]) & (counts[None, :] > 0)
    nxt = jnp.min(jnp.where(later, ids[None, :], MOE_EXPERTS), axis=1)
    nxt = jnp.where(nxt == MOE_EXPERTS, -1, nxt).astype(jnp.int32)
    next_expert = nxt[block_expert]
    nxt2 = jnp.where(nxt >= 0, nxt[jnp.maximum(nxt, 0)], -1)
    next2_expert = nxt2[block_expert]
    row = pl.BlockSpec((MOE_BLOCK, PACKED), lambda i, *_: (i, 0))
    grid_spec = pltpu.PrefetchScalarGridSpec(
        num_scalar_prefetch=5,
        grid=(n_blocks,),
        in_specs=[row, pl.BlockSpec(memory_space=pl.ANY), pl.BlockSpec(memory_space=pl.ANY)],
        out_specs=row,
        scratch_shapes=[pltpu.VMEM((WEIGHT_SLOTS, D_MODEL, 2 * MOE_D_EXPERT), F32),
                        pltpu.VMEM((WEIGHT_SLOTS, MOE_D_EXPERT, D_MODEL), F32),
                        pltpu.VMEM((D_MODEL, 2 * MOE_D_EXPERT), BF16),
                        pltpu.VMEM((MOE_D_EXPERT, D_MODEL), BF16),
                        pltpu.SemaphoreType.DMA((2, WEIGHT_SLOTS))],
    )
    return pl.pallas_call(
        functools.partial(_moe_ffn_kernel, layer * MOE_EXPERTS),
        grid_spec=grid_spec,
        out_shape=jax.ShapeDtypeStruct((n_rows, PACKED), jnp.int32),
        compiler_params=_cparams("arbitrary"),
        name="moe_ffn",
    )(block_expert, nact, ordinal, next_expert, next2_expert, xs, w_up, w_down)


def _combine_ln_kernel(dest_ref, x_ref, route_ref, ys_hbm, g_ref, b_ref, o_ref, ob_ref, ybuf,
                       sem):
    i = pl.program_id(0)
    nsteps = pl.num_programs(0)
    slot = i % 2
    tm = x_ref.shape[0]

    def start_gather(step, sl):
        def body(g, c):
            base = (step * tm + g * SUBLANES) * MOE_TOPK
            for t in range(SUBLANES):
                for k in range(MOE_TOPK):
                    d = dest_ref[base + t * MOE_TOPK + k]
                    pltpu.make_async_copy(ys_hbm.at[pl.ds(d, 1)], ybuf.at[sl, k, g, pl.ds(t, 1)],
                                          sem.at[sl]).start()
            return c
        lax.fori_loop(0, tm // SUBLANES, body, 0)

    @pl.when(i == 0)
    def _():
        start_gather(0, 0)

    @pl.when(i + 1 < nsteps)
    def _():
        start_gather(i + 1, 1 - slot)

    for k in range(MOE_TOPK):
        pltpu.make_async_copy(ybuf.at[1 - slot, k], ybuf.at[slot, k], sem.at[slot]).wait()
    route = route_ref[...]
    y0_lo, y0_hi = _unpack_rows(ybuf[slot, 0].reshape(tm, PACKED))
    y1_lo, y1_hi = _unpack_rows(ybuf[slot, 1].reshape(tm, PACKED))
    g0 = route[:, 0:1]
    g1 = route[:, 1:2]
    ffn = jnp.concatenate([g0 * y0_lo + g1 * y1_lo, g0 * y0_hi + g1 * y1_hi], axis=1)
    out = _layer_norm_rows(DN_ALPHA * x_ref[...] + ffn, g_ref[...], b_ref[...])
    o_ref[...] = out
    ob_ref[...] = out.astype(BF16)


def _combine_ln(x, route, ys, dest, g, b):
    n = x.shape[0]
    tm = ROW_TILE
    vec = pl.BlockSpec((1, D_MODEL), lambda i, d: (0, 0))
    row = pl.BlockSpec((tm, D_MODEL), lambda i, d: (i, 0))
    grid_spec = pltpu.PrefetchScalarGridSpec(
        num_scalar_prefetch=1,
        grid=(n // tm,),
        in_specs=[pl.BlockSpec((tm, D_MODEL), lambda i, d: (i, 0)),
                  pl.BlockSpec((tm, LANES), lambda i, d: (i, 0)),
                  pl.BlockSpec(memory_space=pl.ANY), vec, vec],
        out_specs=[row, row],
        scratch_shapes=[pltpu.VMEM((2, MOE_TOPK, tm // SUBLANES, SUBLANES, PACKED), jnp.int32),
                        pltpu.SemaphoreType.DMA((2,))],
    )
    return pl.pallas_call(
        _combine_ln_kernel,
        grid_spec=grid_spec,
        out_shape=[jax.ShapeDtypeStruct((n, D_MODEL), F32),
                   jax.ShapeDtypeStruct((n, D_MODEL), BF16)],
        compiler_params=_cparams("arbitrary"),
        name="combine_ln",
    )(dest, x, route, ys, g.reshape(1, D_MODEL), b.reshape(1, D_MODEL))


def kernel(x, hg_w_in, hg_lb_logits, hg_norm_w, hg_w_out, cv_w_in, cv_w, cv_w_out, ln_g, ln_b,
           moe_w_group, moe_b_group, moe_w_expert, moe_b_expert, moe_w_up, moe_w_down):
    batch, seq, d = x.shape
    n = batch * seq
    xf = x.reshape(n, d)
    w_up_all = moe_w_up.reshape(DEPTH * MOE_EXPERTS, D_MODEL, 2 * MOE_D_EXPERT)
    w_down_all = moe_w_down.reshape(DEPTH * MOE_EXPERTS, MOE_D_EXPERT, D_MODEL)
    xin = xf
    for layer in range(DEPTH):
        j = layer // 2
        router_ops = _router_operands(moe_w_group[layer], moe_b_group[layer],
                                      moe_w_expert[layer], moe_b_expert[layer])
        if layer % 2 == 0:
            w = hg_w_in[j]
            w_a = jnp.concatenate([w[:, :D_MODEL], w[:, 3 * D_MODEL:]], axis=1).astype(BF16)
            w_f = w[:, D_MODEL:3 * D_MODEL].astype(BF16)
            proj_a = _matmul(xin, w_a, BF16)
            proj_f = _matmul(xin, w_f, F32)
            o_f, o_b = _gla(proj_a, proj_f, hg_lb_logits, layer, batch, seq)
            xf, xpk, route, counts = _hgrn_out(
                o_f, o_b, proj_a, hg_norm_w[j], hg_w_out[j].astype(BF16), xf,
                ln_g[layer, 0], ln_b[layer, 0], router_ops)
        else:
            proj = _matmul(xin, cv_w_in[j].astype(BF16), BF16)
            xf, xpk, route, counts = _conv_out(
                proj, cv_w[j], cv_w_out[j].astype(BF16), xf,
                ln_g[layer, 0], ln_b[layer, 0], seq, router_ops)
        dest, block_expert, nact, fill_start, fill_end, n_rows = _moe_plan(route, counts, n)
        xs = _dispatch(xpk, dest, fill_start, fill_end, nact, n_rows)
        ys = _moe_ffn(xs, block_expert, nact, counts, w_up_all, w_down_all, layer)
        xf, xin = _combine_ln(xf, route, ys, dest, ln_g[layer, 1], ln_b[layer, 1])
    return xf.reshape(batch, seq, d)
```

```python
import functools

import jax
import jax.numpy as jnp
from jax import lax
from jax.experimental import pallas as pl
from jax.experimental.pallas import tpu as pltpu

D_MODEL = 1024
DEPTH = 4
HG_DK = 128
HG_HEADS = D_MODEL // HG_DK
HG_STREAMS = 5
CONV_STREAMS = 3
MOE_GROUPS = 4
MOE_EXPERTS_PER_GROUP = 8
MOE_EXPERTS = MOE_GROUPS * MOE_EXPERTS_PER_GROUP
MOE_TOPK = 2
MOE_D_EXPERT = D_MODEL // 2
MOE_BLOCK = 256
DN_ALPHA = (2.0 * DEPTH) ** 0.25
LN_EPS = 1e-5
RMS_EPS = 1e-6

LANES = 128
SUBLANES = 8
BF16_SUBLANES = 16
VMEM_LIMIT = 48 * 1024 * 1024
GLA_CHUNK = 64
GLA_TBLOCK = 1024
GLA_UNROLL = 2
ROW_TILE = 256
NEG_BIG = -1e30

BF16 = jnp.bfloat16
F32 = jnp.float32


def _cparams(*sem):
    return pltpu.CompilerParams(dimension_semantics=sem, vmem_limit_bytes=VMEM_LIMIT)


def _mm_kernel(x_ref, w_ref, o_ref):
    o_ref[...] = jnp.dot(x_ref[...].astype(BF16), w_ref[...],
                         preferred_element_type=F32).astype(o_ref.dtype)


def _matmul(x, w, out_dtype, tm=512, tn=1024):
    n, k = x.shape
    nn = w.shape[1]
    return pl.pallas_call(
        _mm_kernel,
        grid=(nn // tn, n // tm),
        in_specs=[pl.BlockSpec((tm, k), lambda j, i: (i, 0)),
                  pl.BlockSpec((k, tn), lambda j, i: (0, j))],
        out_specs=pl.BlockSpec((tm, tn), lambda j, i: (i, j)),
        out_shape=jax.ShapeDtypeStruct((n, nn), out_dtype),
        compiler_params=_cparams("arbitrary", "arbitrary"),
        name="in_proj",
    )(x, w)


PACKED = D_MODEL // 2


def _pack_rows(x):
    lo = pltpu.bitcast(x[:, :PACKED].astype(BF16).astype(F32), jnp.int32)
    hi = pltpu.bitcast(x[:, PACKED:].astype(BF16).astype(F32), jnp.int32)
    return lax.shift_right_logical(lo, 16) | hi


def _unpack_rows(w):
    lo = pltpu.bitcast(lax.shift_left(w, 16), F32)
    hi = pltpu.bitcast(w & jnp.int32(-65536), F32)
    return lo, hi


def _layer_norm_rows(y, g, b):
    mu = jnp.mean(y, axis=-1, keepdims=True)
    yc = y - mu
    var = jnp.mean(yc * yc, axis=-1, keepdims=True)
    return yc * lax.rsqrt(var + LN_EPS) * g + b


def _rows(x, blocks, size):
    parts = [x[b * size:(b + 1) * size, :] for b in blocks]
    return parts[0] if len(parts) == 1 else jnp.concatenate(parts, axis=0)


def _gla_chunk(q_s, k_s, v_s, lf_s, g_s, acc_s, o_ref, st_ref, d, start, slot, reverse):
    c = GLA_CHUNK
    t8 = SUBLANES
    nt_dims = (((1,), (1,)), ((), ()))

    def tt(ref, i):
        return ref[d, pl.ds(start + i, t8, stride=t8), :]

    lf = [tt(lf_s, i) for i in range(t8)]
    qt = [tt(q_s, i) for i in range(t8)]
    kt = [tt(k_s, i) for i in range(t8)]
    vt = [tt(v_s, i) for i in range(t8)]

    gi = [None] * t8
    prev = None
    for i in (range(t8 - 1, -1, -1) if reverse else range(t8)):
        gi[i] = lf[i] if prev is None else prev + lf[i]
        prev = gi[i]
    tot = prev
    sub = lax.broadcasted_iota(jnp.int32, (t8, LANES), 0)
    incl = tot
    s = 1
    while s < t8:
        if reverse:
            incl = incl + jnp.where(sub + s < t8, pltpu.roll(incl, t8 - s, 0), 0.0)
        else:
            incl = incl + jnp.where(sub >= s, pltpu.roll(incl, s, 0), 0.0)
        s *= 2
    excl = incl - tot
    for i in range(t8):
        g_s[slot, pl.ds(i, t8, stride=t8), :] = gi[i] + excl

    acc = []
    for i in range(t8):
        a_i = jnp.sum(qt[i] * kt[i], axis=-1, keepdims=True) * vt[i]
        for r in (range(i + 1, t8) if reverse else range(i)):
            w = qt[i] * kt[r] * jnp.exp(gi[i] - gi[r])
            a_i = a_i + jnp.sum(w, axis=-1, keepdims=True) * vt[r]
        acc.append(a_i)

    sl = pl.ds(start, c)
    q = q_s[d, sl, :]
    kk = k_s[d, sl, :]
    v = v_s[d, sl, :]
    g = g_s[slot]
    vb = v.astype(BF16)
    st = st_ref[d]

    o = lax.dot_general((q * jnp.exp(g)).astype(BF16), st.astype(BF16), nt_dims,
                        preferred_element_type=F32)

    half = c // 2
    qd_l, kd_l, v_l, q_tiles = [], [], [], []
    b = 2 * t8
    while b <= c:
        h = b // 2
        nblk = c // b
        lo = [2 * m for m in range(nblk)]
        hi = [2 * m + 1 for m in range(nblk)]
        q_half, k_half = (lo, hi) if reverse else (hi, lo)
        refs = [g[m * b + h:m * b + h + 1, :] if reverse else g[m * b + h - 1:m * b + h, :]
                for m in range(nblk)]
        gref = jnp.concatenate([jnp.broadcast_to(r, (h, LANES)) for r in refs], axis=0) \
            if nblk > 1 else jnp.broadcast_to(refs[0], (h, LANES))
        qd_l.append(_rows(q, q_half, h) * jnp.exp(_rows(g, q_half, h) - gref))
        kd_l.append(_rows(kk, k_half, h) * jnp.exp(gref - _rows(g, k_half, h)))
        v_l.append(_rows(v, k_half, h))
        q_tiles.append([hb * (h // t8) + t for hb in q_half for t in range(h // t8)])
        b *= 2
    nlev = len(qd_l)
    p = lax.dot_general(jnp.concatenate(qd_l, axis=0).astype(BF16),
                        jnp.concatenate(kd_l, axis=0).astype(BF16), nt_dims,
                        preferred_element_type=F32)
    ii = lax.broadcasted_iota(jnp.int32, p.shape, 0)
    jj = lax.broadcasted_iota(jnp.int32, p.shape, 1)
    keep = None
    for lv in range(nlev):
        h = t8 << lv
        m = (ii // half == lv) & (jj // half == lv) & ((ii // h) == (jj // h))
        keep = m if keep is None else keep | m
    p = jnp.where(keep, p, 0.0)
    res = jnp.dot(p.astype(BF16), jnp.concatenate(v_l, axis=0).astype(BF16),
                  preferred_element_type=F32)
    contrib = [None] * (c // t8)
    for lv in range(nlev):
        for n_, tile in enumerate(q_tiles[lv]):
            piece = res[lv * half + n_ * t8:lv * half + (n_ + 1) * t8, :]
            contrib[tile] = piece if contrib[tile] is None else contrib[tile] + piece
    zero = jnp.zeros((t8, LANES), F32)
    o = o + jnp.concatenate([zero if p_ is None else p_ for p_ in contrib], axis=0)

    gl = g[0:1, :] if reverse else g[c - 1:c, :]
    kd = (kk * jnp.exp(gl - g)).astype(BF16)
    upd = lax.dot_general(vb, kd, (((0,), (0,)), ((), ())), preferred_element_type=F32)
    st_ref[d] = st * jnp.exp(gl) + upd

    for i in range(t8):
        acc_s[slot, pl.ds(i, t8, stride=t8), :] = acc[i]
    o_ref[sl, :] = (o + acc_s[slot]).astype(o_ref.dtype)


def _gla_kernel(layer, lbl_ref, qf_ref, ff_ref, vf_ref, qb_ref, fb_ref, vb_ref,
                of_ref, ob_ref, st_ref, q_s, k_s, v_s, lf_s, g_s, acc_s):
    tb = GLA_TBLOCK
    nc = tb // GLA_CHUNK

    @pl.when(pl.program_id(2) == 0)
    def _():
        st_ref[...] = jnp.zeros_like(st_ref)

    if layer > 0:
        lg = lbl_ref[...]
        e = jnp.exp(lg - jnp.max(lg, axis=0, keepdims=True))
        sm = e / jnp.sum(e, axis=0, keepdims=True)
        lb = sm[1]
        for l in range(2, layer + 1):
            lb = lb + sm[l]

    for d, (q_ref, f_ref, v_ref) in enumerate(((qf_ref, ff_ref, vf_ref),
                                               (qb_ref, fb_ref, vb_ref))):
        qx = q_ref[...].astype(F32)
        fx = f_ref[...]
        v_s[d] = v_ref[...].astype(F32)
        t = jnp.exp(-jnp.abs(fx))
        r = 1.0 / (1.0 + t)
        logsig = jnp.minimum(fx, 0.0) - jnp.log(1.0 + t)
        sig_neg = jnp.where(fx >= 0.0, t * r, r)
        if layer > 0:
            lbd = lb[d:d + 1, :]
            a = jnp.log(lbd)
            bb = jnp.log1p(-lbd) + logsig
            lf = jnp.maximum(a, bb) + jnp.log(1.0 + jnp.exp(-jnp.abs(a - bb)))
            kk = (1.0 - lbd) * sig_neg
        else:
            lf = logsig
            kk = sig_neg
        q_s[d] = qx * (0.5 + 0.5 * jnp.tanh(0.5 * qx))
        k_s[d] = kk
        lf_s[d] = lf

    def body(ci, carry):
        for u in range(GLA_UNROLL):
            cf = ci * GLA_UNROLL + u
            sf = pl.multiple_of(cf * GLA_CHUNK, GLA_CHUNK)
            sb = pl.multiple_of((nc - 1 - cf) * GLA_CHUNK, GLA_CHUNK)
            _gla_chunk(q_s, k_s, v_s, lf_s, g_s, acc_s, of_ref, st_ref, 0, sf, 2 * u, False)
            _gla_chunk(q_s, k_s, v_s, lf_s, g_s, acc_s, ob_ref, st_ref, 1, sb, 2 * u + 1, True)
        return carry

    lax.fori_loop(0, nc // GLA_UNROLL, body, 0)


def _gla(proj_a, proj_f, lb_logits, layer, batch, seq):
    n = proj_a.shape[0]
    tb = GLA_TBLOCK
    nb = seq // tb
    h8 = HG_HEADS

    def spec(stream, rev):
        if rev:
            return pl.BlockSpec((tb, LANES), lambda b, h, c: (b * nb + nb - 1 - c, stream * h8 + h))
        return pl.BlockSpec((tb, LANES), lambda b, h, c: (b * nb + c, stream * h8 + h))

    o_f_spec = pl.BlockSpec((tb, LANES), lambda b, h, c: (b * nb + c, h))
    o_b_spec = pl.BlockSpec((tb, LANES), lambda b, h, c: (b * nb + nb - 1 - c, h))
    return pl.pallas_call(
        functools.partial(_gla_kernel, layer),
        grid=(batch, h8, nb),
        in_specs=[pl.BlockSpec((DEPTH, 2, LANES), lambda b, h, c: (0, 0, h)),
                  spec(0, False), spec(0, False), spec(1, False),
                  spec(0, True), spec(1, True), spec(1, True)],
        out_specs=[o_f_spec, o_b_spec],
        out_shape=[jax.ShapeDtypeStruct((n, D_MODEL), BF16)] * 2,
        scratch_shapes=[pltpu.VMEM((2, HG_DK, HG_DK), F32)]
        + [pltpu.VMEM((2, tb, LANES), F32)] * 4
        + [pltpu.VMEM((2 * GLA_UNROLL, GLA_CHUNK, LANES), F32)] * 2,
        compiler_params=_cparams("arbitrary", "arbitrary", "arbitrary"),
        name="gla",
    )(lb_logits, proj_a, proj_f, proj_a, proj_a, proj_f, proj_a)


def _hgrn_out_kernel(of_ref, ob_ref, gate_ref, nw_ref, w_ref, x_ref, g_ref, b_ref,
                     rwh_ref, rwl_ref, rb_ref, tri_ref,
                     o_ref, op_ref, route_ref, cnt_ref, carry_ref):
    o = of_ref[...].astype(F32) + ob_ref[...].astype(F32)
    parts = []
    for h in range(HG_HEADS):
        oh = o[:, h * LANES:(h + 1) * LANES]
        ms = jnp.mean(oh * oh, axis=-1, keepdims=True)
        parts.append(oh * lax.rsqrt(ms + RMS_EPS))
    gate = gate_ref[...].astype(F32)
    y = jnp.concatenate(parts, axis=-1) * nw_ref[...] * (gate * jax.nn.sigmoid(gate))
    mix = jnp.dot(y.astype(BF16), w_ref[...], preferred_element_type=F32)
    out = _layer_norm_rows(DN_ALPHA * x_ref[...] + mix, g_ref[...], b_ref[...])
    o_ref[...] = out
    op_ref[...] = _pack_rows(out)
    _route_tile(out, rwh_ref, rwl_ref, rb_ref, tri_ref, route_ref, cnt_ref, carry_ref)


def _hgrn_out(o_f, o_b, proj_a, norm_w, w_out, x, g, b, router_ops):
    n = x.shape[0]
    tm = ROW_TILE
    row = pl.BlockSpec((tm, D_MODEL), lambda i: (i, 0))
    vec = pl.BlockSpec((1, D_MODEL), lambda i: (0, 0))
    r_in, r_out, r_scratch = _router_specs()
    xo, xpk, route, cnt = pl.pallas_call(
        _hgrn_out_kernel,
        grid=(n // tm,),
        in_specs=[row, row, pl.BlockSpec((tm, D_MODEL), lambda i: (i, 2)), vec,
                  pl.BlockSpec((D_MODEL, D_MODEL), lambda i: (0, 0)), row, vec, vec] + r_in,
        out_specs=[row, pl.BlockSpec((tm, PACKED), lambda i: (i, 0))] + r_out,
        out_shape=[jax.ShapeDtypeStruct((n, D_MODEL), F32),
                   jax.ShapeDtypeStruct((n, PACKED), jnp.int32)] + _router_out_shapes(n),
        scratch_shapes=r_scratch,
        compiler_params=_cparams("arbitrary"),
        name="hgrn_out",
    )(o_f, o_b, proj_a, jnp.tile(norm_w, HG_HEADS).reshape(1, D_MODEL), w_out, x,
      g.reshape(1, D_MODEL), b.reshape(1, D_MODEL), *router_ops)
    return xo, xpk, route, _expert_counts(cnt)


def _conv_out_kernel(tiles_per_seq, bg_ref, cg_ref, h_ref, cgp_ref, hp_ref, cgn_ref, hn_ref,
                     cw_ref, w_ref, x_ref, g_ref, b_ref, rwh_ref, rwl_ref, rb_ref, tri_ref,
                     o_ref, op_ref, route_ref, cnt_ref, carry_ref):
    i = pl.program_id(0)
    tm = cg_ref.shape[0]
    u = cg_ref[...].astype(F32) * h_ref[...].astype(F32)
    first = (i % tiles_per_seq) == 0
    last = (i % tiles_per_seq) == tiles_per_seq - 1
    hr = cgp_ref.shape[0]
    u_halo_prev = cgp_ref[...].astype(F32) * hp_ref[...].astype(F32)
    u_halo_next = cgn_ref[...].astype(F32) * hn_ref[...].astype(F32)
    u_prev_row = jnp.where(first, 0.0, u_halo_prev[hr - 1:hr, :])
    u_next_row = jnp.where(last, 0.0, u_halo_next[0:1, :])
    rows = lax.broadcasted_iota(jnp.int32, u.shape, 0)
    u_prev = jnp.where(rows == 0, u_prev_row, pltpu.roll(u, 1, 0))
    u_next = jnp.where(rows == tm - 1, u_next_row, pltpu.roll(u, tm - 1, 0))
    cw = cw_ref[...]
    y = u_prev * cw[0:1, :] + u * cw[1:2, :] + u_next * cw[2:3, :]
    mix = jnp.dot((bg_ref[...].astype(F32) * y).astype(BF16), w_ref[...],
                  preferred_element_type=F32)
    out = _layer_norm_rows(DN_ALPHA * x_ref[...] + mix, g_ref[...], b_ref[...])
    o_ref[...] = out
    op_ref[...] = _pack_rows(out)
    _route_tile(out, rwh_ref, rwl_ref, rb_ref, tri_ref, route_ref, cnt_ref, carry_ref)


def _conv_out(proj, conv_w, w_out, x, g, b, seq, router_ops):
    n = x.shape[0]
    tm = ROW_TILE
    halo = BF16_SUBLANES
    rh = tm // halo
    nblk = n // halo
    row = pl.BlockSpec((tm, D_MODEL), lambda i: (i, 0))
    vec = pl.BlockSpec((1, D_MODEL), lambda i: (0, 0))

    def main(stream):
        return pl.BlockSpec((tm, D_MODEL), lambda i: (i, stream))

    def prev(stream):
        return pl.BlockSpec((halo, D_MODEL), lambda i: (jnp.maximum(i * rh - 1, 0), stream))

    def nxt(stream):
        return pl.BlockSpec((halo, D_MODEL),
                            lambda i: (jnp.minimum((i + 1) * rh, nblk - 1), stream))

    r_in, r_out, r_scratch = _router_specs()
    xo, xpk, route, cnt = pl.pallas_call(
        functools.partial(_conv_out_kernel, seq // tm),
        grid=(n // tm,),
        in_specs=[main(0), main(1), main(2), prev(1), prev(2), nxt(1), nxt(2),
                  pl.BlockSpec((3, D_MODEL), lambda i: (0, 0)),
                  pl.BlockSpec((D_MODEL, D_MODEL), lambda i: (0, 0)), row, vec, vec] + r_in,
        out_specs=[row, pl.BlockSpec((tm, PACKED), lambda i: (i, 0))] + r_out,
        out_shape=[jax.ShapeDtypeStruct((n, D_MODEL), F32),
                   jax.ShapeDtypeStruct((n, PACKED), jnp.int32)] + _router_out_shapes(n),
        scratch_shapes=r_scratch,
        compiler_params=_cparams("arbitrary"),
        name="conv_out",
    )(proj, proj, proj, proj, proj, proj, proj, conv_w, w_out, x,
      g.reshape(1, D_MODEL), b.reshape(1, D_MODEL), *router_ops)
    return xo, xpk, route, _expert_counts(cnt)


def _route_tile(x, wh_ref, wl_ref, b_ref, tri_ref, o_ref, cnt_ref, carry_ref):
    @pl.when(pl.program_id(0) == 0)
    def _():
        carry_ref[...] = jnp.zeros_like(carry_ref)

    xh = x.astype(BF16)
    xl = (x - xh.astype(F32)).astype(BF16)
    wh = wh_ref[...]
    logits = (jnp.dot(xh, wh, preferred_element_type=F32)
              + jnp.dot(xl, wh, preferred_element_type=F32)
              + jnp.dot(xh, wl_ref[...], preferred_element_type=F32)) + b_ref[...]
    eg = MOE_EXPERTS_PER_GROUP
    lt = logits.T
    grp = lt[0:SUBLANES]
    exp_t = [lt[SUBLANES + eg * g:SUBLANES + eg * (g + 1)] for g in range(MOE_GROUPS)]
    row = lax.broadcasted_iota(jnp.int32, grp.shape, 0).astype(F32)
    gmax = jnp.max(grp, axis=0, keepdims=True)
    gsum = jnp.sum(jnp.exp(grp - gmax), axis=0, keepdims=True)
    p_group = 1.0 / gsum
    g_sel = jnp.min(jnp.where(grp == gmax, row, 99.0), axis=0, keepdims=True)
    el = exp_t[MOE_GROUPS - 1]
    for g in range(MOE_GROUPS - 2, -1, -1):
        el = jnp.where(g_sel == g, exp_t[g], el)
    t1 = jnp.max(el, axis=0, keepdims=True)
    i1 = jnp.min(jnp.where(el == t1, row, 99.0), axis=0, keepdims=True)
    el2 = jnp.where(row == i1, NEG_BIG, el)
    t2 = jnp.max(el2, axis=0, keepdims=True)
    i2 = jnp.min(jnp.where(el2 == t2, row, 99.0), axis=0, keepdims=True)
    z = jnp.exp(t2 - t1)
    g1 = p_group / (1.0 + z)
    g2 = g1 * z

    is1 = row == i1
    is2 = row == i2
    in_g = [g_sel == g for g in range(MOE_GROUPS)]
    onehot = jnp.concatenate([jnp.where(in_g[g] & (is1 | is2), 1.0, 0.0)
                              for g in range(MOE_GROUPS)], axis=0)
    prefix = jnp.dot(onehot.astype(BF16), tri_ref[...], preferred_element_type=F32)
    before = prefix + carry_ref[...]
    r1 = jnp.zeros_like(g1)
    r2 = jnp.zeros_like(g1)
    for g in range(MOE_GROUPS):
        bg = before[eg * g:eg * (g + 1)]
        r1 = r1 + jnp.sum(jnp.where(in_g[g] & is1, bg, 0.0), axis=0, keepdims=True)
        r2 = r2 + jnp.sum(jnp.where(in_g[g] & is2, bg, 0.0), axis=0, keepdims=True)
    e1 = g_sel * eg + i1
    e2 = g_sel * eg + i2
    out = jnp.zeros_like(row)
    for r, val in enumerate((g1, g2, e1, e2, r1, r2)):
        out = jnp.where(row == r, val, out)
    o_ref[...] = out
    carry_ref[...] = carry_ref[...] + jnp.sum(onehot, axis=1, keepdims=True)
    cnt_ref[...] = carry_ref[...]


def _router_operands(w_group, b_group, w_expert, b_expert):
    gpad = SUBLANES - MOE_GROUPS
    pad = LANES - SUBLANES - MOE_EXPERTS
    w = jnp.concatenate([w_group, jnp.zeros((D_MODEL, gpad), F32), w_expert,
                         jnp.zeros((D_MODEL, pad), F32)], axis=1)
    wh = w.astype(BF16)
    wl = (w - wh.astype(F32)).astype(BF16)
    b = jnp.concatenate([b_group, jnp.full((gpad,), NEG_BIG, F32), b_expert,
                         jnp.zeros((pad,), F32)]).reshape(1, LANES)
    r = jnp.arange(ROW_TILE, dtype=jnp.int32)
    tri = (r[:, None] < r[None, :]).astype(BF16)
    return wh, wl, b, tri


def _router_specs():
    in_specs = [pl.BlockSpec((D_MODEL, LANES), lambda i: (0, 0)),
                pl.BlockSpec((D_MODEL, LANES), lambda i: (0, 0)),
                pl.BlockSpec((1, LANES), lambda i: (0, 0)),
                pl.BlockSpec((ROW_TILE, ROW_TILE), lambda i: (0, 0))]
    out_specs = [pl.BlockSpec((SUBLANES, ROW_TILE), lambda i: (0, i)),
                 pl.BlockSpec((MOE_EXPERTS, ROW_TILE), lambda i: (0, 0))]
    scratch = [pltpu.VMEM((MOE_EXPERTS, ROW_TILE), F32)]
    return in_specs, out_specs, scratch


def _router_out_shapes(n):
    return [jax.ShapeDtypeStruct((SUBLANES, n), F32),
            jax.ShapeDtypeStruct((MOE_EXPERTS, ROW_TILE), F32)]


def _expert_counts(cnt):
    return cnt[:, 0].astype(jnp.int32)


DISPATCH_TOKENS = 256
DISPATCH_BUFS = 3
WEIGHT_SLOTS = 3


def _moe_plan(route, counts, n):
    nk = n * MOE_TOPK
    n_rows = -(-nk // MOE_BLOCK) * MOE_BLOCK + MOE_EXPERTS * MOE_BLOCK
    n_blocks = n_rows // MOE_BLOCK
    padded = (counts + MOE_BLOCK - 1) // MOE_BLOCK * MOE_BLOCK
    pad_end = jnp.cumsum(padded)
    pad_start = pad_end - padded
    e = route[MOE_TOPK:2 * MOE_TOPK].T.astype(jnp.int32)
    rank = route[2 * MOE_TOPK:3 * MOE_TOPK].T.astype(jnp.int32)
    ids = jnp.arange(MOE_EXPERTS, dtype=jnp.int32)
    base = jnp.sum(jnp.where(e[:, :, None] == ids, pad_start, 0), axis=-1)
    dest = (base + rank).reshape(nk)
    block_start = jnp.arange(n_blocks, dtype=jnp.int32) * MOE_BLOCK
    block_expert = jnp.minimum(jnp.sum(block_start[:, None] >= pad_end[None, :], axis=1),
                               MOE_EXPERTS - 1).astype(jnp.int32)
    nact = (pad_end[-1:] // MOE_BLOCK).astype(jnp.int32)
    fill_start = (pad_start + counts).astype(jnp.int32)
    return dest, block_expert, nact, fill_start, pad_end.astype(jnp.int32), n_rows


def _dispatch_kernel(dest_ref, fs_ref, fe_ref, nact_ref, x_hbm, xs_hbm, xbuf, zbuf, lsem, sem,
                     zsem):
    i = pl.program_id(0)
    nsteps = pl.num_programs(0)
    slot = i % DISPATCH_BUFS
    ct = DISPATCH_TOKENS
    gt = ct // SUBLANES
    n_blocks = xs_hbm.shape[0] // MOE_BLOCK

    def load(step, sl):
        start = pl.multiple_of(step * gt, gt)
        return pltpu.make_async_copy(x_hbm.at[pl.ds(start, gt)], xbuf.at[sl], lsem.at[sl])

    def wait_step(sl):
        for _ in range(MOE_TOPK):
            pltpu.make_async_copy(x_hbm.at[pl.ds(0, gt)], xbuf.at[sl], sem.at[sl]).wait()

    @pl.when(i == 0)
    def _():
        for s in range(DISPATCH_BUFS - 1):
            load(s, s).start()

    load(i, slot).wait()

    def body(g, c):
        base = (i * ct + g * SUBLANES) * MOE_TOPK
        for t in range(SUBLANES):
            for k in range(MOE_TOPK):
                d = dest_ref[base + t * MOE_TOPK + k]
                pltpu.make_async_copy(xbuf.at[slot, g, pl.ds(t, 1)], xs_hbm.at[pl.ds(d, 1)],
                                      sem.at[slot]).start()
        return c
    lax.fori_loop(0, gt, body, 0)

    @pl.when(i >= 1)
    def _():
        wait_step((i - 1) % DISPATCH_BUFS)

    @pl.when(i + DISPATCH_BUFS - 1 < nsteps)
    def _():
        load(i + DISPATCH_BUFS - 1, (i + DISPATCH_BUFS - 1) % DISPATCH_BUFS).start()

    @pl.when(i == nsteps - 1)
    def _():
        wait_step(slot)
        zbuf[...] = jnp.zeros_like(zbuf)

        def fill_rows(wait):
            def per_expert(e, c):
                def per_row(r, c2):
                    cp = pltpu.make_async_copy(zbuf.at[pl.ds(0, 1)], xs_hbm.at[pl.ds(r, 1)],
                                               zsem.at[0])
                    cp.wait() if wait else cp.start()
                    return c2
                lax.fori_loop(fs_ref[e], fe_ref[e], per_row, 0)
                return c
            lax.fori_loop(0, MOE_EXPERTS, per_expert, 0)

        def fill_blocks(wait):
            def per_block(b, c):
                start = pl.multiple_of(b * MOE_BLOCK, MOE_BLOCK)
                cp = pltpu.make_async_copy(zbuf, xs_hbm.at[pl.ds(start, MOE_BLOCK)], zsem.at[1])
                cp.wait() if wait else cp.start()
                return c
            lax.fori_loop(nact_ref[0], n_blocks, per_block, 0)

        fill_rows(False)
        fill_blocks(False)
        fill_rows(True)
        fill_blocks(True)


def _dispatch(x, dest, fill_start, fill_end, nact, n_rows):
    n, width = x.shape
    grid_spec = pltpu.PrefetchScalarGridSpec(
        num_scalar_prefetch=4,
        grid=(n // DISPATCH_TOKENS,),
        in_specs=[pl.BlockSpec(memory_space=pl.ANY)],
        out_specs=pl.BlockSpec(memory_space=pl.ANY),
        scratch_shapes=[pltpu.VMEM((DISPATCH_BUFS, DISPATCH_TOKENS // SUBLANES, SUBLANES, width),
                                   x.dtype),
                        pltpu.VMEM((MOE_BLOCK, width), x.dtype),
                        pltpu.SemaphoreType.DMA((DISPATCH_BUFS,)),
                        pltpu.SemaphoreType.DMA((DISPATCH_BUFS,)),
                        pltpu.SemaphoreType.DMA((2,))],
    )
    return pl.pallas_call(
        _dispatch_kernel,
        grid_spec=grid_spec,
        out_shape=jax.ShapeDtypeStruct((n_rows, width), x.dtype),
        compiler_params=_cparams("arbitrary"),
        name="dispatch",
    )(dest, fill_start, fill_end, nact, x.reshape(n // SUBLANES, SUBLANES, width))


def _moe_ffn_kernel(e0, be_ref, nact_ref, ord_ref, nxt_ref, nxt2_ref, xs_ref, wup_hbm, wdn_hbm,
                    ys_ref, wup_f, wdn_f, wup_bf, wdn_bf, wsem):
    i = pl.program_id(0)
    active = i < nact_ref[0]
    new_expert = (i == 0) | (be_ref[i] != be_ref[jnp.maximum(i - 1, 0)])

    def fetch(e, sl):
        return (pltpu.make_async_copy(wup_hbm.at[e0 + e], wup_f.at[sl], wsem.at[0, sl]),
                pltpu.make_async_copy(wdn_hbm.at[e0 + e], wdn_f.at[sl], wsem.at[1, sl]))

    @pl.when(active & (i == 0))
    def _():
        for cp in fetch(be_ref[0], 0):
            cp.start()

        @pl.when(nxt_ref[0] >= 0)
        def _():
            for cp in fetch(nxt_ref[0], 1):
                cp.start()

    @pl.when(active & new_expert)
    def _():
        k = ord_ref[i]
        sl = k % WEIGHT_SLOTS
        for cp in fetch(be_ref[i], sl):
            cp.wait()

        @pl.when(nxt2_ref[i] >= 0)
        def _():
            for cp in fetch(nxt2_ref[i], (k + 2) % WEIGHT_SLOTS):
                cp.start()

        wup_bf[...] = wup_f[sl].astype(BF16)
        wdn_bf[...] = wdn_f[sl].astype(BF16)

    @pl.when(active)
    def _():
        x_lo, x_hi = _unpack_rows(xs_ref[...])
        xb = jnp.concatenate([x_lo.astype(BF16), x_hi.astype(BF16)], axis=1)
        hcat = jnp.dot(xb, wup_bf[...], preferred_element_type=F32)
        hg = hcat[:, :MOE_D_EXPERT]
        hu = hcat[:, MOE_D_EXPERT:]
        act = (hg * jax.nn.sigmoid(hg) * hu).astype(BF16)
        ys_ref[...] = _pack_rows(jnp.dot(act, wdn_bf[...], preferred_element_type=F32))

    @pl.when(jnp.logical_not(active))
    def _():
        ys_ref[...] = jnp.zeros_like(ys_ref)


def _moe_ffn(xs, block_expert, nact, counts, w_up, w_down, layer):
    n_rows = xs.shape[0]
    n_blocks = n_rows // MOE_BLOCK
    first = jnp.concatenate([jnp.ones((1,), jnp.int32),
                             (block_expert[1:] != block_expert[:-1]).astype(jnp.int32)])
    ordinal = (jnp.cumsum(first) - 1).astype(jnp.int32)
    ids = jnp.arange(MOE_EXPERTS, dtype=jnp.int32)
    later = (ids[None, :] > ids[:, None]) & (counts[None, :] > 0)
    nxt = jnp.min(jnp.where(later, ids[None, :], MOE_EXPERTS), axis=1)
    nxt = jnp.where(nxt == MOE_EXPERTS, -1, nxt).astype(jnp.int32)
    next_expert = nxt[block_expert]
    nxt2 = jnp.where(nxt >= 0, nxt[jnp.maximum(nxt, 0)], -1)
    next2_expert = nxt2[block_expert]
    row = pl.BlockSpec((MOE_BLOCK, PACKED), lambda i, *_: (i, 0))
    grid_spec = pltpu.PrefetchScalarGridSpec(
        num_scalar_prefetch=5,
        grid=(n_blocks,),
        in_specs=[row, pl.BlockSpec(memory_space=pl.ANY), pl.BlockSpec(memory_space=pl.ANY)],
        out_specs=row,
        scratch_shapes=[pltpu.VMEM((WEIGHT_SLOTS, D_MODEL, 2 * MOE_D_EXPERT), F32),
                        pltpu.VMEM((WEIGHT_SLOTS, MOE_D_EXPERT, D_MODEL), F32),
                        pltpu.VMEM((D_MODEL, 2 * MOE_D_EXPERT), BF16),
                        pltpu.VMEM((MOE_D_EXPERT, D_MODEL), BF16),
                        pltpu.SemaphoreType.DMA((2, WEIGHT_SLOTS))],
    )
    return pl.pallas_call(
        functools.partial(_moe_ffn_kernel, layer * MOE_EXPERTS),
        grid_spec=grid_spec,
        out_shape=jax.ShapeDtypeStruct((n_rows, PACKED), jnp.int32),
        compiler_params=_cparams("arbitrary"),
        name="moe_ffn",
    )(block_expert, nact, ordinal, next_expert, next2_expert, xs, w_up, w_down)


def _combine_ln_kernel(dest_ref, x_ref, route_ref, ys_hbm, g_ref, b_ref, o_ref, ob_ref, ybuf,
                       sem):
    i = pl.program_id(0)
    nsteps = pl.num_programs(0)
    slot = i % 2
    tm = x_ref.shape[0]

    def start_gather(step, sl):
        def body(g, c):
            base = (step * tm + g * SUBLANES) * MOE_TOPK
            for t in range(SUBLANES):
                for k in range(MOE_TOPK):
                    d = dest_ref[base + t * MOE_TOPK + k]
                    pltpu.make_async_copy(ys_hbm.at[pl.ds(d, 1)], ybuf.at[sl, k, g, pl.ds(t, 1)],
                                          sem.at[sl]).start()
            return c
        lax.fori_loop(0, tm // SUBLANES, body, 0)

    @pl.when(i == 0)
    def _():
        start_gather(0, 0)

    @pl.when(i + 1 < nsteps)
    def _():
        start_gather(i + 1, 1 - slot)

    for k in range(MOE_TOPK):
        pltpu.make_async_copy(ybuf.at[1 - slot, k], ybuf.at[slot, k], sem.at[slot]).wait()
    route = route_ref[...]
    y0_lo, y0_hi = _unpack_rows(ybuf[slot, 0].reshape(tm, PACKED))
    y1_lo, y1_hi = _unpack_rows(ybuf[slot, 1].reshape(tm, PACKED))
    g0 = route[:, 0:1]
    g1 = route[:, 1:2]
    ffn = jnp.concatenate([g0 * y0_lo + g1 * y1_lo, g0 * y0_hi + g1 * y1_hi], axis=1)
    out = _layer_norm_rows(DN_ALPHA * x_ref[...] + ffn, g_ref[...], b_ref[...])
    o_ref[...] = out
    ob_ref[...] = out.astype(BF16)


def _combine_ln(x, route, ys, dest, g, b):
    n = x.shape[0]
    tm = ROW_TILE
    vec = pl.BlockSpec((1, D_MODEL), lambda i, d: (0, 0))
    row = pl.BlockSpec((tm, D_MODEL), lambda i, d: (i, 0))
    grid_spec = pltpu.PrefetchScalarGridSpec(
        num_scalar_prefetch=1,
        grid=(n // tm,),
        in_specs=[pl.BlockSpec((tm, D_MODEL), lambda i, d: (i, 0)),
                  pl.BlockSpec((tm, MOE_TOPK), lambda i, d: (i, 0)),
                  pl.BlockSpec(memory_space=pl.ANY), vec, vec],
        out_specs=[row, row],
        scratch_shapes=[pltpu.VMEM((2, MOE_TOPK, tm // SUBLANES, SUBLANES, PACKED), jnp.int32),
                        pltpu.SemaphoreType.DMA((2,))],
    )
    return pl.pallas_call(
        _combine_ln_kernel,
        grid_spec=grid_spec,
        out_shape=[jax.ShapeDtypeStruct((n, D_MODEL), F32),
                   jax.ShapeDtypeStruct((n, D_MODEL), BF16)],
        compiler_params=_cparams("arbitrary"),
        name="combine_ln",
    )(dest, x, route[0:MOE_TOPK].T, ys, g.reshape(1, D_MODEL), b.reshape(1, D_MODEL))


def kernel(x, hg_w_in, hg_lb_logits, hg_norm_w, hg_w_out, cv_w_in, cv_w, cv_w_out, ln_g, ln_b,
           moe_w_group, moe_b_group, moe_w_expert, moe_b_expert, moe_w_up, moe_w_down):
    batch, seq, d = x.shape
    n = batch * seq
    xf = x.reshape(n, d)
    w_up_all = moe_w_up.reshape(DEPTH * MOE_EXPERTS, D_MODEL, 2 * MOE_D_EXPERT)
    w_down_all = moe_w_down.reshape(DEPTH * MOE_EXPERTS, MOE_D_EXPERT, D_MODEL)
    xin = xf
    for layer in range(DEPTH):
        j = layer // 2
        router_ops = _router_operands(moe_w_group[layer], moe_b_group[layer],
                                      moe_w_expert[layer], moe_b_expert[layer])
        if layer % 2 == 0:
            w = hg_w_in[j]
            w_a = jnp.concatenate([w[:, :D_MODEL], w[:, 3 * D_MODEL:]], axis=1).astype(BF16)
            w_f = w[:, D_MODEL:3 * D_MODEL].astype(BF16)
            proj_a = _matmul(xin, w_a, BF16)
            proj_f = _matmul(xin, w_f, F32)
            o_f, o_b = _gla(proj_a, proj_f, hg_lb_logits, layer, batch, seq)
            xf, xpk, route, counts = _hgrn_out(
                o_f, o_b, proj_a, hg_norm_w[j], hg_w_out[j].astype(BF16), xf,
                ln_g[layer, 0], ln_b[layer, 0], router_ops)
        else:
            proj = _matmul(xin, cv_w_in[j].astype(BF16), BF16)
            xf, xpk, route, counts = _conv_out(
                proj, cv_w[j], cv_w_out[j].astype(BF16), xf,
                ln_g[layer, 0], ln_b[layer, 0], seq, router_ops)
        dest, block_expert, nact, fill_start, fill_end, n_rows = _moe_plan(route, counts, n)
        xs = _dispatch(xpk, dest, fill_start, fill_end, nact, n_rows)
        ys = _moe_ffn(xs, block_expert, nact, counts, w_up_all, w_down_all, layer)
        xf, xin = _combine_ln(xf, route, ys, dest, ln_g[layer, 1], ln_b[layer, 1])
    return xf.reshape(batch, seq, d)
```

```python
import functools

import jax
import jax.numpy as jnp
from jax import lax
from jax.experimental import pallas as pl
from jax.experimental.pallas import tpu as pltpu

D_MODEL = 1024
DEPTH = 4
HG_DK = 128
HG_HEADS = D_MODEL // HG_DK
HG_STREAMS = 5
CONV_STREAMS = 3
MOE_GROUPS = 4
MOE_EXPERTS_PER_GROUP = 8
MOE_EXPERTS = MOE_GROUPS * MOE_EXPERTS_PER_GROUP
MOE_TOPK = 2
MOE_D_EXPERT = D_MODEL // 2
MOE_BLOCK = 256
DN_ALPHA = (2.0 * DEPTH) ** 0.25
LN_EPS = 1e-5
RMS_EPS = 1e-6

LANES = 128
SUBLANES = 8
BF16_SUBLANES = 16
VMEM_LIMIT = 48 * 1024 * 1024
GLA_CHUNK = 64
GLA_TBLOCK = 2048
GLA_UNROLL = 2
ROW_TILE = 256
NEG_BIG = -1e30

BF16 = jnp.bfloat16
F32 = jnp.float32


def _cparams(*sem):
    return pltpu.CompilerParams(dimension_semantics=sem, vmem_limit_bytes=VMEM_LIMIT)


def _mm_kernel(x_ref, w_ref, o_ref):
    o_ref[...] = jnp.dot(x_ref[...].astype(BF16), w_ref[...],
                         preferred_element_type=F32).astype(o_ref.dtype)


def _matmul(x, w, out_dtype, tm=2048, tn=1024):
    n, k = x.shape
    nn = w.shape[1]
    return pl.pallas_call(
        _mm_kernel,
        grid=(nn // tn, n // tm),
        in_specs=[pl.BlockSpec((tm, k), lambda j, i: (i, 0)),
                  pl.BlockSpec((k, tn), lambda j, i: (0, j))],
        out_specs=pl.BlockSpec((tm, tn), lambda j, i: (i, j)),
        out_shape=jax.ShapeDtypeStruct((n, nn), out_dtype),
        compiler_params=_cparams("arbitrary", "arbitrary"),
        name="in_proj",
    )(x, w)


PACKED = D_MODEL // 2


def _pack_rows(x):
    lo = pltpu.bitcast(x[:, :PACKED].astype(BF16).astype(F32), jnp.int32)
    hi = pltpu.bitcast(x[:, PACKED:].astype(BF16).astype(F32), jnp.int32)
    return lax.shift_right_logical(lo, 16) | hi


def _unpack_rows(w):
    lo = pltpu.bitcast(lax.shift_left(w, 16), F32)
    hi = pltpu.bitcast(w & jnp.int32(-65536), F32)
    return lo, hi


def _layer_norm_rows(y, g, b):
    mu = jnp.mean(y, axis=-1, keepdims=True)
    yc = y - mu
    var = jnp.mean(yc * yc, axis=-1, keepdims=True)
    return yc * lax.rsqrt(var + LN_EPS) * g + b


def _rows(x, blocks, size):
    parts = [x[b * size:(b + 1) * size, :] for b in blocks]
    return parts[0] if len(parts) == 1 else jnp.concatenate(parts, axis=0)


def _gla_chunk(q_s, k_s, v_s, lf_s, g_s, acc_s, o_ref, st_ref, d, start, slot, reverse):
    c = GLA_CHUNK
    t8 = SUBLANES
    nt_dims = (((1,), (1,)), ((), ()))

    def tt(ref, i):
        return ref[d, pl.ds(start + i, t8, stride=t8), :]

    lf = [tt(lf_s, i) for i in range(t8)]
    qt = [tt(q_s, i) for i in range(t8)]
    kt = [tt(k_s, i) for i in range(t8)]
    vt = [tt(v_s, i) for i in range(t8)]

    gi = [None] * t8
    prev = None
    for i in (range(t8 - 1, -1, -1) if reverse else range(t8)):
        gi[i] = lf[i] if prev is None else prev + lf[i]
        prev = gi[i]
    tot = prev
    sub = lax.broadcasted_iota(jnp.int32, (t8, LANES), 0)
    incl = tot
    s = 1
    while s < t8:
        if reverse:
            incl = incl + jnp.where(sub + s < t8, pltpu.roll(incl, t8 - s, 0), 0.0)
        else:
            incl = incl + jnp.where(sub >= s, pltpu.roll(incl, s, 0), 0.0)
        s *= 2
    excl = incl - tot
    for i in range(t8):
        g_s[slot, pl.ds(i, t8, stride=t8), :] = gi[i] + excl

    acc = []
    for i in range(t8):
        a_i = jnp.sum(qt[i] * kt[i], axis=-1, keepdims=True) * vt[i]
        for r in (range(i + 1, t8) if reverse else range(i)):
            w = qt[i] * kt[r] * jnp.exp(gi[i] - gi[r])
            a_i = a_i + jnp.sum(w, axis=-1, keepdims=True) * vt[r]
        acc.append(a_i)

    sl = pl.ds(start, c)
    q = q_s[d, sl, :]
    kk = k_s[d, sl, :]
    v = v_s[d, sl, :]
    g = g_s[slot]
    vb = v.astype(BF16)
    st = st_ref[d]

    o = lax.dot_general((q * jnp.exp(g)).astype(BF16), st.astype(BF16), nt_dims,
                        preferred_element_type=F32)

    half = c // 2
    qd_l, kd_l, v_l, q_tiles = [], [], [], []
    b = 2 * t8
    while b <= c:
        h = b // 2
        nblk = c // b
        lo = [2 * m for m in range(nblk)]
        hi = [2 * m + 1 for m in range(nblk)]
        q_half, k_half = (lo, hi) if reverse else (hi, lo)
        refs = [g[m * b + h:m * b + h + 1, :] if reverse else g[m * b + h - 1:m * b + h, :]
                for m in range(nblk)]
        gref = jnp.concatenate([jnp.broadcast_to(r, (h, LANES)) for r in refs], axis=0) \
            if nblk > 1 else jnp.broadcast_to(refs[0], (h, LANES))
        qd_l.append(_rows(q, q_half, h) * jnp.exp(_rows(g, q_half, h) - gref))
        kd_l.append(_rows(kk, k_half, h) * jnp.exp(gref - _rows(g, k_half, h)))
        v_l.append(_rows(v, k_half, h))
        q_tiles.append([hb * (h // t8) + t for hb in q_half for t in range(h // t8)])
        b *= 2
    nlev = len(qd_l)
    p = lax.dot_general(jnp.concatenate(qd_l, axis=0).astype(BF16),
                        jnp.concatenate(kd_l, axis=0).astype(BF16), nt_dims,
                        preferred_element_type=F32)
    ii = lax.broadcasted_iota(jnp.int32, p.shape, 0)
    jj = lax.broadcasted_iota(jnp.int32, p.shape, 1)
    keep = None
    for lv in range(nlev):
        h = t8 << lv
        m = (ii // half == lv) & (jj // half == lv) & ((ii // h) == (jj // h))
        keep = m if keep is None else keep | m
    p = jnp.where(keep, p, 0.0)
    res = jnp.dot(p.astype(BF16), jnp.concatenate(v_l, axis=0).astype(BF16),
                  preferred_element_type=F32)
    contrib = [None] * (c // t8)
    for lv in range(nlev):
        for n_, tile in enumerate(q_tiles[lv]):
            piece = res[lv * half + n_ * t8:lv * half + (n_ + 1) * t8, :]
            contrib[tile] = piece if contrib[tile] is None else contrib[tile] + piece
    zero = jnp.zeros((t8, LANES), F32)
    o = o + jnp.concatenate([zero if p_ is None else p_ for p_ in contrib], axis=0)

    gl = g[0:1, :] if reverse else g[c - 1:c, :]
    kd = (kk * jnp.exp(gl - g)).astype(BF16)
    upd = lax.dot_general(vb, kd, (((0,), (0,)), ((), ())), preferred_element_type=F32)
    st_ref[d] = st * jnp.exp(gl) + upd

    for i in range(t8):
        acc_s[slot, pl.ds(i, t8, stride=t8), :] = acc[i]
    o_ref[sl, :] = (o + acc_s[slot]).astype(o_ref.dtype)


def _gla_kernel(layer, lbl_ref, qf_ref, ff_ref, vf_ref, qb_ref, fb_ref, vb_ref,
                of_ref, ob_ref, st_ref, q_s, k_s, v_s, lf_s, g_s, acc_s):
    tb = GLA_TBLOCK
    nc = tb // GLA_CHUNK

    @pl.when(pl.program_id(2) == 0)
    def _():
        st_ref[...] = jnp.zeros_like(st_ref)

    if layer > 0:
        lg = lbl_ref[...]
        e = jnp.exp(lg - jnp.max(lg, axis=0, keepdims=True))
        sm = e / jnp.sum(e, axis=0, keepdims=True)
        lb = sm[1]
        for l in range(2, layer + 1):
            lb = lb + sm[l]

    for d, (q_ref, f_ref, v_ref) in enumerate(((qf_ref, ff_ref, vf_ref),
                                               (qb_ref, fb_ref, vb_ref))):
        qx = q_ref[...].astype(F32)
        fx = f_ref[...]
        v_s[d] = v_ref[...].astype(F32)
        t = jnp.exp(-jnp.abs(fx))
        r = 1.0 / (1.0 + t)
        logsig = jnp.minimum(fx, 0.0) - jnp.log(1.0 + t)
        sig_neg = jnp.where(fx >= 0.0, t * r, r)
        if layer > 0:
            lbd = lb[d:d + 1, :]
            a = jnp.log(lbd)
            bb = jnp.log1p(-lbd) + logsig
            lf = jnp.maximum(a, bb) + jnp.log(1.0 + jnp.exp(-jnp.abs(a - bb)))
            kk = (1.0 - lbd) * sig_neg
        else:
            lf = logsig
            kk = sig_neg
        q_s[d] = qx * (0.5 + 0.5 * jnp.tanh(0.5 * qx))
        k_s[d] = kk
        lf_s[d] = lf

    def body(ci, carry):
        for u in range(GLA_UNROLL):
            cf = ci * GLA_UNROLL + u
            sf = pl.multiple_of(cf * GLA_CHUNK, GLA_CHUNK)
            sb = pl.multiple_of((nc - 1 - cf) * GLA_CHUNK, GLA_CHUNK)
            _gla_chunk(q_s, k_s, v_s, lf_s, g_s, acc_s, of_ref, st_ref, 0, sf, 2 * u, False)
            _gla_chunk(q_s, k_s, v_s, lf_s, g_s, acc_s, ob_ref, st_ref, 1, sb, 2 * u + 1, True)
        return carry

    lax.fori_loop(0, nc // GLA_UNROLL, body, 0)


def _gla(proj_a, proj_f, lb_logits, layer, batch, seq):
    n = proj_a.shape[0]
    tb = GLA_TBLOCK
    nb = seq // tb
    h8 = HG_HEADS

    def spec(stream, rev):
        if rev:
            return pl.BlockSpec((tb, LANES), lambda b, h, c: (b * nb + nb - 1 - c, stream * h8 + h))
        return pl.BlockSpec((tb, LANES), lambda b, h, c: (b * nb + c, stream * h8 + h))

    o_f_spec = pl.BlockSpec((tb, LANES), lambda b, h, c: (b * nb + c, h))
    o_b_spec = pl.BlockSpec((tb, LANES), lambda b, h, c: (b * nb + nb - 1 - c, h))
    return pl.pallas_call(
        functools.partial(_gla_kernel, layer),
        grid=(batch, h8, nb),
        in_specs=[pl.BlockSpec((DEPTH, 2, LANES), lambda b, h, c: (0, 0, h)),
                  spec(0, False), spec(0, False), spec(1, False),
                  spec(0, True), spec(1, True), spec(1, True)],
        out_specs=[o_f_spec, o_b_spec],
        out_shape=[jax.ShapeDtypeStruct((n, D_MODEL), BF16)] * 2,
        scratch_shapes=[pltpu.VMEM((2, HG_DK, HG_DK), F32)]
        + [pltpu.VMEM((2, tb, LANES), F32)] * 4
        + [pltpu.VMEM((2 * GLA_UNROLL, GLA_CHUNK, LANES), F32)] * 2,
        compiler_params=_cparams("arbitrary", "arbitrary", "arbitrary"),
        name="gla",
    )(lb_logits, proj_a, proj_f, proj_a, proj_a, proj_f, proj_a)


def _hgrn_out_kernel(of_ref, ob_ref, gate_ref, nw_ref, w_ref, x_ref, g_ref, b_ref,
                     rw_ref, rb_ref, tri_ref,
                     o_ref, op_ref, route_ref, cnt_ref, carry_ref):
    o = of_ref[...].astype(F32) + ob_ref[...].astype(F32)
    parts = []
    for h in range(HG_HEADS):
        oh = o[:, h * LANES:(h + 1) * LANES]
        ms = jnp.mean(oh * oh, axis=-1, keepdims=True)
        parts.append(oh * lax.rsqrt(ms + RMS_EPS))
    gate = gate_ref[...].astype(F32)
    y = jnp.concatenate(parts, axis=-1) * nw_ref[...] * (gate * jax.nn.sigmoid(gate))
    mix = jnp.dot(y.astype(BF16), w_ref[...], preferred_element_type=F32)
    out = _layer_norm_rows(DN_ALPHA * x_ref[...] + mix, g_ref[...], b_ref[...])
    o_ref[...] = out
    op_ref[...] = _pack_rows(out)
    _route_tile(out, rw_ref, rb_ref, tri_ref, route_ref, cnt_ref, carry_ref)


def _hgrn_out(o_f, o_b, proj_a, norm_w, w_out, x, g, b, router_ops):
    n = x.shape[0]
    tm = ROW_TILE
    row = pl.BlockSpec((tm, D_MODEL), lambda i: (i, 0))
    vec = pl.BlockSpec((1, D_MODEL), lambda i: (0, 0))
    r_in, r_out, r_scratch = _router_specs()
    xo, xpk, route, cnt = pl.pallas_call(
        _hgrn_out_kernel,
        grid=(n // tm,),
        in_specs=[row, row, pl.BlockSpec((tm, D_MODEL), lambda i: (i, 2)), vec,
                  pl.BlockSpec((D_MODEL, D_MODEL), lambda i: (0, 0)), row, vec, vec] + r_in,
        out_specs=[row, pl.BlockSpec((tm, PACKED), lambda i: (i, 0))] + r_out,
        out_shape=[jax.ShapeDtypeStruct((n, D_MODEL), F32),
                   jax.ShapeDtypeStruct((n, PACKED), jnp.int32)] + _router_out_shapes(n),
        scratch_shapes=r_scratch,
        compiler_params=_cparams("arbitrary"),
        name="hgrn_out",
    )(o_f, o_b, proj_a, jnp.tile(norm_w, HG_HEADS).reshape(1, D_MODEL), w_out, x,
      g.reshape(1, D_MODEL), b.reshape(1, D_MODEL), *router_ops)
    return xo, xpk, route, _expert_counts(cnt)


def _conv_out_kernel(tiles_per_seq, bg_ref, cg_ref, h_ref, cgp_ref, hp_ref, cgn_ref, hn_ref,
                     cw_ref, w_ref, x_ref, g_ref, b_ref, rw_ref, rb_ref, tri_ref,
                     o_ref, op_ref, route_ref, cnt_ref, carry_ref):
    i = pl.program_id(0)
    tm = cg_ref.shape[0]
    u = cg_ref[...].astype(F32) * h_ref[...].astype(F32)
    first = (i % tiles_per_seq) == 0
    last = (i % tiles_per_seq) == tiles_per_seq - 1
    hr = cgp_ref.shape[0]
    u_halo_prev = cgp_ref[...].astype(F32) * hp_ref[...].astype(F32)
    u_halo_next = cgn_ref[...].astype(F32) * hn_ref[...].astype(F32)
    u_prev_row = jnp.where(first, 0.0, u_halo_prev[hr - 1:hr, :])
    u_next_row = jnp.where(last, 0.0, u_halo_next[0:1, :])
    rows = lax.broadcasted_iota(jnp.int32, u.shape, 0)
    u_prev = jnp.where(rows == 0, u_prev_row, pltpu.roll(u, 1, 0))
    u_next = jnp.where(rows == tm - 1, u_next_row, pltpu.roll(u, tm - 1, 0))
    cw = cw_ref[...]
    y = u_prev * cw[0:1, :] + u * cw[1:2, :] + u_next * cw[2:3, :]
    mix = jnp.dot((bg_ref[...].astype(F32) * y).astype(BF16), w_ref[...],
                  preferred_element_type=F32)
    out = _layer_norm_rows(DN_ALPHA * x_ref[...] + mix, g_ref[...], b_ref[...])
    o_ref[...] = out
    op_ref[...] = _pack_rows(out)
    _route_tile(out, rw_ref, rb_ref, tri_ref, route_ref, cnt_ref, carry_ref)


def _conv_out(proj, conv_w, w_out, x, g, b, seq, router_ops):
    n = x.shape[0]
    tm = ROW_TILE
    halo = BF16_SUBLANES
    rh = tm // halo
    nblk = n // halo
    row = pl.BlockSpec((tm, D_MODEL), lambda i: (i, 0))
    vec = pl.BlockSpec((1, D_MODEL), lambda i: (0, 0))

    def main(stream):
        return pl.BlockSpec((tm, D_MODEL), lambda i: (i, stream))

    def prev(stream):
        return pl.BlockSpec((halo, D_MODEL), lambda i: (jnp.maximum(i * rh - 1, 0), stream))

    def nxt(stream):
        return pl.BlockSpec((halo, D_MODEL),
                            lambda i: (jnp.minimum((i + 1) * rh, nblk - 1), stream))

    r_in, r_out, r_scratch = _router_specs()
    xo, xpk, route, cnt = pl.pallas_call(
        functools.partial(_conv_out_kernel, seq // tm),
        grid=(n // tm,),
        in_specs=[main(0), main(1), main(2), prev(1), prev(2), nxt(1), nxt(2),
                  pl.BlockSpec((3, D_MODEL), lambda i: (0, 0)),
                  pl.BlockSpec((D_MODEL, D_MODEL), lambda i: (0, 0)), row, vec, vec] + r_in,
        out_specs=[row, pl.BlockSpec((tm, PACKED), lambda i: (i, 0))] + r_out,
        out_shape=[jax.ShapeDtypeStruct((n, D_MODEL), F32),
                   jax.ShapeDtypeStruct((n, PACKED), jnp.int32)] + _router_out_shapes(n),
        scratch_shapes=r_scratch,
        compiler_params=_cparams("arbitrary"),
        name="conv_out",
    )(proj, proj, proj, proj, proj, proj, proj, conv_w, w_out, x,
      g.reshape(1, D_MODEL), b.reshape(1, D_MODEL), *router_ops)
    return xo, xpk, route, _expert_counts(cnt)


def _route_tile(x, whl_ref, b_ref, tri_ref, o_ref, cnt_ref, carry_ref):
    @pl.when(pl.program_id(0) == 0)
    def _():
        carry_ref[...] = jnp.zeros_like(carry_ref)

    xh = x.astype(BF16)
    xl = (x - xh.astype(F32)).astype(BF16)
    both = jnp.dot(xh, whl_ref[...], preferred_element_type=F32)
    logits = (both[:, :LANES] + both[:, LANES:]
              + jnp.dot(xl, whl_ref[:, :LANES], preferred_element_type=F32)) + b_ref[...]
    eg = MOE_EXPERTS_PER_GROUP
    lt = logits.T
    grp = lt[0:SUBLANES]
    exp_t = [lt[SUBLANES + eg * g:SUBLANES + eg * (g + 1)] for g in range(MOE_GROUPS)]
    row = lax.broadcasted_iota(jnp.int32, grp.shape, 0).astype(F32)
    gmax = jnp.max(grp, axis=0, keepdims=True)
    gsum = jnp.sum(jnp.exp(grp - gmax), axis=0, keepdims=True)
    p_group = 1.0 / gsum
    g_sel = jnp.min(jnp.where(grp == gmax, row, 99.0), axis=0, keepdims=True)
    el = exp_t[MOE_GROUPS - 1]
    for g in range(MOE_GROUPS - 2, -1, -1):
        el = jnp.where(g_sel == g, exp_t[g], el)
    t1 = jnp.max(el, axis=0, keepdims=True)
    i1 = jnp.min(jnp.where(el == t1, row, 99.0), axis=0, keepdims=True)
    el2 = jnp.where(row == i1, NEG_BIG, el)
    t2 = jnp.max(el2, axis=0, keepdims=True)
    i2 = jnp.min(jnp.where(el2 == t2, row, 99.0), axis=0, keepdims=True)
    z = jnp.exp(t2 - t1)
    g1 = p_group / (1.0 + z)
    g2 = g1 * z

    is1 = row == i1
    is2 = row == i2
    in_g = [g_sel == g for g in range(MOE_GROUPS)]
    onehot = jnp.concatenate([jnp.where(in_g[g] & (is1 | is2), 1.0, 0.0)
                              for g in range(MOE_GROUPS)], axis=0)
    prefix = jnp.dot(onehot.astype(BF16), tri_ref[...], preferred_element_type=F32)
    before = prefix + carry_ref[...]
    r1 = jnp.zeros_like(g1)
    r2 = jnp.zeros_like(g1)
    for g in range(MOE_GROUPS):
        bg = before[eg * g:eg * (g + 1)]
        r1 = r1 + jnp.sum(jnp.where(in_g[g] & is1, bg, 0.0), axis=0, keepdims=True)
        r2 = r2 + jnp.sum(jnp.where(in_g[g] & is2, bg, 0.0), axis=0, keepdims=True)
    e1 = g_sel * eg + i1
    e2 = g_sel * eg + i2
    out = jnp.zeros_like(row)
    for r, val in enumerate((g1, g2, e1, e2, r1, r2)):
        out = jnp.where(row == r, val, out)
    o_ref[...] = out
    carry_ref[...] = carry_ref[...] + jnp.sum(onehot, axis=1, keepdims=True)
    cnt_ref[...] = carry_ref[...]


def _router_operands(w_group, b_group, w_expert, b_expert):
    gpad = SUBLANES - MOE_GROUPS
    pad = LANES - SUBLANES - MOE_EXPERTS
    w = jnp.concatenate([w_group, jnp.zeros((D_MODEL, gpad), F32), w_expert,
                         jnp.zeros((D_MODEL, pad), F32)], axis=1)
    wh = w.astype(BF16)
    wl = (w - wh.astype(F32)).astype(BF16)
    b = jnp.concatenate([b_group, jnp.full((gpad,), NEG_BIG, F32), b_expert,
                         jnp.zeros((pad,), F32)]).reshape(1, LANES)
    r = jnp.arange(ROW_TILE, dtype=jnp.int32)
    tri = (r[:, None] < r[None, :]).astype(BF16)
    return jnp.concatenate([wh, wl], axis=1), b, tri


def _router_specs():
    in_specs = [pl.BlockSpec((D_MODEL, 2 * LANES), lambda i: (0, 0)),
                pl.BlockSpec((1, LANES), lambda i: (0, 0)),
                pl.BlockSpec((ROW_TILE, ROW_TILE), lambda i: (0, 0))]
    out_specs = [pl.BlockSpec((SUBLANES, ROW_TILE), lambda i: (0, i)),
                 pl.BlockSpec((MOE_EXPERTS, ROW_TILE), lambda i: (0, 0))]
    scratch = [pltpu.VMEM((MOE_EXPERTS, ROW_TILE), F32)]
    return in_specs, out_specs, scratch


def _router_out_shapes(n):
    return [jax.ShapeDtypeStruct((SUBLANES, n), F32),
            jax.ShapeDtypeStruct((MOE_EXPERTS, ROW_TILE), F32)]


def _expert_counts(cnt):
    return cnt[:, 0].astype(jnp.int32)


DISPATCH_TOKENS = 256
DISPATCH_BUFS = 3
WEIGHT_SLOTS = 3


def _moe_plan(route, counts, n):
    nk = n * MOE_TOPK
    n_rows = -(-nk // MOE_BLOCK) * MOE_BLOCK + MOE_EXPERTS * MOE_BLOCK
    n_blocks = n_rows // MOE_BLOCK
    padded = (counts + MOE_BLOCK - 1) // MOE_BLOCK * MOE_BLOCK
    pad_end = jnp.cumsum(padded)
    pad_start = pad_end - padded
    e = route[MOE_TOPK:2 * MOE_TOPK].T.astype(jnp.int32)
    rank = route[2 * MOE_TOPK:3 * MOE_TOPK].T.astype(jnp.int32)
    ids = jnp.arange(MOE_EXPERTS, dtype=jnp.int32)
    base = jnp.sum(jnp.where(e[:, :, None] == ids, pad_start, 0), axis=-1)
    dest = (base + rank).reshape(nk)
    block_start = jnp.arange(n_blocks, dtype=jnp.int32) * MOE_BLOCK
    block_expert = jnp.minimum(jnp.sum(block_start[:, None] >= pad_end[None, :], axis=1),
                               MOE_EXPERTS - 1).astype(jnp.int32)
    nact = (pad_end[-1:] // MOE_BLOCK).astype(jnp.int32)
    fill_start = (pad_start + counts).astype(jnp.int32)
    return dest, block_expert, nact, fill_start, pad_end.astype(jnp.int32), n_rows


def _dispatch_kernel(dest_ref, fs_ref, fe_ref, nact_ref, x_hbm, xs_hbm, xbuf, zbuf, lsem, sem,
                     zsem):
    i = pl.program_id(0)
    nsteps = pl.num_programs(0)
    slot = i % DISPATCH_BUFS
    ct = DISPATCH_TOKENS
    gt = ct // SUBLANES
    n_blocks = xs_hbm.shape[0] // MOE_BLOCK

    def load(step, sl):
        start = pl.multiple_of(step * gt, gt)
        return pltpu.make_async_copy(x_hbm.at[pl.ds(start, gt)], xbuf.at[sl], lsem.at[sl])

    def wait_step(sl):
        for _ in range(MOE_TOPK):
            pltpu.make_async_copy(x_hbm.at[pl.ds(0, gt)], xbuf.at[sl], sem.at[sl]).wait()

    @pl.when(i == 0)
    def _():
        for s in range(DISPATCH_BUFS - 1):
            load(s, s).start()

    load(i, slot).wait()

    def body(g, c):
        base = (i * ct + g * SUBLANES) * MOE_TOPK
        for t in range(SUBLANES):
            for k in range(MOE_TOPK):
                d = dest_ref[base + t * MOE_TOPK + k]
                pltpu.make_async_copy(xbuf.at[slot, g, pl.ds(t, 1)], xs_hbm.at[pl.ds(d, 1)],
                                      sem.at[slot]).start()
        return c
    lax.fori_loop(0, gt, body, 0)

    @pl.when(i >= 1)
    def _():
        wait_step((i - 1) % DISPATCH_BUFS)

    @pl.when(i + DISPATCH_BUFS - 1 < nsteps)
    def _():
        load(i + DISPATCH_BUFS - 1, (i + DISPATCH_BUFS - 1) % DISPATCH_BUFS).start()

    @pl.when(i == nsteps - 1)
    def _():
        wait_step(slot)
        zbuf[...] = jnp.zeros_like(zbuf)

        def fill_rows(wait):
            def per_expert(e, c):
                def per_row(r, c2):
                    cp = pltpu.make_async_copy(zbuf.at[pl.ds(0, 1)], xs_hbm.at[pl.ds(r, 1)],
                                               zsem.at[0])
                    cp.wait() if wait else cp.start()
                    return c2
                lax.fori_loop(fs_ref[e], fe_ref[e], per_row, 0)
                return c
            lax.fori_loop(0, MOE_EXPERTS, per_expert, 0)

        def fill_blocks(wait):
            def per_block(b, c):
                start = pl.multiple_of(b * MOE_BLOCK, MOE_BLOCK)
                cp = pltpu.make_async_copy(zbuf, xs_hbm.at[pl.ds(start, MOE_BLOCK)], zsem.at[1])
                cp.wait() if wait else cp.start()
                return c
            lax.fori_loop(nact_ref[0], n_blocks, per_block, 0)

        fill_rows(False)
        fill_blocks(False)
        fill_rows(True)
        fill_blocks(True)


def _dispatch(x, dest, fill_start, fill_end, nact, n_rows):
    n, width = x.shape
    grid_spec = pltpu.PrefetchScalarGridSpec(
        num_scalar_prefetch=4,
        grid=(n // DISPATCH_TOKENS,),
        in_specs=[pl.BlockSpec(memory_space=pl.ANY)],
        out_specs=pl.BlockSpec(memory_space=pl.ANY),
        scratch_shapes=[pltpu.VMEM((DISPATCH_BUFS, DISPATCH_TOKENS // SUBLANES, SUBLANES, width),
                                   x.dtype),
                        pltpu.VMEM((MOE_BLOCK, width), x.dtype),
                        pltpu.SemaphoreType.DMA((DISPATCH_BUFS,)),
                        pltpu.SemaphoreType.DMA((DISPATCH_BUFS,)),
                        pltpu.SemaphoreType.DMA((2,))],
    )
    return pl.pallas_call(
        _dispatch_kernel,
        grid_spec=grid_spec,
        out_shape=jax.ShapeDtypeStruct((n_rows, width), x.dtype),
        compiler_params=_cparams("arbitrary"),
        name="dispatch",
    )(dest, fill_start, fill_end, nact, x.reshape(n // SUBLANES, SUBLANES, width))


def _moe_ffn_kernel(e0, be_ref, nact_ref, ord_ref, nxt_ref, nxt2_ref, xs_ref, wup_hbm, wdn_hbm,
                    ys_ref, wup_f, wdn_f, wup_bf, wdn_bf, wsem):
    i = pl.program_id(0)
    active = i < nact_ref[0]
    new_expert = (i == 0) | (be_ref[i] != be_ref[jnp.maximum(i - 1, 0)])

    def fetch(e, sl):
        return (pltpu.make_async_copy(wup_hbm.at[e0 + e], wup_f.at[sl], wsem.at[0, sl]),
                pltpu.make_async_copy(wdn_hbm.at[e0 + e], wdn_f.at[sl], wsem.at[1, sl]))

    @pl.when(active & (i == 0))
    def _():
        for cp in fetch(be_ref[0], 0):
            cp.start()

        @pl.when(nxt_ref[0] >= 0)
        def _():
            for cp in fetch(nxt_ref[0], 1):
                cp.start()

    @pl.when(active & new_expert)
    def _():
        k = ord_ref[i]
        sl = k % WEIGHT_SLOTS
        for cp in fetch(be_ref[i], sl):
            cp.wait()

        @pl.when(nxt2_ref[i] >= 0)
        def _():
            for cp in fetch(nxt2_ref[i], (k + 2) % WEIGHT_SLOTS):
                cp.start()

        wup_bf[...] = wup_f[sl].astype(BF16)
        wdn_bf[...] = wdn_f[sl].astype(BF16)

    @pl.when(active)
    def _():
        x_lo, x_hi = _unpack_rows(xs_ref[...])
        xb = jnp.concatenate([x_lo.astype(BF16), x_hi.astype(BF16)], axis=1)
        hcat = jnp.dot(xb, wup_bf[...], preferred_element_type=F32)
        hg = hcat[:, :MOE_D_EXPERT]
        hu = hcat[:, MOE_D_EXPERT:]
        act = (hg * jax.nn.sigmoid(hg) * hu).astype(BF16)
        ys_ref[...] = _pack_rows(jnp.dot(act, wdn_bf[...], preferred_element_type=F32))

    @pl.when(jnp.logical_not(active))
    def _():
        ys_ref[...] = jnp.zeros_like(ys_ref)


def _moe_ffn(xs, block_expert, nact, counts, w_up, w_down, layer):
    n_rows = xs.shape[0]
    n_blocks = n_rows // MOE_BLOCK
    first = jnp.concatenate([jnp.ones((1,), jnp.int32),
                             (block_expert[1:] != block_expert[:-1]).astype(jnp.int32)])
    ordinal = (jnp.cumsum(first) - 1).astype(jnp.int32)
    ids = jnp.arange(MOE_EXPERTS, dtype=jnp.int32)
    later = (ids[None, :] > ids[:, None]) & (counts[None, :] > 0)
    nxt = jnp.min(jnp.where(later, ids[None, :], MOE_EXPERTS), axis=1)
    nxt = jnp.where(nxt == MOE_EXPERTS, -1, nxt).astype(jnp.int32)
    next_expert = nxt[block_expert]
    nxt2 = jnp.where(nxt >= 0, nxt[jnp.maximum(nxt, 0)], -1)
    next2_expert = nxt2[block_expert]
    row = pl.BlockSpec((MOE_BLOCK, PACKED), lambda i, *_: (i, 0))
    grid_spec = pltpu.PrefetchScalarGridSpec(
        num_scalar_prefetch=5,
        grid=(n_blocks,),
        in_specs=[row, pl.BlockSpec(memory_space=pl.ANY), pl.BlockSpec(memory_space=pl.ANY)],
        out_specs=row,
        scratch_shapes=[pltpu.VMEM((WEIGHT_SLOTS, D_MODEL, 2 * MOE_D_EXPERT), F32),
                        pltpu.VMEM((WEIGHT_SLOTS, MOE_D_EXPERT, D_MODEL), F32),
                        pltpu.VMEM((D_MODEL, 2 * MOE_D_EXPERT), BF16),
                        pltpu.VMEM((MOE_D_EXPERT, D_MODEL), BF16),
                        pltpu.SemaphoreType.DMA((2, WEIGHT_SLOTS))],
    )
    return pl.pallas_call(
        functools.partial(_moe_ffn_kernel, layer * MOE_EXPERTS),
        grid_spec=grid_spec,
        out_shape=jax.ShapeDtypeStruct((n_rows, PACKED), jnp.int32),
        compiler_params=_cparams("arbitrary"),
        name="moe_ffn",
    )(block_expert, nact, ordinal, next_expert, next2_expert, xs, w_up, w_down)


def _combine_ln_kernel(dest_ref, x_ref, route_ref, ys_hbm, g_ref, b_ref, o_ref, ob_ref, ybuf,
                       sem):
    i = pl.program_id(0)
    nsteps = pl.num_programs(0)
    slot = i % 2
    tm = x_ref.shape[0]

    def start_gather(step, sl):
        def body(g, c):
            base = (step * tm + g * SUBLANES) * MOE_TOPK
            for t in range(SUBLANES):
                for k in range(MOE_TOPK):
                    d = dest_ref[base + t * MOE_TOPK + k]
                    pltpu.make_async_copy(ys_hbm.at[pl.ds(d, 1)], ybuf.at[sl, k, g, pl.ds(t, 1)],
                                          sem.at[sl]).start()
            return c
        lax.fori_loop(0, tm // SUBLANES, body, 0)

    @pl.when(i == 0)
    def _():
        start_gather(0, 0)

    @pl.when(i + 1 < nsteps)
    def _():
        start_gather(i + 1, 1 - slot)

    for k in range(MOE_TOPK):
        pltpu.make_async_copy(ybuf.at[1 - slot, k], ybuf.at[slot, k], sem.at[slot]).wait()
    route = route_ref[...]
    y0_lo, y0_hi = _unpack_rows(ybuf[slot, 0].reshape(tm, PACKED))
    y1_lo, y1_hi = _unpack_rows(ybuf[slot, 1].reshape(tm, PACKED))
    g0 = route[:, 0:1]
    g1 = route[:, 1:2]
    ffn = jnp.concatenate([g0 * y0_lo + g1 * y1_lo, g0 * y0_hi + g1 * y1_hi], axis=1)
    out = _layer_norm_rows(DN_ALPHA * x_ref[...] + ffn, g_ref[...], b_ref[...])
    o_ref[...] = out
    ob_ref[...] = out.astype(BF16)


def _combine_ln(x, route, ys, dest, g, b):
    n = x.shape[0]
    tm = ROW_TILE
    vec = pl.BlockSpec((1, D_MODEL), lambda i, d: (0, 0))
    row = pl.BlockSpec((tm, D_MODEL), lambda i, d: (i, 0))
    grid_spec = pltpu.PrefetchScalarGridSpec(
        num_scalar_prefetch=1,
        grid=(n // tm,),
        in_specs=[pl.BlockSpec((tm, D_MODEL), lambda i, d: (i, 0)),
                  pl.BlockSpec((tm, MOE_TOPK), lambda i, d: (i, 0)),
                  pl.BlockSpec(memory_space=pl.ANY), vec, vec],
        out_specs=[row, row],
        scratch_shapes=[pltpu.VMEM((2, MOE_TOPK, tm // SUBLANES, SUBLANES, PACKED), jnp.int32),
                        pltpu.SemaphoreType.DMA((2,))],
    )
    return pl.pallas_call(
        _combine_ln_kernel,
        grid_spec=grid_spec,
        out_shape=[jax.ShapeDtypeStruct((n, D_MODEL), F32),
                   jax.ShapeDtypeStruct((n, D_MODEL), BF16)],
        compiler_params=_cparams("arbitrary"),
        name="combine_ln",
    )(dest, x, route[0:MOE_TOPK].T, ys, g.reshape(1, D_MODEL), b.reshape(1, D_MODEL))


def kernel(x, hg_w_in, hg_lb_logits, hg_norm_w, hg_w_out, cv_w_in, cv_w, cv_w_out, ln_g, ln_b,
           moe_w_group, moe_b_group, moe_w_expert, moe_b_expert, moe_w_up, moe_w_down):
    batch, seq, d = x.shape
    n = batch * seq
    xf = x.reshape(n, d)
    w_up_all = moe_w_up.reshape(DEPTH * MOE_EXPERTS, D_MODEL, 2 * MOE_D_EXPERT)
    w_down_all = moe_w_down.reshape(DEPTH * MOE_EXPERTS, MOE_D_EXPERT, D_MODEL)
    xin = xf
    for layer in range(DEPTH):
        j = layer // 2
        router_ops = _router_operands(moe_w_group[layer], moe_b_group[layer],
                                      moe_w_expert[layer], moe_b_expert[layer])
        if layer % 2 == 0:
            w = hg_w_in[j]
            w_a = jnp.concatenate([w[:, :D_MODEL], w[:, 3 * D_MODEL:]], axis=1).astype(BF16)
            w_f = w[:, D_MODEL:3 * D_MODEL].astype(BF16)
            proj_a = _matmul(xin, w_a, BF16)
            proj_f = _matmul(xin, w_f, F32)
            o_f, o_b = _gla(proj_a, proj_f, hg_lb_logits, layer, batch, seq)
            xf, xpk, route, counts = _hgrn_out(
                o_f, o_b, proj_a, hg_norm_w[j], hg_w_out[j].astype(BF16), xf,
                ln_g[layer, 0], ln_b[layer, 0], router_ops)
        else:
            proj = _matmul(xin, cv_w_in[j].astype(BF16), BF16)
            xf, xpk, route, counts = _conv_out(
                proj, cv_w[j], cv_w_out[j].astype(BF16), xf,
                ln_g[layer, 0], ln_b[layer, 0], seq, router_ops)
        dest, block_expert, nact, fill_start, fill_end, n_rows = _moe_plan(route, counts, n)
        xs = _dispatch(xpk, dest, fill_start, fill_end, nact, n_rows)
        ys = _moe_ffn(xs, block_expert, nact, counts, w_up_all, w_down_all, layer)
        xf, xin = _combine_ln(xf, route, ys, dest, ln_g[layer, 1], ln_b[layer, 1])
    return xf.reshape(batch, seq, d)
```

```python
import functools

import jax
import jax.numpy as jnp
from jax import lax
from jax.experimental import pallas as pl
from jax.experimental.pallas import tpu as pltpu

D_MODEL = 1024
DEPTH = 4
HG_DK = 128
HG_HEADS = D_MODEL // HG_DK
HG_STREAMS = 5
CONV_STREAMS = 3
MOE_GROUPS = 4
MOE_EXPERTS_PER_GROUP = 8
MOE_EXPERTS = MOE_GROUPS * MOE_EXPERTS_PER_GROUP
MOE_TOPK = 2
MOE_D_EXPERT = D_MODEL // 2
MOE_BLOCK = 256
DN_ALPHA = (2.0 * DEPTH) ** 0.25
LN_EPS = 1e-5
RMS_EPS = 1e-6

LANES = 128
SUBLANES = 8
BF16_SUBLANES = 16
VMEM_LIMIT = 48 * 1024 * 1024
GLA_CHUNK = 64
GLA_TBLOCK = 2048
GLA_UNROLL = 2
ROW_TILE = 256
NEG_BIG = -1e30

BF16 = jnp.bfloat16
F32 = jnp.float32


def _cparams(*sem):
    return pltpu.CompilerParams(dimension_semantics=sem, vmem_limit_bytes=VMEM_LIMIT)


def _mm_kernel(x_ref, w_ref, o_ref):
    o_ref[...] = jnp.dot(x_ref[...].astype(BF16), w_ref[...],
                         preferred_element_type=F32).astype(o_ref.dtype)


def _matmul(x, w, out_dtype, tm=2048, tn=1024):
    n, k = x.shape
    nn = w.shape[1]
    return pl.pallas_call(
        _mm_kernel,
        grid=(nn // tn, n // tm),
        in_specs=[pl.BlockSpec((tm, k), lambda j, i: (i, 0)),
                  pl.BlockSpec((k, tn), lambda j, i: (0, j))],
        out_specs=pl.BlockSpec((tm, tn), lambda j, i: (i, j)),
        out_shape=jax.ShapeDtypeStruct((n, nn), out_dtype),
        compiler_params=_cparams("arbitrary", "arbitrary"),
        name="in_proj",
    )(x, w)


PACKED = D_MODEL // 2


def _pack_rows(x):
    lo = pltpu.bitcast(x[:, :PACKED].astype(BF16).astype(F32), jnp.int32)
    hi = pltpu.bitcast(x[:, PACKED:].astype(BF16).astype(F32), jnp.int32)
    return lax.shift_right_logical(lo, 16) | hi


def _unpack_rows(w):
    lo = pltpu.bitcast(lax.shift_left(w, 16), F32)
    hi = pltpu.bitcast(w & jnp.int32(-65536), F32)
    return lo, hi


def _layer_norm_rows(y, g, b):
    mu = jnp.mean(y, axis=-1, keepdims=True)
    yc = y - mu
    var = jnp.mean(yc * yc, axis=-1, keepdims=True)
    return yc * lax.rsqrt(var + LN_EPS) * g + b


def _rows(x, blocks, size):
    parts = [x[b * size:(b + 1) * size, :] for b in blocks]
    return parts[0] if len(parts) == 1 else jnp.concatenate(parts, axis=0)


def _gla_chunk(q_s, k_s, v_s, lf_s, g_s, acc_s, o_ref, st_ref, d, start, slot, reverse):
    c = GLA_CHUNK
    t8 = SUBLANES
    nt_dims = (((1,), (1,)), ((), ()))

    def tt(ref, i):
        return ref[d, pl.ds(start + i, t8, stride=t8), :]

    lf = [tt(lf_s, i) for i in range(t8)]
    qt = [tt(q_s, i) for i in range(t8)]
    kt = [tt(k_s, i) for i in range(t8)]
    vt = [tt(v_s, i) for i in range(t8)]

    gi = [None] * t8
    prev = None
    for i in (range(t8 - 1, -1, -1) if reverse else range(t8)):
        gi[i] = lf[i] if prev is None else prev + lf[i]
        prev = gi[i]
    tot = prev
    sub = lax.broadcasted_iota(jnp.int32, (t8, LANES), 0)
    incl = tot
    s = 1
    while s < t8:
        if reverse:
            incl = incl + jnp.where(sub + s < t8, pltpu.roll(incl, t8 - s, 0), 0.0)
        else:
            incl = incl + jnp.where(sub >= s, pltpu.roll(incl, s, 0), 0.0)
        s *= 2
    excl = incl - tot
    for i in range(t8):
        g_s[slot, pl.ds(i, t8, stride=t8), :] = gi[i] + excl

    acc = []
    for i in range(t8):
        a_i = jnp.sum(qt[i] * kt[i], axis=-1, keepdims=True) * vt[i]
        for r in (range(i + 1, t8) if reverse else range(i)):
            w = qt[i] * kt[r] * jnp.exp(gi[i] - gi[r])
            a_i = a_i + jnp.sum(w, axis=-1, keepdims=True) * vt[r]
        acc.append(a_i)

    sl = pl.ds(start, c)
    q = q_s[d, sl, :]
    kk = k_s[d, sl, :]
    v = v_s[d, sl, :]
    g = g_s[slot]
    vb = v.astype(BF16)
    st = st_ref[d]

    o = lax.dot_general((q * jnp.exp(g)).astype(BF16), st.astype(BF16), nt_dims,
                        preferred_element_type=F32)

    half = c // 2
    qd_l, kd_l, v_l, q_tiles = [], [], [], []
    b = 2 * t8
    while b <= c:
        h = b // 2
        nblk = c // b
        lo = [2 * m for m in range(nblk)]
        hi = [2 * m + 1 for m in range(nblk)]
        q_half, k_half = (lo, hi) if reverse else (hi, lo)
        refs = [g[m * b + h:m * b + h + 1, :] if reverse else g[m * b + h - 1:m * b + h, :]
                for m in range(nblk)]
        gref = jnp.concatenate([jnp.broadcast_to(r, (h, LANES)) for r in refs], axis=0) \
            if nblk > 1 else jnp.broadcast_to(refs[0], (h, LANES))
        qd_l.append(_rows(q, q_half, h) * jnp.exp(_rows(g, q_half, h) - gref))
        kd_l.append(_rows(kk, k_half, h) * jnp.exp(gref - _rows(g, k_half, h)))
        v_l.append(_rows(v, k_half, h))
        q_tiles.append([hb * (h // t8) + t for hb in q_half for t in range(h // t8)])
        b *= 2
    nlev = len(qd_l)
    p = lax.dot_general(jnp.concatenate(qd_l, axis=0).astype(BF16),
                        jnp.concatenate(kd_l, axis=0).astype(BF16), nt_dims,
                        preferred_element_type=F32)
    ii = lax.broadcasted_iota(jnp.int32, p.shape, 0)
    jj = lax.broadcasted_iota(jnp.int32, p.shape, 1)
    keep = None
    for lv in range(nlev):
        h = t8 << lv
        m = (ii // half == lv) & (jj // half == lv) & ((ii // h) == (jj // h))
        keep = m if keep is None else keep | m
    p = jnp.where(keep, p, 0.0)
    res = jnp.dot(p.astype(BF16), jnp.concatenate(v_l, axis=0).astype(BF16),
                  preferred_element_type=F32)
    contrib = [None] * (c // t8)
    for lv in range(nlev):
        for n_, tile in enumerate(q_tiles[lv]):
            piece = res[lv * half + n_ * t8:lv * half + (n_ + 1) * t8, :]
            contrib[tile] = piece if contrib[tile] is None else contrib[tile] + piece
    zero = jnp.zeros((t8, LANES), F32)
    o = o + jnp.concatenate([zero if p_ is None else p_ for p_ in contrib], axis=0)

    gl = g[0:1, :] if reverse else g[c - 1:c, :]
    kd = (kk * jnp.exp(gl - g)).astype(BF16)
    upd = lax.dot_general(vb, kd, (((0,), (0,)), ((), ())), preferred_element_type=F32)
    st_ref[d] = st * jnp.exp(gl) + upd

    for i in range(t8):
        acc_s[slot, pl.ds(i, t8, stride=t8), :] = acc[i]
    o_ref[sl, :] = (o + acc_s[slot]).astype(o_ref.dtype)


def _gla_kernel(layer, lbl_ref, qf_ref, ff_ref, vf_ref, qb_ref, fb_ref, vb_ref,
                of_ref, ob_ref, st_ref, q_s, k_s, v_s, lf_s, g_s, acc_s):
    tb = GLA_TBLOCK
    nc = tb // GLA_CHUNK

    @pl.when(pl.program_id(2) == 0)
    def _():
        st_ref[...] = jnp.zeros_like(st_ref)

    if layer > 0:
        lg = lbl_ref[...]
        e = jnp.exp(lg - jnp.max(lg, axis=0, keepdims=True))
        sm = e / jnp.sum(e, axis=0, keepdims=True)
        lb = sm[1]
        for l in range(2, layer + 1):
            lb = lb + sm[l]

    for d, (q_ref, f_ref, v_ref) in enumerate(((qf_ref, ff_ref, vf_ref),
                                               (qb_ref, fb_ref, vb_ref))):
        qx = q_ref[...].astype(F32)
        fx = f_ref[...]
        v_s[d] = v_ref[...].astype(F32)
        t = jnp.exp(-jnp.abs(fx))
        r = 1.0 / (1.0 + t)
        logsig = jnp.minimum(fx, 0.0) - jnp.log(1.0 + t)
        sig_neg = jnp.where(fx >= 0.0, t * r, r)
        if layer > 0:
            lbd = lb[d:d + 1, :]
            a = jnp.log(lbd)
            bb = jnp.log1p(-lbd) + logsig
            lf = jnp.maximum(a, bb) + jnp.log(1.0 + jnp.exp(-jnp.abs(a - bb)))
            kk = (1.0 - lbd) * sig_neg
        else:
            lf = logsig
            kk = sig_neg
        q_s[d] = qx * (0.5 + 0.5 * jnp.tanh(0.5 * qx))
        k_s[d] = kk
        lf_s[d] = lf

    def body(ci, carry):
        for u in range(GLA_UNROLL):
            cf = ci * GLA_UNROLL + u
            sf = pl.multiple_of(cf * GLA_CHUNK, GLA_CHUNK)
            sb = pl.multiple_of((nc - 1 - cf) * GLA_CHUNK, GLA_CHUNK)
            _gla_chunk(q_s, k_s, v_s, lf_s, g_s, acc_s, of_ref, st_ref, 0, sf, 2 * u, False)
            _gla_chunk(q_s, k_s, v_s, lf_s, g_s, acc_s, ob_ref, st_ref, 1, sb, 2 * u + 1, True)
        return carry

    lax.fori_loop(0, nc // GLA_UNROLL, body, 0)


def _gla(proj_a, proj_f, lb_logits, layer, batch, seq):
    n = proj_a.shape[0]
    tb = GLA_TBLOCK
    nb = seq // tb
    h8 = HG_HEADS

    def spec(stream, rev):
        if rev:
            return pl.BlockSpec((tb, LANES), lambda b, h, c: (b * nb + nb - 1 - c, stream * h8 + h))
        return pl.BlockSpec((tb, LANES), lambda b, h, c: (b * nb + c, stream * h8 + h))

    o_f_spec = pl.BlockSpec((tb, LANES), lambda b, h, c: (b * nb + c, h))
    o_b_spec = pl.BlockSpec((tb, LANES), lambda b, h, c: (b * nb + nb - 1 - c, h))
    return pl.pallas_call(
        functools.partial(_gla_kernel, layer),
        grid=(batch, h8, nb),
        in_specs=[pl.BlockSpec((DEPTH, 2, LANES), lambda b, h, c: (0, 0, h)),
                  spec(0, False), spec(0, False), spec(1, False),
                  spec(0, True), spec(1, True), spec(1, True)],
        out_specs=[o_f_spec, o_b_spec],
        out_shape=[jax.ShapeDtypeStruct((n, D_MODEL), BF16)] * 2,
        scratch_shapes=[pltpu.VMEM((2, HG_DK, HG_DK), F32)]
        + [pltpu.VMEM((2, tb, LANES), F32)] * 4
        + [pltpu.VMEM((2 * GLA_UNROLL, GLA_CHUNK, LANES), F32)] * 2,
        compiler_params=_cparams("arbitrary", "arbitrary", "arbitrary"),
        name="gla",
    )(lb_logits, proj_a, proj_f, proj_a, proj_a, proj_f, proj_a)


def _hgrn_out_kernel(of_ref, ob_ref, gate_ref, nw_ref, w_ref, x_ref, g_ref, b_ref,
                     rw_ref, rb_ref, tri_ref,
                     o_ref, op_ref, route_ref, cnt_ref, carry_ref):
    o = of_ref[...].astype(F32) + ob_ref[...].astype(F32)
    parts = []
    for h in range(HG_HEADS):
        oh = o[:, h * LANES:(h + 1) * LANES]
        ms = jnp.mean(oh * oh, axis=-1, keepdims=True)
        parts.append(oh * lax.rsqrt(ms + RMS_EPS))
    gate = gate_ref[...].astype(F32)
    y = jnp.concatenate(parts, axis=-1) * nw_ref[...] * (gate * jax.nn.sigmoid(gate))
    mix = jnp.dot(y.astype(BF16), w_ref[...], preferred_element_type=F32)
    out = _layer_norm_rows(DN_ALPHA * x_ref[...] + mix, g_ref[...], b_ref[...])
    o_ref[...] = out
    op_ref[...] = _pack_rows(out)
    _route_tile(out, rw_ref, rb_ref, tri_ref, route_ref, cnt_ref, carry_ref)


def _hgrn_out(o_f, o_b, proj_a, norm_w, w_out, x, g, b, router_ops):
    n = x.shape[0]
    tm = ROW_TILE
    row = pl.BlockSpec((tm, D_MODEL), lambda i: (i, 0))
    vec = pl.BlockSpec((1, D_MODEL), lambda i: (0, 0))
    r_in, r_out, r_scratch = _router_specs()
    xo, xpk, route, cnt = pl.pallas_call(
        _hgrn_out_kernel,
        grid=(n // tm,),
        in_specs=[row, row, pl.BlockSpec((tm, D_MODEL), lambda i: (i, 2)), vec,
                  pl.BlockSpec((D_MODEL, D_MODEL), lambda i: (0, 0)), row, vec, vec] + r_in,
        out_specs=[row, pl.BlockSpec((tm, PACKED), lambda i: (i, 0))] + r_out,
        out_shape=[jax.ShapeDtypeStruct((n, D_MODEL), F32),
                   jax.ShapeDtypeStruct((n, PACKED), jnp.int32)] + _router_out_shapes(n),
        scratch_shapes=r_scratch,
        compiler_params=_cparams("arbitrary"),
        name="hgrn_out",
    )(o_f, o_b, proj_a, jnp.tile(norm_w, HG_HEADS).reshape(1, D_MODEL), w_out, x,
      g.reshape(1, D_MODEL), b.reshape(1, D_MODEL), *router_ops)
    return xo, xpk, route, _expert_counts(cnt)


def _conv_out_kernel(tiles_per_seq, bg_ref, cg_ref, h_ref, cgp_ref, hp_ref, cgn_ref, hn_ref,
                     cw_ref, w_ref, x_ref, g_ref, b_ref, rw_ref, rb_ref, tri_ref,
                     o_ref, op_ref, route_ref, cnt_ref, carry_ref):
    i = pl.program_id(0)
    tm = cg_ref.shape[0]
    u = cg_ref[...].astype(F32) * h_ref[...].astype(F32)
    first = (i % tiles_per_seq) == 0
    last = (i % tiles_per_seq) == tiles_per_seq - 1
    hr = cgp_ref.shape[0]
    u_halo_prev = cgp_ref[...].astype(F32) * hp_ref[...].astype(F32)
    u_halo_next = cgn_ref[...].astype(F32) * hn_ref[...].astype(F32)
    u_prev_row = jnp.where(first, 0.0, u_halo_prev[hr - 1:hr, :])
    u_next_row = jnp.where(last, 0.0, u_halo_next[0:1, :])
    rows = lax.broadcasted_iota(jnp.int32, u.shape, 0)
    u_prev = jnp.where(rows == 0, u_prev_row, pltpu.roll(u, 1, 0))
    u_next = jnp.where(rows == tm - 1, u_next_row, pltpu.roll(u, tm - 1, 0))
    cw = cw_ref[...]
    y = u_prev * cw[0:1, :] + u * cw[1:2, :] + u_next * cw[2:3, :]
    mix = jnp.dot((bg_ref[...].astype(F32) * y).astype(BF16), w_ref[...],
                  preferred_element_type=F32)
    out = _layer_norm_rows(DN_ALPHA * x_ref[...] + mix, g_ref[...], b_ref[...])
    o_ref[...] = out
    op_ref[...] = _pack_rows(out)
    _route_tile(out, rw_ref, rb_ref, tri_ref, route_ref, cnt_ref, carry_ref)


def _conv_out(proj, conv_w, w_out, x, g, b, seq, router_ops):
    n = x.shape[0]
    tm = ROW_TILE
    halo = BF16_SUBLANES
    rh = tm // halo
    nblk = n // halo
    row = pl.BlockSpec((tm, D_MODEL), lambda i: (i, 0))
    vec = pl.BlockSpec((1, D_MODEL), lambda i: (0, 0))

    def main(stream):
        return pl.BlockSpec((tm, D_MODEL), lambda i: (i, stream))

    def prev(stream):
        return pl.BlockSpec((halo, D_MODEL), lambda i: (jnp.maximum(i * rh - 1, 0), stream))

    def nxt(stream):
        return pl.BlockSpec((halo, D_MODEL),
                            lambda i: (jnp.minimum((i + 1) * rh, nblk - 1), stream))

    r_in, r_out, r_scratch = _router_specs()
    xo, xpk, route, cnt = pl.pallas_call(
        functools.partial(_conv_out_kernel, seq // tm),
        grid=(n // tm,),
        in_specs=[main(0), main(1), main(2), prev(1), prev(2), nxt(1), nxt(2),
                  pl.BlockSpec((3, D_MODEL), lambda i: (0, 0)),
                  pl.BlockSpec((D_MODEL, D_MODEL), lambda i: (0, 0)), row, vec, vec] + r_in,
        out_specs=[row, pl.BlockSpec((tm, PACKED), lambda i: (i, 0))] + r_out,
        out_shape=[jax.ShapeDtypeStruct((n, D_MODEL), F32),
                   jax.ShapeDtypeStruct((n, PACKED), jnp.int32)] + _router_out_shapes(n),
        scratch_shapes=r_scratch,
        compiler_params=_cparams("arbitrary"),
        name="conv_out",
    )(proj, proj, proj, proj, proj, proj, proj, conv_w, w_out, x,
      g.reshape(1, D_MODEL), b.reshape(1, D_MODEL), *router_ops)
    return xo, xpk, route, _expert_counts(cnt)


def _route_tile(x, whl_ref, b_ref, tri_ref, o_ref, cnt_ref, carry_ref):
    @pl.when(pl.program_id(0) == 0)
    def _():
        carry_ref[...] = jnp.zeros_like(carry_ref)

    xh = x.astype(BF16)
    xl = (x - xh.astype(F32)).astype(BF16)
    both = jnp.dot(xh, whl_ref[...], preferred_element_type=F32)
    logits = (both[:, :LANES] + both[:, LANES:]
              + jnp.dot(xl, whl_ref[:, :LANES], preferred_element_type=F32)) + b_ref[...]
    eg = MOE_EXPERTS_PER_GROUP
    lt = logits.T
    grp = lt[0:SUBLANES]
    exp_t = [lt[SUBLANES + eg * g:SUBLANES + eg * (g + 1)] for g in range(MOE_GROUPS)]
    row = lax.broadcasted_iota(jnp.int32, grp.shape, 0).astype(F32)
    gmax = jnp.max(grp, axis=0, keepdims=True)
    gsum = jnp.sum(jnp.exp(grp - gmax), axis=0, keepdims=True)
    p_group = 1.0 / gsum
    g_sel = jnp.min(jnp.where(grp == gmax, row, 99.0), axis=0, keepdims=True)
    el = exp_t[MOE_GROUPS - 1]
    for g in range(MOE_GROUPS - 2, -1, -1):
        el = jnp.where(g_sel == g, exp_t[g], el)
    t1 = jnp.max(el, axis=0, keepdims=True)
    i1 = jnp.min(jnp.where(el == t1, row, 99.0), axis=0, keepdims=True)
    el2 = jnp.where(row == i1, NEG_BIG, el)
    t2 = jnp.max(el2, axis=0, keepdims=True)
    i2 = jnp.min(jnp.where(el2 == t2, row, 99.0), axis=0, keepdims=True)
    z = jnp.exp(t2 - t1)
    g1 = p_group / (1.0 + z)
    g2 = g1 * z

    is1 = row == i1
    is2 = row == i2
    in_g = [g_sel == g for g in range(MOE_GROUPS)]
    onehot = jnp.concatenate([jnp.where(in_g[g] & (is1 | is2), 1.0, 0.0)
                              for g in range(MOE_GROUPS)], axis=0)
    prefix = jnp.dot(onehot.astype(BF16), tri_ref[...], preferred_element_type=F32)
    before = prefix + carry_ref[...]
    r1 = jnp.zeros_like(g1)
    r2 = jnp.zeros_like(g1)
    for g in range(MOE_GROUPS):
        bg = before[eg * g:eg * (g + 1)]
        r1 = r1 + jnp.sum(jnp.where(in_g[g] & is1, bg, 0.0), axis=0, keepdims=True)
        r2 = r2 + jnp.sum(jnp.where(in_g[g] & is2, bg, 0.0), axis=0, keepdims=True)
    e1 = g_sel * eg + i1
    e2 = g_sel * eg + i2
    out = jnp.zeros_like(row)
    for r, val in enumerate((g1, g2, e1, e2, r1, r2)):
        out = jnp.where(row == r, val, out)
    o_ref[...] = out
    carry_ref[...] = carry_ref[...] + jnp.sum(onehot, axis=1, keepdims=True)
    cnt_ref[...] = carry_ref[...]


def _router_operands(w_group, b_group, w_expert, b_expert):
    gpad = SUBLANES - MOE_GROUPS
    pad = LANES - SUBLANES - MOE_EXPERTS
    w = jnp.concatenate([w_group, jnp.zeros((D_MODEL, gpad), F32), w_expert,
                         jnp.zeros((D_MODEL, pad), F32)], axis=1)
    wh = w.astype(BF16)
    wl = (w - wh.astype(F32)).astype(BF16)
    b = jnp.concatenate([b_group, jnp.full((gpad,), NEG_BIG, F32), b_expert,
                         jnp.zeros((pad,), F32)]).reshape(1, LANES)
    r = jnp.arange(ROW_TILE, dtype=jnp.int32)
    tri = (r[:, None] < r[None, :]).astype(BF16)
    return jnp.concatenate([wh, wl], axis=1), b, tri


def _router_specs():
    in_specs = [pl.BlockSpec((D_MODEL, 2 * LANES), lambda i: (0, 0)),
                pl.BlockSpec((1, LANES), lambda i: (0, 0)),
                pl.BlockSpec((ROW_TILE, ROW_TILE), lambda i: (0, 0))]
    out_specs = [pl.BlockSpec((SUBLANES, ROW_TILE), lambda i: (0, i)),
                 pl.BlockSpec((MOE_EXPERTS, ROW_TILE), lambda i: (0, 0))]
    scratch = [pltpu.VMEM((MOE_EXPERTS, ROW_TILE), F32)]
    return in_specs, out_specs, scratch


def _router_out_shapes(n):
    return [jax.ShapeDtypeStruct((SUBLANES, n), F32),
            jax.ShapeDtypeStruct((MOE_EXPERTS, ROW_TILE), F32)]


def _expert_counts(cnt):
    return cnt[:, 0].astype(jnp.int32)


DISPATCH_TOKENS = 256
DISPATCH_BUFS = 3
WEIGHT_SLOTS = 3


def _dest_kernel(ps_ref, route_ref, o_ref):
    e = route_ref[MOE_TOPK:2 * MOE_TOPK, :]
    rank = route_ref[2 * MOE_TOPK:3 * MOE_TOPK, :]
    base = jnp.zeros_like(e)
    for x in range(MOE_EXPERTS):
        base = jnp.where(e == x, ps_ref[x].astype(F32), base)
    o_ref[...] = (base + rank).astype(jnp.int32)


def _moe_plan(route, counts, n):
    nk = n * MOE_TOPK
    n_rows = -(-nk // MOE_BLOCK) * MOE_BLOCK + MOE_EXPERTS * MOE_BLOCK
    n_blocks = n_rows // MOE_BLOCK
    padded = (counts + MOE_BLOCK - 1) // MOE_BLOCK * MOE_BLOCK
    pad_end = jnp.cumsum(padded)
    pad_start = pad_end - padded
    dest = pl.pallas_call(
        _dest_kernel,
        grid_spec=pltpu.PrefetchScalarGridSpec(
            num_scalar_prefetch=1, grid=(1,),
            in_specs=[pl.BlockSpec((SUBLANES, n), lambda i, ps: (0, 0))],
            out_specs=pl.BlockSpec((MOE_TOPK, n), lambda i, ps: (0, 0))),
        out_shape=jax.ShapeDtypeStruct((MOE_TOPK, n), jnp.int32),
        compiler_params=_cparams("arbitrary"),
        name="dest",
    )(pad_start.astype(jnp.int32), route).reshape(nk)
    block_start = jnp.arange(n_blocks, dtype=jnp.int32) * MOE_BLOCK
    block_expert = jnp.minimum(jnp.sum(block_start[:, None] >= pad_end[None, :], axis=1),
                               MOE_EXPERTS - 1).astype(jnp.int32)
    nact = (pad_end[-1:] // MOE_BLOCK).astype(jnp.int32)
    fill_start = (pad_start + counts).astype(jnp.int32)
    return dest, block_expert, nact, fill_start, pad_end.astype(jnp.int32), n_rows


def _dispatch_kernel(dest_ref, fs_ref, fe_ref, nact_ref, x_hbm, xs_hbm, xbuf, zbuf, lsem, sem,
                     zsem):
    i = pl.program_id(0)
    nsteps = pl.num_programs(0)
    slot = i % DISPATCH_BUFS
    ct = DISPATCH_TOKENS
    gt = ct // SUBLANES
    n_blocks = xs_hbm.shape[0] // MOE_BLOCK

    def load(step, sl):
        start = pl.multiple_of(step * gt, gt)
        return pltpu.make_async_copy(x_hbm.at[pl.ds(start, gt)], xbuf.at[sl], lsem.at[sl])

    def wait_step(sl):
        for _ in range(MOE_TOPK):
            pltpu.make_async_copy(x_hbm.at[pl.ds(0, gt)], xbuf.at[sl], sem.at[sl]).wait()

    @pl.when(i == 0)
    def _():
        for s in range(DISPATCH_BUFS - 1):
            load(s, s).start()

    load(i, slot).wait()

    n_tok = nsteps * ct

    def body(g, c):
        base = i * ct + g * SUBLANES
        for t in range(SUBLANES):
            for k in range(MOE_TOPK):
                d = dest_ref[k * n_tok + base + t]
                pltpu.make_async_copy(xbuf.at[slot, g, pl.ds(t, 1)], xs_hbm.at[pl.ds(d, 1)],
                                      sem.at[slot]).start()
        return c
    lax.fori_loop(0, gt, body, 0)

    @pl.when(i >= 1)
    def _():
        wait_step((i - 1) % DISPATCH_BUFS)

    @pl.when(i + DISPATCH_BUFS - 1 < nsteps)
    def _():
        load(i + DISPATCH_BUFS - 1, (i + DISPATCH_BUFS - 1) % DISPATCH_BUFS).start()

    @pl.when(i == nsteps - 1)
    def _():
        wait_step(slot)
        zbuf[...] = jnp.zeros_like(zbuf)

        def fill_rows(wait):
            def per_expert(e, c):
                def per_row(r, c2):
                    cp = pltpu.make_async_copy(zbuf.at[pl.ds(0, 1)], xs_hbm.at[pl.ds(r, 1)],
                                               zsem.at[0])
                    cp.wait() if wait else cp.start()
                    return c2
                lax.fori_loop(fs_ref[e], fe_ref[e], per_row, 0)
                return c
            lax.fori_loop(0, MOE_EXPERTS, per_expert, 0)

        def fill_blocks(wait):
            def per_block(b, c):
                start = pl.multiple_of(b * MOE_BLOCK, MOE_BLOCK)
                cp = pltpu.make_async_copy(zbuf, xs_hbm.at[pl.ds(start, MOE_BLOCK)], zsem.at[1])
                cp.wait() if wait else cp.start()
                return c
            lax.fori_loop(nact_ref[0], n_blocks, per_block, 0)

        fill_rows(False)
        fill_blocks(False)
        fill_rows(True)
        fill_blocks(True)


def _dispatch(x, dest, fill_start, fill_end, nact, n_rows):
    n, width = x.shape
    grid_spec = pltpu.PrefetchScalarGridSpec(
        num_scalar_prefetch=4,
        grid=(n // DISPATCH_TOKENS,),
        in_specs=[pl.BlockSpec(memory_space=pl.ANY)],
        out_specs=pl.BlockSpec(memory_space=pl.ANY),
        scratch_shapes=[pltpu.VMEM((DISPATCH_BUFS, DISPATCH_TOKENS // SUBLANES, SUBLANES, width),
                                   x.dtype),
                        pltpu.VMEM((MOE_BLOCK, width), x.dtype),
                        pltpu.SemaphoreType.DMA((DISPATCH_BUFS,)),
                        pltpu.SemaphoreType.DMA((DISPATCH_BUFS,)),
                        pltpu.SemaphoreType.DMA((2,))],
    )
    return pl.pallas_call(
        _dispatch_kernel,
        grid_spec=grid_spec,
        out_shape=jax.ShapeDtypeStruct((n_rows, width), x.dtype),
        compiler_params=_cparams("arbitrary"),
        name="dispatch",
    )(dest, fill_start, fill_end, nact, x.reshape(n // SUBLANES, SUBLANES, width))


def _moe_ffn_kernel(e0, be_ref, nact_ref, ord_ref, nxt_ref, nxt2_ref, xs_ref, wup_hbm, wdn_hbm,
                    ys_ref, wup_f, wdn_f, wup_bf, wdn_bf, wsem):
    i = pl.program_id(0)
    active = i < nact_ref[0]
    new_expert = (i == 0) | (be_ref[i] != be_ref[jnp.maximum(i - 1, 0)])

    def fetch(e, sl):
        return (pltpu.make_async_copy(wup_hbm.at[e0 + e], wup_f.at[sl], wsem.at[0, sl]),
                pltpu.make_async_copy(wdn_hbm.at[e0 + e], wdn_f.at[sl], wsem.at[1, sl]))

    @pl.when(active & (i == 0))
    def _():
        for cp in fetch(be_ref[0], 0):
            cp.start()

        @pl.when(nxt_ref[0] >= 0)
        def _():
            for cp in fetch(nxt_ref[0], 1):
                cp.start()

    @pl.when(active & new_expert)
    def _():
        k = ord_ref[i]
        sl = k % WEIGHT_SLOTS
        for cp in fetch(be_ref[i], sl):
            cp.wait()

        @pl.when(nxt2_ref[i] >= 0)
        def _():
            for cp in fetch(nxt2_ref[i], (k + 2) % WEIGHT_SLOTS):
                cp.start()

        wup_bf[...] = wup_f[sl].astype(BF16)
        wdn_bf[...] = wdn_f[sl].astype(BF16)

    @pl.when(active)
    def _():
        x_lo, x_hi = _unpack_rows(xs_ref[...])
        xb = jnp.concatenate([x_lo.astype(BF16), x_hi.astype(BF16)], axis=1)
        hcat = jnp.dot(xb, wup_bf[...], preferred_element_type=F32)
        hg = hcat[:, :MOE_D_EXPERT]
        hu = hcat[:, MOE_D_EXPERT:]
        act = (hg * jax.nn.sigmoid(hg) * hu).astype(BF16)
        ys_ref[...] = _pack_rows(jnp.dot(act, wdn_bf[...], preferred_element_type=F32))

    @pl.when(jnp.logical_not(active))
    def _():
        ys_ref[...] = jnp.zeros_like(ys_ref)


def _moe_ffn(xs, block_expert, nact, counts, w_up, w_down, layer):
    n_rows = xs.shape[0]
    n_blocks = n_rows // MOE_BLOCK
    first = jnp.concatenate([jnp.ones((1,), jnp.int32),
                             (block_expert[1:] != block_expert[:-1]).astype(jnp.int32)])
    ordinal = (jnp.cumsum(first) - 1).astype(jnp.int32)
    ids = jnp.arange(MOE_EXPERTS, dtype=jnp.int32)
    later = (ids[None, :] > ids[:, None]) & (counts[None, :] > 0)
    nxt = jnp.min(jnp.where(later, ids[None, :], MOE_EXPERTS), axis=1)
    nxt = jnp.where(nxt == MOE_EXPERTS, -1, nxt).astype(jnp.int32)
    next_expert = nxt[block_expert]
    nxt2 = jnp.where(nxt >= 0, nxt[jnp.maximum(nxt, 0)], -1)
    next2_expert = nxt2[block_expert]
    row = pl.BlockSpec((MOE_BLOCK, PACKED), lambda i, *_: (i, 0))
    grid_spec = pltpu.PrefetchScalarGridSpec(
        num_scalar_prefetch=5,
        grid=(n_blocks,),
        in_specs=[row, pl.BlockSpec(memory_space=pl.ANY), pl.BlockSpec(memory_space=pl.ANY)],
        out_specs=row,
        scratch_shapes=[pltpu.VMEM((WEIGHT_SLOTS, D_MODEL, 2 * MOE_D_EXPERT), F32),
                        pltpu.VMEM((WEIGHT_SLOTS, MOE_D_EXPERT, D_MODEL), F32),
                        pltpu.VMEM((D_MODEL, 2 * MOE_D_EXPERT), BF16),
                        pltpu.VMEM((MOE_D_EXPERT, D_MODEL), BF16),
                        pltpu.SemaphoreType.DMA((2, WEIGHT_SLOTS))],
    )
    return pl.pallas_call(
        functools.partial(_moe_ffn_kernel, layer * MOE_EXPERTS),
        grid_spec=grid_spec,
        out_shape=jax.ShapeDtypeStruct((n_rows, PACKED), jnp.int32),
        compiler_params=_cparams("arbitrary"),
        name="moe_ffn",
    )(block_expert, nact, ordinal, next_expert, next2_expert, xs, w_up, w_down)


def _combine_ln_kernel(dest_ref, x_ref, route_ref, ys_hbm, g_ref, b_ref, o_ref, ob_ref, ybuf,
                       sem):
    i = pl.program_id(0)
    nsteps = pl.num_programs(0)
    slot = i % 2
    tm = x_ref.shape[0]

    def start_gather(step, sl):
        def body(g, c):
            base = step * tm + g * SUBLANES
            for t in range(SUBLANES):
                for k in range(MOE_TOPK):
                    d = dest_ref[k * (nsteps * tm) + base + t]
                    pltpu.make_async_copy(ys_hbm.at[pl.ds(d, 1)], ybuf.at[sl, k, g, pl.ds(t, 1)],
                                          sem.at[sl]).start()
            return c
        lax.fori_loop(0, tm // SUBLANES, body, 0)

    @pl.when(i == 0)
    def _():
        start_gather(0, 0)

    @pl.when(i + 1 < nsteps)
    def _():
        start_gather(i + 1, 1 - slot)

    for k in range(MOE_TOPK):
        pltpu.make_async_copy(ybuf.at[1 - slot, k], ybuf.at[slot, k], sem.at[slot]).wait()
    route = route_ref[...]
    y0_lo, y0_hi = _unpack_rows(ybuf[slot, 0].reshape(tm, PACKED))
    y1_lo, y1_hi = _unpack_rows(ybuf[slot, 1].reshape(tm, PACKED))
    g0 = route[:, 0:1]
    g1 = route[:, 1:2]
    ffn = jnp.concatenate([g0 * y0_lo + g1 * y1_lo, g0 * y0_hi + g1 * y1_hi], axis=1)
    out = _layer_norm_rows(DN_ALPHA * x_ref[...] + ffn, g_ref[...], b_ref[...])
    o_ref[...] = out
    ob_ref[...] = out.astype(BF16)


def _combine_ln(x, route, ys, dest, g, b):
    n = x.shape[0]
    tm = ROW_TILE
    vec = pl.BlockSpec((1, D_MODEL), lambda i, d: (0, 0))
    row = pl.BlockSpec((tm, D_MODEL), lambda i, d: (i, 0))
    grid_spec = pltpu.PrefetchScalarGridSpec(
        num_scalar_prefetch=1,
        grid=(n // tm,),
        in_specs=[pl.BlockSpec((tm, D_MODEL), lambda i, d: (i, 0)),
                  pl.BlockSpec((tm, MOE_TOPK), lambda i, d: (i, 0)),
                  pl.BlockSpec(memory_space=pl.ANY), vec, vec],
        out_specs=[row, row],
        scratch_shapes=[pltpu.VMEM((2, MOE_TOPK, tm // SUBLANES, SUBLANES, PACKED), jnp.int32),
                        pltpu.SemaphoreType.DMA((2,))],
    )
    return pl.pallas_call(
        _combine_ln_kernel,
        grid_spec=grid_spec,
        out_shape=[jax.ShapeDtypeStruct((n, D_MODEL), F32),
                   jax.ShapeDtypeStruct((n, D_MODEL), BF16)],
        compiler_params=_cparams("arbitrary"),
        name="combine_ln",
    )(dest, x, route[0:MOE_TOPK].T, ys, g.reshape(1, D_MODEL), b.reshape(1, D_MODEL))


def kernel(x, hg_w_in, hg_lb_logits, hg_norm_w, hg_w_out, cv_w_in, cv_w, cv_w_out, ln_g, ln_b,
           moe_w_group, moe_b_group, moe_w_expert, moe_b_expert, moe_w_up, moe_w_down):
    batch, seq, d = x.shape
    n = batch * seq
    xf = x.reshape(n, d)
    w_up_all = moe_w_up.reshape(DEPTH * MOE_EXPERTS, D_MODEL, 2 * MOE_D_EXPERT)
    w_down_all = moe_w_down.reshape(DEPTH * MOE_EXPERTS, MOE_D_EXPERT, D_MODEL)
    xin = xf
    for layer in range(DEPTH):
        j = layer // 2
        router_ops = _router_operands(moe_w_group[layer], moe_b_group[layer],
                                      moe_w_expert[layer], moe_b_expert[layer])
        if layer % 2 == 0:
            w = hg_w_in[j]
            w_a = jnp.concatenate([w[:, :D_MODEL], w[:, 3 * D_MODEL:]], axis=1).astype(BF16)
            w_f = w[:, D_MODEL:3 * D_MODEL].astype(BF16)
            proj_a = _matmul(xin, w_a, BF16)
            proj_f = _matmul(xin, w_f, F32)
            o_f, o_b = _gla(proj_a, proj_f, hg_lb_logits, layer, batch, seq)
            xf, xpk, route, counts = _hgrn_out(
                o_f, o_b, proj_a, hg_norm_w[j], hg_w_out[j].astype(BF16), xf,
                ln_g[layer, 0], ln_b[layer, 0], router_ops)
        else:
            proj = _matmul(xin, cv_w_in[j].astype(BF16), BF16)
            xf, xpk, route, counts = _conv_out(
                proj, cv_w[j], cv_w_out[j].astype(BF16), xf,
                ln_g[layer, 0], ln_b[layer, 0], seq, router_ops)
        dest, block_expert, nact, fill_start, fill_end, n_rows = _moe_plan(route, counts, n)
        xs = _dispatch(xpk, dest, fill_start, fill_end, nact, n_rows)
        ys = _moe_ffn(xs, block_expert, nact, counts, w_up_all, w_down_all, layer)
        xf, xin = _combine_ln(xf, route, ys, dest, ln_g[layer, 1], ln_b[layer, 1])
    return xf.reshape(batch, seq, d)
```

```python
import functools

import jax
import jax.numpy as jnp
from jax import lax
from jax.experimental import pallas as pl
from jax.experimental.pallas import tpu as pltpu

D_MODEL = 1024
DEPTH = 4
HG_DK = 128
HG_HEADS = D_MODEL // HG_DK
HG_STREAMS = 5
CONV_STREAMS = 3
MOE_GROUPS = 4
MOE_EXPERTS_PER_GROUP = 8
MOE_EXPERTS = MOE_GROUPS * MOE_EXPERTS_PER_GROUP
MOE_TOPK = 2
MOE_D_EXPERT = D_MODEL // 2
MOE_BLOCK = 256
DN_ALPHA = (2.0 * DEPTH) ** 0.25
LN_EPS = 1e-5
RMS_EPS = 1e-6

LANES = 128
SUBLANES = 8
BF16_SUBLANES = 16
VMEM_LIMIT = 48 * 1024 * 1024
GLA_CHUNK = 64
GLA_TBLOCK = 2048
GLA_UNROLL = 2
ROW_TILE = 256
NEG_BIG = -1e30

BF16 = jnp.bfloat16
F32 = jnp.float32


def _cparams(*sem):
    return pltpu.CompilerParams(dimension_semantics=sem, vmem_limit_bytes=VMEM_LIMIT)


def _mm_kernel(x_ref, w_ref, o_ref):
    o_ref[...] = jnp.dot(x_ref[...].astype(BF16), w_ref[...],
                         preferred_element_type=F32).astype(o_ref.dtype)


def _matmul(x, w, out_dtype, tm=2048, tn=1024):
    n, k = x.shape
    nn = w.shape[1]
    return pl.pallas_call(
        _mm_kernel,
        grid=(nn // tn, n // tm),
        in_specs=[pl.BlockSpec((tm, k), lambda j, i: (i, 0)),
                  pl.BlockSpec((k, tn), lambda j, i: (0, j))],
        out_specs=pl.BlockSpec((tm, tn), lambda j, i: (i, j)),
        out_shape=jax.ShapeDtypeStruct((n, nn), out_dtype),
        compiler_params=_cparams("arbitrary", "arbitrary"),
        name="in_proj",
    )(x, w)


PACKED = D_MODEL // 2


def _pack_rows(x):
    lo = pltpu.bitcast(x[:, :PACKED].astype(BF16).astype(F32), jnp.int32)
    hi = pltpu.bitcast(x[:, PACKED:].astype(BF16).astype(F32), jnp.int32)
    return lax.shift_right_logical(lo, 16) | hi


def _unpack_rows(w):
    lo = pltpu.bitcast(lax.shift_left(w, 16), F32)
    hi = pltpu.bitcast(w & jnp.int32(-65536), F32)
    return lo, hi


def _layer_norm_rows(y, g, b):
    mu = jnp.mean(y, axis=-1, keepdims=True)
    yc = y - mu
    var = jnp.mean(yc * yc, axis=-1, keepdims=True)
    return yc * lax.rsqrt(var + LN_EPS) * g + b


def _rows(x, blocks, size):
    parts = [x[b * size:(b + 1) * size, :] for b in blocks]
    return parts[0] if len(parts) == 1 else jnp.concatenate(parts, axis=0)


def _gla_chunk(q_s, k_s, v_s, lf_s, g_s, acc_s, o_ref, st_ref, d, start, slot, reverse):
    c = GLA_CHUNK
    t8 = SUBLANES
    nt_dims = (((1,), (1,)), ((), ()))

    def tt(ref, i):
        return ref[d, pl.ds(start + i, t8, stride=t8), :]

    lf = [tt(lf_s, i) for i in range(t8)]
    qt = [tt(q_s, i) for i in range(t8)]
    kt = [tt(k_s, i) for i in range(t8)]
    vt = [tt(v_s, i) for i in range(t8)]

    gi = [None] * t8
    prev = None
    for i in (range(t8 - 1, -1, -1) if reverse else range(t8)):
        gi[i] = lf[i] if prev is None else prev + lf[i]
        prev = gi[i]
    tot = prev
    sub = lax.broadcasted_iota(jnp.int32, (t8, LANES), 0)
    incl = tot
    s = 1
    while s < t8:
        if reverse:
            incl = incl + jnp.where(sub + s < t8, pltpu.roll(incl, t8 - s, 0), 0.0)
        else:
            incl = incl + jnp.where(sub >= s, pltpu.roll(incl, s, 0), 0.0)
        s *= 2
    excl = incl - tot
    for i in range(t8):
        g_s[slot, pl.ds(i, t8, stride=t8), :] = gi[i] + excl

    acc = []
    for i in range(t8):
        a_i = jnp.sum(qt[i] * kt[i], axis=-1, keepdims=True) * vt[i]
        for r in (range(i + 1, t8) if reverse else range(i)):
            w = qt[i] * kt[r] * jnp.exp(gi[i] - gi[r])
            a_i = a_i + jnp.sum(w, axis=-1, keepdims=True) * vt[r]
        acc.append(a_i)

    sl = pl.ds(start, c)
    q = q_s[d, sl, :]
    kk = k_s[d, sl, :]
    v = v_s[d, sl, :]
    g = g_s[slot]
    vb = v.astype(BF16)
    st = st_ref[d]

    o = lax.dot_general((q * jnp.exp(g)).astype(BF16), st.astype(BF16), nt_dims,
                        preferred_element_type=F32)

    half = c // 2
    qd_l, kd_l, v_l, q_tiles = [], [], [], []
    b = 2 * t8
    while b <= c:
        h = b // 2
        nblk = c // b
        lo = [2 * m for m in range(nblk)]
        hi = [2 * m + 1 for m in range(nblk)]
        q_half, k_half = (lo, hi) if reverse else (hi, lo)
        refs = [g[m * b + h:m * b + h + 1, :] if reverse else g[m * b + h - 1:m * b + h, :]
                for m in range(nblk)]
        gref = jnp.concatenate([jnp.broadcast_to(r, (h, LANES)) for r in refs], axis=0) \
            if nblk > 1 else jnp.broadcast_to(refs[0], (h, LANES))
        qd_l.append(_rows(q, q_half, h) * jnp.exp(_rows(g, q_half, h) - gref))
        kd_l.append(_rows(kk, k_half, h) * jnp.exp(gref - _rows(g, k_half, h)))
        v_l.append(_rows(v, k_half, h))
        q_tiles.append([hb * (h // t8) + t for hb in q_half for t in range(h // t8)])
        b *= 2
    nlev = len(qd_l)
    p = lax.dot_general(jnp.concatenate(qd_l, axis=0).astype(BF16),
                        jnp.concatenate(kd_l, axis=0).astype(BF16), nt_dims,
                        preferred_element_type=F32)
    ii = lax.broadcasted_iota(jnp.int32, p.shape, 0)
    jj = lax.broadcasted_iota(jnp.int32, p.shape, 1)
    keep = None
    for lv in range(nlev):
        h = t8 << lv
        m = (ii // half == lv) & (jj // half == lv) & ((ii // h) == (jj // h))
        keep = m if keep is None else keep | m
    p = jnp.where(keep, p, 0.0)
    res = jnp.dot(p.astype(BF16), jnp.concatenate(v_l, axis=0).astype(BF16),
                  preferred_element_type=F32)
    contrib = [None] * (c // t8)
    for lv in range(nlev):
        for n_, tile in enumerate(q_tiles[lv]):
            piece = res[lv * half + n_ * t8:lv * half + (n_ + 1) * t8, :]
            contrib[tile] = piece if contrib[tile] is None else contrib[tile] + piece
    zero = jnp.zeros((t8, LANES), F32)
    o = o + jnp.concatenate([zero if p_ is None else p_ for p_ in contrib], axis=0)

    gl = g[0:1, :] if reverse else g[c - 1:c, :]
    kd = (kk * jnp.exp(gl - g)).astype(BF16)
    upd = lax.dot_general(vb, kd, (((0,), (0,)), ((), ())), preferred_element_type=F32)
    st_ref[d] = st * jnp.exp(gl) + upd

    for i in range(t8):
        acc_s[slot, pl.ds(i, t8, stride=t8), :] = acc[i]
    o_ref[sl, :] = (o + acc_s[slot]).astype(o_ref.dtype)


def _gla_kernel(layer, lbl_ref, qf_ref, ff_ref, vf_ref, qb_ref, fb_ref, vb_ref,
                of_ref, ob_ref, st_ref, q_s, k_s, v_s, lf_s, g_s, acc_s):
    tb = GLA_TBLOCK
    nc = tb // GLA_CHUNK

    @pl.when(pl.program_id(2) == 0)
    def _():
        st_ref[...] = jnp.zeros_like(st_ref)

    if layer > 0:
        lg = lbl_ref[...]
        e = jnp.exp(lg - jnp.max(lg, axis=0, keepdims=True))
        sm = e / jnp.sum(e, axis=0, keepdims=True)
        lb = sm[1]
        for l in range(2, layer + 1):
            lb = lb + sm[l]

    for d, (q_ref, f_ref, v_ref) in enumerate(((qf_ref, ff_ref, vf_ref),
                                               (qb_ref, fb_ref, vb_ref))):
        qx = q_ref[...].astype(F32)
        fx = f_ref[...]
        v_s[d] = v_ref[...].astype(F32)
        t = jnp.exp(-jnp.abs(fx))
        r = 1.0 / (1.0 + t)
        logsig = jnp.minimum(fx, 0.0) - jnp.log(1.0 + t)
        sig_neg = jnp.where(fx >= 0.0, t * r, r)
        if layer > 0:
            lbd = lb[d:d + 1, :]
            a = jnp.log(lbd)
            bb = jnp.log1p(-lbd) + logsig
            lf = jnp.maximum(a, bb) + jnp.log(1.0 + jnp.exp(-jnp.abs(a - bb)))
            kk = (1.0 - lbd) * sig_neg
        else:
            lf = logsig
            kk = sig_neg
        q_s[d] = qx * (0.5 + 0.5 * jnp.tanh(0.5 * qx))
        k_s[d] = kk
        lf_s[d] = lf

    def body(ci, carry):
        for u in range(GLA_UNROLL):
            cf = ci * GLA_UNROLL + u
            sf = pl.multiple_of(cf * GLA_CHUNK, GLA_CHUNK)
            sb = pl.multiple_of((nc - 1 - cf) * GLA_CHUNK, GLA_CHUNK)
            _gla_chunk(q_s, k_s, v_s, lf_s, g_s, acc_s, of_ref, st_ref, 0, sf, 2 * u, False)
            _gla_chunk(q_s, k_s, v_s, lf_s, g_s, acc_s, ob_ref, st_ref, 1, sb, 2 * u + 1, True)
        return carry

    lax.fori_loop(0, nc // GLA_UNROLL, body, 0)


def _gla(proj_a, proj_f, lb_logits, layer, batch, seq):
    n = proj_a.shape[0]
    tb = GLA_TBLOCK
    nb = seq // tb
    h8 = HG_HEADS

    def spec(stream, rev):
        if rev:
            return pl.BlockSpec((tb, LANES), lambda b, h, c: (b * nb + nb - 1 - c, stream * h8 + h))
        return pl.BlockSpec((tb, LANES), lambda b, h, c: (b * nb + c, stream * h8 + h))

    o_f_spec = pl.BlockSpec((tb, LANES), lambda b, h, c: (b * nb + c, h))
    o_b_spec = pl.BlockSpec((tb, LANES), lambda b, h, c: (b * nb + nb - 1 - c, h))
    return pl.pallas_call(
        functools.partial(_gla_kernel, layer),
        grid=(batch, h8, nb),
        in_specs=[pl.BlockSpec((DEPTH, 2, LANES), lambda b, h, c: (0, 0, h)),
                  spec(0, False), spec(0, False), spec(1, False),
                  spec(0, True), spec(1, True), spec(1, True)],
        out_specs=[o_f_spec, o_b_spec],
        out_shape=[jax.ShapeDtypeStruct((n, D_MODEL), BF16)] * 2,
        scratch_shapes=[pltpu.VMEM((2, HG_DK, HG_DK), F32)]
        + [pltpu.VMEM((2, tb, LANES), F32)] * 4
        + [pltpu.VMEM((2 * GLA_UNROLL, GLA_CHUNK, LANES), F32)] * 2,
        compiler_params=_cparams("arbitrary", "arbitrary", "arbitrary"),
        name="gla",
    )(lb_logits, proj_a, proj_f, proj_a, proj_a, proj_f, proj_a)


def _hgrn_out_kernel(of_ref, ob_ref, gate_ref, nw_ref, w_ref, x_ref, g_ref, b_ref,
                     rw_ref, rb_ref, tri_ref,
                     o_ref, op_ref, route_ref, cnt_ref, carry_ref):
    o = of_ref[...].astype(F32) + ob_ref[...].astype(F32)
    parts = []
    for h in range(HG_HEADS):
        oh = o[:, h * LANES:(h + 1) * LANES]
        ms = jnp.mean(oh * oh, axis=-1, keepdims=True)
        parts.append(oh * lax.rsqrt(ms + RMS_EPS))
    gate = gate_ref[...].astype(F32)
    y = jnp.concatenate(parts, axis=-1) * nw_ref[...] * (gate * jax.nn.sigmoid(gate))
    mix = jnp.dot(y.astype(BF16), w_ref[...], preferred_element_type=F32)
    out = _layer_norm_rows(DN_ALPHA * x_ref[...] + mix, g_ref[...], b_ref[...])
    o_ref[...] = out
    op_ref[...] = _pack_rows(out)
    _route_tile(out, rw_ref, rb_ref, tri_ref, route_ref, cnt_ref, carry_ref)


def _hgrn_out(o_f, o_b, proj_a, norm_w, w_out, x, g, b, router_ops):
    n = x.shape[0]
    tm = ROW_TILE
    row = pl.BlockSpec((tm, D_MODEL), lambda i: (i, 0))
    vec = pl.BlockSpec((1, D_MODEL), lambda i: (0, 0))
    r_in, r_out, r_scratch = _router_specs()
    xo, xpk, route, cnt = pl.pallas_call(
        _hgrn_out_kernel,
        grid=(n // tm,),
        in_specs=[row, row, pl.BlockSpec((tm, D_MODEL), lambda i: (i, 2)), vec,
                  pl.BlockSpec((D_MODEL, D_MODEL), lambda i: (0, 0)), row, vec, vec] + r_in,
        out_specs=[row, pl.BlockSpec((tm, PACKED), lambda i: (i, 0))] + r_out,
        out_shape=[jax.ShapeDtypeStruct((n, D_MODEL), F32),
                   jax.ShapeDtypeStruct((n, PACKED), jnp.int32)] + _router_out_shapes(n),
        scratch_shapes=r_scratch,
        compiler_params=_cparams("arbitrary"),
        name="hgrn_out",
    )(o_f, o_b, proj_a, jnp.tile(norm_w, HG_HEADS).reshape(1, D_MODEL), w_out, x,
      g.reshape(1, D_MODEL), b.reshape(1, D_MODEL), *router_ops)
    return xo, xpk, route, _expert_counts(cnt)


def _conv_out_kernel(tiles_per_seq, bg_ref, cg_ref, h_ref, cgp_ref, hp_ref, cgn_ref, hn_ref,
                     cw_ref, w_ref, x_ref, g_ref, b_ref, rw_ref, rb_ref, tri_ref,
                     o_ref, op_ref, route_ref, cnt_ref, carry_ref):
    i = pl.program_id(0)
    tm = cg_ref.shape[0]
    u = cg_ref[...].astype(F32) * h_ref[...].astype(F32)
    first = (i % tiles_per_seq) == 0
    last = (i % tiles_per_seq) == tiles_per_seq - 1
    hr = cgp_ref.shape[0]
    u_halo_prev = cgp_ref[...].astype(F32) * hp_ref[...].astype(F32)
    u_halo_next = cgn_ref[...].astype(F32) * hn_ref[...].astype(F32)
    u_prev_row = jnp.where(first, 0.0, u_halo_prev[hr - 1:hr, :])
    u_next_row = jnp.where(last, 0.0, u_halo_next[0:1, :])
    rows = lax.broadcasted_iota(jnp.int32, u.shape, 0)
    u_prev = jnp.where(rows == 0, u_prev_row, pltpu.roll(u, 1, 0))
    u_next = jnp.where(rows == tm - 1, u_next_row, pltpu.roll(u, tm - 1, 0))
    cw = cw_ref[...]
    y = u_prev * cw[0:1, :] + u * cw[1:2, :] + u_next * cw[2:3, :]
    mix = jnp.dot((bg_ref[...].astype(F32) * y).astype(BF16), w_ref[...],
                  preferred_element_type=F32)
    out = _layer_norm_rows(DN_ALPHA * x_ref[...] + mix, g_ref[...], b_ref[...])
    o_ref[...] = out
    op_ref[...] = _pack_rows(out)
    _route_tile(out, rw_ref, rb_ref, tri_ref, route_ref, cnt_ref, carry_ref)


def _conv_out(proj, conv_w, w_out, x, g, b, seq, router_ops):
    n = x.shape[0]
    tm = ROW_TILE
    halo = BF16_SUBLANES
    rh = tm // halo
    nblk = n // halo
    row = pl.BlockSpec((tm, D_MODEL), lambda i: (i, 0))
    vec = pl.BlockSpec((1, D_MODEL), lambda i: (0, 0))

    def main(stream):
        return pl.BlockSpec((tm, D_MODEL), lambda i: (i, stream))

    def prev(stream):
        return pl.BlockSpec((halo, D_MODEL), lambda i: (jnp.maximum(i * rh - 1, 0), stream))

    def nxt(stream):
        return pl.BlockSpec((halo, D_MODEL),
                            lambda i: (jnp.minimum((i + 1) * rh, nblk - 1), stream))

    r_in, r_out, r_scratch = _router_specs()
    xo, xpk, route, cnt = pl.pallas_call(
        functools.partial(_conv_out_kernel, seq // tm),
        grid=(n // tm,),
        in_specs=[main(0), main(1), main(2), prev(1), prev(2), nxt(1), nxt(2),
                  pl.BlockSpec((3, D_MODEL), lambda i: (0, 0)),
                  pl.BlockSpec((D_MODEL, D_MODEL), lambda i: (0, 0)), row, vec, vec] + r_in,
        out_specs=[row, pl.BlockSpec((tm, PACKED), lambda i: (i, 0))] + r_out,
        out_shape=[jax.ShapeDtypeStruct((n, D_MODEL), F32),
                   jax.ShapeDtypeStruct((n, PACKED), jnp.int32)] + _router_out_shapes(n),
        scratch_shapes=r_scratch,
        compiler_params=_cparams("arbitrary"),
        name="conv_out",
    )(proj, proj, proj, proj, proj, proj, proj, conv_w, w_out, x,
      g.reshape(1, D_MODEL), b.reshape(1, D_MODEL), *router_ops)
    return xo, xpk, route, _expert_counts(cnt)


def _route_tile(x, whl_ref, b_ref, tri_ref, o_ref, cnt_ref, carry_ref):
    @pl.when(pl.program_id(0) == 0)
    def _():
        carry_ref[...] = jnp.zeros_like(carry_ref)

    xh = x.astype(BF16)
    xl = (x - xh.astype(F32)).astype(BF16)
    both = jnp.dot(xh, whl_ref[...], preferred_element_type=F32)
    logits = (both[:, :LANES] + both[:, LANES:]
              + jnp.dot(xl, whl_ref[:, :LANES], preferred_element_type=F32)) + b_ref[...]
    eg = MOE_EXPERTS_PER_GROUP
    lt = logits.T
    grp = lt[0:SUBLANES]
    exp_t = [lt[SUBLANES + eg * g:SUBLANES + eg * (g + 1)] for g in range(MOE_GROUPS)]
    row = lax.broadcasted_iota(jnp.int32, grp.shape, 0).astype(F32)
    gmax = jnp.max(grp, axis=0, keepdims=True)
    gsum = jnp.sum(jnp.exp(grp - gmax), axis=0, keepdims=True)
    p_group = 1.0 / gsum
    g_sel = jnp.min(jnp.where(grp == gmax, row, 99.0), axis=0, keepdims=True)
    el = exp_t[MOE_GROUPS - 1]
    for g in range(MOE_GROUPS - 2, -1, -1):
        el = jnp.where(g_sel == g, exp_t[g], el)
    t1 = jnp.max(el, axis=0, keepdims=True)
    i1 = jnp.min(jnp.where(el == t1, row, 99.0), axis=0, keepdims=True)
    el2 = jnp.where(row == i1, NEG_BIG, el)
    t2 = jnp.max(el2, axis=0, keepdims=True)
    i2 = jnp.min(jnp.where(el2 == t2, row, 99.0), axis=0, keepdims=True)
    z = jnp.exp(t2 - t1)
    g1 = p_group / (1.0 + z)
    g2 = g1 * z

    is1 = row == i1
    is2 = row == i2
    in_g = [g_sel == g for g in range(MOE_GROUPS)]
    onehot = jnp.concatenate([jnp.where(in_g[g] & (is1 | is2), 1.0, 0.0)
                              for g in range(MOE_GROUPS)], axis=0)
    prefix = jnp.dot(onehot.astype(BF16), tri_ref[...], preferred_element_type=F32)
    before = prefix + carry_ref[...]
    r1 = jnp.zeros_like(g1)
    r2 = jnp.zeros_like(g1)
    for g in range(MOE_GROUPS):
        bg = before[eg * g:eg * (g + 1)]
        r1 = r1 + jnp.sum(jnp.where(in_g[g] & is1, bg, 0.0), axis=0, keepdims=True)
        r2 = r2 + jnp.sum(jnp.where(in_g[g] & is2, bg, 0.0), axis=0, keepdims=True)
    e1 = g_sel * eg + i1
    e2 = g_sel * eg + i2
    out = jnp.zeros_like(row)
    for r, val in enumerate((g1, g2, e1, e2, r1, r2)):
        out = jnp.where(row == r, val, out)
    o_ref[...] = out
    carry_ref[...] = carry_ref[...] + jnp.sum(onehot, axis=1, keepdims=True)
    cnt_ref[...] = carry_ref[...]


def _router_operands(w_group, b_group, w_expert, b_expert):
    gpad = SUBLANES - MOE_GROUPS
    pad = LANES - SUBLANES - MOE_EXPERTS
    w = jnp.concatenate([w_group, jnp.zeros((D_MODEL, gpad), F32), w_expert,
                         jnp.zeros((D_MODEL, pad), F32)], axis=1)
    wh = w.astype(BF16)
    wl = (w - wh.astype(F32)).astype(BF16)
    b = jnp.concatenate([b_group, jnp.full((gpad,), NEG_BIG, F32), b_expert,
                         jnp.zeros((pad,), F32)]).reshape(1, LANES)
    r = jnp.arange(ROW_TILE, dtype=jnp.int32)
    tri = (r[:, None] < r[None, :]).astype(BF16)
    return jnp.concatenate([wh, wl], axis=1), b, tri


def _router_specs():
    in_specs = [pl.BlockSpec((D_MODEL, 2 * LANES), lambda i: (0, 0)),
                pl.BlockSpec((1, LANES), lambda i: (0, 0)),
                pl.BlockSpec((ROW_TILE, ROW_TILE), lambda i: (0, 0))]
    out_specs = [pl.BlockSpec((SUBLANES, ROW_TILE), lambda i: (0, i)),
                 pl.BlockSpec((MOE_EXPERTS, ROW_TILE), lambda i: (0, 0))]
    scratch = [pltpu.VMEM((MOE_EXPERTS, ROW_TILE), F32)]
    return in_specs, out_specs, scratch


def _router_out_shapes(n):
    return [jax.ShapeDtypeStruct((SUBLANES, n), F32),
            jax.ShapeDtypeStruct((MOE_EXPERTS, ROW_TILE), F32)]


def _expert_counts(cnt):
    return cnt[:, 0].astype(jnp.int32)


DISPATCH_TOKENS = 256
DISPATCH_BUFS = 3
WEIGHT_SLOTS = 3


def _dest_kernel(ps_ref, route_ref, o_ref):
    e = route_ref[MOE_TOPK:2 * MOE_TOPK, :]
    rank = route_ref[2 * MOE_TOPK:3 * MOE_TOPK, :]
    base = jnp.zeros_like(e)
    for x in range(MOE_EXPERTS):
        base = jnp.where(e == x, ps_ref[x].astype(F32), base)
    o_ref[...] = (base + rank).astype(jnp.int32)


def _moe_plan(route, counts, n):
    nk = n * MOE_TOPK
    n_rows = -(-nk // MOE_BLOCK) * MOE_BLOCK + MOE_EXPERTS * MOE_BLOCK
    n_blocks = n_rows // MOE_BLOCK
    padded = (counts + MOE_BLOCK - 1) // MOE_BLOCK * MOE_BLOCK
    pad_end = jnp.cumsum(padded)
    pad_start = pad_end - padded
    dest = pl.pallas_call(
        _dest_kernel,
        grid_spec=pltpu.PrefetchScalarGridSpec(
            num_scalar_prefetch=1, grid=(1,),
            in_specs=[pl.BlockSpec((SUBLANES, n), lambda i, ps: (0, 0))],
            out_specs=pl.BlockSpec((MOE_TOPK, n), lambda i, ps: (0, 0))),
        out_shape=jax.ShapeDtypeStruct((MOE_TOPK, n), jnp.int32),
        compiler_params=_cparams("arbitrary"),
        name="dest",
    )(pad_start.astype(jnp.int32), route).reshape(nk)
    block_start = jnp.arange(n_blocks, dtype=jnp.int32) * MOE_BLOCK
    block_expert = jnp.minimum(jnp.sum(block_start[:, None] >= pad_end[None, :], axis=1),
                               MOE_EXPERTS - 1).astype(jnp.int32)
    nact = (pad_end[-1:] // MOE_BLOCK).astype(jnp.int32)
    fill_start = (pad_start + counts).astype(jnp.int32)
    return dest, block_expert, nact, fill_start, pad_end.astype(jnp.int32), n_rows


def _dispatch_kernel(dest_ref, fs_ref, fe_ref, nact_ref, x_hbm, xs_hbm, xbuf, zbuf, lsem, sem,
                     zsem):
    i = pl.program_id(0)
    nsteps = pl.num_programs(0)
    slot = i % DISPATCH_BUFS
    ct = DISPATCH_TOKENS
    gt = ct // SUBLANES
    n_blocks = xs_hbm.shape[0] // MOE_BLOCK

    def load(step, sl):
        start = pl.multiple_of(step * gt, gt)
        return pltpu.make_async_copy(x_hbm.at[pl.ds(start, gt)], xbuf.at[sl], lsem.at[sl])

    def wait_step(sl):
        for _ in range(MOE_TOPK):
            pltpu.make_async_copy(x_hbm.at[pl.ds(0, gt)], xbuf.at[sl], sem.at[sl]).wait()

    @pl.when(i == 0)
    def _():
        for s in range(DISPATCH_BUFS - 1):
            load(s, s).start()

    load(i, slot).wait()

    n_tok = nsteps * ct

    def body(g, c):
        base = i * ct + g * SUBLANES
        for t in range(SUBLANES):
            for k in range(MOE_TOPK):
                d = dest_ref[k * n_tok + base + t]
                pltpu.make_async_copy(xbuf.at[slot, g, pl.ds(t, 1)], xs_hbm.at[pl.ds(d, 1)],
                                      sem.at[slot]).start()
        return c
    lax.fori_loop(0, gt, body, 0)

    @pl.when(i >= 1)
    def _():
        wait_step((i - 1) % DISPATCH_BUFS)

    @pl.when(i + DISPATCH_BUFS - 1 < nsteps)
    def _():
        load(i + DISPATCH_BUFS - 1, (i + DISPATCH_BUFS - 1) % DISPATCH_BUFS).start()

    @pl.when(i == nsteps - 1)
    def _():
        wait_step(slot)
        zbuf[...] = jnp.zeros_like(zbuf)

        def fill_rows(wait):
            def per_expert(e, c):
                def per_row(r, c2):
                    cp = pltpu.make_async_copy(zbuf.at[pl.ds(0, 1)], xs_hbm.at[pl.ds(r, 1)],
                                               zsem.at[0])
                    cp.wait() if wait else cp.start()
                    return c2
                lax.fori_loop(fs_ref[e], fe_ref[e], per_row, 0)
                return c
            lax.fori_loop(0, MOE_EXPERTS, per_expert, 0)

        def fill_blocks(wait):
            def per_block(b, c):
                start = pl.multiple_of(b * MOE_BLOCK, MOE_BLOCK)
                cp = pltpu.make_async_copy(zbuf, xs_hbm.at[pl.ds(start, MOE_BLOCK)], zsem.at[1])
                cp.wait() if wait else cp.start()
                return c
            lax.fori_loop(nact_ref[0], n_blocks, per_block, 0)

        fill_rows(False)
        fill_blocks(False)
        fill_rows(True)
        fill_blocks(True)


def _dispatch(x, dest, fill_start, fill_end, nact, n_rows):
    n, width = x.shape
    grid_spec = pltpu.PrefetchScalarGridSpec(
        num_scalar_prefetch=4,
        grid=(n // DISPATCH_TOKENS,),
        in_specs=[pl.BlockSpec(memory_space=pl.ANY)],
        out_specs=pl.BlockSpec(memory_space=pl.ANY),
        scratch_shapes=[pltpu.VMEM((DISPATCH_BUFS, DISPATCH_TOKENS // SUBLANES, SUBLANES, width),
                                   x.dtype),
                        pltpu.VMEM((MOE_BLOCK, width), x.dtype),
                        pltpu.SemaphoreType.DMA((DISPATCH_BUFS,)),
                        pltpu.SemaphoreType.DMA((DISPATCH_BUFS,)),
                        pltpu.SemaphoreType.DMA((2,))],
    )
    return pl.pallas_call(
        _dispatch_kernel,
        grid_spec=grid_spec,
        out_shape=jax.ShapeDtypeStruct((n_rows, width), x.dtype),
        compiler_params=_cparams("arbitrary"),
        name="dispatch",
    )(dest, fill_start, fill_end, nact, x.reshape(n // SUBLANES, SUBLANES, width))


def _moe_ffn_kernel(e0, be_ref, nact_ref, ord_ref, nxt_ref, nxt2_ref, xs_ref, wup_hbm, wdn_hbm,
                    ys_ref, wup_f, wdn_f, wup_bf, wdn_bf, wsem):
    i = pl.program_id(0)
    active = i < nact_ref[0]
    new_expert = (i == 0) | (be_ref[i] != be_ref[jnp.maximum(i - 1, 0)])

    def fetch(e, sl):
        return (pltpu.make_async_copy(wup_hbm.at[e0 + e], wup_f.at[sl], wsem.at[0, sl]),
                pltpu.make_async_copy(wdn_hbm.at[e0 + e], wdn_f.at[sl], wsem.at[1, sl]))

    @pl.when(active & (i == 0))
    def _():
        for cp in fetch(be_ref[0], 0):
            cp.start()

        @pl.when(nxt_ref[be_ref[0]] >= 0)
        def _():
            for cp in fetch(nxt_ref[be_ref[0]], 1):
                cp.start()

    @pl.when(active & new_expert)
    def _():
        k = ord_ref[i]
        sl = k % WEIGHT_SLOTS
        for cp in fetch(be_ref[i], sl):
            cp.wait()

        @pl.when(nxt2_ref[be_ref[i]] >= 0)
        def _():
            for cp in fetch(nxt2_ref[be_ref[i]], (k + 2) % WEIGHT_SLOTS):
                cp.start()

        wup_bf[...] = wup_f[sl].astype(BF16)
        wdn_bf[...] = wdn_f[sl].astype(BF16)

    @pl.when(active)
    def _():
        x_lo, x_hi = _unpack_rows(xs_ref[...])
        xb = jnp.concatenate([x_lo.astype(BF16), x_hi.astype(BF16)], axis=1)
        hcat = jnp.dot(xb, wup_bf[...], preferred_element_type=F32)
        hg = hcat[:, :MOE_D_EXPERT]
        hu = hcat[:, MOE_D_EXPERT:]
        act = (hg * jax.nn.sigmoid(hg) * hu).astype(BF16)
        ys_ref[...] = _pack_rows(jnp.dot(act, wdn_bf[...], preferred_element_type=F32))

    @pl.when(jnp.logical_not(active))
    def _():
        ys_ref[...] = jnp.zeros_like(ys_ref)


def _moe_ffn(xs, block_expert, nact, counts, w_up, w_down, layer):
    n_rows = xs.shape[0]
    n_blocks = n_rows // MOE_BLOCK
    first = jnp.concatenate([jnp.ones((1,), jnp.int32),
                             (block_expert[1:] != block_expert[:-1]).astype(jnp.int32)])
    ordinal = (jnp.cumsum(first) - 1).astype(jnp.int32)
    ids = jnp.arange(MOE_EXPERTS, dtype=jnp.int32)
    has_rows = counts[None, :] > 0
    later = (ids[None, :] > ids[:, None]) & has_rows
    nxt = jnp.min(jnp.where(later, ids[None, :], MOE_EXPERTS), axis=1)
    later2 = (ids[None, :] > nxt[:, None]) & has_rows
    nxt2 = jnp.min(jnp.where(later2, ids[None, :], MOE_EXPERTS), axis=1)
    next_expert = jnp.where(nxt == MOE_EXPERTS, -1, nxt).astype(jnp.int32)
    next2_expert = jnp.where(nxt2 == MOE_EXPERTS, -1, nxt2).astype(jnp.int32)
    row = pl.BlockSpec((MOE_BLOCK, PACKED), lambda i, *_: (i, 0))
    grid_spec = pltpu.PrefetchScalarGridSpec(
        num_scalar_prefetch=5,
        grid=(n_blocks,),
        in_specs=[row, pl.BlockSpec(memory_space=pl.ANY), pl.BlockSpec(memory_space=pl.ANY)],
        out_specs=row,
        scratch_shapes=[pltpu.VMEM((WEIGHT_SLOTS, D_MODEL, 2 * MOE_D_EXPERT), F32),
                        pltpu.VMEM((WEIGHT_SLOTS, MOE_D_EXPERT, D_MODEL), F32),
                        pltpu.VMEM((D_MODEL, 2 * MOE_D_EXPERT), BF16),
                        pltpu.VMEM((MOE_D_EXPERT, D_MODEL), BF16),
                        pltpu.SemaphoreType.DMA((2, WEIGHT_SLOTS))],
    )
    return pl.pallas_call(
        functools.partial(_moe_ffn_kernel, layer * MOE_EXPERTS),
        grid_spec=grid_spec,
        out_shape=jax.ShapeDtypeStruct((n_rows, PACKED), jnp.int32),
        compiler_params=_cparams("arbitrary"),
        name="moe_ffn",
    )(block_expert, nact, ordinal, next_expert, next2_expert, xs, w_up, w_down)


def _combine_ln_kernel(dest_ref, x_ref, route_ref, ys_hbm, g_ref, b_ref, o_ref, ob_ref, ybuf,
                       sem):
    i = pl.program_id(0)
    nsteps = pl.num_programs(0)
    slot = i % 2
    tm = x_ref.shape[0]

    def start_gather(step, sl):
        def body(g, c):
            base = step * tm + g * SUBLANES
            for t in range(SUBLANES):
                for k in range(MOE_TOPK):
                    d = dest_ref[k * (nsteps * tm) + base + t]
                    pltpu.make_async_copy(ys_hbm.at[pl.ds(d, 1)], ybuf.at[sl, k, g, pl.ds(t, 1)],
                                          sem.at[sl]).start()
            return c
        lax.fori_loop(0, tm // SUBLANES, body, 0)

    @pl.when(i == 0)
    def _():
        start_gather(0, 0)

    @pl.when(i + 1 < nsteps)
    def _():
        start_gather(i + 1, 1 - slot)

    for k in range(MOE_TOPK):
        pltpu.make_async_copy(ybuf.at[1 - slot, k], ybuf.at[slot, k], sem.at[slot]).wait()
    route = route_ref[...]
    y0_lo, y0_hi = _unpack_rows(ybuf[slot, 0].reshape(tm, PACKED))
    y1_lo, y1_hi = _unpack_rows(ybuf[slot, 1].reshape(tm, PACKED))
    g0 = route[:, 0:1]
    g1 = route[:, 1:2]
    ffn = jnp.concatenate([g0 * y0_lo + g1 * y1_lo, g0 * y0_hi + g1 * y1_hi], axis=1)
    out = _layer_norm_rows(DN_ALPHA * x_ref[...] + ffn, g_ref[...], b_ref[...])
    o_ref[...] = out
    ob_ref[...] = out.astype(BF16)


def _combine_ln(x, route, ys, dest, g, b):
    n = x.shape[0]
    tm = ROW_TILE
    vec = pl.BlockSpec((1, D_MODEL), lambda i, d: (0, 0))
    row = pl.BlockSpec((tm, D_MODEL), lambda i, d: (i, 0))
    grid_spec = pltpu.PrefetchScalarGridSpec(
        num_scalar_prefetch=1,
        grid=(n // tm,),
        in_specs=[pl.BlockSpec((tm, D_MODEL), lambda i, d: (i, 0)),
                  pl.BlockSpec((tm, MOE_TOPK), lambda i, d: (i, 0)),
                  pl.BlockSpec(memory_space=pl.ANY), vec, vec],
        out_specs=[row, row],
        scratch_shapes=[pltpu.VMEM((2, MOE_TOPK, tm // SUBLANES, SUBLANES, PACKED), jnp.int32),
                        pltpu.SemaphoreType.DMA((2,))],
    )
    return pl.pallas_call(
        _combine_ln_kernel,
        grid_spec=grid_spec,
        out_shape=[jax.ShapeDtypeStruct((n, D_MODEL), F32),
                   jax.ShapeDtypeStruct((n, D_MODEL), BF16)],
        compiler_params=_cparams("arbitrary"),
        name="combine_ln",
    )(dest, x, route[0:MOE_TOPK].T, ys, g.reshape(1, D_MODEL), b.reshape(1, D_MODEL))


def kernel(x, hg_w_in, hg_lb_logits, hg_norm_w, hg_w_out, cv_w_in, cv_w, cv_w_out, ln_g, ln_b,
           moe_w_group, moe_b_group, moe_w_expert, moe_b_expert, moe_w_up, moe_w_down):
    batch, seq, d = x.shape
    n = batch * seq
    xf = x.reshape(n, d)
    w_up_all = moe_w_up.reshape(DEPTH * MOE_EXPERTS, D_MODEL, 2 * MOE_D_EXPERT)
    w_down_all = moe_w_down.reshape(DEPTH * MOE_EXPERTS, MOE_D_EXPERT, D_MODEL)
    xin = xf
    for layer in range(DEPTH):
        j = layer // 2
        router_ops = _router_operands(moe_w_group[layer], moe_b_group[layer],
                                      moe_w_expert[layer], moe_b_expert[layer])
        if layer % 2 == 0:
            w = hg_w_in[j]
            w_a = jnp.concatenate([w[:, :D_MODEL], w[:, 3 * D_MODEL:]], axis=1).astype(BF16)
            w_f = w[:, D_MODEL:3 * D_MODEL].astype(BF16)
            proj_a = _matmul(xin, w_a, BF16)
            proj_f = _matmul(xin, w_f, F32)
            o_f, o_b = _gla(proj_a, proj_f, hg_lb_logits, layer, batch, seq)
            xf, xpk, route, counts = _hgrn_out(
                o_f, o_b, proj_a, hg_norm_w[j], hg_w_out[j].astype(BF16), xf,
                ln_g[layer, 0], ln_b[layer, 0], router_ops)
        else:
            proj = _matmul(xin, cv_w_in[j].astype(BF16), BF16)
            xf, xpk, route, counts = _conv_out(
                proj, cv_w[j], cv_w_out[j].astype(BF16), xf,
                ln_g[layer, 0], ln_b[layer, 0], seq, router_ops)
        dest, block_expert, nact, fill_start, fill_end, n_rows = _moe_plan(route, counts, n)
        xs = _dispatch(xpk, dest, fill_start, fill_end, nact, n_rows)
        ys = _moe_ffn(xs, block_expert, nact, counts, w_up_all, w_down_all, layer)
        xf, xin = _combine_ln(xf, route, ys, dest, ln_g[layer, 1], ln_b[layer, 1])
    return xf.reshape(batch, seq, d)
```

```python
import functools

import jax
import jax.numpy as jnp
from jax import lax
from jax.experimental import pallas as pl
from jax.experimental.pallas import tpu as pltpu

D_MODEL = 1024
DEPTH = 4
HG_DK = 128
HG_HEADS = D_MODEL // HG_DK
HG_STREAMS = 5
CONV_STREAMS = 3
MOE_GROUPS = 4
MOE_EXPERTS_PER_GROUP = 8
MOE_EXPERTS = MOE_GROUPS * MOE_EXPERTS_PER_GROUP
MOE_TOPK = 2
MOE_D_EXPERT = D_MODEL // 2
MOE_BLOCK = 256
DN_ALPHA = (2.0 * DEPTH) ** 0.25
LN_EPS = 1e-5
RMS_EPS = 1e-6

LANES = 128
SUBLANES = 8
BF16_SUBLANES = 16
VMEM_LIMIT = 48 * 1024 * 1024
GLA_CHUNK = 64
GLA_TBLOCK = 2048
GLA_UNROLL = 2
ROW_TILE = 256
NEG_BIG = -1e30
LOG2_E = 1.4426950408889634

BF16 = jnp.bfloat16
F32 = jnp.float32


def _cparams(*sem):
    return pltpu.CompilerParams(dimension_semantics=sem, vmem_limit_bytes=VMEM_LIMIT)


def _mm_kernel(x_ref, w_ref, o_ref):
    o_ref[...] = jnp.dot(x_ref[...].astype(BF16), w_ref[...],
                         preferred_element_type=F32).astype(o_ref.dtype)


def _matmul(x, w, out_dtype, tm=2048, tn=1024):
    n, k = x.shape
    nn = w.shape[1]
    return pl.pallas_call(
        _mm_kernel,
        grid=(nn // tn, n // tm),
        in_specs=[pl.BlockSpec((tm, k), lambda j, i: (i, 0)),
                  pl.BlockSpec((k, tn), lambda j, i: (0, j))],
        out_specs=pl.BlockSpec((tm, tn), lambda j, i: (i, j)),
        out_shape=jax.ShapeDtypeStruct((n, nn), out_dtype),
        compiler_params=_cparams("arbitrary", "arbitrary"),
        name="in_proj",
    )(x, w)


PACKED = D_MODEL // 2


def _pack_rows(x):
    lo = pltpu.bitcast(x[:, :PACKED].astype(BF16).astype(F32), jnp.int32)
    hi = pltpu.bitcast(x[:, PACKED:].astype(BF16).astype(F32), jnp.int32)
    return lax.shift_right_logical(lo, 16) | hi


def _unpack_rows(w):
    lo = pltpu.bitcast(lax.shift_left(w, 16), F32)
    hi = pltpu.bitcast(w & jnp.int32(-65536), F32)
    return lo, hi


def _layer_norm_rows(y, g, b):
    mu = jnp.mean(y, axis=-1, keepdims=True)
    yc = y - mu
    var = jnp.mean(yc * yc, axis=-1, keepdims=True)
    return yc * lax.rsqrt(var + LN_EPS) * g + b


def _rows(x, blocks, size):
    parts = [x[b * size:(b + 1) * size, :] for b in blocks]
    return parts[0] if len(parts) == 1 else jnp.concatenate(parts, axis=0)


def _gla_chunk(q_s, k_s, v_s, lf_s, g_s, acc_s, o_ref, st_ref, d, start, slot, reverse):
    c = GLA_CHUNK
    t8 = SUBLANES
    nt_dims = (((1,), (1,)), ((), ()))

    def tt(ref, i):
        return ref[d, pl.ds(start + i, t8, stride=t8), :]

    lf = [tt(lf_s, i) for i in range(t8)]
    qt = [tt(q_s, i) for i in range(t8)]
    kt = [tt(k_s, i) for i in range(t8)]
    vt = [tt(v_s, i) for i in range(t8)]

    gi = [None] * t8
    prev = None
    for i in (range(t8 - 1, -1, -1) if reverse else range(t8)):
        gi[i] = lf[i] if prev is None else prev + lf[i]
        prev = gi[i]
    tot = prev
    sub = lax.broadcasted_iota(jnp.int32, (t8, LANES), 0)
    incl = tot
    s = 1
    while s < t8:
        if reverse:
            incl = incl + jnp.where(sub + s < t8, pltpu.roll(incl, t8 - s, 0), 0.0)
        else:
            incl = incl + jnp.where(sub >= s, pltpu.roll(incl, s, 0), 0.0)
        s *= 2
    excl = incl - tot
    for i in range(t8):
        g_s[slot, pl.ds(i, t8, stride=t8), :] = gi[i] + excl

    acc = []
    for i in range(t8):
        a_i = jnp.sum(qt[i] * kt[i], axis=-1, keepdims=True) * vt[i]
        for r in (range(i + 1, t8) if reverse else range(i)):
            w = qt[i] * kt[r] * jnp.exp2(gi[i] - gi[r])
            a_i = a_i + jnp.sum(w, axis=-1, keepdims=True) * vt[r]
        acc.append(a_i)

    sl = pl.ds(start, c)
    q = q_s[d, sl, :]
    kk = k_s[d, sl, :]
    v = v_s[d, sl, :]
    g = g_s[slot]
    vb = v.astype(BF16)
    st = st_ref[d]

    o = lax.dot_general((q * jnp.exp2(g)).astype(BF16), st.astype(BF16), nt_dims,
                        preferred_element_type=F32)

    half = c // 2
    qd_l, kd_l, v_l, q_tiles = [], [], [], []
    b = 2 * t8
    while b <= c:
        h = b // 2
        nblk = c // b
        lo = [2 * m for m in range(nblk)]
        hi = [2 * m + 1 for m in range(nblk)]
        q_half, k_half = (lo, hi) if reverse else (hi, lo)
        refs = [g[m * b + h:m * b + h + 1, :] if reverse else g[m * b + h - 1:m * b + h, :]
                for m in range(nblk)]
        gref = jnp.concatenate([jnp.broadcast_to(r, (h, LANES)) for r in refs], axis=0) \
            if nblk > 1 else jnp.broadcast_to(refs[0], (h, LANES))
        qd_l.append(_rows(q, q_half, h) * jnp.exp2(_rows(g, q_half, h) - gref))
        kd_l.append(_rows(kk, k_half, h) * jnp.exp2(gref - _rows(g, k_half, h)))
        v_l.append(_rows(v, k_half, h))
        q_tiles.append([hb * (h // t8) + t for hb in q_half for t in range(h // t8)])
        b *= 2
    nlev = len(qd_l)
    p = lax.dot_general(jnp.concatenate(qd_l, axis=0).astype(BF16),
                        jnp.concatenate(kd_l, axis=0).astype(BF16), nt_dims,
                        preferred_element_type=F32)
    ii = lax.broadcasted_iota(jnp.int32, p.shape, 0)
    jj = lax.broadcasted_iota(jnp.int32, p.shape, 1)
    keep = None
    for lv in range(nlev):
        h = t8 << lv
        m = (ii // half == lv) & (jj // half == lv) & ((ii // h) == (jj // h))
        keep = m if keep is None else keep | m
    p = jnp.where(keep, p, 0.0)
    res = jnp.dot(p.astype(BF16), jnp.concatenate(v_l, axis=0).astype(BF16),
                  preferred_element_type=F32)
    contrib = [None] * (c // t8)
    for lv in range(nlev):
        for n_, tile in enumerate(q_tiles[lv]):
            piece = res[lv * half + n_ * t8:lv * half + (n_ + 1) * t8, :]
            contrib[tile] = piece if contrib[tile] is None else contrib[tile] + piece
    zero = jnp.zeros((t8, LANES), F32)
    o = o + jnp.concatenate([zero if p_ is None else p_ for p_ in contrib], axis=0)

    gl = g[0:1, :] if reverse else g[c - 1:c, :]
    kd = (kk * jnp.exp2(gl - g)).astype(BF16)
    upd = lax.dot_general(vb, kd, (((0,), (0,)), ((), ())), preferred_element_type=F32)
    st_ref[d] = st * jnp.exp2(gl) + upd

    for i in range(t8):
        acc_s[slot, pl.ds(i, t8, stride=t8), :] = acc[i]
    o_ref[sl, :] = (o + acc_s[slot]).astype(o_ref.dtype)


def _gla_kernel(layer, lbl_ref, qf_ref, ff_ref, vf_ref, qb_ref, fb_ref, vb_ref,
                of_ref, ob_ref, st_ref, q_s, k_s, v_s, lf_s, g_s, acc_s):
    tb = GLA_TBLOCK
    nc = tb // GLA_CHUNK

    @pl.when(pl.program_id(2) == 0)
    def _():
        st_ref[...] = jnp.zeros_like(st_ref)

    if layer > 0:
        lg = lbl_ref[...]
        e = jnp.exp(lg - jnp.max(lg, axis=0, keepdims=True))
        sm = e / jnp.sum(e, axis=0, keepdims=True)
        lb = sm[1]
        for l in range(2, layer + 1):
            lb = lb + sm[l]

    for d, (q_ref, f_ref, v_ref) in enumerate(((qf_ref, ff_ref, vf_ref),
                                               (qb_ref, fb_ref, vb_ref))):
        qx = q_ref[...].astype(F32)
        fx = f_ref[...]
        v_s[d] = v_ref[...].astype(F32)
        t = jnp.exp(-jnp.abs(fx))
        r = 1.0 / (1.0 + t)
        logsig = jnp.minimum(fx, 0.0) - jnp.log(1.0 + t)
        sig_neg = jnp.where(fx >= 0.0, t * r, r)
        if layer > 0:
            lbd = lb[d:d + 1, :]
            a = jnp.log(lbd)
            bb = jnp.log1p(-lbd) + logsig
            lf = jnp.maximum(a, bb) + jnp.log(1.0 + jnp.exp(-jnp.abs(a - bb)))
            kk = (1.0 - lbd) * sig_neg
        else:
            lf = logsig
            kk = sig_neg
        q_s[d] = qx * (0.5 + 0.5 * jnp.tanh(0.5 * qx))
        k_s[d] = kk
        lf_s[d] = lf * LOG2_E

    def body(ci, carry):
        for u in range(GLA_UNROLL):
            cf = ci * GLA_UNROLL + u
            sf = pl.multiple_of(cf * GLA_CHUNK, GLA_CHUNK)
            sb = pl.multiple_of((nc - 1 - cf) * GLA_CHUNK, GLA_CHUNK)
            _gla_chunk(q_s, k_s, v_s, lf_s, g_s, acc_s, of_ref, st_ref, 0, sf, 2 * u, False)
            _gla_chunk(q_s, k_s, v_s, lf_s, g_s, acc_s, ob_ref, st_ref, 1, sb, 2 * u + 1, True)
        return carry

    lax.fori_loop(0, nc // GLA_UNROLL, body, 0)


def _gla(proj_a, proj_f, lb_logits, layer, batch, seq):
    n = proj_a.shape[0]
    tb = GLA_TBLOCK
    nb = seq // tb
    h8 = HG_HEADS

    def spec(stream, rev):
        if rev:
            return pl.BlockSpec((tb, LANES), lambda b, h, c: (b * nb + nb - 1 - c, stream * h8 + h))
        return pl.BlockSpec((tb, LANES), lambda b, h, c: (b * nb + c, stream * h8 + h))

    o_f_spec = pl.BlockSpec((tb, LANES), lambda b, h, c: (b * nb + c, h))
    o_b_spec = pl.BlockSpec((tb, LANES), lambda b, h, c: (b * nb + nb - 1 - c, h))
    return pl.pallas_call(
        functools.partial(_gla_kernel, layer),
        grid=(batch, h8, nb),
        in_specs=[pl.BlockSpec((DEPTH, 2, LANES), lambda b, h, c: (0, 0, h)),
                  spec(0, False), spec(0, False), spec(1, False),
                  spec(0, True), spec(1, True), spec(1, True)],
        out_specs=[o_f_spec, o_b_spec],
        out_shape=[jax.ShapeDtypeStruct((n, D_MODEL), BF16)] * 2,
        scratch_shapes=[pltpu.VMEM((2, HG_DK, HG_DK), F32)]
        + [pltpu.VMEM((2, tb, LANES), F32)] * 4
        + [pltpu.VMEM((2 * GLA_UNROLL, GLA_CHUNK, LANES), F32)] * 2,
        compiler_params=_cparams("arbitrary", "arbitrary", "arbitrary"),
        name="gla",
    )(lb_logits, proj_a, proj_f, proj_a, proj_a, proj_f, proj_a)


def _hgrn_out_kernel(of_ref, ob_ref, gate_ref, nw_ref, w_ref, x_ref, g_ref, b_ref,
                     rw_ref, rb_ref, tri_ref,
                     o_ref, op_ref, route_ref, cnt_ref, carry_ref):
    o = of_ref[...].astype(F32) + ob_ref[...].astype(F32)
    parts = []
    for h in range(HG_HEADS):
        oh = o[:, h * LANES:(h + 1) * LANES]
        ms = jnp.mean(oh * oh, axis=-1, keepdims=True)
        parts.append(oh * lax.rsqrt(ms + RMS_EPS))
    gate = gate_ref[...].astype(F32)
    y = jnp.concatenate(parts, axis=-1) * nw_ref[...] * (gate * jax.nn.sigmoid(gate))
    mix = jnp.dot(y.astype(BF16), w_ref[...], preferred_element_type=F32)
    out = _layer_norm_rows(DN_ALPHA * x_ref[...] + mix, g_ref[...], b_ref[...])
    o_ref[...] = out
    op_ref[...] = _pack_rows(out)
    _route_tile(out, rw_ref, rb_ref, tri_ref, route_ref, cnt_ref, carry_ref)


def _hgrn_out(o_f, o_b, proj_a, norm_w, w_out, x, g, b, router_ops):
    n = x.shape[0]
    tm = ROW_TILE
    row = pl.BlockSpec((tm, D_MODEL), lambda i: (i, 0))
    vec = pl.BlockSpec((1, D_MODEL), lambda i: (0, 0))
    r_in, r_out, r_scratch = _router_specs()
    xo, xpk, route, cnt = pl.pallas_call(
        _hgrn_out_kernel,
        grid=(n // tm,),
        in_specs=[row, row, pl.BlockSpec((tm, D_MODEL), lambda i: (i, 2)), vec,
                  pl.BlockSpec((D_MODEL, D_MODEL), lambda i: (0, 0)), row, vec, vec] + r_in,
        out_specs=[row, pl.BlockSpec((tm, PACKED), lambda i: (i, 0))] + r_out,
        out_shape=[jax.ShapeDtypeStruct((n, D_MODEL), F32),
                   jax.ShapeDtypeStruct((n, PACKED), jnp.int32)] + _router_out_shapes(n),
        scratch_shapes=r_scratch,
        compiler_params=_cparams("arbitrary"),
        name="hgrn_out",
    )(o_f, o_b, proj_a, jnp.tile(norm_w, HG_HEADS).reshape(1, D_MODEL), w_out, x,
      g.reshape(1, D_MODEL), b.reshape(1, D_MODEL), *router_ops)
    return xo, xpk, route, _expert_counts(cnt)


def _conv_out_kernel(tiles_per_seq, bg_ref, cg_ref, h_ref, cgp_ref, hp_ref, cgn_ref, hn_ref,
                     cw_ref, w_ref, x_ref, g_ref, b_ref, rw_ref, rb_ref, tri_ref,
                     o_ref, op_ref, route_ref, cnt_ref, carry_ref):
    i = pl.program_id(0)
    tm = cg_ref.shape[0]
    u = cg_ref[...].astype(F32) * h_ref[...].astype(F32)
    first = (i % tiles_per_seq) == 0
    last = (i % tiles_per_seq) == tiles_per_seq - 1
    hr = cgp_ref.shape[0]
    u_halo_prev = cgp_ref[...].astype(F32) * hp_ref[...].astype(F32)
    u_halo_next = cgn_ref[...].astype(F32) * hn_ref[...].astype(F32)
    u_prev_row = jnp.where(first, 0.0, u_halo_prev[hr - 1:hr, :])
    u_next_row = jnp.where(last, 0.0, u_halo_next[0:1, :])
    rows = lax.broadcasted_iota(jnp.int32, u.shape, 0)
    u_prev = jnp.where(rows == 0, u_prev_row, pltpu.roll(u, 1, 0))
    u_next = jnp.where(rows == tm - 1, u_next_row, pltpu.roll(u, tm - 1, 0))
    cw = cw_ref[...]
    y = u_prev * cw[0:1, :] + u * cw[1:2, :] + u_next * cw[2:3, :]
    mix = jnp.dot((bg_ref[...].astype(F32) * y).astype(BF16), w_ref[...],
                  preferred_element_type=F32)
    out = _layer_norm_rows(DN_ALPHA * x_ref[...] + mix, g_ref[...], b_ref[...])
    o_ref[...] = out
    op_ref[...] = _pack_rows(out)
    _route_tile(out, rw_ref, rb_ref, tri_ref, route_ref, cnt_ref, carry_ref)


def _conv_out(proj, conv_w, w_out, x, g, b, seq, router_ops):
    n = x.shape[0]
    tm = ROW_TILE
    halo = BF16_SUBLANES
    rh = tm // halo
    nblk = n // halo
    row = pl.BlockSpec((tm, D_MODEL), lambda i: (i, 0))
    vec = pl.BlockSpec((1, D_MODEL), lambda i: (0, 0))

    def main(stream):
        return pl.BlockSpec((tm, D_MODEL), lambda i: (i, stream))

    def prev(stream):
        return pl.BlockSpec((halo, D_MODEL), lambda i: (jnp.maximum(i * rh - 1, 0), stream))

    def nxt(stream):
        return pl.BlockSpec((halo, D_MODEL),
                            lambda i: (jnp.minimum((i + 1) * rh, nblk - 1), stream))

    r_in, r_out, r_scratch = _router_specs()
    xo, xpk, route, cnt = pl.pallas_call(
        functools.partial(_conv_out_kernel, seq // tm),
        grid=(n // tm,),
        in_specs=[main(0), main(1), main(2), prev(1), prev(2), nxt(1), nxt(2),
                  pl.BlockSpec((3, D_MODEL), lambda i: (0, 0)),
                  pl.BlockSpec((D_MODEL, D_MODEL), lambda i: (0, 0)), row, vec, vec] + r_in,
        out_specs=[row, pl.BlockSpec((tm, PACKED), lambda i: (i, 0))] + r_out,
        out_shape=[jax.ShapeDtypeStruct((n, D_MODEL), F32),
                   jax.ShapeDtypeStruct((n, PACKED), jnp.int32)] + _router_out_shapes(n),
        scratch_shapes=r_scratch,
        compiler_params=_cparams("arbitrary"),
        name="conv_out",
    )(proj, proj, proj, proj, proj, proj, proj, conv_w, w_out, x,
      g.reshape(1, D_MODEL), b.reshape(1, D_MODEL), *router_ops)
    return xo, xpk, route, _expert_counts(cnt)


def _route_tile(x, whl_ref, b_ref, tri_ref, o_ref, cnt_ref, carry_ref):
    @pl.when(pl.program_id(0) == 0)
    def _():
        carry_ref[...] = jnp.zeros_like(carry_ref)

    xh = x.astype(BF16)
    xl = (x - xh.astype(F32)).astype(BF16)
    both = jnp.dot(xh, whl_ref[...], preferred_element_type=F32)
    logits = (both[:, :LANES] + both[:, LANES:]
              + jnp.dot(xl, whl_ref[:, :LANES], preferred_element_type=F32)) + b_ref[...]
    eg = MOE_EXPERTS_PER_GROUP
    lt = logits.T
    grp = lt[0:SUBLANES]
    exp_t = [lt[SUBLANES + eg * g:SUBLANES + eg * (g + 1)] for g in range(MOE_GROUPS)]
    row = lax.broadcasted_iota(jnp.int32, grp.shape, 0).astype(F32)
    gmax = jnp.max(grp, axis=0, keepdims=True)
    gsum = jnp.sum(jnp.exp(grp - gmax), axis=0, keepdims=True)
    p_group = 1.0 / gsum
    g_sel = jnp.min(jnp.where(grp == gmax, row, 99.0), axis=0, keepdims=True)
    el = exp_t[MOE_GROUPS - 1]
    for g in range(MOE_GROUPS - 2, -1, -1):
        el = jnp.where(g_sel == g, exp_t[g], el)
    t1 = jnp.max(el, axis=0, keepdims=True)
    i1 = jnp.min(jnp.where(el == t1, row, 99.0), axis=0, keepdims=True)
    el2 = jnp.where(row == i1, NEG_BIG, el)
    t2 = jnp.max(el2, axis=0, keepdims=True)
    i2 = jnp.min(jnp.where(el2 == t2, row, 99.0), axis=0, keepdims=True)
    z = jnp.exp(t2 - t1)
    g1 = p_group / (1.0 + z)
    g2 = g1 * z

    is1 = row == i1
    is2 = row == i2
    in_g = [g_sel == g for g in range(MOE_GROUPS)]
    onehot = jnp.concatenate([jnp.where(in_g[g] & (is1 | is2), 1.0, 0.0)
                              for g in range(MOE_GROUPS)], axis=0)
    prefix = jnp.dot(onehot.astype(BF16), tri_ref[...], preferred_element_type=F32)
    before = prefix + carry_ref[...]
    r1 = jnp.zeros_like(g1)
    r2 = jnp.zeros_like(g1)
    for g in range(MOE_GROUPS):
        bg = before[eg * g:eg * (g + 1)]
        r1 = r1 + jnp.sum(jnp.where(in_g[g] & is1, bg, 0.0), axis=0, keepdims=True)
        r2 = r2 + jnp.sum(jnp.where(in_g[g] & is2, bg, 0.0), axis=0, keepdims=True)
    e1 = g_sel * eg + i1
    e2 = g_sel * eg + i2
    out = jnp.zeros_like(row)
    for r, val in enumerate((g1, g2, e1, e2, r1, r2)):
        out = jnp.where(row == r, val, out)
    o_ref[...] = out
    carry_ref[...] = carry_ref[...] + jnp.sum(onehot, axis=1, keepdims=True)
    cnt_ref[...] = carry_ref[...]


def _router_operands(w_group, b_group, w_expert, b_expert):
    gpad = SUBLANES - MOE_GROUPS
    pad = LANES - SUBLANES - MOE_EXPERTS
    w = jnp.concatenate([w_group, jnp.zeros((D_MODEL, gpad), F32), w_expert,
                         jnp.zeros((D_MODEL, pad), F32)], axis=1)
    wh = w.astype(BF16)
    wl = (w - wh.astype(F32)).astype(BF16)
    b = jnp.concatenate([b_group, jnp.full((gpad,), NEG_BIG, F32), b_expert,
                         jnp.zeros((pad,), F32)]).reshape(1, LANES)
    r = jnp.arange(ROW_TILE, dtype=jnp.int32)
    tri = (r[:, None] < r[None, :]).astype(BF16)
    return jnp.concatenate([wh, wl], axis=1), b, tri


def _router_specs():
    in_specs = [pl.BlockSpec((D_MODEL, 2 * LANES), lambda i: (0, 0)),
                pl.BlockSpec((1, LANES), lambda i: (0, 0)),
                pl.BlockSpec((ROW_TILE, ROW_TILE), lambda i: (0, 0))]
    out_specs = [pl.BlockSpec((SUBLANES, ROW_TILE), lambda i: (0, i)),
                 pl.BlockSpec((MOE_EXPERTS, ROW_TILE), lambda i: (0, 0))]
    scratch = [pltpu.VMEM((MOE_EXPERTS, ROW_TILE), F32)]
    return in_specs, out_specs, scratch


def _router_out_shapes(n):
    return [jax.ShapeDtypeStruct((SUBLANES, n), F32),
            jax.ShapeDtypeStruct((MOE_EXPERTS, ROW_TILE), F32)]


def _expert_counts(cnt):
    return cnt[:, 0].astype(jnp.int32)


DISPATCH_TOKENS = 256
DISPATCH_BUFS = 3
WEIGHT_SLOTS = 3


def _dest_kernel(ps_ref, route_ref, o_ref):
    e = route_ref[MOE_TOPK:2 * MOE_TOPK, :]
    rank = route_ref[2 * MOE_TOPK:3 * MOE_TOPK, :]
    base = jnp.zeros_like(e)
    for x in range(MOE_EXPERTS):
        base = jnp.where(e == x, ps_ref[x].astype(F32), base)
    o_ref[...] = (base + rank).astype(jnp.int32)


def _moe_plan(route, counts, n):
    nk = n * MOE_TOPK
    n_rows = -(-nk // MOE_BLOCK) * MOE_BLOCK + MOE_EXPERTS * MOE_BLOCK
    n_blocks = n_rows // MOE_BLOCK
    padded = (counts + MOE_BLOCK - 1) // MOE_BLOCK * MOE_BLOCK
    pad_end = jnp.cumsum(padded)
    pad_start = pad_end - padded
    dest = pl.pallas_call(
        _dest_kernel,
        grid_spec=pltpu.PrefetchScalarGridSpec(
            num_scalar_prefetch=1, grid=(1,),
            in_specs=[pl.BlockSpec((SUBLANES, n), lambda i, ps: (0, 0))],
            out_specs=pl.BlockSpec((MOE_TOPK, n), lambda i, ps: (0, 0))),
        out_shape=jax.ShapeDtypeStruct((MOE_TOPK, n), jnp.int32),
        compiler_params=_cparams("arbitrary"),
        name="dest",
    )(pad_start.astype(jnp.int32), route).reshape(nk)
    block_start = jnp.arange(n_blocks, dtype=jnp.int32) * MOE_BLOCK
    block_expert = jnp.minimum(jnp.sum(block_start[:, None] >= pad_end[None, :], axis=1),
                               MOE_EXPERTS - 1).astype(jnp.int32)
    nact = (pad_end[-1:] // MOE_BLOCK).astype(jnp.int32)
    fill_start = (pad_start + counts).astype(jnp.int32)
    return dest, block_expert, nact, fill_start, pad_end.astype(jnp.int32), n_rows


def _dispatch_kernel(dest_ref, fs_ref, fe_ref, nact_ref, x_hbm, xs_hbm, xbuf, zbuf, lsem, sem,
                     zsem):
    i = pl.program_id(0)
    nsteps = pl.num_programs(0)
    slot = i % DISPATCH_BUFS
    ct = DISPATCH_TOKENS
    gt = ct // SUBLANES
    n_blocks = xs_hbm.shape[0] // MOE_BLOCK

    def load(step, sl):
        start = pl.multiple_of(step * gt, gt)
        return pltpu.make_async_copy(x_hbm.at[pl.ds(start, gt)], xbuf.at[sl], lsem.at[sl])

    def wait_step(sl):
        for _ in range(MOE_TOPK):
            pltpu.make_async_copy(x_hbm.at[pl.ds(0, gt)], xbuf.at[sl], sem.at[sl]).wait()

    @pl.when(i == 0)
    def _():
        for s in range(DISPATCH_BUFS - 1):
            load(s, s).start()

    load(i, slot).wait()

    n_tok = nsteps * ct

    def body(g, c):
        base = i * ct + g * SUBLANES
        for t in range(SUBLANES):
            for k in range(MOE_TOPK):
                d = dest_ref[k * n_tok + base + t]
                pltpu.make_async_copy(xbuf.at[slot, g, pl.ds(t, 1)], xs_hbm.at[pl.ds(d, 1)],
                                      sem.at[slot]).start(priority=k)
        return c
    lax.fori_loop(0, gt, body, 0)

    @pl.when(i >= 1)
    def _():
        wait_step((i - 1) % DISPATCH_BUFS)

    @pl.when(i + DISPATCH_BUFS - 1 < nsteps)
    def _():
        load(i + DISPATCH_BUFS - 1, (i + DISPATCH_BUFS - 1) % DISPATCH_BUFS).start()

    @pl.when(i == nsteps - 1)
    def _():
        wait_step(slot)
        zbuf[...] = jnp.zeros_like(zbuf)

        def fill_rows(wait):
            def per_expert(e, c):
                def per_row(r, c2):
                    cp = pltpu.make_async_copy(zbuf.at[pl.ds(0, 1)], xs_hbm.at[pl.ds(r, 1)],
                                               zsem.at[0])
                    cp.wait() if wait else cp.start()
                    return c2
                lax.fori_loop(fs_ref[e], fe_ref[e], per_row, 0)
                return c
            lax.fori_loop(0, MOE_EXPERTS, per_expert, 0)

        def fill_blocks(wait):
            def per_block(b, c):
                start = pl.multiple_of(b * MOE_BLOCK, MOE_BLOCK)
                cp = pltpu.make_async_copy(zbuf, xs_hbm.at[pl.ds(start, MOE_BLOCK)], zsem.at[1])
                cp.wait() if wait else cp.start()
                return c
            lax.fori_loop(nact_ref[0], n_blocks, per_block, 0)

        fill_rows(False)
        fill_blocks(False)
        fill_rows(True)
        fill_blocks(True)


def _dispatch(x, dest, fill_start, fill_end, nact, n_rows):
    n, width = x.shape
    grid_spec = pltpu.PrefetchScalarGridSpec(
        num_scalar_prefetch=4,
        grid=(n // DISPATCH_TOKENS,),
        in_specs=[pl.BlockSpec(memory_space=pl.ANY)],
        out_specs=pl.BlockSpec(memory_space=pl.ANY),
        scratch_shapes=[pltpu.VMEM((DISPATCH_BUFS, DISPATCH_TOKENS // SUBLANES, SUBLANES, width),
                                   x.dtype),
                        pltpu.VMEM((MOE_BLOCK, width), x.dtype),
                        pltpu.SemaphoreType.DMA((DISPATCH_BUFS,)),
                        pltpu.SemaphoreType.DMA((DISPATCH_BUFS,)),
                        pltpu.SemaphoreType.DMA((2,))],
    )
    return pl.pallas_call(
        _dispatch_kernel,
        grid_spec=grid_spec,
        out_shape=jax.ShapeDtypeStruct((n_rows, width), x.dtype),
        compiler_params=_cparams("arbitrary"),
        name="dispatch",
    )(dest, fill_start, fill_end, nact, x.reshape(n // SUBLANES, SUBLANES, width))


def _moe_ffn_kernel(e0, be_ref, nact_ref, ord_ref, nxt_ref, nxt2_ref, xs_ref, wup_hbm, wdn_hbm,
                    ys_ref, wup_f, wdn_f, wup_bf, wdn_bf, wsem):
    i = pl.program_id(0)
    active = i < nact_ref[0]
    new_expert = (i == 0) | (be_ref[i] != be_ref[jnp.maximum(i - 1, 0)])

    def fetch(e, sl):
        return (pltpu.make_async_copy(wup_hbm.at[e0 + e], wup_f.at[sl], wsem.at[0, sl]),
                pltpu.make_async_copy(wdn_hbm.at[e0 + e], wdn_f.at[sl], wsem.at[1, sl]))

    @pl.when(active & (i == 0))
    def _():
        for cp in fetch(be_ref[0], 0):
            cp.start()

        @pl.when(nxt_ref[be_ref[0]] >= 0)
        def _():
            for cp in fetch(nxt_ref[be_ref[0]], 1):
                cp.start()

    @pl.when(active & new_expert)
    def _():
        k = ord_ref[i]
        sl = k % WEIGHT_SLOTS
        for cp in fetch(be_ref[i], sl):
            cp.wait()

        @pl.when(nxt2_ref[be_ref[i]] >= 0)
        def _():
            for cp in fetch(nxt2_ref[be_ref[i]], (k + 2) % WEIGHT_SLOTS):
                cp.start()

        wup_bf[...] = wup_f[sl].astype(BF16)
        wdn_bf[...] = wdn_f[sl].astype(BF16)

    @pl.when(active)
    def _():
        x_lo, x_hi = _unpack_rows(xs_ref[...])
        xb = jnp.concatenate([x_lo.astype(BF16), x_hi.astype(BF16)], axis=1)
        hcat = jnp.dot(xb, wup_bf[...], preferred_element_type=F32)
        hg = hcat[:, :MOE_D_EXPERT]
        hu = hcat[:, MOE_D_EXPERT:]
        act = (hg * jax.nn.sigmoid(hg) * hu).astype(BF16)
        ys_ref[...] = _pack_rows(jnp.dot(act, wdn_bf[...], preferred_element_type=F32))

    @pl.when(jnp.logical_not(active))
    def _():
        ys_ref[...] = jnp.zeros_like(ys_ref)


def _moe_ffn(xs, block_expert, nact, counts, w_up, w_down, layer):
    n_rows = xs.shape[0]
    n_blocks = n_rows // MOE_BLOCK
    first = jnp.concatenate([jnp.ones((1,), jnp.int32),
                             (block_expert[1:] != block_expert[:-1]).astype(jnp.int32)])
    ordinal = (jnp.cumsum(first) - 1).astype(jnp.int32)
    ids = jnp.arange(MOE_EXPERTS, dtype=jnp.int32)
    has_rows = counts[None, :] > 0
    later = (ids[None, :] > ids[:, None]) & has_rows
    nxt = jnp.min(jnp.where(later, ids[None, :], MOE_EXPERTS), axis=1)
    later2 = (ids[None, :] > nxt[:, None]) & has_rows
    nxt2 = jnp.min(jnp.where(later2, ids[None, :], MOE_EXPERTS), axis=1)
    next_expert = jnp.where(nxt == MOE_EXPERTS, -1, nxt).astype(jnp.int32)
    next2_expert = jnp.where(nxt2 == MOE_EXPERTS, -1, nxt2).astype(jnp.int32)
    row = pl.BlockSpec((MOE_BLOCK, PACKED), lambda i, *_: (i, 0))
    grid_spec = pltpu.PrefetchScalarGridSpec(
        num_scalar_prefetch=5,
        grid=(n_blocks,),
        in_specs=[row, pl.BlockSpec(memory_space=pl.ANY), pl.BlockSpec(memory_space=pl.ANY)],
        out_specs=row,
        scratch_shapes=[pltpu.VMEM((WEIGHT_SLOTS, D_MODEL, 2 * MOE_D_EXPERT), F32),
                        pltpu.VMEM((WEIGHT_SLOTS, MOE_D_EXPERT, D_MODEL), F32),
                        pltpu.VMEM((D_MODEL, 2 * MOE_D_EXPERT), BF16),
                        pltpu.VMEM((MOE_D_EXPERT, D_MODEL), BF16),
                        pltpu.SemaphoreType.DMA((2, WEIGHT_SLOTS))],
    )
    return pl.pallas_call(
        functools.partial(_moe_ffn_kernel, layer * MOE_EXPERTS),
        grid_spec=grid_spec,
        out_shape=jax.ShapeDtypeStruct((n_rows, PACKED), jnp.int32),
        compiler_params=_cparams("arbitrary"),
        name="moe_ffn",
    )(block_expert, nact, ordinal, next_expert, next2_expert, xs, w_up, w_down)


def _combine_ln_kernel(dest_ref, x_ref, route_ref, ys_hbm, g_ref, b_ref, o_ref, ob_ref, ybuf,
                       sem):
    i = pl.program_id(0)
    nsteps = pl.num_programs(0)
    slot = i % 2
    tm = x_ref.shape[0]

    def start_gather(step, sl):
        def body(g, c):
            base = step * tm + g * SUBLANES
            for t in range(SUBLANES):
                for k in range(MOE_TOPK):
                    d = dest_ref[k * (nsteps * tm) + base + t]
                    pltpu.make_async_copy(ys_hbm.at[pl.ds(d, 1)], ybuf.at[sl, k, g, pl.ds(t, 1)],
                                          sem.at[sl]).start(priority=k)
            return c
        lax.fori_loop(0, tm // SUBLANES, body, 0)

    @pl.when(i == 0)
    def _():
        start_gather(0, 0)

    @pl.when(i + 1 < nsteps)
    def _():
        start_gather(i + 1, 1 - slot)

    for k in range(MOE_TOPK):
        pltpu.make_async_copy(ybuf.at[1 - slot, k], ybuf.at[slot, k], sem.at[slot]).wait()
    route = route_ref[...]
    y0_lo, y0_hi = _unpack_rows(ybuf[slot, 0].reshape(tm, PACKED))
    y1_lo, y1_hi = _unpack_rows(ybuf[slot, 1].reshape(tm, PACKED))
    g0 = route[:, 0:1]
    g1 = route[:, 1:2]
    ffn = jnp.concatenate([g0 * y0_lo + g1 * y1_lo, g0 * y0_hi + g1 * y1_hi], axis=1)
    out = _layer_norm_rows(DN_ALPHA * x_ref[...] + ffn, g_ref[...], b_ref[...])
    o_ref[...] = out
    ob_ref[...] = out.astype(BF16)


def _combine_ln(x, route, ys, dest, g, b):
    n = x.shape[0]
    tm = ROW_TILE
    vec = pl.BlockSpec((1, D_MODEL), lambda i, d: (0, 0))
    row = pl.BlockSpec((tm, D_MODEL), lambda i, d: (i, 0))
    grid_spec = pltpu.PrefetchScalarGridSpec(
        num_scalar_prefetch=1,
        grid=(n // tm,),
        in_specs=[pl.BlockSpec((tm, D_MODEL), lambda i, d: (i, 0)),
                  pl.BlockSpec((tm, MOE_TOPK), lambda i, d: (i, 0)),
                  pl.BlockSpec(memory_space=pl.ANY), vec, vec],
        out_specs=[row, row],
        scratch_shapes=[pltpu.VMEM((2, MOE_TOPK, tm // SUBLANES, SUBLANES, PACKED), jnp.int32),
                        pltpu.SemaphoreType.DMA((2,))],
    )
    return pl.pallas_call(
        _combine_ln_kernel,
        grid_spec=grid_spec,
        out_shape=[jax.ShapeDtypeStruct((n, D_MODEL), F32),
                   jax.ShapeDtypeStruct((n, D_MODEL), BF16)],
        compiler_params=_cparams("arbitrary"),
        name="combine_ln",
    )(dest, x, route[0:MOE_TOPK].T, ys, g.reshape(1, D_MODEL), b.reshape(1, D_MODEL))


def kernel(x, hg_w_in, hg_lb_logits, hg_norm_w, hg_w_out, cv_w_in, cv_w, cv_w_out, ln_g, ln_b,
           moe_w_group, moe_b_group, moe_w_expert, moe_b_expert, moe_w_up, moe_w_down):
    batch, seq, d = x.shape
    n = batch * seq
    xf = x.reshape(n, d)
    w_up_all = moe_w_up.reshape(DEPTH * MOE_EXPERTS, D_MODEL, 2 * MOE_D_EXPERT)
    w_down_all = moe_w_down.reshape(DEPTH * MOE_EXPERTS, MOE_D_EXPERT, D_MODEL)
    xin = xf
    for layer in range(DEPTH):
        j = layer // 2
        router_ops = _router_operands(moe_w_group[layer], moe_b_group[layer],
                                      moe_w_expert[layer], moe_b_expert[layer])
        if layer % 2 == 0:
            w = hg_w_in[j]
            w_a = jnp.concatenate([w[:, :D_MODEL], w[:, 3 * D_MODEL:]], axis=1).astype(BF16)
            w_f = w[:, D_MODEL:3 * D_MODEL].astype(BF16)
            proj_a = _matmul(xin, w_a, BF16)
            proj_f = _matmul(xin, w_f, F32)
            o_f, o_b = _gla(proj_a, proj_f, hg_lb_logits, layer, batch, seq)
            xf, xpk, route, counts = _hgrn_out(
                o_f, o_b, proj_a, hg_norm_w[j], hg_w_out[j].astype(BF16), xf,
                ln_g[layer, 0], ln_b[layer, 0], router_ops)
        else:
            proj = _matmul(xin, cv_w_in[j].astype(BF16), BF16)
            xf, xpk, route, counts = _conv_out(
                proj, cv_w[j], cv_w_out[j].astype(BF16), xf,
                ln_g[layer, 0], ln_b[layer, 0], seq, router_ops)
        dest, block_expert, nact, fill_start, fill_end, n_rows = _moe_plan(route, counts, n)
        xs = _dispatch(xpk, dest, fill_start, fill_end, nact, n_rows)
        ys = _moe_ffn(xs, block_expert, nact, counts, w_up_all, w_down_all, layer)
        xf, xin = _combine_ln(xf, route, ys, dest, ln_g[layer, 1], ln_b[layer, 1])
    return xf.reshape(batch, seq, d)
```

```python
import functools

import jax
import jax.numpy as jnp
from jax import lax
from jax.experimental import pallas as pl
from jax.experimental.pallas import tpu as pltpu

D_MODEL = 1024
DEPTH = 4
HG_DK = 128
HG_HEADS = D_MODEL // HG_DK
HG_STREAMS = 5
CONV_STREAMS = 3
MOE_GROUPS = 4
MOE_EXPERTS_PER_GROUP = 8
MOE_EXPERTS = MOE_GROUPS * MOE_EXPERTS_PER_GROUP
MOE_TOPK = 2
MOE_D_EXPERT = D_MODEL // 2
MOE_BLOCK = 256
DN_ALPHA = (2.0 * DEPTH) ** 0.25
LN_EPS = 1e-5
RMS_EPS = 1e-6

LANES = 128
SUBLANES = 8
BF16_SUBLANES = 16
VMEM_LIMIT = 48 * 1024 * 1024
GLA_CHUNK = 64
GLA_TBLOCK = 2048
GLA_UNROLL = 2
ROW_TILE = 256
NEG_BIG = -1e30
LOG2_E = 1.4426950408889634

BF16 = jnp.bfloat16
F32 = jnp.float32


def _cparams(*sem):
    return pltpu.CompilerParams(dimension_semantics=sem, vmem_limit_bytes=VMEM_LIMIT)


def _mm_kernel(x_ref, w_ref, o_ref):
    o_ref[...] = jnp.dot(x_ref[...].astype(BF16), w_ref[...],
                         preferred_element_type=F32).astype(o_ref.dtype)


def _matmul(x, w, out_dtype, tm=2048, tn=1024):
    n, k = x.shape
    nn = w.shape[1]
    return pl.pallas_call(
        _mm_kernel,
        grid=(nn // tn, n // tm),
        in_specs=[pl.BlockSpec((tm, k), lambda j, i: (i, 0)),
                  pl.BlockSpec((k, tn), lambda j, i: (0, j))],
        out_specs=pl.BlockSpec((tm, tn), lambda j, i: (i, j)),
        out_shape=jax.ShapeDtypeStruct((n, nn), out_dtype),
        compiler_params=_cparams("arbitrary", "arbitrary"),
        name="in_proj",
    )(x, w)


PACKED = D_MODEL // 2


def _pack_rows(x):
    lo = pltpu.bitcast(x[:, :PACKED].astype(BF16).astype(F32), jnp.int32)
    hi = pltpu.bitcast(x[:, PACKED:].astype(BF16).astype(F32), jnp.int32)
    return lax.shift_right_logical(lo, 16) | hi


def _unpack_rows(w):
    lo = pltpu.bitcast(lax.shift_left(w, 16), F32)
    hi = pltpu.bitcast(w & jnp.int32(-65536), F32)
    return lo, hi


def _silu(x):
    return x * (0.5 + 0.5 * jnp.tanh(0.5 * x))


def _layer_norm_rows(y, g, b):
    mu = jnp.mean(y, axis=-1, keepdims=True)
    yc = y - mu
    var = jnp.mean(yc * yc, axis=-1, keepdims=True)
    return yc * lax.rsqrt(var + LN_EPS) * g + b


def _rows(x, blocks, size):
    parts = [x[b * size:(b + 1) * size, :] for b in blocks]
    return parts[0] if len(parts) == 1 else jnp.concatenate(parts, axis=0)


def _gla_chunk(q_s, k_s, v_s, lf_s, g_s, acc_s, o_ref, st_ref, d, start, slot, reverse):
    c = GLA_CHUNK
    t8 = SUBLANES
    nt_dims = (((1,), (1,)), ((), ()))

    def tt(ref, i):
        return ref[d, pl.ds(start + i, t8, stride=t8), :]

    lf = [tt(lf_s, i) for i in range(t8)]
    qt = [tt(q_s, i) for i in range(t8)]
    kt = [tt(k_s, i) for i in range(t8)]
    vt = [tt(v_s, i) for i in range(t8)]

    gi = [None] * t8
    prev = None
    for i in (range(t8 - 1, -1, -1) if reverse else range(t8)):
        gi[i] = lf[i] if prev is None else prev + lf[i]
        prev = gi[i]
    tot = prev
    sub = lax.broadcasted_iota(jnp.int32, (t8, LANES), 0)
    incl = tot
    s = 1
    while s < t8:
        if reverse:
            incl = incl + jnp.where(sub + s < t8, pltpu.roll(incl, t8 - s, 0), 0.0)
        else:
            incl = incl + jnp.where(sub >= s, pltpu.roll(incl, s, 0), 0.0)
        s *= 2
    excl = incl - tot
    for i in range(t8):
        g_s[slot, pl.ds(i, t8, stride=t8), :] = gi[i] + excl

    acc = []
    for i in range(t8):
        a_i = jnp.sum(qt[i] * kt[i], axis=-1, keepdims=True) * vt[i]
        for r in (range(i + 1, t8) if reverse else range(i)):
            w = qt[i] * kt[r] * jnp.exp2(gi[i] - gi[r])
            a_i = a_i + jnp.sum(w, axis=-1, keepdims=True) * vt[r]
        acc.append(a_i)

    sl = pl.ds(start, c)
    q = q_s[d, sl, :]
    kk = k_s[d, sl, :]
    v = v_s[d, sl, :]
    g = g_s[slot]
    vb = v.astype(BF16)
    st = st_ref[d]

    o = lax.dot_general((q * jnp.exp2(g)).astype(BF16), st.astype(BF16), nt_dims,
                        preferred_element_type=F32)

    half = c // 2
    qd_l, kd_l, v_l, q_tiles = [], [], [], []
    b = 2 * t8
    while b <= c:
        h = b // 2
        nblk = c // b
        lo = [2 * m for m in range(nblk)]
        hi = [2 * m + 1 for m in range(nblk)]
        q_half, k_half = (lo, hi) if reverse else (hi, lo)
        refs = [g[m * b + h:m * b + h + 1, :] if reverse else g[m * b + h - 1:m * b + h, :]
                for m in range(nblk)]
        gref = jnp.concatenate([jnp.broadcast_to(r, (h, LANES)) for r in refs], axis=0) \
            if nblk > 1 else jnp.broadcast_to(refs[0], (h, LANES))
        qd_l.append(_rows(q, q_half, h) * jnp.exp2(_rows(g, q_half, h) - gref))
        kd_l.append(_rows(kk, k_half, h) * jnp.exp2(gref - _rows(g, k_half, h)))
        v_l.append(_rows(v, k_half, h))
        q_tiles.append([hb * (h // t8) + t for hb in q_half for t in range(h // t8)])
        b *= 2
    nlev = len(qd_l)
    p = lax.dot_general(jnp.concatenate(qd_l, axis=0).astype(BF16),
                        jnp.concatenate(kd_l, axis=0).astype(BF16), nt_dims,
                        preferred_element_type=F32)
    ii = lax.broadcasted_iota(jnp.int32, p.shape, 0)
    jj = lax.broadcasted_iota(jnp.int32, p.shape, 1)
    keep = None
    for lv in range(nlev):
        h = t8 << lv
        m = (ii // half == lv) & (jj // half == lv) & ((ii // h) == (jj // h))
        keep = m if keep is None else keep | m
    p = jnp.where(keep, p, 0.0)
    res = jnp.dot(p.astype(BF16), jnp.concatenate(v_l, axis=0).astype(BF16),
                  preferred_element_type=F32)
    contrib = [None] * (c // t8)
    for lv in range(nlev):
        for n_, tile in enumerate(q_tiles[lv]):
            piece = res[lv * half + n_ * t8:lv * half + (n_ + 1) * t8, :]
            contrib[tile] = piece if contrib[tile] is None else contrib[tile] + piece
    zero = jnp.zeros((t8, LANES), F32)
    o = o + jnp.concatenate([zero if p_ is None else p_ for p_ in contrib], axis=0)

    gl = g[0:1, :] if reverse else g[c - 1:c, :]
    kd = (kk * jnp.exp2(gl - g)).astype(BF16)
    upd = lax.dot_general(vb, kd, (((0,), (0,)), ((), ())), preferred_element_type=F32)
    st_ref[d] = st * jnp.exp2(gl) + upd

    for i in range(t8):
        acc_s[slot, pl.ds(i, t8, stride=t8), :] = acc[i]
    o_ref[sl, :] = (o + acc_s[slot]).astype(o_ref.dtype)


def _gla_kernel(layer, lbl_ref, qf_ref, ff_ref, vf_ref, qb_ref, fb_ref, vb_ref,
                of_ref, ob_ref, st_ref, q_s, k_s, v_s, lf_s, g_s, acc_s):
    tb = GLA_TBLOCK
    nc = tb // GLA_CHUNK

    @pl.when(pl.program_id(2) == 0)
    def _():
        st_ref[...] = jnp.zeros_like(st_ref)

    if layer > 0:
        lg = lbl_ref[...]
        e = jnp.exp(lg - jnp.max(lg, axis=0, keepdims=True))
        sm = e / jnp.sum(e, axis=0, keepdims=True)
        lb = sm[1]
        for l in range(2, layer + 1):
            lb = lb + sm[l]

    for d, (q_ref, f_ref, v_ref) in enumerate(((qf_ref, ff_ref, vf_ref),
                                               (qb_ref, fb_ref, vb_ref))):
        qx = q_ref[...].astype(F32)
        fx = f_ref[...]
        v_s[d] = v_ref[...].astype(F32)
        t = jnp.exp(-jnp.abs(fx))
        r = 1.0 / (1.0 + t)
        logsig = jnp.minimum(fx, 0.0) - jnp.log(1.0 + t)
        sig_neg = jnp.where(fx >= 0.0, t * r, r)
        if layer > 0:
            lbd = lb[d:d + 1, :]
            a = jnp.log(lbd)
            bb = jnp.log1p(-lbd) + logsig
            lf = jnp.maximum(a, bb) + jnp.log(1.0 + jnp.exp(-jnp.abs(a - bb)))
            kk = (1.0 - lbd) * sig_neg
        else:
            lf = logsig
            kk = sig_neg
        q_s[d] = _silu(qx)
        k_s[d] = kk
        lf_s[d] = lf * LOG2_E

    def body(ci, carry):
        for u in range(GLA_UNROLL):
            cf = ci * GLA_UNROLL + u
            sf = pl.multiple_of(cf * GLA_CHUNK, GLA_CHUNK)
            sb = pl.multiple_of((nc - 1 - cf) * GLA_CHUNK, GLA_CHUNK)
            _gla_chunk(q_s, k_s, v_s, lf_s, g_s, acc_s, of_ref, st_ref, 0, sf, 2 * u, False)
            _gla_chunk(q_s, k_s, v_s, lf_s, g_s, acc_s, ob_ref, st_ref, 1, sb, 2 * u + 1, True)
        return carry

    lax.fori_loop(0, nc // GLA_UNROLL, body, 0)


def _gla(proj_a, proj_f, lb_logits, layer, batch, seq):
    n = proj_a.shape[0]
    tb = GLA_TBLOCK
    nb = seq // tb
    h8 = HG_HEADS

    def spec(stream, rev):
        if rev:
            return pl.BlockSpec((tb, LANES), lambda b, h, c: (b * nb + nb - 1 - c, stream * h8 + h))
        return pl.BlockSpec((tb, LANES), lambda b, h, c: (b * nb + c, stream * h8 + h))

    o_f_spec = pl.BlockSpec((tb, LANES), lambda b, h, c: (b * nb + c, h))
    o_b_spec = pl.BlockSpec((tb, LANES), lambda b, h, c: (b * nb + nb - 1 - c, h))
    return pl.pallas_call(
        functools.partial(_gla_kernel, layer),
        grid=(batch, h8, nb),
        in_specs=[pl.BlockSpec((DEPTH, 2, LANES), lambda b, h, c: (0, 0, h)),
                  spec(0, False), spec(0, False), spec(1, False),
                  spec(0, True), spec(1, True), spec(1, True)],
        out_specs=[o_f_spec, o_b_spec],
        out_shape=[jax.ShapeDtypeStruct((n, D_MODEL), BF16)] * 2,
        scratch_shapes=[pltpu.VMEM((2, HG_DK, HG_DK), F32)]
        + [pltpu.VMEM((2, tb, LANES), F32)] * 4
        + [pltpu.VMEM((2 * GLA_UNROLL, GLA_CHUNK, LANES), F32)] * 2,
        compiler_params=_cparams("arbitrary", "arbitrary", "arbitrary"),
        name="gla",
    )(lb_logits, proj_a, proj_f, proj_a, proj_a, proj_f, proj_a)


def _hgrn_out_kernel(of_ref, ob_ref, gate_ref, nw_ref, w_ref, x_ref, g_ref, b_ref,
                     rw_ref, rb_ref, tri_ref,
                     o_ref, op_ref, route_ref, cnt_ref, carry_ref):
    o = of_ref[...].astype(F32) + ob_ref[...].astype(F32)
    parts = []
    for h in range(HG_HEADS):
        oh = o[:, h * LANES:(h + 1) * LANES]
        ms = jnp.mean(oh * oh, axis=-1, keepdims=True)
        parts.append(oh * lax.rsqrt(ms + RMS_EPS))
    gate = gate_ref[...].astype(F32)
    y = jnp.concatenate(parts, axis=-1) * nw_ref[...] * _silu(gate)
    mix = jnp.dot(y.astype(BF16), w_ref[...], preferred_element_type=F32)
    out = _layer_norm_rows(DN_ALPHA * x_ref[...] + mix, g_ref[...], b_ref[...])
    o_ref[...] = out
    op_ref[...] = _pack_rows(out)
    _route_tile(out, rw_ref, rb_ref, tri_ref, route_ref, cnt_ref, carry_ref)


def _hgrn_out(o_f, o_b, proj_a, norm_w, w_out, x, g, b, router_ops):
    n = x.shape[0]
    tm = ROW_TILE
    row = pl.BlockSpec((tm, D_MODEL), lambda i: (i, 0))
    vec = pl.BlockSpec((1, D_MODEL), lambda i: (0, 0))
    r_in, r_out, r_scratch = _router_specs()
    xo, xpk, route, cnt = pl.pallas_call(
        _hgrn_out_kernel,
        grid=(n // tm,),
        in_specs=[row, row, pl.BlockSpec((tm, D_MODEL), lambda i: (i, 2)), vec,
                  pl.BlockSpec((D_MODEL, D_MODEL), lambda i: (0, 0)), row, vec, vec] + r_in,
        out_specs=[row, pl.BlockSpec((tm, PACKED), lambda i: (i, 0))] + r_out,
        out_shape=[jax.ShapeDtypeStruct((n, D_MODEL), F32),
                   jax.ShapeDtypeStruct((n, PACKED), jnp.int32)] + _router_out_shapes(n),
        scratch_shapes=r_scratch,
        compiler_params=_cparams("arbitrary"),
        name="hgrn_out",
    )(o_f, o_b, proj_a, jnp.tile(norm_w, HG_HEADS).reshape(1, D_MODEL), w_out, x,
      g.reshape(1, D_MODEL), b.reshape(1, D_MODEL), *router_ops)
    return xo, xpk, route, _expert_counts(cnt)


def _conv_out_kernel(tiles_per_seq, bg_ref, cg_ref, h_ref, cgp_ref, hp_ref, cgn_ref, hn_ref,
                     cw_ref, w_ref, x_ref, g_ref, b_ref, rw_ref, rb_ref, tri_ref,
                     o_ref, op_ref, route_ref, cnt_ref, carry_ref):
    i = pl.program_id(0)
    tm = cg_ref.shape[0]
    u = cg_ref[...].astype(F32) * h_ref[...].astype(F32)
    first = (i % tiles_per_seq) == 0
    last = (i % tiles_per_seq) == tiles_per_seq - 1
    hr = cgp_ref.shape[0]
    u_halo_prev = cgp_ref[...].astype(F32) * hp_ref[...].astype(F32)
    u_halo_next = cgn_ref[...].astype(F32) * hn_ref[...].astype(F32)
    u_prev_row = jnp.where(first, 0.0, u_halo_prev[hr - 1:hr, :])
    u_next_row = jnp.where(last, 0.0, u_halo_next[0:1, :])
    rows = lax.broadcasted_iota(jnp.int32, u.shape, 0)
    u_prev = jnp.where(rows == 0, u_prev_row, pltpu.roll(u, 1, 0))
    u_next = jnp.where(rows == tm - 1, u_next_row, pltpu.roll(u, tm - 1, 0))
    cw = cw_ref[...]
    y = u_prev * cw[0:1, :] + u * cw[1:2, :] + u_next * cw[2:3, :]
    mix = jnp.dot((bg_ref[...].astype(F32) * y).astype(BF16), w_ref[...],
                  preferred_element_type=F32)
    out = _layer_norm_rows(DN_ALPHA * x_ref[...] + mix, g_ref[...], b_ref[...])
    o_ref[...] = out
    op_ref[...] = _pack_rows(out)
    _route_tile(out, rw_ref, rb_ref, tri_ref, route_ref, cnt_ref, carry_ref)


def _conv_out(proj, conv_w, w_out, x, g, b, seq, router_ops):
    n = x.shape[0]
    tm = ROW_TILE
    halo = BF16_SUBLANES
    rh = tm // halo
    nblk = n // halo
    row = pl.BlockSpec((tm, D_MODEL), lambda i: (i, 0))
    vec = pl.BlockSpec((1, D_MODEL), lambda i: (0, 0))

    def main(stream):
        return pl.BlockSpec((tm, D_MODEL), lambda i: (i, stream))

    def prev(stream):
        return pl.BlockSpec((halo, D_MODEL), lambda i: (jnp.maximum(i * rh - 1, 0), stream))

    def nxt(stream):
        return pl.BlockSpec((halo, D_MODEL),
                            lambda i: (jnp.minimum((i + 1) * rh, nblk - 1), stream))

    r_in, r_out, r_scratch = _router_specs()
    xo, xpk, route, cnt = pl.pallas_call(
        functools.partial(_conv_out_kernel, seq // tm),
        grid=(n // tm,),
        in_specs=[main(0), main(1), main(2), prev(1), prev(2), nxt(1), nxt(2),
                  pl.BlockSpec((3, D_MODEL), lambda i: (0, 0)),
                  pl.BlockSpec((D_MODEL, D_MODEL), lambda i: (0, 0)), row, vec, vec] + r_in,
        out_specs=[row, pl.BlockSpec((tm, PACKED), lambda i: (i, 0))] + r_out,
        out_shape=[jax.ShapeDtypeStruct((n, D_MODEL), F32),
                   jax.ShapeDtypeStruct((n, PACKED), jnp.int32)] + _router_out_shapes(n),
        scratch_shapes=r_scratch,
        compiler_params=_cparams("arbitrary"),
        name="conv_out",
    )(proj, proj, proj, proj, proj, proj, proj, conv_w, w_out, x,
      g.reshape(1, D_MODEL), b.reshape(1, D_MODEL), *router_ops)
    return xo, xpk, route, _expert_counts(cnt)


def _route_tile(x, whl_ref, b_ref, tri_ref, o_ref, cnt_ref, carry_ref):
    @pl.when(pl.program_id(0) == 0)
    def _():
        carry_ref[...] = jnp.zeros_like(carry_ref)

    xh = x.astype(BF16)
    xl = (x - xh.astype(F32)).astype(BF16)
    both = jnp.dot(xh, whl_ref[...], preferred_element_type=F32)
    logits = (both[:, :LANES] + both[:, LANES:]
              + jnp.dot(xl, whl_ref[:, :LANES], preferred_element_type=F32)) + b_ref[...]
    eg = MOE_EXPERTS_PER_GROUP
    lt = logits.T
    grp = lt[0:SUBLANES]
    exp_t = [lt[SUBLANES + eg * g:SUBLANES + eg * (g + 1)] for g in range(MOE_GROUPS)]
    row = lax.broadcasted_iota(jnp.int32, grp.shape, 0).astype(F32)
    gmax = jnp.max(grp, axis=0, keepdims=True)
    gsum = jnp.sum(jnp.exp(grp - gmax), axis=0, keepdims=True)
    p_group = 1.0 / gsum
    g_sel = jnp.min(jnp.where(grp == gmax, row, 99.0), axis=0, keepdims=True)
    el = exp_t[MOE_GROUPS - 1]
    for g in range(MOE_GROUPS - 2, -1, -1):
        el = jnp.where(g_sel == g, exp_t[g], el)
    t1 = jnp.max(el, axis=0, keepdims=True)
    i1 = jnp.min(jnp.where(el == t1, row, 99.0), axis=0, keepdims=True)
    el2 = jnp.where(row == i1, NEG_BIG, el)
    t2 = jnp.max(el2, axis=0, keepdims=True)
    i2 = jnp.min(jnp.where(el2 == t2, row, 99.0), axis=0, keepdims=True)
    z = jnp.exp(t2 - t1)
    g1 = p_group / (1.0 + z)
    g2 = g1 * z

    is1 = row == i1
    is2 = row == i2
    in_g = [g_sel == g for g in range(MOE_GROUPS)]
    onehot = jnp.concatenate([jnp.where(in_g[g] & (is1 | is2), 1.0, 0.0)
                              for g in range(MOE_GROUPS)], axis=0)
    prefix = jnp.dot(onehot.astype(BF16), tri_ref[...], preferred_element_type=F32)
    before = prefix + carry_ref[...]
    r1 = jnp.zeros_like(g1)
    r2 = jnp.zeros_like(g1)
    for g in range(MOE_GROUPS):
        bg = before[eg * g:eg * (g + 1)]
        r1 = r1 + jnp.sum(jnp.where(in_g[g] & is1, bg, 0.0), axis=0, keepdims=True)
        r2 = r2 + jnp.sum(jnp.where(in_g[g] & is2, bg, 0.0), axis=0, keepdims=True)
    e1 = g_sel * eg + i1
    e2 = g_sel * eg + i2
    out = jnp.zeros_like(row)
    for r, val in enumerate((g1, g2, e1, e2, r1, r2)):
        out = jnp.where(row == r, val, out)
    o_ref[...] = out
    carry_ref[...] = carry_ref[...] + jnp.sum(onehot, axis=1, keepdims=True)
    cnt_ref[...] = carry_ref[...]


def _router_operands(w_group, b_group, w_expert, b_expert):
    gpad = SUBLANES - MOE_GROUPS
    pad = LANES - SUBLANES - MOE_EXPERTS
    w = jnp.concatenate([w_group, jnp.zeros((D_MODEL, gpad), F32), w_expert,
                         jnp.zeros((D_MODEL, pad), F32)], axis=1)
    wh = w.astype(BF16)
    wl = (w - wh.astype(F32)).astype(BF16)
    b = jnp.concatenate([b_group, jnp.full((gpad,), NEG_BIG, F32), b_expert,
                         jnp.zeros((pad,), F32)]).reshape(1, LANES)
    r = jnp.arange(ROW_TILE, dtype=jnp.int32)
    tri = (r[:, None] < r[None, :]).astype(BF16)
    return jnp.concatenate([wh, wl], axis=1), b, tri


def _router_specs():
    in_specs = [pl.BlockSpec((D_MODEL, 2 * LANES), lambda i: (0, 0)),
                pl.BlockSpec((1, LANES), lambda i: (0, 0)),
                pl.BlockSpec((ROW_TILE, ROW_TILE), lambda i: (0, 0))]
    out_specs = [pl.BlockSpec((SUBLANES, ROW_TILE), lambda i: (0, i)),
                 pl.BlockSpec((MOE_EXPERTS, ROW_TILE), lambda i: (0, 0))]
    scratch = [pltpu.VMEM((MOE_EXPERTS, ROW_TILE), F32)]
    return in_specs, out_specs, scratch


def _router_out_shapes(n):
    return [jax.ShapeDtypeStruct((SUBLANES, n), F32),
            jax.ShapeDtypeStruct((MOE_EXPERTS, ROW_TILE), F32)]


def _expert_counts(cnt):
    return cnt[:, 0].astype(jnp.int32)


DISPATCH_TOKENS = 256
DISPATCH_BUFS = 3
WEIGHT_SLOTS = 3


def _dest_kernel(ps_ref, route_ref, o_ref):
    e = route_ref[MOE_TOPK:2 * MOE_TOPK, :]
    rank = route_ref[2 * MOE_TOPK:3 * MOE_TOPK, :]
    base = jnp.zeros_like(e)
    for x in range(MOE_EXPERTS):
        base = jnp.where(e == x, ps_ref[x].astype(F32), base)
    o_ref[...] = (base + rank).astype(jnp.int32)


def _moe_plan(route, counts, n):
    nk = n * MOE_TOPK
    n_rows = -(-nk // MOE_BLOCK) * MOE_BLOCK + MOE_EXPERTS * MOE_BLOCK
    n_blocks = n_rows // MOE_BLOCK
    padded = (counts + MOE_BLOCK - 1) // MOE_BLOCK * MOE_BLOCK
    pad_end = jnp.cumsum(padded)
    pad_start = pad_end - padded
    dest = pl.pallas_call(
        _dest_kernel,
        grid_spec=pltpu.PrefetchScalarGridSpec(
            num_scalar_prefetch=1, grid=(1,),
            in_specs=[pl.BlockSpec((SUBLANES, n), lambda i, ps: (0, 0))],
            out_specs=pl.BlockSpec((MOE_TOPK, n), lambda i, ps: (0, 0))),
        out_shape=jax.ShapeDtypeStruct((MOE_TOPK, n), jnp.int32),
        compiler_params=_cparams("arbitrary"),
        name="dest",
    )(pad_start.astype(jnp.int32), route).reshape(nk)
    block_start = jnp.arange(n_blocks, dtype=jnp.int32) * MOE_BLOCK
    block_expert = jnp.minimum(jnp.sum(block_start[:, None] >= pad_end[None, :], axis=1),
                               MOE_EXPERTS - 1).astype(jnp.int32)
    nact = (pad_end[-1:] // MOE_BLOCK).astype(jnp.int32)
    fill_start = (pad_start + counts).astype(jnp.int32)
    return dest, block_expert, nact, fill_start, pad_end.astype(jnp.int32), n_rows


def _dispatch_kernel(dest_ref, fs_ref, fe_ref, nact_ref, x_hbm, xs_hbm, xbuf, zbuf, lsem, sem,
                     zsem):
    i = pl.program_id(0)
    nsteps = pl.num_programs(0)
    slot = i % DISPATCH_BUFS
    ct = DISPATCH_TOKENS
    gt = ct // SUBLANES
    n_blocks = xs_hbm.shape[0] // MOE_BLOCK

    def load(step, sl):
        start = pl.multiple_of(step * gt, gt)
        return pltpu.make_async_copy(x_hbm.at[pl.ds(start, gt)], xbuf.at[sl], lsem.at[sl])

    def wait_step(sl):
        for _ in range(MOE_TOPK):
            pltpu.make_async_copy(x_hbm.at[pl.ds(0, gt)], xbuf.at[sl], sem.at[sl]).wait()

    @pl.when(i == 0)
    def _():
        for s in range(DISPATCH_BUFS - 1):
            load(s, s).start()

        zbuf[...] = jnp.zeros_like(zbuf)

        def block_copy(b):
            start = pl.multiple_of(b * MOE_BLOCK, MOE_BLOCK)
            return pltpu.make_async_copy(zbuf, xs_hbm.at[pl.ds(start, MOE_BLOCK)], zsem.at[0])

        def fill(wait):
            def per_expert(e, c):
                @pl.when(fs_ref[e] < fe_ref[e])
                def _():
                    cp = block_copy(fe_ref[e] // MOE_BLOCK - 1)
                    cp.wait() if wait else cp.start()
                return c
            lax.fori_loop(0, MOE_EXPERTS, per_expert, 0)

            def per_block(b, c):
                cp = block_copy(b)
                cp.wait() if wait else cp.start()
                return c
            lax.fori_loop(nact_ref[0], n_blocks, per_block, 0)

        fill(False)
        fill(True)

    load(i, slot).wait()

    n_tok = nsteps * ct

    def body(g, c):
        base = i * ct + g * SUBLANES
        for t in range(SUBLANES):
            for k in range(MOE_TOPK):
                d = dest_ref[k * n_tok + base + t]
                pltpu.make_async_copy(xbuf.at[slot, g, pl.ds(t, 1)], xs_hbm.at[pl.ds(d, 1)],
                                      sem.at[slot]).start(priority=k)
        return c
    lax.fori_loop(0, gt, body, 0)

    @pl.when(i >= 1)
    def _():
        wait_step((i - 1) % DISPATCH_BUFS)

    @pl.when(i + DISPATCH_BUFS - 1 < nsteps)
    def _():
        load(i + DISPATCH_BUFS - 1, (i + DISPATCH_BUFS - 1) % DISPATCH_BUFS).start()

    @pl.when(i == nsteps - 1)
    def _():
        wait_step(slot)


def _dispatch(x, dest, fill_start, fill_end, nact, n_rows):
    n, width = x.shape
    grid_spec = pltpu.PrefetchScalarGridSpec(
        num_scalar_prefetch=4,
        grid=(n // DISPATCH_TOKENS,),
        in_specs=[pl.BlockSpec(memory_space=pl.ANY)],
        out_specs=pl.BlockSpec(memory_space=pl.ANY),
        scratch_shapes=[pltpu.VMEM((DISPATCH_BUFS, DISPATCH_TOKENS // SUBLANES, SUBLANES, width),
                                   x.dtype),
                        pltpu.VMEM((MOE_BLOCK, width), x.dtype),
                        pltpu.SemaphoreType.DMA((DISPATCH_BUFS,)),
                        pltpu.SemaphoreType.DMA((DISPATCH_BUFS,)),
                        pltpu.SemaphoreType.DMA((2,))],
    )
    return pl.pallas_call(
        _dispatch_kernel,
        grid_spec=grid_spec,
        out_shape=jax.ShapeDtypeStruct((n_rows, width), x.dtype),
        compiler_params=_cparams("arbitrary"),
        name="dispatch",
    )(dest, fill_start, fill_end, nact, x.reshape(n // SUBLANES, SUBLANES, width))


def _moe_ffn_kernel(e0, be_ref, nact_ref, ord_ref, nxt_ref, nxt2_ref, xs_ref, wup_hbm, wdn_hbm,
                    ys_ref, wup_f, wdn_f, wup_bf, wdn_bf, wsem):
    i = pl.program_id(0)
    active = i < nact_ref[0]
    new_expert = (i == 0) | (be_ref[i] != be_ref[jnp.maximum(i - 1, 0)])

    def fetch(e, sl):
        return (pltpu.make_async_copy(wup_hbm.at[e0 + e], wup_f.at[sl], wsem.at[0, sl]),
                pltpu.make_async_copy(wdn_hbm.at[e0 + e], wdn_f.at[sl], wsem.at[1, sl]))

    @pl.when(active & (i == 0))
    def _():
        for cp in fetch(be_ref[0], 0):
            cp.start()

        @pl.when(nxt_ref[be_ref[0]] >= 0)
        def _():
            for cp in fetch(nxt_ref[be_ref[0]], 1):
                cp.start()

    @pl.when(active & new_expert)
    def _():
        k = ord_ref[i]
        sl = k % WEIGHT_SLOTS
        for cp in fetch(be_ref[i], sl):
            cp.wait()

        @pl.when(nxt2_ref[be_ref[i]] >= 0)
        def _():
            for cp in fetch(nxt2_ref[be_ref[i]], (k + 2) % WEIGHT_SLOTS):
                cp.start()

        wup_bf[...] = wup_f[sl].astype(BF16)
        wdn_bf[...] = wdn_f[sl].astype(BF16)

    @pl.when(active)
    def _():
        x_lo, x_hi = _unpack_rows(xs_ref[...])
        xb = jnp.concatenate([x_lo.astype(BF16), x_hi.astype(BF16)], axis=1)
        hcat = jnp.dot(xb, wup_bf[...], preferred_element_type=F32)
        hg = hcat[:, :MOE_D_EXPERT]
        hu = hcat[:, MOE_D_EXPERT:]
        act = (_silu(hg) * hu).astype(BF16)
        ys_ref[...] = _pack_rows(jnp.dot(act, wdn_bf[...], preferred_element_type=F32))

    @pl.when(jnp.logical_not(active))
    def _():
        ys_ref[...] = jnp.zeros_like(ys_ref)


def _moe_ffn(xs, block_expert, nact, counts, w_up, w_down, layer):
    n_rows = xs.shape[0]
    n_blocks = n_rows // MOE_BLOCK
    first = jnp.concatenate([jnp.ones((1,), jnp.int32),
                             (block_expert[1:] != block_expert[:-1]).astype(jnp.int32)])
    ordinal = (jnp.cumsum(first) - 1).astype(jnp.int32)
    ids = jnp.arange(MOE_EXPERTS, dtype=jnp.int32)
    has_rows = counts[None, :] > 0
    later = (ids[None, :] > ids[:, None]) & has_rows
    nxt = jnp.min(jnp.where(later, ids[None, :], MOE_EXPERTS), axis=1)
    later2 = (ids[None, :] > nxt[:, None]) & has_rows
    nxt2 = jnp.min(jnp.where(later2, ids[None, :], MOE_EXPERTS), axis=1)
    next_expert = jnp.where(nxt == MOE_EXPERTS, -1, nxt).astype(jnp.int32)
    next2_expert = jnp.where(nxt2 == MOE_EXPERTS, -1, nxt2).astype(jnp.int32)
    row = pl.BlockSpec((MOE_BLOCK, PACKED), lambda i, *_: (i, 0))
    grid_spec = pltpu.PrefetchScalarGridSpec(
        num_scalar_prefetch=5,
        grid=(n_blocks,),
        in_specs=[row, pl.BlockSpec(memory_space=pl.ANY), pl.BlockSpec(memory_space=pl.ANY)],
        out_specs=row,
        scratch_shapes=[pltpu.VMEM((WEIGHT_SLOTS, D_MODEL, 2 * MOE_D_EXPERT), F32),
                        pltpu.VMEM((WEIGHT_SLOTS, MOE_D_EXPERT, D_MODEL), F32),
                        pltpu.VMEM((D_MODEL, 2 * MOE_D_EXPERT), BF16),
                        pltpu.VMEM((MOE_D_EXPERT, D_MODEL), BF16),
                        pltpu.SemaphoreType.DMA((2, WEIGHT_SLOTS))],
    )
    return pl.pallas_call(
        functools.partial(_moe_ffn_kernel, layer * MOE_EXPERTS),
        grid_spec=grid_spec,
        out_shape=jax.ShapeDtypeStruct((n_rows, PACKED), jnp.int32),
        compiler_params=_cparams("arbitrary"),
        name="moe_ffn",
    )(block_expert, nact, ordinal, next_expert, next2_expert, xs, w_up, w_down)


def _combine_ln_kernel(dest_ref, x_ref, route_ref, ys_hbm, g_ref, b_ref, o_ref, ob_ref, ybuf,
                       sem):
    i = pl.program_id(0)
    nsteps = pl.num_programs(0)
    slot = i % 2
    tm = x_ref.shape[0]

    def start_gather(step, sl):
        def body(g, c):
            base = step * tm + g * SUBLANES
            for t in range(SUBLANES):
                for k in range(MOE_TOPK):
                    d = dest_ref[k * (nsteps * tm) + base + t]
                    pltpu.make_async_copy(ys_hbm.at[pl.ds(d, 1)], ybuf.at[sl, k, g, pl.ds(t, 1)],
                                          sem.at[sl]).start(priority=k)
            return c
        lax.fori_loop(0, tm // SUBLANES, body, 0)

    @pl.when(i == 0)
    def _():
        start_gather(0, 0)

    @pl.when(i + 1 < nsteps)
    def _():
        start_gather(i + 1, 1 - slot)

    for k in range(MOE_TOPK):
        pltpu.make_async_copy(ybuf.at[1 - slot, k], ybuf.at[slot, k], sem.at[slot]).wait()
    route = route_ref[...]
    y0_lo, y0_hi = _unpack_rows(ybuf[slot, 0].reshape(tm, PACKED))
    y1_lo, y1_hi = _unpack_rows(ybuf[slot, 1].reshape(tm, PACKED))
    g0 = route[:, 0:1]
    g1 = route[:, 1:2]
    ffn = jnp.concatenate([g0 * y0_lo + g1 * y1_lo, g0 * y0_hi + g1 * y1_hi], axis=1)
    out = _layer_norm_rows(DN_ALPHA * x_ref[...] + ffn, g_ref[...], b_ref[...])
    o_ref[...] = out
    ob_ref[...] = out.astype(BF16)


def _combine_ln(x, route, ys, dest, g, b):
    n = x.shape[0]
    tm = ROW_TILE
    vec = pl.BlockSpec((1, D_MODEL), lambda i, d: (0, 0))
    row = pl.BlockSpec((tm, D_MODEL), lambda i, d: (i, 0))
    grid_spec = pltpu.PrefetchScalarGridSpec(
        num_scalar_prefetch=1,
        grid=(n // tm,),
        in_specs=[pl.BlockSpec((tm, D_MODEL), lambda i, d: (i, 0)),
                  pl.BlockSpec((tm, MOE_TOPK), lambda i, d: (i, 0)),
                  pl.BlockSpec(memory_space=pl.ANY), vec, vec],
        out_specs=[row, row],
        scratch_shapes=[pltpu.VMEM((2, MOE_TOPK, tm // SUBLANES, SUBLANES, PACKED), jnp.int32),
                        pltpu.SemaphoreType.DMA((2,))],
    )
    return pl.pallas_call(
        _combine_ln_kernel,
        grid_spec=grid_spec,
        out_shape=[jax.ShapeDtypeStruct((n, D_MODEL), F32),
                   jax.ShapeDtypeStruct((n, D_MODEL), BF16)],
        compiler_params=_cparams("arbitrary"),
        name="combine_ln",
    )(dest, x, route[0:MOE_TOPK].T, ys, g.reshape(1, D_MODEL), b.reshape(1, D_MODEL))


def kernel(x, hg_w_in, hg_lb_logits, hg_norm_w, hg_w_out, cv_w_in, cv_w, cv_w_out, ln_g, ln_b,
           moe_w_group, moe_b_group, moe_w_expert, moe_b_expert, moe_w_up, moe_w_down):
    batch, seq, d = x.shape
    n = batch * seq
    xf = x.reshape(n, d)
    w_up_all = moe_w_up.reshape(DEPTH * MOE_EXPERTS, D_MODEL, 2 * MOE_D_EXPERT)
    w_down_all = moe_w_down.reshape(DEPTH * MOE_EXPERTS, MOE_D_EXPERT, D_MODEL)
    xin = xf
    for layer in range(DEPTH):
        j = layer // 2
        router_ops = _router_operands(moe_w_group[layer], moe_b_group[layer],
                                      moe_w_expert[layer], moe_b_expert[layer])
        if layer % 2 == 0:
            w = hg_w_in[j]
            w_a = jnp.concatenate([w[:, :D_MODEL], w[:, 3 * D_MODEL:]], axis=1).astype(BF16)
            w_f = w[:, D_MODEL:3 * D_MODEL].astype(BF16)
            proj_a = _matmul(xin, w_a, BF16)
            proj_f = _matmul(xin, w_f, F32)
            o_f, o_b = _gla(proj_a, proj_f, hg_lb_logits, layer, batch, seq)
            xf, xpk, route, counts = _hgrn_out(
                o_f, o_b, proj_a, hg_norm_w[j], hg_w_out[j].astype(BF16), xf,
                ln_g[layer, 0], ln_b[layer, 0], router_ops)
        else:
            proj = _matmul(xin, cv_w_in[j].astype(BF16), BF16)
            xf, xpk, route, counts = _conv_out(
                proj, cv_w[j], cv_w_out[j].astype(BF16), xf,
                ln_g[layer, 0], ln_b[layer, 0], seq, router_ops)
        dest, block_expert, nact, fill_start, fill_end, n_rows = _moe_plan(route, counts, n)
        xs = _dispatch(xpk, dest, fill_start, fill_end, nact, n_rows)
        ys = _moe_ffn(xs, block_expert, nact, counts, w_up_all, w_down_all, layer)
        xf, xin = _combine_ln(xf, route, ys, dest, ln_g[layer, 1], ln_b[layer, 1])
    return xf.reshape(batch, seq, d)
```

```python
import functools

import jax
import jax.numpy as jnp
from jax import lax
from jax.experimental import pallas as pl
from jax.experimental.pallas import tpu as pltpu

D_MODEL = 1024
DEPTH = 4
HG_DK = 128
HG_HEADS = D_MODEL // HG_DK
HG_STREAMS = 5
CONV_STREAMS = 3
MOE_GROUPS = 4
MOE_EXPERTS_PER_GROUP = 8
MOE_EXPERTS = MOE_GROUPS * MOE_EXPERTS_PER_GROUP
MOE_TOPK = 2
MOE_D_EXPERT = D_MODEL // 2
MOE_BLOCK = 256
DN_ALPHA = (2.0 * DEPTH) ** 0.25
LN_EPS = 1e-5
RMS_EPS = 1e-6

LANES = 128
SUBLANES = 8
BF16_SUBLANES = 16
VMEM_LIMIT = 48 * 1024 * 1024
GLA_CHUNK = 64
GLA_TBLOCK = 2048
GLA_UNROLL = 2
ROW_TILE = 512
NEG_BIG = -1e30
LOG2_E = 1.4426950408889634

BF16 = jnp.bfloat16
F32 = jnp.float32


def _cparams(*sem):
    return pltpu.CompilerParams(dimension_semantics=sem, vmem_limit_bytes=VMEM_LIMIT)


def _mm_kernel(x_ref, w_ref, o_ref):
    o_ref[...] = jnp.dot(x_ref[...].astype(BF16), w_ref[...],
                         preferred_element_type=F32).astype(o_ref.dtype)


def _matmul(x, w, out_dtype, tm=2048, tn=1024):
    n, k = x.shape
    nn = w.shape[1]
    return pl.pallas_call(
        _mm_kernel,
        grid=(nn // tn, n // tm),
        in_specs=[pl.BlockSpec((tm, k), lambda j, i: (i, 0)),
                  pl.BlockSpec((k, tn), lambda j, i: (0, j))],
        out_specs=pl.BlockSpec((tm, tn), lambda j, i: (i, j)),
        out_shape=jax.ShapeDtypeStruct((n, nn), out_dtype),
        compiler_params=_cparams("arbitrary", "arbitrary"),
        name="in_proj",
    )(x, w)


PACKED = D_MODEL // 2


def _pack_rows(x):
    lo = pltpu.bitcast(x[:, :PACKED].astype(BF16).astype(F32), jnp.int32)
    hi = pltpu.bitcast(x[:, PACKED:].astype(BF16).astype(F32), jnp.int32)
    return lax.shift_right_logical(lo, 16) | hi


def _unpack_rows(w):
    lo = pltpu.bitcast(lax.shift_left(w, 16), F32)
    hi = pltpu.bitcast(w & jnp.int32(-65536), F32)
    return lo, hi


def _silu(x):
    return x * (0.5 + 0.5 * jnp.tanh(0.5 * x))


def _layer_norm_rows(y, g, b):
    mu = jnp.mean(y, axis=-1, keepdims=True)
    yc = y - mu
    var = jnp.mean(yc * yc, axis=-1, keepdims=True)
    return yc * lax.rsqrt(var + LN_EPS) * g + b


def _rows(x, blocks, size):
    parts = [x[b * size:(b + 1) * size, :] for b in blocks]
    return parts[0] if len(parts) == 1 else jnp.concatenate(parts, axis=0)


def _gla_chunk(q_s, k_s, v_s, lf_s, g_s, acc_s, o_ref, st_ref, d, start, slot, reverse):
    c = GLA_CHUNK
    t8 = SUBLANES
    nt_dims = (((1,), (1,)), ((), ()))

    def tt(ref, i):
        return ref[d, pl.ds(start + i, t8, stride=t8), :]

    lf = [tt(lf_s, i) for i in range(t8)]
    qt = [tt(q_s, i) for i in range(t8)]
    kt = [tt(k_s, i) for i in range(t8)]
    vt = [tt(v_s, i) for i in range(t8)]

    gi = [None] * t8
    prev = None
    for i in (range(t8 - 1, -1, -1) if reverse else range(t8)):
        gi[i] = lf[i] if prev is None else prev + lf[i]
        prev = gi[i]
    tot = prev
    sub = lax.broadcasted_iota(jnp.int32, (t8, LANES), 0)
    incl = tot
    s = 1
    while s < t8:
        if reverse:
            incl = incl + jnp.where(sub + s < t8, pltpu.roll(incl, t8 - s, 0), 0.0)
        else:
            incl = incl + jnp.where(sub >= s, pltpu.roll(incl, s, 0), 0.0)
        s *= 2
    excl = incl - tot
    for i in range(t8):
        g_s[slot, pl.ds(i, t8, stride=t8), :] = gi[i] + excl

    acc = []
    for i in range(t8):
        a_i = jnp.sum(qt[i] * kt[i], axis=-1, keepdims=True) * vt[i]
        for r in (range(i + 1, t8) if reverse else range(i)):
            w = qt[i] * kt[r] * jnp.exp2(gi[i] - gi[r])
            a_i = a_i + jnp.sum(w, axis=-1, keepdims=True) * vt[r]
        acc.append(a_i)

    sl = pl.ds(start, c)
    q = q_s[d, sl, :]
    kk = k_s[d, sl, :]
    v = v_s[d, sl, :]
    g = g_s[slot]
    vb = v.astype(BF16)
    st = st_ref[d]

    o = lax.dot_general((q * jnp.exp2(g)).astype(BF16), st.astype(BF16), nt_dims,
                        preferred_element_type=F32)

    half = c // 2
    qd_l, kd_l, v_l, q_tiles = [], [], [], []
    b = 2 * t8
    while b <= c:
        h = b // 2
        nblk = c // b
        lo = [2 * m for m in range(nblk)]
        hi = [2 * m + 1 for m in range(nblk)]
        q_half, k_half = (lo, hi) if reverse else (hi, lo)
        refs = [g[m * b + h:m * b + h + 1, :] if reverse else g[m * b + h - 1:m * b + h, :]
                for m in range(nblk)]
        gref = jnp.concatenate([jnp.broadcast_to(r, (h, LANES)) for r in refs], axis=0) \
            if nblk > 1 else jnp.broadcast_to(refs[0], (h, LANES))
        qd_l.append(_rows(q, q_half, h) * jnp.exp2(_rows(g, q_half, h) - gref))
        kd_l.append(_rows(kk, k_half, h) * jnp.exp2(gref - _rows(g, k_half, h)))
        v_l.append(_rows(v, k_half, h))
        q_tiles.append([hb * (h // t8) + t for hb in q_half for t in range(h // t8)])
        b *= 2
    nlev = len(qd_l)
    p = lax.dot_general(jnp.concatenate(qd_l, axis=0).astype(BF16),
                        jnp.concatenate(kd_l, axis=0).astype(BF16), nt_dims,
                        preferred_element_type=F32)
    ii = lax.broadcasted_iota(jnp.int32, p.shape, 0)
    jj = lax.broadcasted_iota(jnp.int32, p.shape, 1)
    keep = None
    for lv in range(nlev):
        h = t8 << lv
        m = (ii // half == lv) & (jj // half == lv) & ((ii // h) == (jj // h))
        keep = m if keep is None else keep | m
    p = jnp.where(keep, p, 0.0)
    res = jnp.dot(p.astype(BF16), jnp.concatenate(v_l, axis=0).astype(BF16),
                  preferred_element_type=F32)
    contrib = [None] * (c // t8)
    for lv in range(nlev):
        for n_, tile in enumerate(q_tiles[lv]):
            piece = res[lv * half + n_ * t8:lv * half + (n_ + 1) * t8, :]
            contrib[tile] = piece if contrib[tile] is None else contrib[tile] + piece
    zero = jnp.zeros((t8, LANES), F32)
    o = o + jnp.concatenate([zero if p_ is None else p_ for p_ in contrib], axis=0)

    gl = g[0:1, :] if reverse else g[c - 1:c, :]
    kd = (kk * jnp.exp2(gl - g)).astype(BF16)
    upd = lax.dot_general(vb, kd, (((0,), (0,)), ((), ())), preferred_element_type=F32)
    st_ref[d] = st * jnp.exp2(gl) + upd

    for i in range(t8):
        acc_s[slot, pl.ds(i, t8, stride=t8), :] = acc[i]
    o_ref[sl, :] = (o + acc_s[slot]).astype(o_ref.dtype)


def _gla_kernel(layer, lbl_ref, qf_ref, ff_ref, vf_ref, qb_ref, fb_ref, vb_ref,
                of_ref, ob_ref, st_ref, q_s, k_s, v_s, lf_s, g_s, acc_s):
    tb = GLA_TBLOCK
    nc = tb // GLA_CHUNK

    @pl.when(pl.program_id(2) == 0)
    def _():
        st_ref[...] = jnp.zeros_like(st_ref)

    if layer > 0:
        lg = lbl_ref[...]
        e = jnp.exp(lg - jnp.max(lg, axis=0, keepdims=True))
        sm = e / jnp.sum(e, axis=0, keepdims=True)
        lb = sm[1]
        for l in range(2, layer + 1):
            lb = lb + sm[l]

    for d, (q_ref, f_ref, v_ref) in enumerate(((qf_ref, ff_ref, vf_ref),
                                               (qb_ref, fb_ref, vb_ref))):
        qx = q_ref[...].astype(F32)
        fx = f_ref[...]
        v_s[d] = v_ref[...].astype(F32)
        t = jnp.exp(-jnp.abs(fx))
        r = 1.0 / (1.0 + t)
        logsig = jnp.minimum(fx, 0.0) - jnp.log(1.0 + t)
        sig_neg = jnp.where(fx >= 0.0, t * r, r)
        if layer > 0:
            lbd = lb[d:d + 1, :]
            a = jnp.log(lbd)
            bb = jnp.log1p(-lbd) + logsig
            lf = jnp.maximum(a, bb) + jnp.log(1.0 + jnp.exp(-jnp.abs(a - bb)))
            kk = (1.0 - lbd) * sig_neg
        else:
            lf = logsig
            kk = sig_neg
        q_s[d] = _silu(qx)
        k_s[d] = kk
        lf_s[d] = lf * LOG2_E

    def body(ci, carry):
        for u in range(GLA_UNROLL):
            cf = ci * GLA_UNROLL + u
            sf = pl.multiple_of(cf * GLA_CHUNK, GLA_CHUNK)
            sb = pl.multiple_of((nc - 1 - cf) * GLA_CHUNK, GLA_CHUNK)
            _gla_chunk(q_s, k_s, v_s, lf_s, g_s, acc_s, of_ref, st_ref, 0, sf, 2 * u, False)
            _gla_chunk(q_s, k_s, v_s, lf_s, g_s, acc_s, ob_ref, st_ref, 1, sb, 2 * u + 1, True)
        return carry

    lax.fori_loop(0, nc // GLA_UNROLL, body, 0)


def _gla(proj_a, proj_f, lb_logits, layer, batch, seq):
    n = proj_a.shape[0]
    tb = GLA_TBLOCK
    nb = seq // tb
    h8 = HG_HEADS

    def spec(stream, rev):
        if rev:
            return pl.BlockSpec((tb, LANES), lambda b, h, c: (b * nb + nb - 1 - c, stream * h8 + h))
        return pl.BlockSpec((tb, LANES), lambda b, h, c: (b * nb + c, stream * h8 + h))

    o_f_spec = pl.BlockSpec((tb, LANES), lambda b, h, c: (b * nb + c, h))
    o_b_spec = pl.BlockSpec((tb, LANES), lambda b, h, c: (b * nb + nb - 1 - c, h))
    return pl.pallas_call(
        functools.partial(_gla_kernel, layer),
        grid=(batch, h8, nb),
        in_specs=[pl.BlockSpec((DEPTH, 2, LANES), lambda b, h, c: (0, 0, h)),
                  spec(0, False), spec(0, False), spec(1, False),
                  spec(0, True), spec(1, True), spec(1, True)],
        out_specs=[o_f_spec, o_b_spec],
        out_shape=[jax.ShapeDtypeStruct((n, D_MODEL), BF16)] * 2,
        scratch_shapes=[pltpu.VMEM((2, HG_DK, HG_DK), F32)]
        + [pltpu.VMEM((2, tb, LANES), F32)] * 4
        + [pltpu.VMEM((2 * GLA_UNROLL, GLA_CHUNK, LANES), F32)] * 2,
        compiler_params=_cparams("arbitrary", "arbitrary", "arbitrary"),
        name="gla",
    )(lb_logits, proj_a, proj_f, proj_a, proj_a, proj_f, proj_a)


def _hgrn_out_kernel(of_ref, ob_ref, gate_ref, nw_ref, w_ref, x_ref, g_ref, b_ref,
                     rw_ref, rb_ref, tri_ref,
                     o_ref, op_ref, route_ref, cnt_ref, carry_ref):
    o = of_ref[...].astype(F32) + ob_ref[...].astype(F32)
    parts = []
    for h in range(HG_HEADS):
        oh = o[:, h * LANES:(h + 1) * LANES]
        ms = jnp.mean(oh * oh, axis=-1, keepdims=True)
        parts.append(oh * lax.rsqrt(ms + RMS_EPS))
    gate = gate_ref[...].astype(F32)
    y = jnp.concatenate(parts, axis=-1) * nw_ref[...] * _silu(gate)
    mix = jnp.dot(y.astype(BF16), w_ref[...], preferred_element_type=F32)
    out = _layer_norm_rows(DN_ALPHA * x_ref[...] + mix, g_ref[...], b_ref[...])
    o_ref[...] = out
    op_ref[...] = _pack_rows(out)
    _route_tile(out, rw_ref, rb_ref, tri_ref, route_ref, cnt_ref, carry_ref)


def _hgrn_out(o_f, o_b, proj_a, norm_w, w_out, x, g, b, router_ops):
    n = x.shape[0]
    tm = ROW_TILE
    row = pl.BlockSpec((tm, D_MODEL), lambda i: (i, 0))
    vec = pl.BlockSpec((1, D_MODEL), lambda i: (0, 0))
    r_in, r_out, r_scratch = _router_specs()
    xo, xpk, route, cnt = pl.pallas_call(
        _hgrn_out_kernel,
        grid=(n // tm,),
        in_specs=[row, row, pl.BlockSpec((tm, D_MODEL), lambda i: (i, 2)), vec,
                  pl.BlockSpec((D_MODEL, D_MODEL), lambda i: (0, 0)), row, vec, vec] + r_in,
        out_specs=[row, pl.BlockSpec((tm, PACKED), lambda i: (i, 0))] + r_out,
        out_shape=[jax.ShapeDtypeStruct((n, D_MODEL), F32),
                   jax.ShapeDtypeStruct((n, PACKED), jnp.int32)] + _router_out_shapes(n),
        scratch_shapes=r_scratch,
        compiler_params=_cparams("arbitrary"),
        name="hgrn_out",
    )(o_f, o_b, proj_a, jnp.tile(norm_w, HG_HEADS).reshape(1, D_MODEL), w_out, x,
      g.reshape(1, D_MODEL), b.reshape(1, D_MODEL), *router_ops)
    return xo, xpk, route, _expert_counts(cnt)


def _conv_out_kernel(tiles_per_seq, bg_ref, cg_ref, h_ref, cgp_ref, hp_ref, cgn_ref, hn_ref,
                     cw_ref, w_ref, x_ref, g_ref, b_ref, rw_ref, rb_ref, tri_ref,
                     o_ref, op_ref, route_ref, cnt_ref, carry_ref):
    i = pl.program_id(0)
    tm = cg_ref.shape[0]
    u = cg_ref[...].astype(F32) * h_ref[...].astype(F32)
    first = (i % tiles_per_seq) == 0
    last = (i % tiles_per_seq) == tiles_per_seq - 1
    hr = cgp_ref.shape[0]
    u_halo_prev = cgp_ref[...].astype(F32) * hp_ref[...].astype(F32)
    u_halo_next = cgn_ref[...].astype(F32) * hn_ref[...].astype(F32)
    u_prev_row = jnp.where(first, 0.0, u_halo_prev[hr - 1:hr, :])
    u_next_row = jnp.where(last, 0.0, u_halo_next[0:1, :])
    rows = lax.broadcasted_iota(jnp.int32, u.shape, 0)
    u_prev = jnp.where(rows == 0, u_prev_row, pltpu.roll(u, 1, 0))
    u_next = jnp.where(rows == tm - 1, u_next_row, pltpu.roll(u, tm - 1, 0))
    cw = cw_ref[...]
    y = u_prev * cw[0:1, :] + u * cw[1:2, :] + u_next * cw[2:3, :]
    mix = jnp.dot((bg_ref[...].astype(F32) * y).astype(BF16), w_ref[...],
                  preferred_element_type=F32)
    out = _layer_norm_rows(DN_ALPHA * x_ref[...] + mix, g_ref[...], b_ref[...])
    o_ref[...] = out
    op_ref[...] = _pack_rows(out)
    _route_tile(out, rw_ref, rb_ref, tri_ref, route_ref, cnt_ref, carry_ref)


def _conv_out(proj, conv_w, w_out, x, g, b, seq, router_ops):
    n = x.shape[0]
    tm = ROW_TILE
    halo = BF16_SUBLANES
    rh = tm // halo
    nblk = n // halo
    row = pl.BlockSpec((tm, D_MODEL), lambda i: (i, 0))
    vec = pl.BlockSpec((1, D_MODEL), lambda i: (0, 0))

    def main(stream):
        return pl.BlockSpec((tm, D_MODEL), lambda i: (i, stream))

    def prev(stream):
        return pl.BlockSpec((halo, D_MODEL), lambda i: (jnp.maximum(i * rh - 1, 0), stream))

    def nxt(stream):
        return pl.BlockSpec((halo, D_MODEL),
                            lambda i: (jnp.minimum((i + 1) * rh, nblk - 1), stream))

    r_in, r_out, r_scratch = _router_specs()
    xo, xpk, route, cnt = pl.pallas_call(
        functools.partial(_conv_out_kernel, seq // tm),
        grid=(n // tm,),
        in_specs=[main(0), main(1), main(2), prev(1), prev(2), nxt(1), nxt(2),
                  pl.BlockSpec((3, D_MODEL), lambda i: (0, 0)),
                  pl.BlockSpec((D_MODEL, D_MODEL), lambda i: (0, 0)), row, vec, vec] + r_in,
        out_specs=[row, pl.BlockSpec((tm, PACKED), lambda i: (i, 0))] + r_out,
        out_shape=[jax.ShapeDtypeStruct((n, D_MODEL), F32),
                   jax.ShapeDtypeStruct((n, PACKED), jnp.int32)] + _router_out_shapes(n),
        scratch_shapes=r_scratch,
        compiler_params=_cparams("arbitrary"),
        name="conv_out",
    )(proj, proj, proj, proj, proj, proj, proj, conv_w, w_out, x,
      g.reshape(1, D_MODEL), b.reshape(1, D_MODEL), *router_ops)
    return xo, xpk, route, _expert_counts(cnt)


def _route_tile(x, whl_ref, b_ref, tri_ref, o_ref, cnt_ref, carry_ref):
    @pl.when(pl.program_id(0) == 0)
    def _():
        carry_ref[...] = jnp.zeros_like(carry_ref)

    xh = x.astype(BF16)
    xl = (x - xh.astype(F32)).astype(BF16)
    both = jnp.dot(xh, whl_ref[...], preferred_element_type=F32)
    logits = (both[:, :LANES] + both[:, LANES:]
              + jnp.dot(xl, whl_ref[:, :LANES], preferred_element_type=F32)) + b_ref[...]
    eg = MOE_EXPERTS_PER_GROUP
    lt = logits.T
    grp = lt[0:SUBLANES]
    exp_t = [lt[SUBLANES + eg * g:SUBLANES + eg * (g + 1)] for g in range(MOE_GROUPS)]
    row = lax.broadcasted_iota(jnp.int32, grp.shape, 0).astype(F32)
    gmax = jnp.max(grp, axis=0, keepdims=True)
    gsum = jnp.sum(jnp.exp(grp - gmax), axis=0, keepdims=True)
    p_group = 1.0 / gsum
    g_sel = jnp.min(jnp.where(grp == gmax, row, 99.0), axis=0, keepdims=True)
    el = exp_t[MOE_GROUPS - 1]
    for g in range(MOE_GROUPS - 2, -1, -1):
        el = jnp.where(g_sel == g, exp_t[g], el)
    t1 = jnp.max(el, axis=0, keepdims=True)
    i1 = jnp.min(jnp.where(el == t1, row, 99.0), axis=0, keepdims=True)
    el2 = jnp.where(row == i1, NEG_BIG, el)
    t2 = jnp.max(el2, axis=0, keepdims=True)
    i2 = jnp.min(jnp.where(el2 == t2, row, 99.0), axis=0, keepdims=True)
    z = jnp.exp(t2 - t1)
    g1 = p_group / (1.0 + z)
    g2 = g1 * z

    is1 = row == i1
    is2 = row == i2
    in_g = [g_sel == g for g in range(MOE_GROUPS)]
    onehot = jnp.concatenate([jnp.where(in_g[g] & (is1 | is2), 1.0, 0.0)
                              for g in range(MOE_GROUPS)], axis=0)
    prefix = jnp.dot(onehot.astype(BF16), tri_ref[...], preferred_element_type=F32)
    before = prefix + carry_ref[...]
    r1 = jnp.zeros_like(g1)
    r2 = jnp.zeros_like(g1)
    for g in range(MOE_GROUPS):
        bg = before[eg * g:eg * (g + 1)]
        r1 = r1 + jnp.sum(jnp.where(in_g[g] & is1, bg, 0.0), axis=0, keepdims=True)
        r2 = r2 + jnp.sum(jnp.where(in_g[g] & is2, bg, 0.0), axis=0, keepdims=True)
    e1 = g_sel * eg + i1
    e2 = g_sel * eg + i2
    out = jnp.zeros_like(row)
    for r, val in enumerate((g1, g2, e1, e2, r1, r2)):
        out = jnp.where(row == r, val, out)
    o_ref[...] = out
    carry_ref[...] = carry_ref[...] + jnp.sum(onehot, axis=1, keepdims=True)
    cnt_ref[...] = carry_ref[...]


def _router_operands(w_group, b_group, w_expert, b_expert):
    gpad = SUBLANES - MOE_GROUPS
    pad = LANES - SUBLANES - MOE_EXPERTS
    w = jnp.concatenate([w_group, jnp.zeros((D_MODEL, gpad), F32), w_expert,
                         jnp.zeros((D_MODEL, pad), F32)], axis=1)
    wh = w.astype(BF16)
    wl = (w - wh.astype(F32)).astype(BF16)
    b = jnp.concatenate([b_group, jnp.full((gpad,), NEG_BIG, F32), b_expert,
                         jnp.zeros((pad,), F32)]).reshape(1, LANES)
    r = jnp.arange(ROW_TILE, dtype=jnp.int32)
    tri = (r[:, None] < r[None, :]).astype(BF16)
    return jnp.concatenate([wh, wl], axis=1), b, tri


def _router_specs():
    in_specs = [pl.BlockSpec((D_MODEL, 2 * LANES), lambda i: (0, 0)),
                pl.BlockSpec((1, LANES), lambda i: (0, 0)),
                pl.BlockSpec((ROW_TILE, ROW_TILE), lambda i: (0, 0))]
    out_specs = [pl.BlockSpec((SUBLANES, ROW_TILE), lambda i: (0, i)),
                 pl.BlockSpec((MOE_EXPERTS, ROW_TILE), lambda i: (0, 0))]
    scratch = [pltpu.VMEM((MOE_EXPERTS, ROW_TILE), F32)]
    return in_specs, out_specs, scratch


def _router_out_shapes(n):
    return [jax.ShapeDtypeStruct((SUBLANES, n), F32),
            jax.ShapeDtypeStruct((MOE_EXPERTS, ROW_TILE), F32)]


def _expert_counts(cnt):
    return cnt[:, 0].astype(jnp.int32)


DISPATCH_TOKENS = 256
DISPATCH_BUFS = 3
WEIGHT_SLOTS = 3


def _dest_kernel(ps_ref, route_ref, o_ref):
    e = route_ref[MOE_TOPK:2 * MOE_TOPK, :]
    rank = route_ref[2 * MOE_TOPK:3 * MOE_TOPK, :]
    base = jnp.zeros_like(e)
    for x in range(MOE_EXPERTS):
        base = jnp.where(e == x, ps_ref[x].astype(F32), base)
    o_ref[...] = (base + rank).astype(jnp.int32)


def _moe_plan(route, counts, n):
    nk = n * MOE_TOPK
    n_rows = -(-nk // MOE_BLOCK) * MOE_BLOCK + MOE_EXPERTS * MOE_BLOCK
    n_blocks = n_rows // MOE_BLOCK
    padded = (counts + MOE_BLOCK - 1) // MOE_BLOCK * MOE_BLOCK
    pad_end = jnp.cumsum(padded)
    pad_start = pad_end - padded
    dest = pl.pallas_call(
        _dest_kernel,
        grid_spec=pltpu.PrefetchScalarGridSpec(
            num_scalar_prefetch=1, grid=(1,),
            in_specs=[pl.BlockSpec((SUBLANES, n), lambda i, ps: (0, 0))],
            out_specs=pl.BlockSpec((MOE_TOPK, n), lambda i, ps: (0, 0))),
        out_shape=jax.ShapeDtypeStruct((MOE_TOPK, n), jnp.int32),
        compiler_params=_cparams("arbitrary"),
        name="dest",
    )(pad_start.astype(jnp.int32), route).reshape(nk)
    block_start = jnp.arange(n_blocks, dtype=jnp.int32) * MOE_BLOCK
    block_expert = jnp.minimum(jnp.sum(block_start[:, None] >= pad_end[None, :], axis=1),
                               MOE_EXPERTS - 1).astype(jnp.int32)
    nact = (pad_end[-1:] // MOE_BLOCK).astype(jnp.int32)
    fill_start = (pad_start + counts).astype(jnp.int32)
    return dest, block_expert, nact, fill_start, pad_end.astype(jnp.int32), n_rows


def _dispatch_kernel(dest_ref, fs_ref, fe_ref, nact_ref, x_hbm, xs_hbm, xbuf, zbuf, lsem, sem,
                     zsem):
    i = pl.program_id(0)
    nsteps = pl.num_programs(0)
    slot = i % DISPATCH_BUFS
    ct = DISPATCH_TOKENS
    gt = ct // SUBLANES
    n_blocks = xs_hbm.shape[0] // MOE_BLOCK

    def load(step, sl):
        start = pl.multiple_of(step * gt, gt)
        return pltpu.make_async_copy(x_hbm.at[pl.ds(start, gt)], xbuf.at[sl], lsem.at[sl])

    def wait_step(sl):
        for _ in range(MOE_TOPK):
            pltpu.make_async_copy(x_hbm.at[pl.ds(0, gt)], xbuf.at[sl], sem.at[sl]).wait()

    @pl.when(i == 0)
    def _():
        for s in range(DISPATCH_BUFS - 1):
            load(s, s).start()

        zbuf[...] = jnp.zeros_like(zbuf)

        def block_copy(b):
            start = pl.multiple_of(b * MOE_BLOCK, MOE_BLOCK)
            return pltpu.make_async_copy(zbuf, xs_hbm.at[pl.ds(start, MOE_BLOCK)], zsem.at[0])

        def fill(wait):
            def per_expert(e, c):
                @pl.when(fs_ref[e] < fe_ref[e])
                def _():
                    cp = block_copy(fe_ref[e] // MOE_BLOCK - 1)
                    cp.wait() if wait else cp.start()
                return c
            lax.fori_loop(0, MOE_EXPERTS, per_expert, 0)

            def per_block(b, c):
                cp = block_copy(b)
                cp.wait() if wait else cp.start()
                return c
            lax.fori_loop(nact_ref[0], n_blocks, per_block, 0)

        fill(False)
        fill(True)

    load(i, slot).wait()

    n_tok = nsteps * ct

    def body(g, c):
        base = i * ct + g * SUBLANES
        for t in range(SUBLANES):
            for k in range(MOE_TOPK):
                d = dest_ref[k * n_tok + base + t]
                pltpu.make_async_copy(xbuf.at[slot, g, pl.ds(t, 1)], xs_hbm.at[pl.ds(d, 1)],
                                      sem.at[slot]).start(priority=k)
        return c
    lax.fori_loop(0, gt, body, 0)

    @pl.when(i >= 1)
    def _():
        wait_step((i - 1) % DISPATCH_BUFS)

    @pl.when(i + DISPATCH_BUFS - 1 < nsteps)
    def _():
        load(i + DISPATCH_BUFS - 1, (i + DISPATCH_BUFS - 1) % DISPATCH_BUFS).start()

    @pl.when(i == nsteps - 1)
    def _():
        wait_step(slot)


def _dispatch(x, dest, fill_start, fill_end, nact, n_rows):
    n, width = x.shape
    grid_spec = pltpu.PrefetchScalarGridSpec(
        num_scalar_prefetch=4,
        grid=(n // DISPATCH_TOKENS,),
        in_specs=[pl.BlockSpec(memory_space=pl.ANY)],
        out_specs=pl.BlockSpec(memory_space=pl.ANY),
        scratch_shapes=[pltpu.VMEM((DISPATCH_BUFS, DISPATCH_TOKENS // SUBLANES, SUBLANES, width),
                                   x.dtype),
                        pltpu.VMEM((MOE_BLOCK, width), x.dtype),
                        pltpu.SemaphoreType.DMA((DISPATCH_BUFS,)),
                        pltpu.SemaphoreType.DMA((DISPATCH_BUFS,)),
                        pltpu.SemaphoreType.DMA((2,))],
    )
    return pl.pallas_call(
        _dispatch_kernel,
        grid_spec=grid_spec,
        out_shape=jax.ShapeDtypeStruct((n_rows, width), x.dtype),
        compiler_params=_cparams("arbitrary"),
        name="dispatch",
    )(dest, fill_start, fill_end, nact, x.reshape(n // SUBLANES, SUBLANES, width))


def _moe_ffn_kernel(e0, be_ref, nact_ref, ord_ref, nxt_ref, nxt2_ref, xs_ref, wup_hbm, wdn_hbm,
                    ys_ref, wup_f, wdn_f, wup_bf, wdn_bf, wsem):
    i = pl.program_id(0)
    active = i < nact_ref[0]
    new_expert = (i == 0) | (be_ref[i] != be_ref[jnp.maximum(i - 1, 0)])

    def fetch(e, sl):
        return (pltpu.make_async_copy(wup_hbm.at[e0 + e], wup_f.at[sl], wsem.at[0, sl]),
                pltpu.make_async_copy(wdn_hbm.at[e0 + e], wdn_f.at[sl], wsem.at[1, sl]))

    @pl.when(active & (i == 0))
    def _():
        for cp in fetch(be_ref[0], 0):
            cp.start()

        @pl.when(nxt_ref[be_ref[0]] >= 0)
        def _():
            for cp in fetch(nxt_ref[be_ref[0]], 1):
                cp.start()

    @pl.when(active & new_expert)
    def _():
        k = ord_ref[i]
        sl = k % WEIGHT_SLOTS
        for cp in fetch(be_ref[i], sl):
            cp.wait()

        @pl.when(nxt2_ref[be_ref[i]] >= 0)
        def _():
            for cp in fetch(nxt2_ref[be_ref[i]], (k + 2) % WEIGHT_SLOTS):
                cp.start()

        wup_bf[...] = wup_f[sl].astype(BF16)
        wdn_bf[...] = wdn_f[sl].astype(BF16)

    @pl.when(active)
    def _():
        x_lo, x_hi = _unpack_rows(xs_ref[...])
        xb = jnp.concatenate([x_lo.astype(BF16), x_hi.astype(BF16)], axis=1)
        hcat = jnp.dot(xb, wup_bf[...], preferred_element_type=F32)
        hg = hcat[:, :MOE_D_EXPERT]
        hu = hcat[:, MOE_D_EXPERT:]
        act = (_silu(hg) * hu).astype(BF16)
        ys_ref[...] = _pack_rows(jnp.dot(act, wdn_bf[...], preferred_element_type=F32))

    @pl.when(jnp.logical_not(active))
    def _():
        ys_ref[...] = jnp.zeros_like(ys_ref)


def _moe_ffn(xs, block_expert, nact, counts, w_up, w_down, layer):
    n_rows = xs.shape[0]
    n_blocks = n_rows // MOE_BLOCK
    first = jnp.concatenate([jnp.ones((1,), jnp.int32),
                             (block_expert[1:] != block_expert[:-1]).astype(jnp.int32)])
    ordinal = (jnp.cumsum(first) - 1).astype(jnp.int32)
    ids = jnp.arange(MOE_EXPERTS, dtype=jnp.int32)
    has_rows = counts[None, :] > 0
    later = (ids[None, :] > ids[:, None]) & has_rows
    nxt = jnp.min(jnp.where(later, ids[None, :], MOE_EXPERTS), axis=1)
    later2 = (ids[None, :] > nxt[:, None]) & has_rows
    nxt2 = jnp.min(jnp.where(later2, ids[None, :], MOE_EXPERTS), axis=1)
    next_expert = jnp.where(nxt == MOE_EXPERTS, -1, nxt).astype(jnp.int32)
    next2_expert = jnp.where(nxt2 == MOE_EXPERTS, -1, nxt2).astype(jnp.int32)
    row = pl.BlockSpec((MOE_BLOCK, PACKED), lambda i, *_: (i, 0))
    grid_spec = pltpu.PrefetchScalarGridSpec(
        num_scalar_prefetch=5,
        grid=(n_blocks,),
        in_specs=[row, pl.BlockSpec(memory_space=pl.ANY), pl.BlockSpec(memory_space=pl.ANY)],
        out_specs=row,
        scratch_shapes=[pltpu.VMEM((WEIGHT_SLOTS, D_MODEL, 2 * MOE_D_EXPERT), F32),
                        pltpu.VMEM((WEIGHT_SLOTS, MOE_D_EXPERT, D_MODEL), F32),
                        pltpu.VMEM((D_MODEL, 2 * MOE_D_EXPERT), BF16),
                        pltpu.VMEM((MOE_D_EXPERT, D_MODEL), BF16),
                        pltpu.SemaphoreType.DMA((2, WEIGHT_SLOTS))],
    )
    return pl.pallas_call(
        functools.partial(_moe_ffn_kernel, layer * MOE_EXPERTS),
        grid_spec=grid_spec,
        out_shape=jax.ShapeDtypeStruct((n_rows, PACKED), jnp.int32),
        compiler_params=_cparams("arbitrary"),
        name="moe_ffn",
    )(block_expert, nact, ordinal, next_expert, next2_expert, xs, w_up, w_down)


def _combine_ln_kernel(dest_ref, x_ref, route_ref, ys_hbm, g_ref, b_ref, o_ref, ob_ref, ybuf,
                       sem):
    i = pl.program_id(0)
    nsteps = pl.num_programs(0)
    slot = i % 2
    tm = x_ref.shape[0]

    def start_gather(step, sl):
        def body(g, c):
            base = step * tm + g * SUBLANES
            for t in range(SUBLANES):
                for k in range(MOE_TOPK):
                    d = dest_ref[k * (nsteps * tm) + base + t]
                    pltpu.make_async_copy(ys_hbm.at[pl.ds(d, 1)], ybuf.at[sl, k, g, pl.ds(t, 1)],
                                          sem.at[sl]).start(priority=k)
            return c
        lax.fori_loop(0, tm // SUBLANES, body, 0)

    @pl.when(i == 0)
    def _():
        start_gather(0, 0)

    @pl.when(i + 1 < nsteps)
    def _():
        start_gather(i + 1, 1 - slot)

    for k in range(MOE_TOPK):
        pltpu.make_async_copy(ybuf.at[1 - slot, k], ybuf.at[slot, k], sem.at[slot]).wait()
    route = route_ref[...]
    y0_lo, y0_hi = _unpack_rows(ybuf[slot, 0].reshape(tm, PACKED))
    y1_lo, y1_hi = _unpack_rows(ybuf[slot, 1].reshape(tm, PACKED))
    g0 = route[:, 0:1]
    g1 = route[:, 1:2]
    ffn = jnp.concatenate([g0 * y0_lo + g1 * y1_lo, g0 * y0_hi + g1 * y1_hi], axis=1)
    out = _layer_norm_rows(DN_ALPHA * x_ref[...] + ffn, g_ref[...], b_ref[...])
    o_ref[...] = out
    ob_ref[...] = out.astype(BF16)


def _combine_ln(x, route, ys, dest, g, b):
    n = x.shape[0]
    tm = ROW_TILE
    vec = pl.BlockSpec((1, D_MODEL), lambda i, d: (0, 0))
    row = pl.BlockSpec((tm, D_MODEL), lambda i, d: (i, 0))
    grid_spec = pltpu.PrefetchScalarGridSpec(
        num_scalar_prefetch=1,
        grid=(n // tm,),
        in_specs=[pl.BlockSpec((tm, D_MODEL), lambda i, d: (i, 0)),
                  pl.BlockSpec((tm, MOE_TOPK), lambda i, d: (i, 0)),
                  pl.BlockSpec(memory_space=pl.ANY), vec, vec],
        out_specs=[row, row],
        scratch_shapes=[pltpu.VMEM((2, MOE_TOPK, tm // SUBLANES, SUBLANES, PACKED), jnp.int32),
                        pltpu.SemaphoreType.DMA((2,))],
    )
    return pl.pallas_call(
        _combine_ln_kernel,
        grid_spec=grid_spec,
        out_shape=[jax.ShapeDtypeStruct((n, D_MODEL), F32),
                   jax.ShapeDtypeStruct((n, D_MODEL), BF16)],
        compiler_params=_cparams("arbitrary"),
        name="combine_ln",
    )(dest, x, route[0:MOE_TOPK].T, ys, g.reshape(1, D_MODEL), b.reshape(1, D_MODEL))


def kernel(x, hg_w_in, hg_lb_logits, hg_norm_w, hg_w_out, cv_w_in, cv_w, cv_w_out, ln_g, ln_b,
           moe_w_group, moe_b_group, moe_w_expert, moe_b_expert, moe_w_up, moe_w_down):
    batch, seq, d = x.shape
    n = batch * seq
    xf = x.reshape(n, d)
    w_up_all = moe_w_up.reshape(DEPTH * MOE_EXPERTS, D_MODEL, 2 * MOE_D_EXPERT)
    w_down_all = moe_w_down.reshape(DEPTH * MOE_EXPERTS, MOE_D_EXPERT, D_MODEL)
    xin = xf
    for layer in range(DEPTH):
        j = layer // 2
        router_ops = _router_operands(moe_w_group[layer], moe_b_group[layer],
                                      moe_w_expert[layer], moe_b_expert[layer])
        if layer % 2 == 0:
            w = hg_w_in[j]
            w_a = jnp.concatenate([w[:, :D_MODEL], w[:, 3 * D_MODEL:]], axis=1).astype(BF16)
            w_f = w[:, D_MODEL:3 * D_MODEL].astype(BF16)
            proj_a = _matmul(xin, w_a, BF16)
            proj_f = _matmul(xin, w_f, F32)
            o_f, o_b = _gla(proj_a, proj_f, hg_lb_logits, layer, batch, seq)
            xf, xpk, route, counts = _hgrn_out(
                o_f, o_b, proj_a, hg_norm_w[j], hg_w_out[j].astype(BF16), xf,
                ln_g[layer, 0], ln_b[layer, 0], router_ops)
        else:
            proj = _matmul(xin, cv_w_in[j].astype(BF16), BF16)
            xf, xpk, route, counts = _conv_out(
                proj, cv_w[j], cv_w_out[j].astype(BF16), xf,
                ln_g[layer, 0], ln_b[layer, 0], seq, router_ops)
        dest, block_expert, nact, fill_start, fill_end, n_rows = _moe_plan(route, counts, n)
        xs = _dispatch(xpk, dest, fill_start, fill_end, nact, n_rows)
        ys = _moe_ffn(xs, block_expert, nact, counts, w_up_all, w_down_all, layer)
        xf, xin = _combine_ln(xf, route, ys, dest, ln_g[layer, 1], ln_b[layer, 1])
    return xf.reshape(batch, seq, d)
```

```python
import functools

import jax
import jax.numpy as jnp
from jax import lax
from jax.experimental import pallas as pl
from jax.experimental.pallas import tpu as pltpu

D_MODEL = 1024
DEPTH = 4
HG_DK = 128
HG_HEADS = D_MODEL // HG_DK
HG_STREAMS = 5
CONV_STREAMS = 3
MOE_GROUPS = 4
MOE_EXPERTS_PER_GROUP = 8
MOE_EXPERTS = MOE_GROUPS * MOE_EXPERTS_PER_GROUP
MOE_TOPK = 2
MOE_D_EXPERT = D_MODEL // 2
MOE_BLOCK = 512
DN_ALPHA = (2.0 * DEPTH) ** 0.25
LN_EPS = 1e-5
RMS_EPS = 1e-6

LANES = 128
SUBLANES = 8
BF16_SUBLANES = 16
VMEM_LIMIT = 48 * 1024 * 1024
GLA_CHUNK = 64
GLA_TBLOCK = 2048
GLA_UNROLL = 2
ROW_TILE = 512
NEG_BIG = -1e30
LOG2_E = 1.4426950408889634

BF16 = jnp.bfloat16
F32 = jnp.float32


def _cparams(*sem):
    return pltpu.CompilerParams(dimension_semantics=sem, vmem_limit_bytes=VMEM_LIMIT)


def _mm_kernel(x_ref, w_ref, o_ref):
    o_ref[...] = jnp.dot(x_ref[...].astype(BF16), w_ref[...],
                         preferred_element_type=F32).astype(o_ref.dtype)


def _matmul(x, w, out_dtype, tm=2048, tn=1024):
    n, k = x.shape
    nn = w.shape[1]
    return pl.pallas_call(
        _mm_kernel,
        grid=(nn // tn, n // tm),
        in_specs=[pl.BlockSpec((tm, k), lambda j, i: (i, 0)),
                  pl.BlockSpec((k, tn), lambda j, i: (0, j))],
        out_specs=pl.BlockSpec((tm, tn), lambda j, i: (i, j)),
        out_shape=jax.ShapeDtypeStruct((n, nn), out_dtype),
        compiler_params=_cparams("arbitrary", "arbitrary"),
        name="in_proj",
    )(x, w)


PACKED = D_MODEL // 2


def _pack_rows(x):
    lo = pltpu.bitcast(x[:, :PACKED].astype(BF16).astype(F32), jnp.int32)
    hi = pltpu.bitcast(x[:, PACKED:].astype(BF16).astype(F32), jnp.int32)
    return lax.shift_right_logical(lo, 16) | hi


def _unpack_rows(w):
    lo = pltpu.bitcast(lax.shift_left(w, 16), F32)
    hi = pltpu.bitcast(w & jnp.int32(-65536), F32)
    return lo, hi


def _silu(x):
    return x * (0.5 + 0.5 * jnp.tanh(0.5 * x))


def _layer_norm_rows(y, g, b):
    mu = jnp.mean(y, axis=-1, keepdims=True)
    yc = y - mu
    var = jnp.mean(yc * yc, axis=-1, keepdims=True)
    return yc * lax.rsqrt(var + LN_EPS) * g + b


def _rows(x, blocks, size):
    parts = [x[b * size:(b + 1) * size, :] for b in blocks]
    return parts[0] if len(parts) == 1 else jnp.concatenate(parts, axis=0)


def _gla_chunk(q_s, k_s, v_s, lf_s, g_s, acc_s, o_ref, st_ref, d, start, slot, reverse):
    c = GLA_CHUNK
    t8 = SUBLANES
    nt_dims = (((1,), (1,)), ((), ()))

    def tt(ref, i):
        return ref[d, pl.ds(start + i, t8, stride=t8), :]

    lf = [tt(lf_s, i) for i in range(t8)]
    qt = [tt(q_s, i) for i in range(t8)]
    kt = [tt(k_s, i) for i in range(t8)]
    vt = [tt(v_s, i) for i in range(t8)]

    gi = [None] * t8
    prev = None
    for i in (range(t8 - 1, -1, -1) if reverse else range(t8)):
        gi[i] = lf[i] if prev is None else prev + lf[i]
        prev = gi[i]
    tot = prev
    sub = lax.broadcasted_iota(jnp.int32, (t8, LANES), 0)
    incl = tot
    s = 1
    while s < t8:
        if reverse:
            incl = incl + jnp.where(sub + s < t8, pltpu.roll(incl, t8 - s, 0), 0.0)
        else:
            incl = incl + jnp.where(sub >= s, pltpu.roll(incl, s, 0), 0.0)
        s *= 2
    excl = incl - tot
    for i in range(t8):
        g_s[slot, pl.ds(i, t8, stride=t8), :] = gi[i] + excl

    acc = []
    for i in range(t8):
        a_i = jnp.sum(qt[i] * kt[i], axis=-1, keepdims=True) * vt[i]
        for r in (range(i + 1, t8) if reverse else range(i)):
            w = qt[i] * kt[r] * jnp.exp2(gi[i] - gi[r])
            a_i = a_i + jnp.sum(w, axis=-1, keepdims=True) * vt[r]
        acc.append(a_i)

    sl = pl.ds(start, c)
    q = q_s[d, sl, :]
    kk = k_s[d, sl, :]
    v = v_s[d, sl, :]
    g = g_s[slot]
    vb = v.astype(BF16)
    st = st_ref[d]

    o = lax.dot_general((q * jnp.exp2(g)).astype(BF16), st.astype(BF16), nt_dims,
                        preferred_element_type=F32)

    half = c // 2
    qd_l, kd_l, v_l, q_tiles = [], [], [], []
    b = 2 * t8
    while b <= c:
        h = b // 2
        nblk = c // b
        lo = [2 * m for m in range(nblk)]
        hi = [2 * m + 1 for m in range(nblk)]
        q_half, k_half = (lo, hi) if reverse else (hi, lo)
        refs = [g[m * b + h:m * b + h + 1, :] if reverse else g[m * b + h - 1:m * b + h, :]
                for m in range(nblk)]
        gref = jnp.concatenate([jnp.broadcast_to(r, (h, LANES)) for r in refs], axis=0) \
            if nblk > 1 else jnp.broadcast_to(refs[0], (h, LANES))
        qd_l.append(_rows(q, q_half, h) * jnp.exp2(_rows(g, q_half, h) - gref))
        kd_l.append(_rows(kk, k_half, h) * jnp.exp2(gref - _rows(g, k_half, h)))
        v_l.append(_rows(v, k_half, h))
        q_tiles.append([hb * (h // t8) + t for hb in q_half for t in range(h // t8)])
        b *= 2
    nlev = len(qd_l)
    p = lax.dot_general(jnp.concatenate(qd_l, axis=0).astype(BF16),
                        jnp.concatenate(kd_l, axis=0).astype(BF16), nt_dims,
                        preferred_element_type=F32)
    ii = lax.broadcasted_iota(jnp.int32, p.shape, 0)
    jj = lax.broadcasted_iota(jnp.int32, p.shape, 1)
    keep = None
    for lv in range(nlev):
        h = t8 << lv
        m = (ii // half == lv) & (jj // half == lv) & ((ii // h) == (jj // h))
        keep = m if keep is None else keep | m
    p = jnp.where(keep, p, 0.0)
    res = jnp.dot(p.astype(BF16), jnp.concatenate(v_l, axis=0).astype(BF16),
                  preferred_element_type=F32)
    contrib = [None] * (c // t8)
    for lv in range(nlev):
        for n_, tile in enumerate(q_tiles[lv]):
            piece = res[lv * half + n_ * t8:lv * half + (n_ + 1) * t8, :]
            contrib[tile] = piece if contrib[tile] is None else contrib[tile] + piece
    zero = jnp.zeros((t8, LANES), F32)
    o = o + jnp.concatenate([zero if p_ is None else p_ for p_ in contrib], axis=0)

    gl = g[0:1, :] if reverse else g[c - 1:c, :]
    kd = (kk * jnp.exp2(gl - g)).astype(BF16)
    upd = lax.dot_general(vb, kd, (((0,), (0,)), ((), ())), preferred_element_type=F32)
    st_ref[d] = st * jnp.exp2(gl) + upd

    for i in range(t8):
        acc_s[slot, pl.ds(i, t8, stride=t8), :] = acc[i]
    o_ref[sl, :] = (o + acc_s[slot]).astype(o_ref.dtype)


def _gla_kernel(layer, lbl_ref, qf_ref, ff_ref, vf_ref, qb_ref, fb_ref, vb_ref,
                of_ref, ob_ref, st_ref, q_s, k_s, v_s, lf_s, g_s, acc_s):
    tb = GLA_TBLOCK
    nc = tb // GLA_CHUNK

    @pl.when(pl.program_id(2) == 0)
    def _():
        st_ref[...] = jnp.zeros_like(st_ref)

    if layer > 0:
        lg = lbl_ref[...]
        e = jnp.exp(lg - jnp.max(lg, axis=0, keepdims=True))
        sm = e / jnp.sum(e, axis=0, keepdims=True)
        lb = sm[1]
        for l in range(2, layer + 1):
            lb = lb + sm[l]

    for d, (q_ref, f_ref, v_ref) in enumerate(((qf_ref, ff_ref, vf_ref),
                                               (qb_ref, fb_ref, vb_ref))):
        qx = q_ref[...].astype(F32)
        fx = f_ref[...]
        v_s[d] = v_ref[...].astype(F32)
        t = jnp.exp(-jnp.abs(fx))
        r = 1.0 / (1.0 + t)
        logsig = jnp.minimum(fx, 0.0) - jnp.log(1.0 + t)
        sig_neg = jnp.where(fx >= 0.0, t * r, r)
        if layer > 0:
            lbd = lb[d:d + 1, :]
            a = jnp.log(lbd)
            bb = jnp.log1p(-lbd) + logsig
            lf = jnp.maximum(a, bb) + jnp.log(1.0 + jnp.exp(-jnp.abs(a - bb)))
            kk = (1.0 - lbd) * sig_neg
        else:
            lf = logsig
            kk = sig_neg
        q_s[d] = _silu(qx)
        k_s[d] = kk
        lf_s[d] = lf * LOG2_E

    def body(ci, carry):
        for u in range(GLA_UNROLL):
            cf = ci * GLA_UNROLL + u
            sf = pl.multiple_of(cf * GLA_CHUNK, GLA_CHUNK)
            sb = pl.multiple_of((nc - 1 - cf) * GLA_CHUNK, GLA_CHUNK)
            _gla_chunk(q_s, k_s, v_s, lf_s, g_s, acc_s, of_ref, st_ref, 0, sf, 2 * u, False)
            _gla_chunk(q_s, k_s, v_s, lf_s, g_s, acc_s, ob_ref, st_ref, 1, sb, 2 * u + 1, True)
        return carry

    lax.fori_loop(0, nc // GLA_UNROLL, body, 0)


def _gla(proj_a, proj_f, lb_logits, layer, batch, seq):
    n = proj_a.shape[0]
    tb = GLA_TBLOCK
    nb = seq // tb
    h8 = HG_HEADS

    def spec(stream, rev):
        if rev:
            return pl.BlockSpec((tb, LANES), lambda b, h, c: (b * nb + nb - 1 - c, stream * h8 + h))
        return pl.BlockSpec((tb, LANES), lambda b, h, c: (b * nb + c, stream * h8 + h))

    o_f_spec = pl.BlockSpec((tb, LANES), lambda b, h, c: (b * nb + c, h))
    o_b_spec = pl.BlockSpec((tb, LANES), lambda b, h, c: (b * nb + nb - 1 - c, h))
    return pl.pallas_call(
        functools.partial(_gla_kernel, layer),
        grid=(batch, h8, nb),
        in_specs=[pl.BlockSpec((DEPTH, 2, LANES), lambda b, h, c: (0, 0, h)),
                  spec(0, False), spec(0, False), spec(1, False),
                  spec(0, True), spec(1, True), spec(1, True)],
        out_specs=[o_f_spec, o_b_spec],
        out_shape=[jax.ShapeDtypeStruct((n, D_MODEL), BF16)] * 2,
        scratch_shapes=[pltpu.VMEM((2, HG_DK, HG_DK), F32)]
        + [pltpu.VMEM((2, tb, LANES), F32)] * 4
        + [pltpu.VMEM((2 * GLA_UNROLL, GLA_CHUNK, LANES), F32)] * 2,
        compiler_params=_cparams("arbitrary", "arbitrary", "arbitrary"),
        name="gla",
    )(lb_logits, proj_a, proj_f, proj_a, proj_a, proj_f, proj_a)


def _hgrn_out_kernel(of_ref, ob_ref, gate_ref, nw_ref, w_ref, x_ref, g_ref, b_ref,
                     rw_ref, rb_ref, tri_ref,
                     o_ref, op_ref, route_ref, cnt_ref, carry_ref):
    o = of_ref[...].astype(F32) + ob_ref[...].astype(F32)
    parts = []
    for h in range(HG_HEADS):
        oh = o[:, h * LANES:(h + 1) * LANES]
        ms = jnp.mean(oh * oh, axis=-1, keepdims=True)
        parts.append(oh * lax.rsqrt(ms + RMS_EPS))
    gate = gate_ref[...].astype(F32)
    y = jnp.concatenate(parts, axis=-1) * nw_ref[...] * _silu(gate)
    mix = jnp.dot(y.astype(BF16), w_ref[...], preferred_element_type=F32)
    out = _layer_norm_rows(DN_ALPHA * x_ref[...] + mix, g_ref[...], b_ref[...])
    o_ref[...] = out
    op_ref[...] = _pack_rows(out)
    _route_tile(out, rw_ref, rb_ref, tri_ref, route_ref, cnt_ref, carry_ref)


def _hgrn_out(o_f, o_b, proj_a, norm_w, w_out, x, g, b, router_ops):
    n = x.shape[0]
    tm = ROW_TILE
    row = pl.BlockSpec((tm, D_MODEL), lambda i: (i, 0))
    vec = pl.BlockSpec((1, D_MODEL), lambda i: (0, 0))
    r_in, r_out, r_scratch = _router_specs()
    xo, xpk, route, cnt = pl.pallas_call(
        _hgrn_out_kernel,
        grid=(n // tm,),
        in_specs=[row, row, pl.BlockSpec((tm, D_MODEL), lambda i: (i, 2)), vec,
                  pl.BlockSpec((D_MODEL, D_MODEL), lambda i: (0, 0)), row, vec, vec] + r_in,
        out_specs=[row, pl.BlockSpec((tm, PACKED), lambda i: (i, 0))] + r_out,
        out_shape=[jax.ShapeDtypeStruct((n, D_MODEL), F32),
                   jax.ShapeDtypeStruct((n, PACKED), jnp.int32)] + _router_out_shapes(n),
        scratch_shapes=r_scratch,
        compiler_params=_cparams("arbitrary"),
        name="hgrn_out",
    )(o_f, o_b, proj_a, jnp.tile(norm_w, HG_HEADS).reshape(1, D_MODEL), w_out, x,
      g.reshape(1, D_MODEL), b.reshape(1, D_MODEL), *router_ops)
    return xo, xpk, route, _expert_counts(cnt)


def _conv_out_kernel(tiles_per_seq, bg_ref, cg_ref, h_ref, cgp_ref, hp_ref, cgn_ref, hn_ref,
                     cw_ref, w_ref, x_ref, g_ref, b_ref, rw_ref, rb_ref, tri_ref,
                     o_ref, op_ref, route_ref, cnt_ref, carry_ref):
    i = pl.program_id(0)
    tm = cg_ref.shape[0]
    u = cg_ref[...].astype(F32) * h_ref[...].astype(F32)
    first = (i % tiles_per_seq) == 0
    last = (i % tiles_per_seq) == tiles_per_seq - 1
    hr = cgp_ref.shape[0]
    u_halo_prev = cgp_ref[...].astype(F32) * hp_ref[...].astype(F32)
    u_halo_next = cgn_ref[...].astype(F32) * hn_ref[...].astype(F32)
    u_prev_row = jnp.where(first, 0.0, u_halo_prev[hr - 1:hr, :])
    u_next_row = jnp.where(last, 0.0, u_halo_next[0:1, :])
    rows = lax.broadcasted_iota(jnp.int32, u.shape, 0)
    u_prev = jnp.where(rows == 0, u_prev_row, pltpu.roll(u, 1, 0))
    u_next = jnp.where(rows == tm - 1, u_next_row, pltpu.roll(u, tm - 1, 0))
    cw = cw_ref[...]
    y = u_prev * cw[0:1, :] + u * cw[1:2, :] + u_next * cw[2:3, :]
    mix = jnp.dot((bg_ref[...].astype(F32) * y).astype(BF16), w_ref[...],
                  preferred_element_type=F32)
    out = _layer_norm_rows(DN_ALPHA * x_ref[...] + mix, g_ref[...], b_ref[...])
    o_ref[...] = out
    op_ref[...] = _pack_rows(out)
    _route_tile(out, rw_ref, rb_ref, tri_ref, route_ref, cnt_ref, carry_ref)


def _conv_out(proj, conv_w, w_out, x, g, b, seq, router_ops):
    n = x.shape[0]
    tm = ROW_TILE
    halo = BF16_SUBLANES
    rh = tm // halo
    nblk = n // halo
    row = pl.BlockSpec((tm, D_MODEL), lambda i: (i, 0))
    vec = pl.BlockSpec((1, D_MODEL), lambda i: (0, 0))

    def main(stream):
        return pl.BlockSpec((tm, D_MODEL), lambda i: (i, stream))

    def prev(stream):
        return pl.BlockSpec((halo, D_MODEL), lambda i: (jnp.maximum(i * rh - 1, 0), stream))

    def nxt(stream):
        return pl.BlockSpec((halo, D_MODEL),
                            lambda i: (jnp.minimum((i + 1) * rh, nblk - 1), stream))

    r_in, r_out, r_scratch = _router_specs()
    xo, xpk, route, cnt = pl.pallas_call(
        functools.partial(_conv_out_kernel, seq // tm),
        grid=(n // tm,),
        in_specs=[main(0), main(1), main(2), prev(1), prev(2), nxt(1), nxt(2),
                  pl.BlockSpec((3, D_MODEL), lambda i: (0, 0)),
                  pl.BlockSpec((D_MODEL, D_MODEL), lambda i: (0, 0)), row, vec, vec] + r_in,
        out_specs=[row, pl.BlockSpec((tm, PACKED), lambda i: (i, 0))] + r_out,
        out_shape=[jax.ShapeDtypeStruct((n, D_MODEL), F32),
                   jax.ShapeDtypeStruct((n, PACKED), jnp.int32)] + _router_out_shapes(n),
        scratch_shapes=r_scratch,
        compiler_params=_cparams("arbitrary"),
        name="conv_out",
    )(proj, proj, proj, proj, proj, proj, proj, conv_w, w_out, x,
      g.reshape(1, D_MODEL), b.reshape(1, D_MODEL), *router_ops)
    return xo, xpk, route, _expert_counts(cnt)


def _route_tile(x, whl_ref, b_ref, tri_ref, o_ref, cnt_ref, carry_ref):
    @pl.when(pl.program_id(0) == 0)
    def _():
        carry_ref[...] = jnp.zeros_like(carry_ref)

    xh = x.astype(BF16)
    xl = (x - xh.astype(F32)).astype(BF16)
    both = jnp.dot(xh, whl_ref[...], preferred_element_type=F32)
    logits = (both[:, :LANES] + both[:, LANES:]
              + jnp.dot(xl, whl_ref[:, :LANES], preferred_element_type=F32)) + b_ref[...]
    eg = MOE_EXPERTS_PER_GROUP
    lt = logits.T
    grp = lt[0:SUBLANES]
    exp_t = [lt[SUBLANES + eg * g:SUBLANES + eg * (g + 1)] for g in range(MOE_GROUPS)]
    row = lax.broadcasted_iota(jnp.int32, grp.shape, 0).astype(F32)
    gmax = jnp.max(grp, axis=0, keepdims=True)
    gsum = jnp.sum(jnp.exp(grp - gmax), axis=0, keepdims=True)
    p_group = 1.0 / gsum
    g_sel = jnp.min(jnp.where(grp == gmax, row, 99.0), axis=0, keepdims=True)
    el = exp_t[MOE_GROUPS - 1]
    for g in range(MOE_GROUPS - 2, -1, -1):
        el = jnp.where(g_sel == g, exp_t[g], el)
    t1 = jnp.max(el, axis=0, keepdims=True)
    i1 = jnp.min(jnp.where(el == t1, row, 99.0), axis=0, keepdims=True)
    el2 = jnp.where(row == i1, NEG_BIG, el)
    t2 = jnp.max(el2, axis=0, keepdims=True)
    i2 = jnp.min(jnp.where(el2 == t2, row, 99.0), axis=0, keepdims=True)
    z = jnp.exp(t2 - t1)
    g1 = p_group / (1.0 + z)
    g2 = g1 * z

    is1 = row == i1
    is2 = row == i2
    in_g = [g_sel == g for g in range(MOE_GROUPS)]
    onehot = jnp.concatenate([jnp.where(in_g[g] & (is1 | is2), 1.0, 0.0)
                              for g in range(MOE_GROUPS)], axis=0)
    prefix = jnp.dot(onehot.astype(BF16), tri_ref[...], preferred_element_type=F32)
    before = prefix + carry_ref[...]
    r1 = jnp.zeros_like(g1)
    r2 = jnp.zeros_like(g1)
    for g in range(MOE_GROUPS):
        bg = before[eg * g:eg * (g + 1)]
        r1 = r1 + jnp.sum(jnp.where(in_g[g] & is1, bg, 0.0), axis=0, keepdims=True)
        r2 = r2 + jnp.sum(jnp.where(in_g[g] & is2, bg, 0.0), axis=0, keepdims=True)
    e1 = g_sel * eg + i1
    e2 = g_sel * eg + i2
    out = jnp.zeros_like(row)
    for r, val in enumerate((g1, g2, e1, e2, r1, r2)):
        out = jnp.where(row == r, val, out)
    o_ref[...] = out
    carry_ref[...] = carry_ref[...] + jnp.sum(onehot, axis=1, keepdims=True)
    cnt_ref[...] = carry_ref[...]


def _router_operands(w_group, b_group, w_expert, b_expert):
    gpad = SUBLANES - MOE_GROUPS
    pad = LANES - SUBLANES - MOE_EXPERTS
    w = jnp.concatenate([w_group, jnp.zeros((D_MODEL, gpad), F32), w_expert,
                         jnp.zeros((D_MODEL, pad), F32)], axis=1)
    wh = w.astype(BF16)
    wl = (w - wh.astype(F32)).astype(BF16)
    b = jnp.concatenate([b_group, jnp.full((gpad,), NEG_BIG, F32), b_expert,
                         jnp.zeros((pad,), F32)]).reshape(1, LANES)
    r = jnp.arange(ROW_TILE, dtype=jnp.int32)
    tri = (r[:, None] < r[None, :]).astype(BF16)
    return jnp.concatenate([wh, wl], axis=1), b, tri


def _router_specs():
    in_specs = [pl.BlockSpec((D_MODEL, 2 * LANES), lambda i: (0, 0)),
                pl.BlockSpec((1, LANES), lambda i: (0, 0)),
                pl.BlockSpec((ROW_TILE, ROW_TILE), lambda i: (0, 0))]
    out_specs = [pl.BlockSpec((SUBLANES, ROW_TILE), lambda i: (0, i)),
                 pl.BlockSpec((MOE_EXPERTS, ROW_TILE), lambda i: (0, 0))]
    scratch = [pltpu.VMEM((MOE_EXPERTS, ROW_TILE), F32)]
    return in_specs, out_specs, scratch


def _router_out_shapes(n):
    return [jax.ShapeDtypeStruct((SUBLANES, n), F32),
            jax.ShapeDtypeStruct((MOE_EXPERTS, ROW_TILE), F32)]


def _expert_counts(cnt):
    return cnt[:, 0].astype(jnp.int32)


DISPATCH_TOKENS = 256
DISPATCH_BUFS = 3
WEIGHT_SLOTS = 3


def _dest_kernel(ps_ref, route_ref, o_ref):
    e = route_ref[MOE_TOPK:2 * MOE_TOPK, :]
    rank = route_ref[2 * MOE_TOPK:3 * MOE_TOPK, :]
    base = jnp.zeros_like(e)
    for x in range(MOE_EXPERTS):
        base = jnp.where(e == x, ps_ref[x].astype(F32), base)
    o_ref[...] = (base + rank).astype(jnp.int32)


def _moe_plan(route, counts, n):
    nk = n * MOE_TOPK
    n_rows = -(-nk // MOE_BLOCK) * MOE_BLOCK + MOE_EXPERTS * MOE_BLOCK
    n_blocks = n_rows // MOE_BLOCK
    padded = (counts + MOE_BLOCK - 1) // MOE_BLOCK * MOE_BLOCK
    pad_end = jnp.cumsum(padded)
    pad_start = pad_end - padded
    dest = pl.pallas_call(
        _dest_kernel,
        grid_spec=pltpu.PrefetchScalarGridSpec(
            num_scalar_prefetch=1, grid=(1,),
            in_specs=[pl.BlockSpec((SUBLANES, n), lambda i, ps: (0, 0))],
            out_specs=pl.BlockSpec((MOE_TOPK, n), lambda i, ps: (0, 0))),
        out_shape=jax.ShapeDtypeStruct((MOE_TOPK, n), jnp.int32),
        compiler_params=_cparams("arbitrary"),
        name="dest",
    )(pad_start.astype(jnp.int32), route).reshape(nk)
    block_start = jnp.arange(n_blocks, dtype=jnp.int32) * MOE_BLOCK
    block_expert = jnp.minimum(jnp.sum(block_start[:, None] >= pad_end[None, :], axis=1),
                               MOE_EXPERTS - 1).astype(jnp.int32)
    nact = (pad_end[-1:] // MOE_BLOCK).astype(jnp.int32)
    fill_start = (pad_start + counts).astype(jnp.int32)
    return dest, block_expert, nact, fill_start, pad_end.astype(jnp.int32), n_rows


def _dispatch_kernel(dest_ref, fs_ref, fe_ref, nact_ref, x_hbm, xs_hbm, xbuf, zbuf, lsem, sem,
                     zsem):
    i = pl.program_id(0)
    nsteps = pl.num_programs(0)
    slot = i % DISPATCH_BUFS
    ct = DISPATCH_TOKENS
    gt = ct // SUBLANES
    n_blocks = xs_hbm.shape[0] // MOE_BLOCK

    def load(step, sl):
        start = pl.multiple_of(step * gt, gt)
        return pltpu.make_async_copy(x_hbm.at[pl.ds(start, gt)], xbuf.at[sl], lsem.at[sl])

    def wait_step(sl):
        for _ in range(MOE_TOPK):
            pltpu.make_async_copy(x_hbm.at[pl.ds(0, gt)], xbuf.at[sl], sem.at[sl]).wait()

    @pl.when(i == 0)
    def _():
        for s in range(DISPATCH_BUFS - 1):
            load(s, s).start()

        zbuf[...] = jnp.zeros_like(zbuf)

        def block_copy(b):
            start = pl.multiple_of(b * MOE_BLOCK, MOE_BLOCK)
            return pltpu.make_async_copy(zbuf, xs_hbm.at[pl.ds(start, MOE_BLOCK)], zsem.at[0])

        def fill(wait):
            def per_expert(e, c):
                @pl.when(fs_ref[e] < fe_ref[e])
                def _():
                    cp = block_copy(fe_ref[e] // MOE_BLOCK - 1)
                    cp.wait() if wait else cp.start()
                return c
            lax.fori_loop(0, MOE_EXPERTS, per_expert, 0)

            def per_block(b, c):
                cp = block_copy(b)
                cp.wait() if wait else cp.start()
                return c
            lax.fori_loop(nact_ref[0], n_blocks, per_block, 0)

        fill(False)
        fill(True)

    load(i, slot).wait()

    n_tok = nsteps * ct

    def body(g, c):
        base = i * ct + g * SUBLANES
        for t in range(SUBLANES):
            for k in range(MOE_TOPK):
                d = dest_ref[k * n_tok + base + t]
                pltpu.make_async_copy(xbuf.at[slot, g, pl.ds(t, 1)], xs_hbm.at[pl.ds(d, 1)],
                                      sem.at[slot]).start(priority=k)
        return c
    lax.fori_loop(0, gt, body, 0)

    @pl.when(i >= 1)
    def _():
        wait_step((i - 1) % DISPATCH_BUFS)

    @pl.when(i + DISPATCH_BUFS - 1 < nsteps)
    def _():
        load(i + DISPATCH_BUFS - 1, (i + DISPATCH_BUFS - 1) % DISPATCH_BUFS).start()

    @pl.when(i == nsteps - 1)
    def _():
        wait_step(slot)


def _dispatch(x, dest, fill_start, fill_end, nact, n_rows):
    n, width = x.shape
    grid_spec = pltpu.PrefetchScalarGridSpec(
        num_scalar_prefetch=4,
        grid=(n // DISPATCH_TOKENS,),
        in_specs=[pl.BlockSpec(memory_space=pl.ANY)],
        out_specs=pl.BlockSpec(memory_space=pl.ANY),
        scratch_shapes=[pltpu.VMEM((DISPATCH_BUFS, DISPATCH_TOKENS // SUBLANES, SUBLANES, width),
                                   x.dtype),
                        pltpu.VMEM((MOE_BLOCK, width), x.dtype),
                        pltpu.SemaphoreType.DMA((DISPATCH_BUFS,)),
                        pltpu.SemaphoreType.DMA((DISPATCH_BUFS,)),
                        pltpu.SemaphoreType.DMA((2,))],
    )
    return pl.pallas_call(
        _dispatch_kernel,
        grid_spec=grid_spec,
        out_shape=jax.ShapeDtypeStruct((n_rows, width), x.dtype),
        compiler_params=_cparams("arbitrary"),
        name="dispatch",
    )(dest, fill_start, fill_end, nact, x.reshape(n // SUBLANES, SUBLANES, width))


def _moe_ffn_kernel(e0, be_ref, nact_ref, ord_ref, nxt_ref, nxt2_ref, xs_ref, wup_hbm, wdn_hbm,
                    ys_ref, wup_f, wdn_f, wup_bf, wdn_bf, wsem):
    i = pl.program_id(0)
    active = i < nact_ref[0]
    new_expert = (i == 0) | (be_ref[i] != be_ref[jnp.maximum(i - 1, 0)])

    def fetch(e, sl):
        return (pltpu.make_async_copy(wup_hbm.at[e0 + e], wup_f.at[sl], wsem.at[0, sl]),
                pltpu.make_async_copy(wdn_hbm.at[e0 + e], wdn_f.at[sl], wsem.at[1, sl]))

    @pl.when(active & (i == 0))
    def _():
        for cp in fetch(be_ref[0], 0):
            cp.start()

        @pl.when(nxt_ref[be_ref[0]] >= 0)
        def _():
            for cp in fetch(nxt_ref[be_ref[0]], 1):
                cp.start()

    @pl.when(active & new_expert)
    def _():
        k = ord_ref[i]
        sl = k % WEIGHT_SLOTS
        for cp in fetch(be_ref[i], sl):
            cp.wait()

        @pl.when(nxt2_ref[be_ref[i]] >= 0)
        def _():
            for cp in fetch(nxt2_ref[be_ref[i]], (k + 2) % WEIGHT_SLOTS):
                cp.start()

        wup_bf[...] = wup_f[sl].astype(BF16)
        wdn_bf[...] = wdn_f[sl].astype(BF16)

    @pl.when(active)
    def _():
        x_lo, x_hi = _unpack_rows(xs_ref[...])
        xb = jnp.concatenate([x_lo.astype(BF16), x_hi.astype(BF16)], axis=1)
        hcat = jnp.dot(xb, wup_bf[...], preferred_element_type=F32)
        hg = hcat[:, :MOE_D_EXPERT]
        hu = hcat[:, MOE_D_EXPERT:]
        act = (_silu(hg) * hu).astype(BF16)
        ys_ref[...] = _pack_rows(jnp.dot(act, wdn_bf[...], preferred_element_type=F32))

    @pl.when(jnp.logical_not(active))
    def _():
        ys_ref[...] = jnp.zeros_like(ys_ref)


def _moe_ffn(xs, block_expert, nact, counts, w_up, w_down, layer):
    n_rows = xs.shape[0]
    n_blocks = n_rows // MOE_BLOCK
    first = jnp.concatenate([jnp.ones((1,), jnp.int32),
                             (block_expert[1:] != block_expert[:-1]).astype(jnp.int32)])
    ordinal = (jnp.cumsum(first) - 1).astype(jnp.int32)
    ids = jnp.arange(MOE_EXPERTS, dtype=jnp.int32)
    has_rows = counts[None, :] > 0
    later = (ids[None, :] > ids[:, None]) & has_rows
    nxt = jnp.min(jnp.where(later, ids[None, :], MOE_EXPERTS), axis=1)
    later2 = (ids[None, :] > nxt[:, None]) & has_rows
    nxt2 = jnp.min(jnp.where(later2, ids[None, :], MOE_EXPERTS), axis=1)
    next_expert = jnp.where(nxt == MOE_EXPERTS, -1, nxt).astype(jnp.int32)
    next2_expert = jnp.where(nxt2 == MOE_EXPERTS, -1, nxt2).astype(jnp.int32)
    row = pl.BlockSpec((MOE_BLOCK, PACKED), lambda i, *_: (i, 0))
    grid_spec = pltpu.PrefetchScalarGridSpec(
        num_scalar_prefetch=5,
        grid=(n_blocks,),
        in_specs=[row, pl.BlockSpec(memory_space=pl.ANY), pl.BlockSpec(memory_space=pl.ANY)],
        out_specs=row,
        scratch_shapes=[pltpu.VMEM((WEIGHT_SLOTS, D_MODEL, 2 * MOE_D_EXPERT), F32),
                        pltpu.VMEM((WEIGHT_SLOTS, MOE_D_EXPERT, D_MODEL), F32),
                        pltpu.VMEM((D_MODEL, 2 * MOE_D_EXPERT), BF16),
                        pltpu.VMEM((MOE_D_EXPERT, D_MODEL), BF16),
                        pltpu.SemaphoreType.DMA((2, WEIGHT_SLOTS))],
    )
    return pl.pallas_call(
        functools.partial(_moe_ffn_kernel, layer * MOE_EXPERTS),
        grid_spec=grid_spec,
        out_shape=jax.ShapeDtypeStruct((n_rows, PACKED), jnp.int32),
        compiler_params=_cparams("arbitrary"),
        name="moe_ffn",
    )(block_expert, nact, ordinal, next_expert, next2_expert, xs, w_up, w_down)


def _combine_ln_kernel(dest_ref, x_ref, route_ref, ys_hbm, g_ref, b_ref, o_ref, ob_ref, ybuf,
                       sem):
    i = pl.program_id(0)
    nsteps = pl.num_programs(0)
    slot = i % 2
    tm = x_ref.shape[0]

    def start_gather(step, sl):
        def body(g, c):
            base = step * tm + g * SUBLANES
            for t in range(SUBLANES):
                for k in range(MOE_TOPK):
                    d = dest_ref[k * (nsteps * tm) + base + t]
                    pltpu.make_async_copy(ys_hbm.at[pl.ds(d, 1)], ybuf.at[sl, k, g, pl.ds(t, 1)],
                                          sem.at[sl]).start(priority=k)
            return c
        lax.fori_loop(0, tm // SUBLANES, body, 0)

    @pl.when(i == 0)
    def _():
        start_gather(0, 0)

    @pl.when(i + 1 < nsteps)
    def _():
        start_gather(i + 1, 1 - slot)

    for k in range(MOE_TOPK):
        pltpu.make_async_copy(ybuf.at[1 - slot, k], ybuf.at[slot, k], sem.at[slot]).wait()
    route = route_ref[...]
    y0_lo, y0_hi = _unpack_rows(ybuf[slot, 0].reshape(tm, PACKED))
    y1_lo, y1_hi = _unpack_rows(ybuf[slot, 1].reshape(tm, PACKED))
    g0 = route[:, 0:1]
    g1 = route[:, 1:2]
    ffn = jnp.concatenate([g0 * y0_lo + g1 * y1_lo, g0 * y0_hi + g1 * y1_hi], axis=1)
    out = _layer_norm_rows(DN_ALPHA * x_ref[...] + ffn, g_ref[...], b_ref[...])
    o_ref[...] = out
    ob_ref[...] = out.astype(BF16)


def _combine_ln(x, route, ys, dest, g, b):
    n = x.shape[0]
    tm = ROW_TILE
    vec = pl.BlockSpec((1, D_MODEL), lambda i, d: (0, 0))
    row = pl.BlockSpec((tm, D_MODEL), lambda i, d: (i, 0))
    grid_spec = pltpu.PrefetchScalarGridSpec(
        num_scalar_prefetch=1,
        grid=(n // tm,),
        in_specs=[pl.BlockSpec((tm, D_MODEL), lambda i, d: (i, 0)),
                  pl.BlockSpec((tm, MOE_TOPK), lambda i, d: (i, 0)),
                  pl.BlockSpec(memory_space=pl.ANY), vec, vec],
        out_specs=[row, row],
        scratch_shapes=[pltpu.VMEM((2, MOE_TOPK, tm // SUBLANES, SUBLANES, PACKED), jnp.int32),
                        pltpu.SemaphoreType.DMA((2,))],
    )
    return pl.pallas_call(
        _combine_ln_kernel,
        grid_spec=grid_spec,
        out_shape=[jax.ShapeDtypeStruct((n, D_MODEL), F32),
                   jax.ShapeDtypeStruct((n, D_MODEL), BF16)],
        compiler_params=_cparams("arbitrary"),
        name="combine_ln",
    )(dest, x, route[0:MOE_TOPK].T, ys, g.reshape(1, D_MODEL), b.reshape(1, D_MODEL))


def kernel(x, hg_w_in, hg_lb_logits, hg_norm_w, hg_w_out, cv_w_in, cv_w, cv_w_out, ln_g, ln_b,
           moe_w_group, moe_b_group, moe_w_expert, moe_b_expert, moe_w_up, moe_w_down):
    batch, seq, d = x.shape
    n = batch * seq
    xf = x.reshape(n, d)
    w_up_all = moe_w_up.reshape(DEPTH * MOE_EXPERTS, D_MODEL, 2 * MOE_D_EXPERT)
    w_down_all = moe_w_down.reshape(DEPTH * MOE_EXPERTS, MOE_D_EXPERT, D_MODEL)
    xin = xf
    for layer in range(DEPTH):
        j = layer // 2
        router_ops = _router_operands(moe_w_group[layer], moe_b_group[layer],
                                      moe_w_expert[layer], moe_b_expert[layer])
        if layer % 2 == 0:
            w = hg_w_in[j]
            w_a = jnp.concatenate([w[:, :D_MODEL], w[:, 3 * D_MODEL:]], axis=1).astype(BF16)
            w_f = w[:, D_MODEL:3 * D_MODEL].astype(BF16)
            proj_a = _matmul(xin, w_a, BF16)
            proj_f = _matmul(xin, w_f, F32)
            o_f, o_b = _gla(proj_a, proj_f, hg_lb_logits, layer, batch, seq)
            xf, xpk, route, counts = _hgrn_out(
                o_f, o_b, proj_a, hg_norm_w[j], hg_w_out[j].astype(BF16), xf,
                ln_g[layer, 0], ln_b[layer, 0], router_ops)
        else:
            proj = _matmul(xin, cv_w_in[j].astype(BF16), BF16)
            xf, xpk, route, counts = _conv_out(
                proj, cv_w[j], cv_w_out[j].astype(BF16), xf,
                ln_g[layer, 0], ln_b[layer, 0], seq, router_ops)
        dest, block_expert, nact, fill_start, fill_end, n_rows = _moe_plan(route, counts, n)
        xs = _dispatch(xpk, dest, fill_start, fill_end, nact, n_rows)
        ys = _moe_ffn(xs, block_expert, nact, counts, w_up_all, w_down_all, layer)
        xf, xin = _combine_ln(xf, route, ys, dest, ln_g[layer, 1], ln_b[layer, 1])
    return xf.reshape(batch, seq, d)
```

```python
import functools

import jax
import jax.numpy as jnp
from jax import lax
from jax.experimental import pallas as pl
from jax.experimental.pallas import tpu as pltpu

D_MODEL = 1024
DEPTH = 4
HG_DK = 128
HG_HEADS = D_MODEL // HG_DK
MOE_GROUPS = 4
MOE_EXPERTS_PER_GROUP = 8
MOE_EXPERTS = MOE_GROUPS * MOE_EXPERTS_PER_GROUP
MOE_TOPK = 2
MOE_D_EXPERT = D_MODEL // 2
MOE_BLOCK = 512
DN_ALPHA = (2.0 * DEPTH) ** 0.25
LN_EPS = 1e-5
RMS_EPS = 1e-6

LANES = 128
SUBLANES = 8
BF16_SUBLANES = 16
VMEM_LIMIT = 48 * 1024 * 1024
GLA_CHUNK = 64
GLA_TBLOCK = 4096
GLA_UNROLL = 2
ROW_TILE = 512
NEG_BIG = -1e30
LOG2_E = 1.4426950408889634

BF16 = jnp.bfloat16
F32 = jnp.float32


def _cparams(*sem):
    return pltpu.CompilerParams(dimension_semantics=sem, vmem_limit_bytes=VMEM_LIMIT)


def _mm_kernel(x_ref, w_ref, o_ref, w_bf):
    @pl.when(pl.program_id(1) == 0)
    def _():
        w_bf[...] = w_ref[0].astype(BF16)

    o_ref[...] = jnp.dot(x_ref[...].astype(BF16), w_bf[...],
                         preferred_element_type=F32).astype(o_ref.dtype)


def _matmul(x, w_all, layer, col_tiles, out_dtype, tm=2048, tn=D_MODEL):
    n, k = x.shape
    if x.dtype == F32:
        tm //= 2
    first, rest = col_tiles[0], col_tiles[1] - 1 if len(col_tiles) > 1 else 0
    assert all(c == (first if j == 0 else rest + j) for j, c in enumerate(col_tiles))
    return pl.pallas_call(
        _mm_kernel,
        grid=(len(col_tiles), n // tm),
        in_specs=[pl.BlockSpec((tm, k), lambda j, i: (i, 0)),
                  pl.BlockSpec((1, k, tn),
                               lambda j, i: (layer, 0, jnp.where(j == 0, first, rest + j)))],
        out_specs=pl.BlockSpec((tm, tn), lambda j, i: (i, j)),
        out_shape=jax.ShapeDtypeStruct((n, len(col_tiles) * tn), out_dtype),
        scratch_shapes=[pltpu.VMEM((k, tn), BF16)],
        compiler_params=_cparams("arbitrary", "arbitrary"),
        name="in_proj",
    )(x, w_all)


PACKED = D_MODEL // 2


def _pack_rows(x):
    lo = pltpu.bitcast(x[:, :PACKED].astype(BF16).astype(F32), jnp.int32)
    hi = pltpu.bitcast(x[:, PACKED:].astype(BF16).astype(F32), jnp.int32)
    return lax.shift_right_logical(lo, 16) | hi


def _unpack_rows(w):
    lo = pltpu.bitcast(lax.shift_left(w, 16), F32)
    hi = pltpu.bitcast(w & jnp.int32(-65536), F32)
    return lo, hi


def _silu(x):
    return x * (0.5 + 0.5 * jnp.tanh(0.5 * x))


def _layer_norm_rows(y, g, b):
    mu = jnp.mean(y, axis=-1, keepdims=True)
    yc = y - mu
    var = jnp.mean(yc * yc, axis=-1, keepdims=True)
    return yc * lax.rsqrt(var + LN_EPS) * g + b


def _rows(x, blocks, size):
    parts = [x[b * size:(b + 1) * size, :] for b in blocks]
    return parts[0] if len(parts) == 1 else jnp.concatenate(parts, axis=0)


def _gla_chunk(q_s, k_s, v_s, lf_s, g_s, acc_s, o_ref, st_ref, d, start, slot, reverse):
    c = GLA_CHUNK
    t8 = SUBLANES
    nt_dims = (((1,), (1,)), ((), ()))

    def tt(ref, i):
        return ref[d, pl.ds(start + i, t8, stride=t8), :]

    lf = [tt(lf_s, i) for i in range(t8)]
    qt = [tt(q_s, i) for i in range(t8)]
    kt = [tt(k_s, i) for i in range(t8)]
    vt = [tt(v_s, i) for i in range(t8)]

    gi = [None] * t8
    prev = None
    for i in (range(t8 - 1, -1, -1) if reverse else range(t8)):
        gi[i] = lf[i] if prev is None else prev + lf[i]
        prev = gi[i]
    tot = prev
    sub = lax.broadcasted_iota(jnp.int32, (t8, LANES), 0)
    incl = tot
    s = 1
    while s < t8:
        if reverse:
            incl = incl + jnp.where(sub + s < t8, pltpu.roll(incl, t8 - s, 0), 0.0)
        else:
            incl = incl + jnp.where(sub >= s, pltpu.roll(incl, s, 0), 0.0)
        s *= 2
    excl = incl - tot
    for i in range(t8):
        g_s[slot, pl.ds(i, t8, stride=t8), :] = gi[i] + excl

    acc = []
    for i in range(t8):
        a_i = jnp.sum(qt[i] * kt[i], axis=-1, keepdims=True) * vt[i]
        for r in (range(i + 1, t8) if reverse else range(i)):
            w = qt[i] * kt[r] * jnp.exp2(gi[i] - gi[r])
            a_i = a_i + jnp.sum(w, axis=-1, keepdims=True) * vt[r]
        acc.append(a_i)

    sl = pl.ds(start, c)
    q = q_s[d, sl, :]
    kk = k_s[d, sl, :]
    v = v_s[d, sl, :]
    g = g_s[slot]
    vb = v.astype(BF16)
    st = st_ref[d]

    o = lax.dot_general((q * jnp.exp2(g)).astype(BF16), st.astype(BF16), nt_dims,
                        preferred_element_type=F32)

    half = c // 2
    qd_l, kd_l, v_l, q_tiles = [], [], [], []
    b = 2 * t8
    while b <= c:
        h = b // 2
        nblk = c // b
        lo = [2 * m for m in range(nblk)]
        hi = [2 * m + 1 for m in range(nblk)]
        q_half, k_half = (lo, hi) if reverse else (hi, lo)
        refs = [g[m * b + h:m * b + h + 1, :] if reverse else g[m * b + h - 1:m * b + h, :]
                for m in range(nblk)]
        gref = jnp.concatenate([jnp.broadcast_to(r, (h, LANES)) for r in refs], axis=0) \
            if nblk > 1 else jnp.broadcast_to(refs[0], (h, LANES))
        qd_l.append(_rows(q, q_half, h) * jnp.exp2(_rows(g, q_half, h) - gref))
        kd_l.append(_rows(kk, k_half, h) * jnp.exp2(gref - _rows(g, k_half, h)))
        v_l.append(_rows(v, k_half, h))
        q_tiles.append([hb * (h // t8) + t for hb in q_half for t in range(h // t8)])
        b *= 2
    nlev = len(qd_l)
    p = lax.dot_general(jnp.concatenate(qd_l, axis=0).astype(BF16),
                        jnp.concatenate(kd_l, axis=0).astype(BF16), nt_dims,
                        preferred_element_type=F32)
    ii = lax.broadcasted_iota(jnp.int32, p.shape, 0)
    jj = lax.broadcasted_iota(jnp.int32, p.shape, 1)
    keep = None
    for lv in range(nlev):
        h = t8 << lv
        m = (ii // half == lv) & (jj // half == lv) & ((ii // h) == (jj // h))
        keep = m if keep is None else keep | m
    p = jnp.where(keep, p, 0.0)
    res = jnp.dot(p.astype(BF16), jnp.concatenate(v_l, axis=0).astype(BF16),
                  preferred_element_type=F32)
    contrib = [None] * (c // t8)
    for lv in range(nlev):
        for n_, tile in enumerate(q_tiles[lv]):
            piece = res[lv * half + n_ * t8:lv * half + (n_ + 1) * t8, :]
            contrib[tile] = piece if contrib[tile] is None else contrib[tile] + piece
    zero = jnp.zeros((t8, LANES), F32)
    o = o + jnp.concatenate([zero if p_ is None else p_ for p_ in contrib], axis=0)

    gl = g[0:1, :] if reverse else g[c - 1:c, :]
    kd = (kk * jnp.exp2(gl - g)).astype(BF16)
    upd = lax.dot_general(vb, kd, (((0,), (0,)), ((), ())), preferred_element_type=F32)
    st_ref[d] = st * jnp.exp2(gl) + upd

    for i in range(t8):
        acc_s[slot, pl.ds(i, t8, stride=t8), :] = acc[i]
    o_ref[sl, :] = (o + acc_s[slot]).astype(o_ref.dtype)


def _gla_kernel(layer, lbl_ref, qf_ref, ff_ref, vf_ref, qb_ref, fb_ref, vb_ref,
                of_ref, ob_ref, st_ref, q_s, k_s, v_s, lf_s, g_s, acc_s):
    tb = GLA_TBLOCK
    nc = tb // GLA_CHUNK

    @pl.when(pl.program_id(2) == 0)
    def _():
        st_ref[...] = jnp.zeros_like(st_ref)

    if layer > 0:
        lg = lbl_ref[...]
        e = jnp.exp(lg - jnp.max(lg, axis=0, keepdims=True))
        sm = e / jnp.sum(e, axis=0, keepdims=True)
        lb = sm[1]
        for l in range(2, layer + 1):
            lb = lb + sm[l]

    for d, (q_ref, f_ref, v_ref) in enumerate(((qf_ref, ff_ref, vf_ref),
                                               (qb_ref, fb_ref, vb_ref))):
        qx = q_ref[...].astype(F32)
        fx = f_ref[...]
        v_s[d] = v_ref[...].astype(F32)
        t = jnp.exp(-jnp.abs(fx))
        r = 1.0 / (1.0 + t)
        logsig = jnp.minimum(fx, 0.0) - jnp.log(1.0 + t)
        sig_neg = jnp.where(fx >= 0.0, t * r, r)
        if layer > 0:
            lbd = lb[d:d + 1, :]
            a = jnp.log(lbd)
            bb = jnp.log1p(-lbd) + logsig
            lf = jnp.maximum(a, bb) + jnp.log(1.0 + jnp.exp(-jnp.abs(a - bb)))
            kk = (1.0 - lbd) * sig_neg
        else:
            lf = logsig
            kk = sig_neg
        q_s[d] = _silu(qx)
        k_s[d] = kk
        lf_s[d] = lf * LOG2_E

    def body(ci, carry):
        for u in range(GLA_UNROLL):
            cf = ci * GLA_UNROLL + u
            sf = pl.multiple_of(cf * GLA_CHUNK, GLA_CHUNK)
            sb = pl.multiple_of((nc - 1 - cf) * GLA_CHUNK, GLA_CHUNK)
            _gla_chunk(q_s, k_s, v_s, lf_s, g_s, acc_s, of_ref, st_ref, 0, sf, 2 * u, False)
            _gla_chunk(q_s, k_s, v_s, lf_s, g_s, acc_s, ob_ref, st_ref, 1, sb, 2 * u + 1, True)
        return carry

    lax.fori_loop(0, nc // GLA_UNROLL, body, 0)


def _gla(proj_a, proj_f, lb_logits, layer, batch, seq):
    n = proj_a.shape[0]
    tb = GLA_TBLOCK
    nb = seq // tb
    h8 = HG_HEADS

    def spec(stream, rev):
        if rev:
            return pl.BlockSpec((tb, LANES), lambda b, h, c: (b * nb + nb - 1 - c, stream * h8 + h))
        return pl.BlockSpec((tb, LANES), lambda b, h, c: (b * nb + c, stream * h8 + h))

    o_f_spec = pl.BlockSpec((tb, LANES), lambda b, h, c: (b * nb + c, h))
    o_b_spec = pl.BlockSpec((tb, LANES), lambda b, h, c: (b * nb + nb - 1 - c, h))
    return pl.pallas_call(
        functools.partial(_gla_kernel, layer),
        grid=(batch, h8, nb),
        in_specs=[pl.BlockSpec((DEPTH, 2, LANES), lambda b, h, c: (0, 0, h)),
                  spec(0, False), spec(0, False), spec(1, False),
                  spec(0, True), spec(1, True), spec(1, True)],
        out_specs=[o_f_spec, o_b_spec],
        out_shape=[jax.ShapeDtypeStruct((n, D_MODEL), BF16)] * 2,
        scratch_shapes=[pltpu.VMEM((2, HG_DK, HG_DK), F32)]
        + [pltpu.VMEM((2, tb, LANES), F32)] * 4
        + [pltpu.VMEM((2 * GLA_UNROLL, GLA_CHUNK, LANES), F32)] * 2,
        compiler_params=_cparams("arbitrary", "arbitrary", "arbitrary"),
        name="gla",
    )(lb_logits, proj_a, proj_f, proj_a, proj_a, proj_f, proj_a)


def _hgrn_out_kernel(of_ref, ob_ref, gate_ref, nw_ref, w_ref, x_ref, g_ref, b_ref,
                     rw_ref, rb_ref, tri_ref,
                     o_ref, op_ref, route_ref, cnt_ref, carry_ref):
    o = of_ref[...].astype(F32) + ob_ref[...].astype(F32)
    parts = []
    for h in range(HG_HEADS):
        oh = o[:, h * LANES:(h + 1) * LANES]
        ms = jnp.mean(oh * oh, axis=-1, keepdims=True)
        parts.append(oh * lax.rsqrt(ms + RMS_EPS))
    gate = gate_ref[...].astype(F32)
    y = jnp.concatenate(parts, axis=-1) * nw_ref[...] * _silu(gate)
    mix = jnp.dot(y.astype(BF16), w_ref[...], preferred_element_type=F32)
    out = _layer_norm_rows(DN_ALPHA * x_ref[...] + mix, g_ref[...], b_ref[...])
    o_ref[...] = out
    op_ref[...] = _pack_rows(out)
    _route_tile(out, rw_ref, rb_ref, tri_ref, route_ref, cnt_ref, carry_ref)


def _hgrn_out(o_f, o_b, proj_a, norm_w, w_out, x, g, b, router_ops):
    n = x.shape[0]
    tm = ROW_TILE
    row = pl.BlockSpec((tm, D_MODEL), lambda i: (i, 0))
    vec = pl.BlockSpec((1, D_MODEL), lambda i: (0, 0))
    r_in, r_out, r_scratch = _router_specs()
    xo, xpk, route, cnt = pl.pallas_call(
        _hgrn_out_kernel,
        grid=(n // tm,),
        in_specs=[row, row, pl.BlockSpec((tm, D_MODEL), lambda i: (i, 2)), vec,
                  pl.BlockSpec((D_MODEL, D_MODEL), lambda i: (0, 0)), row, vec, vec] + r_in,
        out_specs=[row, pl.BlockSpec((tm, PACKED), lambda i: (i, 0))] + r_out,
        out_shape=[jax.ShapeDtypeStruct((n, D_MODEL), F32),
                   jax.ShapeDtypeStruct((n, PACKED), jnp.int32)] + _router_out_shapes(n),
        scratch_shapes=r_scratch,
        compiler_params=_cparams("arbitrary"),
        name="hgrn_out",
    )(o_f, o_b, proj_a, jnp.tile(norm_w, HG_HEADS).reshape(1, D_MODEL), w_out, x,
      g.reshape(1, D_MODEL), b.reshape(1, D_MODEL), *router_ops)
    return xo, xpk, route, _expert_counts(cnt)


def _conv_out_kernel(tiles_per_seq, bg_ref, cg_ref, h_ref, cgp_ref, hp_ref, cgn_ref, hn_ref,
                     cw_ref, w_ref, x_ref, g_ref, b_ref, rw_ref, rb_ref, tri_ref,
                     o_ref, op_ref, route_ref, cnt_ref, carry_ref):
    i = pl.program_id(0)
    tm = cg_ref.shape[0]
    u = cg_ref[...].astype(F32) * h_ref[...].astype(F32)
    first = (i % tiles_per_seq) == 0
    last = (i % tiles_per_seq) == tiles_per_seq - 1
    hr = cgp_ref.shape[0]
    u_halo_prev = cgp_ref[...].astype(F32) * hp_ref[...].astype(F32)
    u_halo_next = cgn_ref[...].astype(F32) * hn_ref[...].astype(F32)
    u_prev_row = jnp.where(first, 0.0, u_halo_prev[hr - 1:hr, :])
    u_next_row = jnp.where(last, 0.0, u_halo_next[0:1, :])
    rows = lax.broadcasted_iota(jnp.int32, u.shape, 0)
    u_prev = jnp.where(rows == 0, u_prev_row, pltpu.roll(u, 1, 0))
    u_next = jnp.where(rows == tm - 1, u_next_row, pltpu.roll(u, tm - 1, 0))
    cw = cw_ref[...]
    y = u_prev * cw[0:1, :] + u * cw[1:2, :] + u_next * cw[2:3, :]
    mix = jnp.dot((bg_ref[...].astype(F32) * y).astype(BF16), w_ref[...],
                  preferred_element_type=F32)
    out = _layer_norm_rows(DN_ALPHA * x_ref[...] + mix, g_ref[...], b_ref[...])
    o_ref[...] = out
    op_ref[...] = _pack_rows(out)
    _route_tile(out, rw_ref, rb_ref, tri_ref, route_ref, cnt_ref, carry_ref)


def _conv_out(proj, conv_w, w_out, x, g, b, seq, router_ops):
    n = x.shape[0]
    tm = ROW_TILE
    halo = BF16_SUBLANES
    rh = tm // halo
    nblk = n // halo
    row = pl.BlockSpec((tm, D_MODEL), lambda i: (i, 0))
    vec = pl.BlockSpec((1, D_MODEL), lambda i: (0, 0))

    def main(stream):
        return pl.BlockSpec((tm, D_MODEL), lambda i: (i, stream))

    def prev(stream):
        return pl.BlockSpec((halo, D_MODEL), lambda i: (jnp.maximum(i * rh - 1, 0), stream))

    def nxt(stream):
        return pl.BlockSpec((halo, D_MODEL),
                            lambda i: (jnp.minimum((i + 1) * rh, nblk - 1), stream))

    r_in, r_out, r_scratch = _router_specs()
    xo, xpk, route, cnt = pl.pallas_call(
        functools.partial(_conv_out_kernel, seq // tm),
        grid=(n // tm,),
        in_specs=[main(0), main(1), main(2), prev(1), prev(2), nxt(1), nxt(2),
                  pl.BlockSpec((3, D_MODEL), lambda i: (0, 0)),
                  pl.BlockSpec((D_MODEL, D_MODEL), lambda i: (0, 0)), row, vec, vec] + r_in,
        out_specs=[row, pl.BlockSpec((tm, PACKED), lambda i: (i, 0))] + r_out,
        out_shape=[jax.ShapeDtypeStruct((n, D_MODEL), F32),
                   jax.ShapeDtypeStruct((n, PACKED), jnp.int32)] + _router_out_shapes(n),
        scratch_shapes=r_scratch,
        compiler_params=_cparams("arbitrary"),
        name="conv_out",
    )(proj, proj, proj, proj, proj, proj, proj, conv_w, w_out, x,
      g.reshape(1, D_MODEL), b.reshape(1, D_MODEL), *router_ops)
    return xo, xpk, route, _expert_counts(cnt)


def _route_tile(x, whl_ref, b_ref, tri_ref, o_ref, cnt_ref, carry_ref):
    @pl.when(pl.program_id(0) == 0)
    def _():
        carry_ref[...] = jnp.zeros_like(carry_ref)

    xh = x.astype(BF16)
    xl = (x - xh.astype(F32)).astype(BF16)
    both = jnp.dot(xh, whl_ref[...], preferred_element_type=F32)
    logits = (both[:, :LANES] + both[:, LANES:]
              + jnp.dot(xl, whl_ref[:, :LANES], preferred_element_type=F32)) + b_ref[...]
    eg = MOE_EXPERTS_PER_GROUP
    lt = logits.T
    grp = lt[0:SUBLANES]
    exp_t = [lt[SUBLANES + eg * g:SUBLANES + eg * (g + 1)] for g in range(MOE_GROUPS)]
    row = lax.broadcasted_iota(jnp.int32, grp.shape, 0).astype(F32)
    gmax = jnp.max(grp, axis=0, keepdims=True)
    gsum = jnp.sum(jnp.exp(grp - gmax), axis=0, keepdims=True)
    p_group = 1.0 / gsum
    g_sel = jnp.min(jnp.where(grp == gmax, row, 99.0), axis=0, keepdims=True)
    el = exp_t[MOE_GROUPS - 1]
    for g in range(MOE_GROUPS - 2, -1, -1):
        el = jnp.where(g_sel == g, exp_t[g], el)
    t1 = jnp.max(el, axis=0, keepdims=True)
    i1 = jnp.min(jnp.where(el == t1, row, 99.0), axis=0, keepdims=True)
    el2 = jnp.where(row == i1, NEG_BIG, el)
    t2 = jnp.max(el2, axis=0, keepdims=True)
    i2 = jnp.min(jnp.where(el2 == t2, row, 99.0), axis=0, keepdims=True)
    z = jnp.exp(t2 - t1)
    g1 = p_group / (1.0 + z)
    g2 = g1 * z

    is1 = row == i1
    is2 = row == i2
    in_g = [g_sel == g for g in range(MOE_GROUPS)]
    onehot = jnp.concatenate([jnp.where(in_g[g] & (is1 | is2), 1.0, 0.0)
                              for g in range(MOE_GROUPS)], axis=0)
    prefix = jnp.dot(onehot.astype(BF16), tri_ref[...], preferred_element_type=F32)
    before = prefix + carry_ref[...]
    r1 = jnp.zeros_like(g1)
    r2 = jnp.zeros_like(g1)
    for g in range(MOE_GROUPS):
        bg = before[eg * g:eg * (g + 1)]
        r1 = r1 + jnp.sum(jnp.where(in_g[g] & is1, bg, 0.0), axis=0, keepdims=True)
        r2 = r2 + jnp.sum(jnp.where(in_g[g] & is2, bg, 0.0), axis=0, keepdims=True)
    e1 = g_sel * eg + i1
    e2 = g_sel * eg + i2
    out = jnp.zeros_like(row)
    for r, val in enumerate((g1, g2, e1, e2, r1, r2)):
        out = jnp.where(row == r, val, out)
    o_ref[...] = out
    carry_ref[...] = carry_ref[...] + jnp.sum(onehot, axis=1, keepdims=True)
    cnt_ref[...] = carry_ref[...]


def _router_operands(w_group, b_group, w_expert, b_expert):
    gpad = SUBLANES - MOE_GROUPS
    pad = LANES - SUBLANES - MOE_EXPERTS
    w = jnp.concatenate([w_group, jnp.zeros((D_MODEL, gpad), F32), w_expert,
                         jnp.zeros((D_MODEL, pad), F32)], axis=1)
    wh = w.astype(BF16)
    wl = (w - wh.astype(F32)).astype(BF16)
    b = jnp.concatenate([b_group, jnp.full((gpad,), NEG_BIG, F32), b_expert,
                         jnp.zeros((pad,), F32)]).reshape(1, LANES)
    r = jnp.arange(ROW_TILE, dtype=jnp.int32)
    tri = (r[:, None] < r[None, :]).astype(BF16)
    return jnp.concatenate([wh, wl], axis=1), b, tri


def _router_specs():
    in_specs = [pl.BlockSpec((D_MODEL, 2 * LANES), lambda i: (0, 0)),
                pl.BlockSpec((1, LANES), lambda i: (0, 0)),
                pl.BlockSpec((ROW_TILE, ROW_TILE), lambda i: (0, 0))]
    out_specs = [pl.BlockSpec((SUBLANES, ROW_TILE), lambda i: (0, i)),
                 pl.BlockSpec((MOE_EXPERTS, ROW_TILE), lambda i: (0, 0))]
    scratch = [pltpu.VMEM((MOE_EXPERTS, ROW_TILE), F32)]
    return in_specs, out_specs, scratch


def _router_out_shapes(n):
    return [jax.ShapeDtypeStruct((SUBLANES, n), F32),
            jax.ShapeDtypeStruct((MOE_EXPERTS, ROW_TILE), F32)]


def _expert_counts(cnt):
    return cnt[:, 0].astype(jnp.int32)


DISPATCH_TOKENS = 512
DISPATCH_BUFS = 3
WEIGHT_SLOTS = 3


def _dest_kernel(ps_ref, route_ref, o_ref):
    e = route_ref[MOE_TOPK:2 * MOE_TOPK, :]
    rank = route_ref[2 * MOE_TOPK:3 * MOE_TOPK, :]
    base = jnp.zeros_like(e)
    for x in range(MOE_EXPERTS):
        base = jnp.where(e == x, ps_ref[x].astype(F32), base)
    o_ref[...] = (base + rank).astype(jnp.int32)


def _moe_plan(route, counts, n):
    nk = n * MOE_TOPK
    n_rows = -(-nk // MOE_BLOCK) * MOE_BLOCK + MOE_EXPERTS * MOE_BLOCK
    n_blocks = n_rows // MOE_BLOCK
    padded = (counts + MOE_BLOCK - 1) // MOE_BLOCK * MOE_BLOCK
    pad_end = jnp.cumsum(padded)
    pad_start = pad_end - padded
    dest = pl.pallas_call(
        _dest_kernel,
        grid_spec=pltpu.PrefetchScalarGridSpec(
            num_scalar_prefetch=1, grid=(1,),
            in_specs=[pl.BlockSpec((SUBLANES, n), lambda i, ps: (0, 0))],
            out_specs=pl.BlockSpec((MOE_TOPK, n), lambda i, ps: (0, 0))),
        out_shape=jax.ShapeDtypeStruct((MOE_TOPK, n), jnp.int32),
        compiler_params=_cparams("arbitrary"),
        name="dest",
    )(pad_start.astype(jnp.int32), route).reshape(nk)
    block_start = jnp.arange(n_blocks, dtype=jnp.int32) * MOE_BLOCK
    block_expert = jnp.minimum(jnp.sum(block_start[:, None] >= pad_end[None, :], axis=1),
                               MOE_EXPERTS - 1).astype(jnp.int32)
    nact = (pad_end[-1:] // MOE_BLOCK).astype(jnp.int32)
    fill_start = (pad_start + counts).astype(jnp.int32)
    return dest, block_expert, nact, fill_start, pad_end.astype(jnp.int32), n_rows


def _dispatch_kernel(dest_ref, fs_ref, fe_ref, nact_ref, x_hbm, xs_hbm, xbuf, zbuf, lsem, sem,
                     zsem):
    i = pl.program_id(0)
    nsteps = pl.num_programs(0)
    slot = i % DISPATCH_BUFS
    ct = DISPATCH_TOKENS
    gt = ct // SUBLANES
    n_blocks = xs_hbm.shape[0] // MOE_BLOCK

    def load(step, sl):
        start = pl.multiple_of(step * gt, gt)
        return pltpu.make_async_copy(x_hbm.at[pl.ds(start, gt)], xbuf.at[sl], lsem.at[sl])

    def wait_step(sl):
        for _ in range(MOE_TOPK):
            pltpu.make_async_copy(x_hbm.at[pl.ds(0, gt)], xbuf.at[sl], sem.at[sl]).wait()

    @pl.when(i == 0)
    def _():
        for s in range(DISPATCH_BUFS - 1):
            load(s, s).start()

        zbuf[...] = jnp.zeros_like(zbuf)

        def block_copy(b):
            start = pl.multiple_of(b * MOE_BLOCK, MOE_BLOCK)
            return pltpu.make_async_copy(zbuf, xs_hbm.at[pl.ds(start, MOE_BLOCK)], zsem.at[0])

        def fill(wait):
            def per_expert(e, c):
                @pl.when(fs_ref[e] < fe_ref[e])
                def _():
                    cp = block_copy(fe_ref[e] // MOE_BLOCK - 1)
                    cp.wait() if wait else cp.start()
                return c
            lax.fori_loop(0, MOE_EXPERTS, per_expert, 0)

            def per_block(b, c):
                cp = block_copy(b)
                cp.wait() if wait else cp.start()
                return c
            lax.fori_loop(nact_ref[0], n_blocks, per_block, 0)

        fill(False)
        fill(True)

    load(i, slot).wait()

    n_tok = nsteps * ct

    def body(g, c):
        base = i * ct + g * SUBLANES
        for t in range(SUBLANES):
            for k in range(MOE_TOPK):
                d = dest_ref[k * n_tok + base + t]
                pltpu.make_async_copy(xbuf.at[slot, g, pl.ds(t, 1)], xs_hbm.at[pl.ds(d, 1)],
                                      sem.at[slot]).start(priority=k)
        return c
    lax.fori_loop(0, gt, body, 0)

    @pl.when(i >= 1)
    def _():
        wait_step((i - 1) % DISPATCH_BUFS)

    @pl.when(i + DISPATCH_BUFS - 1 < nsteps)
    def _():
        load(i + DISPATCH_BUFS - 1, (i + DISPATCH_BUFS - 1) % DISPATCH_BUFS).start()

    @pl.when(i == nsteps - 1)
    def _():
        wait_step(slot)


def _dispatch(x, dest, fill_start, fill_end, nact, n_rows):
    n, width = x.shape
    grid_spec = pltpu.PrefetchScalarGridSpec(
        num_scalar_prefetch=4,
        grid=(n // DISPATCH_TOKENS,),
        in_specs=[pl.BlockSpec(memory_space=pl.ANY)],
        out_specs=pl.BlockSpec(memory_space=pl.ANY),
        scratch_shapes=[pltpu.VMEM((DISPATCH_BUFS, DISPATCH_TOKENS // SUBLANES, SUBLANES, width),
                                   x.dtype),
                        pltpu.VMEM((MOE_BLOCK, width), x.dtype),
                        pltpu.SemaphoreType.DMA((DISPATCH_BUFS,)),
                        pltpu.SemaphoreType.DMA((DISPATCH_BUFS,)),
                        pltpu.SemaphoreType.DMA((2,))],
    )
    return pl.pallas_call(
        _dispatch_kernel,
        grid_spec=grid_spec,
        out_shape=jax.ShapeDtypeStruct((n_rows, width), x.dtype),
        compiler_params=_cparams("arbitrary"),
        name="dispatch",
    )(dest, fill_start, fill_end, nact, x.reshape(n // SUBLANES, SUBLANES, width))


def _moe_ffn_kernel(e0, be_ref, nact_ref, ord_ref, nxt_ref, nxt2_ref, xs_ref, wup_hbm, wdn_hbm,
                    ys_ref, wup_f, wdn_f, wup_bf, wdn_bf, wsem):
    i = pl.program_id(0)
    active = i < nact_ref[0]
    new_expert = (i == 0) | (be_ref[i] != be_ref[jnp.maximum(i - 1, 0)])

    def fetch(e, sl):
        return (pltpu.make_async_copy(wup_hbm.at[e0 + e], wup_f.at[sl], wsem.at[0, sl]),
                pltpu.make_async_copy(wdn_hbm.at[e0 + e], wdn_f.at[sl], wsem.at[1, sl]))

    @pl.when(active & (i == 0))
    def _():
        for cp in fetch(be_ref[0], 0):
            cp.start()

        @pl.when(nxt_ref[be_ref[0]] >= 0)
        def _():
            for cp in fetch(nxt_ref[be_ref[0]], 1):
                cp.start()

    @pl.when(active & new_expert)
    def _():
        k = ord_ref[i]
        sl = k % WEIGHT_SLOTS
        for cp in fetch(be_ref[i], sl):
            cp.wait()

        @pl.when(nxt2_ref[be_ref[i]] >= 0)
        def _():
            for cp in fetch(nxt2_ref[be_ref[i]], (k + 2) % WEIGHT_SLOTS):
                cp.start()

        wup_bf[...] = wup_f[sl].astype(BF16)
        wdn_bf[...] = wdn_f[sl].astype(BF16)

    @pl.when(active)
    def _():
        x_lo, x_hi = _unpack_rows(xs_ref[...])
        xb = jnp.concatenate([x_lo.astype(BF16), x_hi.astype(BF16)], axis=1)
        hcat = jnp.dot(xb, wup_bf[...], preferred_element_type=F32)
        hg = hcat[:, :MOE_D_EXPERT]
        hu = hcat[:, MOE_D_EXPERT:]
        act = (_silu(hg) * hu).astype(BF16)
        ys_ref[...] = _pack_rows(jnp.dot(act, wdn_bf[...], preferred_element_type=F32))

    @pl.when(jnp.logical_not(active))
    def _():
        ys_ref[...] = jnp.zeros_like(ys_ref)


def _moe_ffn(xs, block_expert, nact, counts, w_up, w_down, layer):
    n_rows = xs.shape[0]
    n_blocks = n_rows // MOE_BLOCK
    first = jnp.concatenate([jnp.ones((1,), jnp.int32),
                             (block_expert[1:] != block_expert[:-1]).astype(jnp.int32)])
    ordinal = (jnp.cumsum(first) - 1).astype(jnp.int32)
    ids = jnp.arange(MOE_EXPERTS, dtype=jnp.int32)
    has_rows = counts[None, :] > 0
    later = (ids[None, :] > ids[:, None]) & has_rows
    nxt = jnp.min(jnp.where(later, ids[None, :], MOE_EXPERTS), axis=1)
    later2 = (ids[None, :] > nxt[:, None]) & has_rows
    nxt2 = jnp.min(jnp.where(later2, ids[None, :], MOE_EXPERTS), axis=1)
    next_expert = jnp.where(nxt == MOE_EXPERTS, -1, nxt).astype(jnp.int32)
    next2_expert = jnp.where(nxt2 == MOE_EXPERTS, -1, nxt2).astype(jnp.int32)
    row = pl.BlockSpec((MOE_BLOCK, PACKED), lambda i, *_: (i, 0))
    grid_spec = pltpu.PrefetchScalarGridSpec(
        num_scalar_prefetch=5,
        grid=(n_blocks,),
        in_specs=[row, pl.BlockSpec(memory_space=pl.ANY), pl.BlockSpec(memory_space=pl.ANY)],
        out_specs=row,
        scratch_shapes=[pltpu.VMEM((WEIGHT_SLOTS, D_MODEL, 2 * MOE_D_EXPERT), F32),
                        pltpu.VMEM((WEIGHT_SLOTS, MOE_D_EXPERT, D_MODEL), F32),
                        pltpu.VMEM((D_MODEL, 2 * MOE_D_EXPERT), BF16),
                        pltpu.VMEM((MOE_D_EXPERT, D_MODEL), BF16),
                        pltpu.SemaphoreType.DMA((2, WEIGHT_SLOTS))],
    )
    return pl.pallas_call(
        functools.partial(_moe_ffn_kernel, layer * MOE_EXPERTS),
        grid_spec=grid_spec,
        out_shape=jax.ShapeDtypeStruct((n_rows, PACKED), jnp.int32),
        compiler_params=_cparams("arbitrary"),
        name="moe_ffn",
    )(block_expert, nact, ordinal, next_expert, next2_expert, xs, w_up, w_down)


def _combine_ln_kernel(dest_ref, x_ref, route_ref, ys_hbm, g_ref, b_ref, o_ref, ob_ref, ybuf,
                       sem):
    i = pl.program_id(0)
    nsteps = pl.num_programs(0)
    slot = i % 2
    tm = x_ref.shape[0]

    def start_gather(step, sl):
        def body(g, c):
            base = step * tm + g * SUBLANES
            for t in range(SUBLANES):
                for k in range(MOE_TOPK):
                    d = dest_ref[k * (nsteps * tm) + base + t]
                    pltpu.make_async_copy(ys_hbm.at[pl.ds(d, 1)], ybuf.at[sl, k, g, pl.ds(t, 1)],
                                          sem.at[sl]).start(priority=k)
            return c
        lax.fori_loop(0, tm // SUBLANES, body, 0)

    @pl.when(i == 0)
    def _():
        start_gather(0, 0)

    @pl.when(i + 1 < nsteps)
    def _():
        start_gather(i + 1, 1 - slot)

    for k in range(MOE_TOPK):
        pltpu.make_async_copy(ybuf.at[1 - slot, k], ybuf.at[slot, k], sem.at[slot]).wait()
    route = route_ref[...]
    y0_lo, y0_hi = _unpack_rows(ybuf[slot, 0].reshape(tm, PACKED))
    y1_lo, y1_hi = _unpack_rows(ybuf[slot, 1].reshape(tm, PACKED))
    g0 = route[:, 0:1]
    g1 = route[:, 1:2]
    ffn = jnp.concatenate([g0 * y0_lo + g1 * y1_lo, g0 * y0_hi + g1 * y1_hi], axis=1)
    out = _layer_norm_rows(DN_ALPHA * x_ref[...] + ffn, g_ref[...], b_ref[...])
    o_ref[...] = out
    ob_ref[...] = out.astype(BF16)


def _combine_ln(x, route, ys, dest, g, b):
    n = x.shape[0]
    tm = ROW_TILE
    vec = pl.BlockSpec((1, D_MODEL), lambda i, d: (0, 0))
    row = pl.BlockSpec((tm, D_MODEL), lambda i, d: (i, 0))
    grid_spec = pltpu.PrefetchScalarGridSpec(
        num_scalar_prefetch=1,
        grid=(n // tm,),
        in_specs=[pl.BlockSpec((tm, D_MODEL), lambda i, d: (i, 0)),
                  pl.BlockSpec((tm, MOE_TOPK), lambda i, d: (i, 0)),
                  pl.BlockSpec(memory_space=pl.ANY), vec, vec],
        out_specs=[row, row],
        scratch_shapes=[pltpu.VMEM((2, MOE_TOPK, tm // SUBLANES, SUBLANES, PACKED), jnp.int32),
                        pltpu.SemaphoreType.DMA((2,))],
    )
    return pl.pallas_call(
        _combine_ln_kernel,
        grid_spec=grid_spec,
        out_shape=[jax.ShapeDtypeStruct((n, D_MODEL), F32),
                   jax.ShapeDtypeStruct((n, D_MODEL), BF16)],
        compiler_params=_cparams("arbitrary"),
        name="combine_ln",
    )(dest, x, route[0:MOE_TOPK].T, ys, g.reshape(1, D_MODEL), b.reshape(1, D_MODEL))


def kernel(x, hg_w_in, hg_lb_logits, hg_norm_w, hg_w_out, cv_w_in, cv_w, cv_w_out, ln_g, ln_b,
           moe_w_group, moe_b_group, moe_w_expert, moe_b_expert, moe_w_up, moe_w_down):
    batch, seq, d = x.shape
    n = batch * seq
    xf = x.reshape(n, d)
    w_up_all = moe_w_up.reshape(DEPTH * MOE_EXPERTS, D_MODEL, 2 * MOE_D_EXPERT)
    w_down_all = moe_w_down.reshape(DEPTH * MOE_EXPERTS, MOE_D_EXPERT, D_MODEL)
    xin = xf
    for layer in range(DEPTH):
        j = layer // 2
        router_ops = _router_operands(moe_w_group[layer], moe_b_group[layer],
                                      moe_w_expert[layer], moe_b_expert[layer])
        if layer % 2 == 0:
            proj_a = _matmul(xin, hg_w_in, j, (0, 3, 4), BF16)
            proj_f = _matmul(xin, hg_w_in, j, (1, 2), F32)
            o_f, o_b = _gla(proj_a, proj_f, hg_lb_logits, layer, batch, seq)
            xf, xpk, route, counts = _hgrn_out(
                o_f, o_b, proj_a, hg_norm_w[j], hg_w_out[j].astype(BF16), xf,
                ln_g[layer, 0], ln_b[layer, 0], router_ops)
        else:
            proj = _matmul(xin, cv_w_in, j, (0, 1, 2), BF16)
            xf, xpk, route, counts = _conv_out(
                proj, cv_w[j], cv_w_out[j].astype(BF16), xf,
                ln_g[layer, 0], ln_b[layer, 0], seq, router_ops)
        dest, block_expert, nact, fill_start, fill_end, n_rows = _moe_plan(route, counts, n)
        xs = _dispatch(xpk, dest, fill_start, fill_end, nact, n_rows)
        ys = _moe_ffn(xs, block_expert, nact, counts, w_up_all, w_down_all, layer)
        xf, xin = _combine_ln(xf, route, ys, dest, ln_g[layer, 1], ln_b[layer, 1])
    return xf.reshape(batch, seq, d)
```

```python
import functools

import jax
import jax.numpy as jnp
from jax import lax
from jax.experimental import pallas as pl
from jax.experimental.pallas import tpu as pltpu

D_MODEL = 1024
DEPTH = 4
HG_DK = 128
HG_HEADS = D_MODEL // HG_DK
MOE_GROUPS = 4
MOE_EXPERTS_PER_GROUP = 8
MOE_EXPERTS = MOE_GROUPS * MOE_EXPERTS_PER_GROUP
MOE_TOPK = 2
MOE_D_EXPERT = D_MODEL // 2
MOE_BLOCK = 512
DN_ALPHA = (2.0 * DEPTH) ** 0.25
LN_EPS = 1e-5
RMS_EPS = 1e-6

LANES = 128
SUBLANES = 8
BF16_SUBLANES = 16
VMEM_LIMIT = 48 * 1024 * 1024
GLA_CHUNK = 64
GLA_TBLOCK = 4096
GLA_UNROLL = 2
ROW_TILE = 512
NEG_BIG = -1e30
LOG2_E = 1.4426950408889634

BF16 = jnp.bfloat16
F32 = jnp.float32


def _cparams(*sem):
    return pltpu.CompilerParams(dimension_semantics=sem, vmem_limit_bytes=VMEM_LIMIT)


def _mm_kernel(x_ref, w_ref, o_ref):
    o_ref[...] = jnp.dot(x_ref[...].astype(BF16), w_ref[...],
                         preferred_element_type=F32).astype(o_ref.dtype)


def _matmul(x, w, out_dtype, tm=2048, tn=1024):
    n, k = x.shape
    nn = w.shape[1]
    return pl.pallas_call(
        _mm_kernel,
        grid=(nn // tn, n // tm),
        in_specs=[pl.BlockSpec((tm, k), lambda j, i: (i, 0)),
                  pl.BlockSpec((k, tn), lambda j, i: (0, j))],
        out_specs=pl.BlockSpec((tm, tn), lambda j, i: (i, j)),
        out_shape=jax.ShapeDtypeStruct((n, nn), out_dtype),
        compiler_params=_cparams("arbitrary", "arbitrary"),
        name="in_proj",
    )(x, w)


PACKED = D_MODEL // 2


def _pack_rows(x):
    lo = pltpu.bitcast(x[:, :PACKED].astype(BF16).astype(F32), jnp.int32)
    hi = pltpu.bitcast(x[:, PACKED:].astype(BF16).astype(F32), jnp.int32)
    return lax.shift_right_logical(lo, 16) | hi


def _unpack_rows(w):
    lo = pltpu.bitcast(lax.shift_left(w, 16), F32)
    hi = pltpu.bitcast(w & jnp.int32(-65536), F32)
    return lo, hi


def _silu(x):
    return x * (0.5 + 0.5 * jnp.tanh(0.5 * x))


def _layer_norm_rows(y, g, b):
    mu = jnp.mean(y, axis=-1, keepdims=True)
    yc = y - mu
    var = jnp.mean(yc * yc, axis=-1, keepdims=True)
    return yc * lax.rsqrt(var + LN_EPS) * g + b


def _rows(x, blocks, size):
    parts = [x[b * size:(b + 1) * size, :] for b in blocks]
    return parts[0] if len(parts) == 1 else jnp.concatenate(parts, axis=0)


def _gla_chunk(q_s, k_s, v_s, lf_s, g_s, acc_s, o_ref, st_ref, d, start, slot, reverse):
    c = GLA_CHUNK
    t8 = SUBLANES
    nt_dims = (((1,), (1,)), ((), ()))

    def tt(ref, i):
        return ref[d, pl.ds(start + i, t8, stride=t8), :]

    lf = [tt(lf_s, i) for i in range(t8)]
    qt = [tt(q_s, i) for i in range(t8)]
    kt = [tt(k_s, i) for i in range(t8)]
    vt = [tt(v_s, i) for i in range(t8)]

    gi = [None] * t8
    prev = None
    for i in (range(t8 - 1, -1, -1) if reverse else range(t8)):
        gi[i] = lf[i] if prev is None else prev + lf[i]
        prev = gi[i]
    tot = prev
    sub = lax.broadcasted_iota(jnp.int32, (t8, LANES), 0)
    incl = tot
    s = 1
    while s < t8:
        if reverse:
            incl = incl + jnp.where(sub + s < t8, pltpu.roll(incl, t8 - s, 0), 0.0)
        else:
            incl = incl + jnp.where(sub >= s, pltpu.roll(incl, s, 0), 0.0)
        s *= 2
    excl = incl - tot
    for i in range(t8):
        g_s[slot, pl.ds(i, t8, stride=t8), :] = gi[i] + excl

    acc = []
    for i in range(t8):
        a_i = jnp.sum(qt[i] * kt[i], axis=-1, keepdims=True) * vt[i]
        for r in (range(i + 1, t8) if reverse else range(i)):
            w = qt[i] * kt[r] * jnp.exp2(gi[i] - gi[r])
            a_i = a_i + jnp.sum(w, axis=-1, keepdims=True) * vt[r]
        acc.append(a_i)

    sl = pl.ds(start, c)
    q = q_s[d, sl, :]
    kk = k_s[d, sl, :]
    v = v_s[d, sl, :]
    g = g_s[slot]
    vb = v.astype(BF16)
    st = st_ref[d]

    o = lax.dot_general((q * jnp.exp2(g)).astype(BF16), st.astype(BF16), nt_dims,
                        preferred_element_type=F32)

    half = c // 2
    qd_l, kd_l, v_l, q_tiles = [], [], [], []
    b = 2 * t8
    while b <= c:
        h = b // 2
        nblk = c // b
        lo = [2 * m for m in range(nblk)]
        hi = [2 * m + 1 for m in range(nblk)]
        q_half, k_half = (lo, hi) if reverse else (hi, lo)
        refs = [g[m * b + h:m * b + h + 1, :] if reverse else g[m * b + h - 1:m * b + h, :]
                for m in range(nblk)]
        gref = jnp.concatenate([jnp.broadcast_to(r, (h, LANES)) for r in refs], axis=0) \
            if nblk > 1 else jnp.broadcast_to(refs[0], (h, LANES))
        qd_l.append(_rows(q, q_half, h) * jnp.exp2(_rows(g, q_half, h) - gref))
        kd_l.append(_rows(kk, k_half, h) * jnp.exp2(gref - _rows(g, k_half, h)))
        v_l.append(_rows(v, k_half, h))
        q_tiles.append([hb * (h // t8) + t for hb in q_half for t in range(h // t8)])
        b *= 2
    nlev = len(qd_l)
    p = lax.dot_general(jnp.concatenate(qd_l, axis=0).astype(BF16),
                        jnp.concatenate(kd_l, axis=0).astype(BF16), nt_dims,
                        preferred_element_type=F32)
    ii = lax.broadcasted_iota(jnp.int32, p.shape, 0)
    jj = lax.broadcasted_iota(jnp.int32, p.shape, 1)
    keep = None
    for lv in range(nlev):
        h = t8 << lv
        m = (ii // half == lv) & (jj // half == lv) & ((ii // h) == (jj // h))
        keep = m if keep is None else keep | m
    p = jnp.where(keep, p, 0.0)
    res = jnp.dot(p.astype(BF16), jnp.concatenate(v_l, axis=0).astype(BF16),
                  preferred_element_type=F32)
    contrib = [None] * (c // t8)
    for lv in range(nlev):
        for n_, tile in enumerate(q_tiles[lv]):
            piece = res[lv * half + n_ * t8:lv * half + (n_ + 1) * t8, :]
            contrib[tile] = piece if contrib[tile] is None else contrib[tile] + piece
    zero = jnp.zeros((t8, LANES), F32)
    o = o + jnp.concatenate([zero if p_ is None else p_ for p_ in contrib], axis=0)

    gl = g[0:1, :] if reverse else g[c - 1:c, :]
    kd = (kk * jnp.exp2(gl - g)).astype(BF16)
    upd = lax.dot_general(vb, kd, (((0,), (0,)), ((), ())), preferred_element_type=F32)
    st_ref[d] = st * jnp.exp2(gl) + upd

    for i in range(t8):
        acc_s[slot, pl.ds(i, t8, stride=t8), :] = acc[i]
    o_ref[sl, :] = (o + acc_s[slot]).astype(o_ref.dtype)


def _gla_kernel(layer, lbl_ref, qf_ref, ff_ref, vf_ref, qb_ref, fb_ref, vb_ref,
                of_ref, ob_ref, st_ref, q_s, k_s, v_s, lf_s, g_s, acc_s):
    tb = GLA_TBLOCK
    nc = tb // GLA_CHUNK

    @pl.when(pl.program_id(2) == 0)
    def _():
        st_ref[...] = jnp.zeros_like(st_ref)

    if layer > 0:
        lg = lbl_ref[...]
        e = jnp.exp(lg - jnp.max(lg, axis=0, keepdims=True))
        sm = e / jnp.sum(e, axis=0, keepdims=True)
        lb = sm[1]
        for l in range(2, layer + 1):
            lb = lb + sm[l]

    for d, (q_ref, f_ref, v_ref) in enumerate(((qf_ref, ff_ref, vf_ref),
                                               (qb_ref, fb_ref, vb_ref))):
        qx = q_ref[...].astype(F32)
        fx = f_ref[...]
        v_s[d] = v_ref[...].astype(F32)
        t = jnp.exp(-jnp.abs(fx))
        r = 1.0 / (1.0 + t)
        logsig = jnp.minimum(fx, 0.0) - jnp.log(1.0 + t)
        sig_neg = jnp.where(fx >= 0.0, t * r, r)
        if layer > 0:
            lbd = lb[d:d + 1, :]
            a = jnp.log(lbd)
            bb = jnp.log1p(-lbd) + logsig
            lf = jnp.maximum(a, bb) + jnp.log(1.0 + jnp.exp(-jnp.abs(a - bb)))
            kk = (1.0 - lbd) * sig_neg
        else:
            lf = logsig
            kk = sig_neg
        q_s[d] = _silu(qx)
        k_s[d] = kk
        lf_s[d] = lf * LOG2_E

    def body(ci, carry):
        for u in range(GLA_UNROLL):
            cf = ci * GLA_UNROLL + u
            sf = pl.multiple_of(cf * GLA_CHUNK, GLA_CHUNK)
            sb = pl.multiple_of((nc - 1 - cf) * GLA_CHUNK, GLA_CHUNK)
            _gla_chunk(q_s, k_s, v_s, lf_s, g_s, acc_s, of_ref, st_ref, 0, sf, 2 * u, False)
            _gla_chunk(q_s, k_s, v_s, lf_s, g_s, acc_s, ob_ref, st_ref, 1, sb, 2 * u + 1, True)
        return carry

    lax.fori_loop(0, nc // GLA_UNROLL, body, 0)


def _gla(proj_a, proj_f, lb_logits, layer, batch, seq):
    n = proj_a.shape[0]
    tb = GLA_TBLOCK
    nb = seq // tb
    h8 = HG_HEADS

    def spec(stream, rev):
        if rev:
            return pl.BlockSpec((tb, LANES), lambda b, h, c: (b * nb + nb - 1 - c, stream * h8 + h))
        return pl.BlockSpec((tb, LANES), lambda b, h, c: (b * nb + c, stream * h8 + h))

    o_f_spec = pl.BlockSpec((tb, LANES), lambda b, h, c: (b * nb + c, h))
    o_b_spec = pl.BlockSpec((tb, LANES), lambda b, h, c: (b * nb + nb - 1 - c, h))
    return pl.pallas_call(
        functools.partial(_gla_kernel, layer),
        grid=(batch, h8, nb),
        in_specs=[pl.BlockSpec((DEPTH, 2, LANES), lambda b, h, c: (0, 0, h)),
                  spec(0, False), spec(0, False), spec(1, False),
                  spec(0, True), spec(1, True), spec(1, True)],
        out_specs=[o_f_spec, o_b_spec],
        out_shape=[jax.ShapeDtypeStruct((n, D_MODEL), BF16)] * 2,
        scratch_shapes=[pltpu.VMEM((2, HG_DK, HG_DK), F32)]
        + [pltpu.VMEM((2, tb, LANES), F32)] * 4
        + [pltpu.VMEM((2 * GLA_UNROLL, GLA_CHUNK, LANES), F32)] * 2,
        compiler_params=_cparams("arbitrary", "arbitrary", "arbitrary"),
        name="gla",
    )(lb_logits, proj_a, proj_f, proj_a, proj_a, proj_f, proj_a)


def _hgrn_out_kernel(of_ref, ob_ref, gate_ref, nw_ref, w_ref, x_ref, g_ref, b_ref,
                     rw_ref, rb_ref, tri_ref,
                     o_ref, op_ref, route_ref, cnt_ref, carry_ref):
    o = of_ref[...].astype(F32) + ob_ref[...].astype(F32)
    parts = []
    for h in range(HG_HEADS):
        oh = o[:, h * LANES:(h + 1) * LANES]
        ms = jnp.mean(oh * oh, axis=-1, keepdims=True)
        parts.append(oh * lax.rsqrt(ms + RMS_EPS))
    gate = gate_ref[...].astype(F32)
    y = jnp.concatenate(parts, axis=-1) * nw_ref[...] * _silu(gate)
    mix = jnp.dot(y.astype(BF16), w_ref[...], preferred_element_type=F32)
    out = _layer_norm_rows(DN_ALPHA * x_ref[...] + mix, g_ref[...], b_ref[...])
    o_ref[...] = out
    op_ref[...] = _pack_rows(out)
    _route_tile(out, rw_ref, rb_ref, tri_ref, route_ref, cnt_ref, carry_ref)


def _hgrn_out(o_f, o_b, proj_a, norm_w, w_out, x, g, b, router_ops):
    n = x.shape[0]
    tm = ROW_TILE
    row = pl.BlockSpec((tm, D_MODEL), lambda i: (i, 0))
    vec = pl.BlockSpec((1, D_MODEL), lambda i: (0, 0))
    r_in, r_out, r_scratch = _router_specs()
    xo, xpk, route, cnt = pl.pallas_call(
        _hgrn_out_kernel,
        grid=(n // tm,),
        in_specs=[row, row, pl.BlockSpec((tm, D_MODEL), lambda i: (i, 2)), vec,
                  pl.BlockSpec((D_MODEL, D_MODEL), lambda i: (0, 0)), row, vec, vec] + r_in,
        out_specs=[row, pl.BlockSpec((tm, PACKED), lambda i: (i, 0))] + r_out,
        out_shape=[jax.ShapeDtypeStruct((n, D_MODEL), F32),
                   jax.ShapeDtypeStruct((n, PACKED), jnp.int32)] + _router_out_shapes(n),
        scratch_shapes=r_scratch,
        compiler_params=_cparams("arbitrary"),
        name="hgrn_out",
    )(o_f, o_b, proj_a, jnp.tile(norm_w, HG_HEADS).reshape(1, D_MODEL), w_out, x,
      g.reshape(1, D_MODEL), b.reshape(1, D_MODEL), *router_ops)
    return xo, xpk, route, _expert_counts(cnt)


def _conv_out_kernel(tiles_per_seq, bg_ref, cg_ref, h_ref, cgp_ref, hp_ref, cgn_ref, hn_ref,
                     cw_ref, w_ref, x_ref, g_ref, b_ref, rw_ref, rb_ref, tri_ref,
                     o_ref, op_ref, route_ref, cnt_ref, carry_ref):
    i = pl.program_id(0)
    tm = cg_ref.shape[0]
    u = cg_ref[...].astype(F32) * h_ref[...].astype(F32)
    first = (i % tiles_per_seq) == 0
    last = (i % tiles_per_seq) == tiles_per_seq - 1
    hr = cgp_ref.shape[0]
    u_halo_prev = cgp_ref[...].astype(F32) * hp_ref[...].astype(F32)
    u_halo_next = cgn_ref[...].astype(F32) * hn_ref[...].astype(F32)
    u_prev_row = jnp.where(first, 0.0, u_halo_prev[hr - 1:hr, :])
    u_next_row = jnp.where(last, 0.0, u_halo_next[0:1, :])
    rows = lax.broadcasted_iota(jnp.int32, u.shape, 0)
    u_prev = jnp.where(rows == 0, u_prev_row, pltpu.roll(u, 1, 0))
    u_next = jnp.where(rows == tm - 1, u_next_row, pltpu.roll(u, tm - 1, 0))
    cw = cw_ref[...]
    y = u_prev * cw[0:1, :] + u * cw[1:2, :] + u_next * cw[2:3, :]
    mix = jnp.dot((bg_ref[...].astype(F32) * y).astype(BF16), w_ref[...],
                  preferred_element_type=F32)
    out = _layer_norm_rows(DN_ALPHA * x_ref[...] + mix, g_ref[...], b_ref[...])
    o_ref[...] = out
    op_ref[...] = _pack_rows(out)
    _route_tile(out, rw_ref, rb_ref, tri_ref, route_ref, cnt_ref, carry_ref)


def _conv_out(proj, conv_w, w_out, x, g, b, seq, router_ops):
    n = x.shape[0]
    tm = ROW_TILE
    halo = BF16_SUBLANES
    rh = tm // halo
    nblk = n // halo
    row = pl.BlockSpec((tm, D_MODEL), lambda i: (i, 0))
    vec = pl.BlockSpec((1, D_MODEL), lambda i: (0, 0))

    def main(stream):
        return pl.BlockSpec((tm, D_MODEL), lambda i: (i, stream))

    def prev(stream):
        return pl.BlockSpec((halo, D_MODEL), lambda i: (jnp.maximum(i * rh - 1, 0), stream))

    def nxt(stream):
        return pl.BlockSpec((halo, D_MODEL),
                            lambda i: (jnp.minimum((i + 1) * rh, nblk - 1), stream))

    r_in, r_out, r_scratch = _router_specs()
    xo, xpk, route, cnt = pl.pallas_call(
        functools.partial(_conv_out_kernel, seq // tm),
        grid=(n // tm,),
        in_specs=[main(0), main(1), main(2), prev(1), prev(2), nxt(1), nxt(2),
                  pl.BlockSpec((3, D_MODEL), lambda i: (0, 0)),
                  pl.BlockSpec((D_MODEL, D_MODEL), lambda i: (0, 0)), row, vec, vec] + r_in,
        out_specs=[row, pl.BlockSpec((tm, PACKED), lambda i: (i, 0))] + r_out,
        out_shape=[jax.ShapeDtypeStruct((n, D_MODEL), F32),
                   jax.ShapeDtypeStruct((n, PACKED), jnp.int32)] + _router_out_shapes(n),
        scratch_shapes=r_scratch,
        compiler_params=_cparams("arbitrary"),
        name="conv_out",
    )(proj, proj, proj, proj, proj, proj, proj, conv_w, w_out, x,
      g.reshape(1, D_MODEL), b.reshape(1, D_MODEL), *router_ops)
    return xo, xpk, route, _expert_counts(cnt)


def _route_tile(x, whl_ref, b_ref, tri_ref, o_ref, cnt_ref, carry_ref):
    @pl.when(pl.program_id(0) == 0)
    def _():
        carry_ref[...] = jnp.zeros_like(carry_ref)

    xh = x.astype(BF16)
    xl = (x - xh.astype(F32)).astype(BF16)
    both = jnp.dot(xh, whl_ref[...], preferred_element_type=F32)
    logits = (both[:, :LANES] + both[:, LANES:]
              + jnp.dot(xl, whl_ref[:, :LANES], preferred_element_type=F32)) + b_ref[...]
    eg = MOE_EXPERTS_PER_GROUP
    lt = logits.T
    grp = lt[0:SUBLANES]
    exp_t = [lt[SUBLANES + eg * g:SUBLANES + eg * (g + 1)] for g in range(MOE_GROUPS)]
    row = lax.broadcasted_iota(jnp.int32, grp.shape, 0).astype(F32)
    gmax = jnp.max(grp, axis=0, keepdims=True)
    gsum = jnp.sum(jnp.exp(grp - gmax), axis=0, keepdims=True)
    p_group = 1.0 / gsum
    g_sel = jnp.min(jnp.where(grp == gmax, row, 99.0), axis=0, keepdims=True)
    el = exp_t[MOE_GROUPS - 1]
    for g in range(MOE_GROUPS - 2, -1, -1):
        el = jnp.where(g_sel == g, exp_t[g], el)
    t1 = jnp.max(el, axis=0, keepdims=True)
    i1 = jnp.min(jnp.where(el == t1, row, 99.0), axis=0, keepdims=True)
    el2 = jnp.where(row == i1, NEG_BIG, el)
    t2 = jnp.max(el2, axis=0, keepdims=True)
    i2 = jnp.min(jnp.where(el2 == t2, row, 99.0), axis=0, keepdims=True)
    z = jnp.exp(t2 - t1)
    g1 = p_group / (1.0 + z)
    g2 = g1 * z

    is1 = row == i1
    is2 = row == i2
    in_g = [g_sel == g for g in range(MOE_GROUPS)]
    onehot = jnp.concatenate([jnp.where(in_g[g] & (is1 | is2), 1.0, 0.0)
                              for g in range(MOE_GROUPS)], axis=0)
    prefix = jnp.dot(onehot.astype(BF16), tri_ref[...], preferred_element_type=F32)
    before = prefix + carry_ref[...]
    r1 = jnp.zeros_like(g1)
    r2 = jnp.zeros_like(g1)
    for g in range(MOE_GROUPS):
        bg = before[eg * g:eg * (g + 1)]
        r1 = r1 + jnp.sum(jnp.where(in_g[g] & is1, bg, 0.0), axis=0, keepdims=True)
        r2 = r2 + jnp.sum(jnp.where(in_g[g] & is2, bg, 0.0), axis=0, keepdims=True)
    e1 = g_sel * eg + i1
    e2 = g_sel * eg + i2
    out = jnp.zeros_like(row)
    for r, val in enumerate((g1, g2, e1, e2, r1, r2)):
        out = jnp.where(row == r, val, out)
    o_ref[...] = out
    carry_ref[...] = carry_ref[...] + jnp.sum(onehot, axis=1, keepdims=True)
    cnt_ref[...] = carry_ref[...]


def _router_operands(w_group, b_group, w_expert, b_expert):
    gpad = SUBLANES - MOE_GROUPS
    pad = LANES - SUBLANES - MOE_EXPERTS
    w = jnp.concatenate([w_group, jnp.zeros((D_MODEL, gpad), F32), w_expert,
                         jnp.zeros((D_MODEL, pad), F32)], axis=1)
    wh = w.astype(BF16)
    wl = (w - wh.astype(F32)).astype(BF16)
    b = jnp.concatenate([b_group, jnp.full((gpad,), NEG_BIG, F32), b_expert,
                         jnp.zeros((pad,), F32)]).reshape(1, LANES)
    r = jnp.arange(ROW_TILE, dtype=jnp.int32)
    tri = (r[:, None] < r[None, :]).astype(BF16)
    return jnp.concatenate([wh, wl], axis=1), b, tri


def _router_specs():
    in_specs = [pl.BlockSpec((D_MODEL, 2 * LANES), lambda i: (0, 0)),
                pl.BlockSpec((1, LANES), lambda i: (0, 0)),
                pl.BlockSpec((ROW_TILE, ROW_TILE), lambda i: (0, 0))]
    out_specs = [pl.BlockSpec((SUBLANES, ROW_TILE), lambda i: (0, i)),
                 pl.BlockSpec((MOE_EXPERTS, ROW_TILE), lambda i: (0, 0))]
    scratch = [pltpu.VMEM((MOE_EXPERTS, ROW_TILE), F32)]
    return in_specs, out_specs, scratch


def _router_out_shapes(n):
    return [jax.ShapeDtypeStruct((SUBLANES, n), F32),
            jax.ShapeDtypeStruct((MOE_EXPERTS, ROW_TILE), F32)]


def _expert_counts(cnt):
    return cnt[:, 0].astype(jnp.int32)


DISPATCH_TOKENS = 512
DISPATCH_BUFS = 3
WEIGHT_SLOTS = 3


def _dest_kernel(ps_ref, route_ref, o_ref):
    e = route_ref[MOE_TOPK:2 * MOE_TOPK, :]
    rank = route_ref[2 * MOE_TOPK:3 * MOE_TOPK, :]
    base = jnp.zeros_like(e)
    for x in range(MOE_EXPERTS):
        base = jnp.where(e == x, ps_ref[x].astype(F32), base)
    o_ref[...] = (base + rank).astype(jnp.int32)


def _moe_plan(route, counts, n):
    nk = n * MOE_TOPK
    n_rows = -(-nk // MOE_BLOCK) * MOE_BLOCK + MOE_EXPERTS * MOE_BLOCK
    n_blocks = n_rows // MOE_BLOCK
    padded = (counts + MOE_BLOCK - 1) // MOE_BLOCK * MOE_BLOCK
    pad_end = jnp.cumsum(padded)
    pad_start = pad_end - padded
    dest = pl.pallas_call(
        _dest_kernel,
        grid_spec=pltpu.PrefetchScalarGridSpec(
            num_scalar_prefetch=1, grid=(1,),
            in_specs=[pl.BlockSpec((SUBLANES, n), lambda i, ps: (0, 0))],
            out_specs=pl.BlockSpec((MOE_TOPK, n), lambda i, ps: (0, 0))),
        out_shape=jax.ShapeDtypeStruct((MOE_TOPK, n), jnp.int32),
        compiler_params=_cparams("arbitrary"),
        name="dest",
    )(pad_start.astype(jnp.int32), route).reshape(nk)
    block_start = jnp.arange(n_blocks, dtype=jnp.int32) * MOE_BLOCK
    block_expert = jnp.minimum(jnp.sum(block_start[:, None] >= pad_end[None, :], axis=1),
                               MOE_EXPERTS - 1).astype(jnp.int32)
    nact = (pad_end[-1:] // MOE_BLOCK).astype(jnp.int32)
    fill_start = (pad_start + counts).astype(jnp.int32)
    return dest, block_expert, nact, fill_start, pad_end.astype(jnp.int32), n_rows


def _dispatch_kernel(dest_ref, fs_ref, fe_ref, nact_ref, x_hbm, xs_hbm, xbuf, zbuf, lsem, sem,
                     zsem):
    i = pl.program_id(0)
    nsteps = pl.num_programs(0)
    slot = i % DISPATCH_BUFS
    ct = DISPATCH_TOKENS
    gt = ct // SUBLANES
    n_blocks = xs_hbm.shape[0] // MOE_BLOCK

    def load(step, sl):
        start = pl.multiple_of(step * gt, gt)
        return pltpu.make_async_copy(x_hbm.at[pl.ds(start, gt)], xbuf.at[sl], lsem.at[sl])

    def wait_step(sl):
        for _ in range(MOE_TOPK):
            pltpu.make_async_copy(x_hbm.at[pl.ds(0, gt)], xbuf.at[sl], sem.at[sl]).wait()

    @pl.when(i == 0)
    def _():
        for s in range(DISPATCH_BUFS - 1):
            load(s, s).start()

        zbuf[...] = jnp.zeros_like(zbuf)

        def block_copy(b):
            start = pl.multiple_of(b * MOE_BLOCK, MOE_BLOCK)
            return pltpu.make_async_copy(zbuf, xs_hbm.at[pl.ds(start, MOE_BLOCK)], zsem.at[0])

        def fill(wait):
            def per_expert(e, c):
                @pl.when(fs_ref[e] < fe_ref[e])
                def _():
                    cp = block_copy(fe_ref[e] // MOE_BLOCK - 1)
                    cp.wait() if wait else cp.start()
                return c
            lax.fori_loop(0, MOE_EXPERTS, per_expert, 0)

            def per_block(b, c):
                cp = block_copy(b)
                cp.wait() if wait else cp.start()
                return c
            lax.fori_loop(nact_ref[0], n_blocks, per_block, 0)

        fill(False)
        fill(True)

    load(i, slot).wait()

    n_tok = nsteps * ct

    def body(g, c):
        base = i * ct + g * SUBLANES
        for t in range(SUBLANES):
            for k in range(MOE_TOPK):
                d = dest_ref[k * n_tok + base + t]
                pltpu.make_async_copy(xbuf.at[slot, g, pl.ds(t, 1)], xs_hbm.at[pl.ds(d, 1)],
                                      sem.at[slot]).start(priority=k)
        return c
    lax.fori_loop(0, gt, body, 0, unroll=2)

    @pl.when(i >= 1)
    def _():
        wait_step((i - 1) % DISPATCH_BUFS)

    @pl.when(i + DISPATCH_BUFS - 1 < nsteps)
    def _():
        load(i + DISPATCH_BUFS - 1, (i + DISPATCH_BUFS - 1) % DISPATCH_BUFS).start()

    @pl.when(i == nsteps - 1)
    def _():
        wait_step(slot)


def _dispatch(x, dest, fill_start, fill_end, nact, n_rows):
    n, width = x.shape
    grid_spec = pltpu.PrefetchScalarGridSpec(
        num_scalar_prefetch=4,
        grid=(n // DISPATCH_TOKENS,),
        in_specs=[pl.BlockSpec(memory_space=pl.ANY)],
        out_specs=pl.BlockSpec(memory_space=pl.ANY),
        scratch_shapes=[pltpu.VMEM((DISPATCH_BUFS, DISPATCH_TOKENS // SUBLANES, SUBLANES, width),
                                   x.dtype),
                        pltpu.VMEM((MOE_BLOCK, width), x.dtype),
                        pltpu.SemaphoreType.DMA((DISPATCH_BUFS,)),
                        pltpu.SemaphoreType.DMA((DISPATCH_BUFS,)),
                        pltpu.SemaphoreType.DMA((2,))],
    )
    return pl.pallas_call(
        _dispatch_kernel,
        grid_spec=grid_spec,
        out_shape=jax.ShapeDtypeStruct((n_rows, width), x.dtype),
        compiler_params=_cparams("arbitrary"),
        name="dispatch",
    )(dest, fill_start, fill_end, nact, x.reshape(n // SUBLANES, SUBLANES, width))


def _moe_ffn_kernel(e0, be_ref, nact_ref, ord_ref, nxt_ref, nxt2_ref, xs_ref, wup_hbm, wdn_hbm,
                    ys_ref, wup_f, wdn_f, wup_bf, wdn_bf, wsem):
    i = pl.program_id(0)
    active = i < nact_ref[0]
    new_expert = (i == 0) | (be_ref[i] != be_ref[jnp.maximum(i - 1, 0)])

    def fetch(e, sl):
        return (pltpu.make_async_copy(wup_hbm.at[e0 + e], wup_f.at[sl], wsem.at[0, sl]),
                pltpu.make_async_copy(wdn_hbm.at[e0 + e], wdn_f.at[sl], wsem.at[1, sl]))

    @pl.when(active & (i == 0))
    def _():
        for cp in fetch(be_ref[0], 0):
            cp.start()

        @pl.when(nxt_ref[be_ref[0]] >= 0)
        def _():
            for cp in fetch(nxt_ref[be_ref[0]], 1):
                cp.start()

    @pl.when(active & new_expert)
    def _():
        k = ord_ref[i]
        sl = k % WEIGHT_SLOTS
        for cp in fetch(be_ref[i], sl):
            cp.wait()

        @pl.when(nxt2_ref[be_ref[i]] >= 0)
        def _():
            for cp in fetch(nxt2_ref[be_ref[i]], (k + 2) % WEIGHT_SLOTS):
                cp.start()

        wup_bf[...] = wup_f[sl].astype(BF16)
        wdn_bf[...] = wdn_f[sl].astype(BF16)

    @pl.when(active)
    def _():
        x_lo, x_hi = _unpack_rows(xs_ref[...])
        xb = jnp.concatenate([x_lo.astype(BF16), x_hi.astype(BF16)], axis=1)
        hcat = jnp.dot(xb, wup_bf[...], preferred_element_type=F32)
        hg = hcat[:, :MOE_D_EXPERT]
        hu = hcat[:, MOE_D_EXPERT:]
        act = (_silu(hg) * hu).astype(BF16)
        ys_ref[...] = _pack_rows(jnp.dot(act, wdn_bf[...], preferred_element_type=F32))

    @pl.when(jnp.logical_not(active))
    def _():
        ys_ref[...] = jnp.zeros_like(ys_ref)


def _moe_ffn(xs, block_expert, nact, counts, w_up, w_down, layer):
    n_rows = xs.shape[0]
    n_blocks = n_rows // MOE_BLOCK
    first = jnp.concatenate([jnp.ones((1,), jnp.int32),
                             (block_expert[1:] != block_expert[:-1]).astype(jnp.int32)])
    ordinal = (jnp.cumsum(first) - 1).astype(jnp.int32)
    ids = jnp.arange(MOE_EXPERTS, dtype=jnp.int32)
    has_rows = counts[None, :] > 0
    later = (ids[None, :] > ids[:, None]) & has_rows
    nxt = jnp.min(jnp.where(later, ids[None, :], MOE_EXPERTS), axis=1)
    later2 = (ids[None, :] > nxt[:, None]) & has_rows
    nxt2 = jnp.min(jnp.where(later2, ids[None, :], MOE_EXPERTS), axis=1)
    next_expert = jnp.where(nxt == MOE_EXPERTS, -1, nxt).astype(jnp.int32)
    next2_expert = jnp.where(nxt2 == MOE_EXPERTS, -1, nxt2).astype(jnp.int32)
    row = pl.BlockSpec((MOE_BLOCK, PACKED), lambda i, *_: (i, 0))
    grid_spec = pltpu.PrefetchScalarGridSpec(
        num_scalar_prefetch=5,
        grid=(n_blocks,),
        in_specs=[row, pl.BlockSpec(memory_space=pl.ANY), pl.BlockSpec(memory_space=pl.ANY)],
        out_specs=row,
        scratch_shapes=[pltpu.VMEM((WEIGHT_SLOTS, D_MODEL, 2 * MOE_D_EXPERT), F32),
                        pltpu.VMEM((WEIGHT_SLOTS, MOE_D_EXPERT, D_MODEL), F32),
                        pltpu.VMEM((D_MODEL, 2 * MOE_D_EXPERT), BF16),
                        pltpu.VMEM((MOE_D_EXPERT, D_MODEL), BF16),
                        pltpu.SemaphoreType.DMA((2, WEIGHT_SLOTS))],
    )
    return pl.pallas_call(
        functools.partial(_moe_ffn_kernel, layer * MOE_EXPERTS),
        grid_spec=grid_spec,
        out_shape=jax.ShapeDtypeStruct((n_rows, PACKED), jnp.int32),
        compiler_params=_cparams("arbitrary"),
        name="moe_ffn",
    )(block_expert, nact, ordinal, next_expert, next2_expert, xs, w_up, w_down)


def _combine_ln_kernel(dest_ref, x_ref, route_ref, ys_hbm, g_ref, b_ref, o_ref, ob_ref, ybuf,
                       sem):
    i = pl.program_id(0)
    nsteps = pl.num_programs(0)
    slot = i % 2
    tm = x_ref.shape[0]

    def start_gather(step, sl):
        def body(g, c):
            base = step * tm + g * SUBLANES
            for t in range(SUBLANES):
                for k in range(MOE_TOPK):
                    d = dest_ref[k * (nsteps * tm) + base + t]
                    pltpu.make_async_copy(ys_hbm.at[pl.ds(d, 1)], ybuf.at[sl, k, g, pl.ds(t, 1)],
                                          sem.at[sl]).start(priority=k)
            return c
        lax.fori_loop(0, tm // SUBLANES, body, 0, unroll=2)

    @pl.when(i == 0)
    def _():
        start_gather(0, 0)

    @pl.when(i + 1 < nsteps)
    def _():
        start_gather(i + 1, 1 - slot)

    for k in range(MOE_TOPK):
        pltpu.make_async_copy(ybuf.at[1 - slot, k], ybuf.at[slot, k], sem.at[slot]).wait()
    route = route_ref[...]
    y0_lo, y0_hi = _unpack_rows(ybuf[slot, 0].reshape(tm, PACKED))
    y1_lo, y1_hi = _unpack_rows(ybuf[slot, 1].reshape(tm, PACKED))
    g0 = route[:, 0:1]
    g1 = route[:, 1:2]
    ffn = jnp.concatenate([g0 * y0_lo + g1 * y1_lo, g0 * y0_hi + g1 * y1_hi], axis=1)
    out = _layer_norm_rows(DN_ALPHA * x_ref[...] + ffn, g_ref[...], b_ref[...])
    o_ref[...] = out
    ob_ref[...] = out.astype(BF16)


def _combine_ln(x, route, ys, dest, g, b):
    n = x.shape[0]
    tm = ROW_TILE
    vec = pl.BlockSpec((1, D_MODEL), lambda i, d: (0, 0))
    row = pl.BlockSpec((tm, D_MODEL), lambda i, d: (i, 0))
    grid_spec = pltpu.PrefetchScalarGridSpec(
        num_scalar_prefetch=1,
        grid=(n // tm,),
        in_specs=[pl.BlockSpec((tm, D_MODEL), lambda i, d: (i, 0)),
                  pl.BlockSpec((tm, MOE_TOPK), lambda i, d: (i, 0)),
                  pl.BlockSpec(memory_space=pl.ANY), vec, vec],
        out_specs=[row, row],
        scratch_shapes=[pltpu.VMEM((2, MOE_TOPK, tm // SUBLANES, SUBLANES, PACKED), jnp.int32),
                        pltpu.SemaphoreType.DMA((2,))],
    )
    return pl.pallas_call(
        _combine_ln_kernel,
        grid_spec=grid_spec,
        out_shape=[jax.ShapeDtypeStruct((n, D_MODEL), F32),
                   jax.ShapeDtypeStruct((n, D_MODEL), BF16)],
        compiler_params=_cparams("arbitrary"),
        name="combine_ln",
    )(dest, x, route[0:MOE_TOPK].T, ys, g.reshape(1, D_MODEL), b.reshape(1, D_MODEL))


def kernel(x, hg_w_in, hg_lb_logits, hg_norm_w, hg_w_out, cv_w_in, cv_w, cv_w_out, ln_g, ln_b,
           moe_w_group, moe_b_group, moe_w_expert, moe_b_expert, moe_w_up, moe_w_down):
    batch, seq, d = x.shape
    n = batch * seq
    xf = x.reshape(n, d)
    w_up_all = moe_w_up.reshape(DEPTH * MOE_EXPERTS, D_MODEL, 2 * MOE_D_EXPERT)
    w_down_all = moe_w_down.reshape(DEPTH * MOE_EXPERTS, MOE_D_EXPERT, D_MODEL)
    xin = xf
    for layer in range(DEPTH):
        j = layer // 2
        router_ops = _router_operands(moe_w_group[layer], moe_b_group[layer],
                                      moe_w_expert[layer], moe_b_expert[layer])
        if layer % 2 == 0:
            w = hg_w_in[j]
            w_a = jnp.concatenate([w[:, :D_MODEL], w[:, 3 * D_MODEL:]], axis=1).astype(BF16)
            w_f = w[:, D_MODEL:3 * D_MODEL].astype(BF16)
            proj_a = _matmul(xin, w_a, BF16)
            proj_f = _matmul(xin, w_f, F32)
            o_f, o_b = _gla(proj_a, proj_f, hg_lb_logits, layer, batch, seq)
            xf, xpk, route, counts = _hgrn_out(
                o_f, o_b, proj_a, hg_norm_w[j], hg_w_out[j].astype(BF16), xf,
                ln_g[layer, 0], ln_b[layer, 0], router_ops)
        else:
            proj = _matmul(xin, cv_w_in[j].astype(BF16), BF16)
            xf, xpk, route, counts = _conv_out(
                proj, cv_w[j], cv_w_out[j].astype(BF16), xf,
                ln_g[layer, 0], ln_b[layer, 0], seq, router_ops)
        dest, block_expert, nact, fill_start, fill_end, n_rows = _moe_plan(route, counts, n)
        xs = _dispatch(xpk, dest, fill_start, fill_end, nact, n_rows)
        ys = _moe_ffn(xs, block_expert, nact, counts, w_up_all, w_down_all, layer)
        xf, xin = _combine_ln(xf, route, ys, dest, ln_g[layer, 1], ln_b[layer, 1])
    return xf.reshape(batch, seq, d)
```

```python
import functools

import jax
import jax.numpy as jnp
from jax import lax
from jax.experimental import pallas as pl
from jax.experimental.pallas import tpu as pltpu

D_MODEL = 1024
DEPTH = 4
HG_DK = 128
HG_HEADS = D_MODEL // HG_DK
MOE_GROUPS = 4
MOE_EXPERTS_PER_GROUP = 8
MOE_EXPERTS = MOE_GROUPS * MOE_EXPERTS_PER_GROUP
MOE_TOPK = 2
MOE_D_EXPERT = D_MODEL // 2
MOE_BLOCK = 512
DN_ALPHA = (2.0 * DEPTH) ** 0.25
LN_EPS = 1e-5
RMS_EPS = 1e-6

LANES = 128
SUBLANES = 8
BF16_SUBLANES = 16
VMEM_LIMIT = 48 * 1024 * 1024
GLA_CHUNK = 64
GLA_TBLOCK = 4096
GLA_UNROLL = 2
ROW_TILE = 512
COMBINE_TILE = 1024
NEG_BIG = -1e30
LOG2_E = 1.4426950408889634

BF16 = jnp.bfloat16
F32 = jnp.float32


def _cparams(*sem):
    return pltpu.CompilerParams(dimension_semantics=sem, vmem_limit_bytes=VMEM_LIMIT)


def _mm_kernel(x_ref, w_ref, o_ref):
    o_ref[...] = jnp.dot(x_ref[...].astype(BF16), w_ref[...],
                         preferred_element_type=F32).astype(o_ref.dtype)


def _matmul(x, w, out_dtype, tm=2048, tn=1024):
    n, k = x.shape
    nn = w.shape[1]
    return pl.pallas_call(
        _mm_kernel,
        grid=(nn // tn, n // tm),
        in_specs=[pl.BlockSpec((tm, k), lambda j, i: (i, 0)),
                  pl.BlockSpec((k, tn), lambda j, i: (0, j))],
        out_specs=pl.BlockSpec((tm, tn), lambda j, i: (i, j)),
        out_shape=jax.ShapeDtypeStruct((n, nn), out_dtype),
        compiler_params=_cparams("arbitrary", "arbitrary"),
        name="in_proj",
    )(x, w)


PACKED = D_MODEL // 2


def _pack_rows(x):
    lo = pltpu.bitcast(x[:, :PACKED].astype(BF16).astype(F32), jnp.int32)
    hi = pltpu.bitcast(x[:, PACKED:].astype(BF16).astype(F32), jnp.int32)
    return lax.shift_right_logical(lo, 16) | hi


def _unpack_rows(w):
    lo = pltpu.bitcast(lax.shift_left(w, 16), F32)
    hi = pltpu.bitcast(w & jnp.int32(-65536), F32)
    return lo, hi


def _silu(x):
    return x * (0.5 + 0.5 * jnp.tanh(0.5 * x))


def _layer_norm_rows(y, g, b):
    mu = jnp.mean(y, axis=-1, keepdims=True)
    yc = y - mu
    var = jnp.mean(yc * yc, axis=-1, keepdims=True)
    return yc * lax.rsqrt(var + LN_EPS) * g + b


def _rows(x, blocks, size):
    parts = [x[b * size:(b + 1) * size, :] for b in blocks]
    return parts[0] if len(parts) == 1 else jnp.concatenate(parts, axis=0)


def _gla_chunk(q_s, k_s, v_s, lf_s, g_s, acc_s, o_ref, st_ref, d, start, slot, reverse):
    c = GLA_CHUNK
    t8 = SUBLANES
    nt_dims = (((1,), (1,)), ((), ()))

    def tt(ref, i):
        return ref[d, pl.ds(start + i, t8, stride=t8), :]

    lf = [tt(lf_s, i) for i in range(t8)]
    qt = [tt(q_s, i) for i in range(t8)]
    kt = [tt(k_s, i) for i in range(t8)]
    vt = [tt(v_s, i) for i in range(t8)]

    gi = [None] * t8
    prev = None
    for i in (range(t8 - 1, -1, -1) if reverse else range(t8)):
        gi[i] = lf[i] if prev is None else prev + lf[i]
        prev = gi[i]
    tot = prev
    sub = lax.broadcasted_iota(jnp.int32, (t8, LANES), 0)
    incl = tot
    s = 1
    while s < t8:
        if reverse:
            incl = incl + jnp.where(sub + s < t8, pltpu.roll(incl, t8 - s, 0), 0.0)
        else:
            incl = incl + jnp.where(sub >= s, pltpu.roll(incl, s, 0), 0.0)
        s *= 2
    excl = incl - tot
    for i in range(t8):
        g_s[slot, pl.ds(i, t8, stride=t8), :] = gi[i] + excl

    acc = []
    for i in range(t8):
        a_i = jnp.sum(qt[i] * kt[i], axis=-1, keepdims=True) * vt[i]
        for r in (range(i + 1, t8) if reverse else range(i)):
            w = qt[i] * kt[r] * jnp.exp2(gi[i] - gi[r])
            a_i = a_i + jnp.sum(w, axis=-1, keepdims=True) * vt[r]
        acc.append(a_i)

    sl = pl.ds(start, c)
    q = q_s[d, sl, :]
    kk = k_s[d, sl, :]
    v = v_s[d, sl, :]
    g = g_s[slot]
    vb = v.astype(BF16)
    st = st_ref[d]

    o = lax.dot_general((q * jnp.exp2(g)).astype(BF16), st.astype(BF16), nt_dims,
                        preferred_element_type=F32)

    half = c // 2
    qd_l, kd_l, v_l, q_tiles = [], [], [], []
    b = 2 * t8
    while b <= c:
        h = b // 2
        nblk = c // b
        lo = [2 * m for m in range(nblk)]
        hi = [2 * m + 1 for m in range(nblk)]
        q_half, k_half = (lo, hi) if reverse else (hi, lo)
        refs = [g[m * b + h:m * b + h + 1, :] if reverse else g[m * b + h - 1:m * b + h, :]
                for m in range(nblk)]
        gref = jnp.concatenate([jnp.broadcast_to(r, (h, LANES)) for r in refs], axis=0) \
            if nblk > 1 else jnp.broadcast_to(refs[0], (h, LANES))
        qd_l.append(_rows(q, q_half, h) * jnp.exp2(_rows(g, q_half, h) - gref))
        kd_l.append(_rows(kk, k_half, h) * jnp.exp2(gref - _rows(g, k_half, h)))
        v_l.append(_rows(v, k_half, h))
        q_tiles.append([hb * (h // t8) + t for hb in q_half for t in range(h // t8)])
        b *= 2
    nlev = len(qd_l)
    p = lax.dot_general(jnp.concatenate(qd_l, axis=0).astype(BF16),
                        jnp.concatenate(kd_l, axis=0).astype(BF16), nt_dims,
                        preferred_element_type=F32)
    ii = lax.broadcasted_iota(jnp.int32, p.shape, 0)
    jj = lax.broadcasted_iota(jnp.int32, p.shape, 1)
    keep = None
    for lv in range(nlev):
        h = t8 << lv
        m = (ii // half == lv) & (jj // half == lv) & ((ii // h) == (jj // h))
        keep = m if keep is None else keep | m
    p = jnp.where(keep, p, 0.0)
    res = jnp.dot(p.astype(BF16), jnp.concatenate(v_l, axis=0).astype(BF16),
                  preferred_element_type=F32)
    contrib = [None] * (c // t8)
    for lv in range(nlev):
        for n_, tile in enumerate(q_tiles[lv]):
            piece = res[lv * half + n_ * t8:lv * half + (n_ + 1) * t8, :]
            contrib[tile] = piece if contrib[tile] is None else contrib[tile] + piece
    zero = jnp.zeros((t8, LANES), F32)
    o = o + jnp.concatenate([zero if p_ is None else p_ for p_ in contrib], axis=0)

    gl = g[0:1, :] if reverse else g[c - 1:c, :]
    kd = (kk * jnp.exp2(gl - g)).astype(BF16)
    upd = lax.dot_general(vb, kd, (((0,), (0,)), ((), ())), preferred_element_type=F32)
    st_ref[d] = st * jnp.exp2(gl) + upd

    for i in range(t8):
        acc_s[slot, pl.ds(i, t8, stride=t8), :] = acc[i]
    o_ref[sl, :] = (o + acc_s[slot]).astype(o_ref.dtype)


def _gla_kernel(layer, lbl_ref, qf_ref, ff_ref, vf_ref, qb_ref, fb_ref, vb_ref,
                of_ref, ob_ref, st_ref, q_s, k_s, v_s, lf_s, g_s, acc_s):
    tb = GLA_TBLOCK
    nc = tb // GLA_CHUNK

    @pl.when(pl.program_id(2) == 0)
    def _():
        st_ref[...] = jnp.zeros_like(st_ref)

    if layer > 0:
        lg = lbl_ref[...]
        e = jnp.exp(lg - jnp.max(lg, axis=0, keepdims=True))
        sm = e / jnp.sum(e, axis=0, keepdims=True)
        lb = sm[1]
        for l in range(2, layer + 1):
            lb = lb + sm[l]

    for d, (q_ref, f_ref, v_ref) in enumerate(((qf_ref, ff_ref, vf_ref),
                                               (qb_ref, fb_ref, vb_ref))):
        qx = q_ref[...].astype(F32)
        fx = f_ref[...]
        v_s[d] = v_ref[...].astype(F32)
        t = jnp.exp(-jnp.abs(fx))
        r = 1.0 / (1.0 + t)
        logsig = jnp.minimum(fx, 0.0) - jnp.log(1.0 + t)
        sig_neg = jnp.where(fx >= 0.0, t * r, r)
        if layer > 0:
            lbd = lb[d:d + 1, :]
            a = jnp.log(lbd)
            bb = jnp.log1p(-lbd) + logsig
            lf = jnp.maximum(a, bb) + jnp.log(1.0 + jnp.exp(-jnp.abs(a - bb)))
            kk = (1.0 - lbd) * sig_neg
        else:
            lf = logsig
            kk = sig_neg
        q_s[d] = _silu(qx)
        k_s[d] = kk
        lf_s[d] = lf * LOG2_E

    def body(ci, carry):
        for u in range(GLA_UNROLL):
            cf = ci * GLA_UNROLL + u
            sf = pl.multiple_of(cf * GLA_CHUNK, GLA_CHUNK)
            sb = pl.multiple_of((nc - 1 - cf) * GLA_CHUNK, GLA_CHUNK)
            _gla_chunk(q_s, k_s, v_s, lf_s, g_s, acc_s, of_ref, st_ref, 0, sf, 2 * u, False)
            _gla_chunk(q_s, k_s, v_s, lf_s, g_s, acc_s, ob_ref, st_ref, 1, sb, 2 * u + 1, True)
        return carry

    lax.fori_loop(0, nc // GLA_UNROLL, body, 0)


def _gla(proj_a, proj_f, lb_logits, layer, batch, seq):
    n = proj_a.shape[0]
    tb = GLA_TBLOCK
    nb = seq // tb
    h8 = HG_HEADS

    def spec(stream, rev):
        if rev:
            return pl.BlockSpec((tb, LANES), lambda b, h, c: (b * nb + nb - 1 - c, stream * h8 + h))
        return pl.BlockSpec((tb, LANES), lambda b, h, c: (b * nb + c, stream * h8 + h))

    o_f_spec = pl.BlockSpec((tb, LANES), lambda b, h, c: (b * nb + c, h))
    o_b_spec = pl.BlockSpec((tb, LANES), lambda b, h, c: (b * nb + nb - 1 - c, h))
    return pl.pallas_call(
        functools.partial(_gla_kernel, layer),
        grid=(batch, h8, nb),
        in_specs=[pl.BlockSpec((DEPTH, 2, LANES), lambda b, h, c: (0, 0, h)),
                  spec(0, False), spec(0, False), spec(1, False),
                  spec(0, True), spec(1, True), spec(1, True)],
        out_specs=[o_f_spec, o_b_spec],
        out_shape=[jax.ShapeDtypeStruct((n, D_MODEL), BF16)] * 2,
        scratch_shapes=[pltpu.VMEM((2, HG_DK, HG_DK), F32)]
        + [pltpu.VMEM((2, tb, LANES), F32)] * 4
        + [pltpu.VMEM((2 * GLA_UNROLL, GLA_CHUNK, LANES), F32)] * 2,
        compiler_params=_cparams("arbitrary", "arbitrary", "arbitrary"),
        name="gla",
    )(lb_logits, proj_a, proj_f, proj_a, proj_a, proj_f, proj_a)


def _hgrn_out_kernel(of_ref, ob_ref, gate_ref, nw_ref, w_ref, x_ref, g_ref, b_ref,
                     rw_ref, rb_ref, tri_ref,
                     o_ref, op_ref, route_ref, cnt_ref, carry_ref):
    o = of_ref[...].astype(F32) + ob_ref[...].astype(F32)
    parts = []
    for h in range(HG_HEADS):
        oh = o[:, h * LANES:(h + 1) * LANES]
        ms = jnp.mean(oh * oh, axis=-1, keepdims=True)
        parts.append(oh * lax.rsqrt(ms + RMS_EPS))
    gate = gate_ref[...].astype(F32)
    y = jnp.concatenate(parts, axis=-1) * nw_ref[...] * _silu(gate)
    mix = jnp.dot(y.astype(BF16), w_ref[...], preferred_element_type=F32)
    out = _layer_norm_rows(DN_ALPHA * x_ref[...] + mix, g_ref[...], b_ref[...])
    o_ref[...] = out
    op_ref[...] = _pack_rows(out)
    _route_tile(out, rw_ref, rb_ref, tri_ref, route_ref, cnt_ref, carry_ref)


def _hgrn_out(o_f, o_b, proj_a, norm_w, w_out, x, g, b, router_ops):
    n = x.shape[0]
    tm = ROW_TILE
    row = pl.BlockSpec((tm, D_MODEL), lambda i: (i, 0))
    vec = pl.BlockSpec((1, D_MODEL), lambda i: (0, 0))
    r_in, r_out, r_scratch = _router_specs()
    xo, xpk, route, cnt = pl.pallas_call(
        _hgrn_out_kernel,
        grid=(n // tm,),
        in_specs=[row, row, pl.BlockSpec((tm, D_MODEL), lambda i: (i, 2)), vec,
                  pl.BlockSpec((D_MODEL, D_MODEL), lambda i: (0, 0)), row, vec, vec] + r_in,
        out_specs=[row, pl.BlockSpec((tm, PACKED), lambda i: (i, 0))] + r_out,
        out_shape=[jax.ShapeDtypeStruct((n, D_MODEL), F32),
                   jax.ShapeDtypeStruct((n, PACKED), jnp.int32)] + _router_out_shapes(n),
        scratch_shapes=r_scratch,
        compiler_params=_cparams("arbitrary"),
        name="hgrn_out",
    )(o_f, o_b, proj_a, jnp.tile(norm_w, HG_HEADS).reshape(1, D_MODEL), w_out, x,
      g.reshape(1, D_MODEL), b.reshape(1, D_MODEL), *router_ops)
    return xo, xpk, route, _expert_counts(cnt)


def _conv_out_kernel(tiles_per_seq, bg_ref, cg_ref, h_ref, cgp_ref, hp_ref, cgn_ref, hn_ref,
                     cw_ref, w_ref, x_ref, g_ref, b_ref, rw_ref, rb_ref, tri_ref,
                     o_ref, op_ref, route_ref, cnt_ref, carry_ref):
    i = pl.program_id(0)
    tm = cg_ref.shape[0]
    u = cg_ref[...].astype(F32) * h_ref[...].astype(F32)
    first = (i % tiles_per_seq) == 0
    last = (i % tiles_per_seq) == tiles_per_seq - 1
    hr = cgp_ref.shape[0]
    u_halo_prev = cgp_ref[...].astype(F32) * hp_ref[...].astype(F32)
    u_halo_next = cgn_ref[...].astype(F32) * hn_ref[...].astype(F32)
    u_prev_row = jnp.where(first, 0.0, u_halo_prev[hr - 1:hr, :])
    u_next_row = jnp.where(last, 0.0, u_halo_next[0:1, :])
    rows = lax.broadcasted_iota(jnp.int32, u.shape, 0)
    u_prev = jnp.where(rows == 0, u_prev_row, pltpu.roll(u, 1, 0))
    u_next = jnp.where(rows == tm - 1, u_next_row, pltpu.roll(u, tm - 1, 0))
    cw = cw_ref[...]
    y = u_prev * cw[0:1, :] + u * cw[1:2, :] + u_next * cw[2:3, :]
    mix = jnp.dot((bg_ref[...].astype(F32) * y).astype(BF16), w_ref[...],
                  preferred_element_type=F32)
    out = _layer_norm_rows(DN_ALPHA * x_ref[...] + mix, g_ref[...], b_ref[...])
    o_ref[...] = out
    op_ref[...] = _pack_rows(out)
    _route_tile(out, rw_ref, rb_ref, tri_ref, route_ref, cnt_ref, carry_ref)


def _conv_out(proj, conv_w, w_out, x, g, b, seq, router_ops):
    n = x.shape[0]
    tm = ROW_TILE
    halo = BF16_SUBLANES
    rh = tm // halo
    nblk = n // halo
    row = pl.BlockSpec((tm, D_MODEL), lambda i: (i, 0))
    vec = pl.BlockSpec((1, D_MODEL), lambda i: (0, 0))

    def main(stream):
        return pl.BlockSpec((tm, D_MODEL), lambda i: (i, stream))

    def prev(stream):
        return pl.BlockSpec((halo, D_MODEL), lambda i: (jnp.maximum(i * rh - 1, 0), stream))

    def nxt(stream):
        return pl.BlockSpec((halo, D_MODEL),
                            lambda i: (jnp.minimum((i + 1) * rh, nblk - 1), stream))

    r_in, r_out, r_scratch = _router_specs()
    xo, xpk, route, cnt = pl.pallas_call(
        functools.partial(_conv_out_kernel, seq // tm),
        grid=(n // tm,),
        in_specs=[main(0), main(1), main(2), prev(1), prev(2), nxt(1), nxt(2),
                  pl.BlockSpec((3, D_MODEL), lambda i: (0, 0)),
                  pl.BlockSpec((D_MODEL, D_MODEL), lambda i: (0, 0)), row, vec, vec] + r_in,
        out_specs=[row, pl.BlockSpec((tm, PACKED), lambda i: (i, 0))] + r_out,
        out_shape=[jax.ShapeDtypeStruct((n, D_MODEL), F32),
                   jax.ShapeDtypeStruct((n, PACKED), jnp.int32)] + _router_out_shapes(n),
        scratch_shapes=r_scratch,
        compiler_params=_cparams("arbitrary"),
        name="conv_out",
    )(proj, proj, proj, proj, proj, proj, proj, conv_w, w_out, x,
      g.reshape(1, D_MODEL), b.reshape(1, D_MODEL), *router_ops)
    return xo, xpk, route, _expert_counts(cnt)


def _route_tile(x, whl_ref, b_ref, tri_ref, o_ref, cnt_ref, carry_ref):
    @pl.when(pl.program_id(0) == 0)
    def _():
        carry_ref[...] = jnp.zeros_like(carry_ref)

    xh = x.astype(BF16)
    xl = (x - xh.astype(F32)).astype(BF16)
    both = jnp.dot(xh, whl_ref[...], preferred_element_type=F32)
    logits = (both[:, :LANES] + both[:, LANES:]
              + jnp.dot(xl, whl_ref[:, :LANES], preferred_element_type=F32)) + b_ref[...]
    eg = MOE_EXPERTS_PER_GROUP
    lt = logits.T
    grp = lt[0:SUBLANES]
    exp_t = [lt[SUBLANES + eg * g:SUBLANES + eg * (g + 1)] for g in range(MOE_GROUPS)]
    row = lax.broadcasted_iota(jnp.int32, grp.shape, 0).astype(F32)
    gmax = jnp.max(grp, axis=0, keepdims=True)
    gsum = jnp.sum(jnp.exp(grp - gmax), axis=0, keepdims=True)
    p_group = 1.0 / gsum
    g_sel = jnp.min(jnp.where(grp == gmax, row, 99.0), axis=0, keepdims=True)
    el = exp_t[MOE_GROUPS - 1]
    for g in range(MOE_GROUPS - 2, -1, -1):
        el = jnp.where(g_sel == g, exp_t[g], el)
    t1 = jnp.max(el, axis=0, keepdims=True)
    i1 = jnp.min(jnp.where(el == t1, row, 99.0), axis=0, keepdims=True)
    el2 = jnp.where(row == i1, NEG_BIG, el)
    t2 = jnp.max(el2, axis=0, keepdims=True)
    i2 = jnp.min(jnp.where(el2 == t2, row, 99.0), axis=0, keepdims=True)
    z = jnp.exp(t2 - t1)
    g1 = p_group / (1.0 + z)
    g2 = g1 * z

    is1 = row == i1
    is2 = row == i2
    in_g = [g_sel == g for g in range(MOE_GROUPS)]
    onehot = jnp.concatenate([jnp.where(in_g[g] & (is1 | is2), 1.0, 0.0)
                              for g in range(MOE_GROUPS)], axis=0)
    prefix = jnp.dot(onehot.astype(BF16), tri_ref[...], preferred_element_type=F32)
    before = prefix + carry_ref[...]
    r1 = jnp.zeros_like(g1)
    r2 = jnp.zeros_like(g1)
    for g in range(MOE_GROUPS):
        bg = before[eg * g:eg * (g + 1)]
        r1 = r1 + jnp.sum(jnp.where(in_g[g] & is1, bg, 0.0), axis=0, keepdims=True)
        r2 = r2 + jnp.sum(jnp.where(in_g[g] & is2, bg, 0.0), axis=0, keepdims=True)
    e1 = g_sel * eg + i1
    e2 = g_sel * eg + i2
    out = jnp.zeros_like(row)
    for r, val in enumerate((g1, g2, e1, e2, r1, r2)):
        out = jnp.where(row == r, val, out)
    o_ref[...] = out
    carry_ref[...] = carry_ref[...] + jnp.sum(onehot, axis=1, keepdims=True)
    cnt_ref[...] = carry_ref[...]


def _router_operands(w_group, b_group, w_expert, b_expert):
    gpad = SUBLANES - MOE_GROUPS
    pad = LANES - SUBLANES - MOE_EXPERTS
    w = jnp.concatenate([w_group, jnp.zeros((D_MODEL, gpad), F32), w_expert,
                         jnp.zeros((D_MODEL, pad), F32)], axis=1)
    wh = w.astype(BF16)
    wl = (w - wh.astype(F32)).astype(BF16)
    b = jnp.concatenate([b_group, jnp.full((gpad,), NEG_BIG, F32), b_expert,
                         jnp.zeros((pad,), F32)]).reshape(1, LANES)
    r = jnp.arange(ROW_TILE, dtype=jnp.int32)
    tri = (r[:, None] < r[None, :]).astype(BF16)
    return jnp.concatenate([wh, wl], axis=1), b, tri


def _router_specs():
    in_specs = [pl.BlockSpec((D_MODEL, 2 * LANES), lambda i: (0, 0)),
                pl.BlockSpec((1, LANES), lambda i: (0, 0)),
                pl.BlockSpec((ROW_TILE, ROW_TILE), lambda i: (0, 0))]
    out_specs = [pl.BlockSpec((SUBLANES, ROW_TILE), lambda i: (0, i)),
                 pl.BlockSpec((MOE_EXPERTS, ROW_TILE), lambda i: (0, 0))]
    scratch = [pltpu.VMEM((MOE_EXPERTS, ROW_TILE), F32)]
    return in_specs, out_specs, scratch


def _router_out_shapes(n):
    return [jax.ShapeDtypeStruct((SUBLANES, n), F32),
            jax.ShapeDtypeStruct((MOE_EXPERTS, ROW_TILE), F32)]


def _expert_counts(cnt):
    return cnt[:, 0].astype(jnp.int32)


DISPATCH_TOKENS = 512
DISPATCH_BUFS = 3
WEIGHT_SLOTS = 3


def _dest_kernel(ps_ref, route_ref, o_ref):
    e = route_ref[MOE_TOPK:2 * MOE_TOPK, :]
    rank = route_ref[2 * MOE_TOPK:3 * MOE_TOPK, :]
    base = jnp.zeros_like(e)
    for x in range(MOE_EXPERTS):
        base = jnp.where(e == x, ps_ref[x].astype(F32), base)
    o_ref[...] = (base + rank).astype(jnp.int32)


def _moe_plan(route, counts, n):
    nk = n * MOE_TOPK
    n_rows = -(-nk // MOE_BLOCK) * MOE_BLOCK + MOE_EXPERTS * MOE_BLOCK
    n_blocks = n_rows // MOE_BLOCK
    padded = (counts + MOE_BLOCK - 1) // MOE_BLOCK * MOE_BLOCK
    pad_end = jnp.cumsum(padded)
    pad_start = pad_end - padded
    dest = pl.pallas_call(
        _dest_kernel,
        grid_spec=pltpu.PrefetchScalarGridSpec(
            num_scalar_prefetch=1, grid=(1,),
            in_specs=[pl.BlockSpec((SUBLANES, n), lambda i, ps: (0, 0))],
            out_specs=pl.BlockSpec((MOE_TOPK, n), lambda i, ps: (0, 0))),
        out_shape=jax.ShapeDtypeStruct((MOE_TOPK, n), jnp.int32),
        compiler_params=_cparams("arbitrary"),
        name="dest",
    )(pad_start.astype(jnp.int32), route).reshape(nk)
    block_start = jnp.arange(n_blocks, dtype=jnp.int32) * MOE_BLOCK
    block_expert = jnp.minimum(jnp.sum(block_start[:, None] >= pad_end[None, :], axis=1),
                               MOE_EXPERTS - 1).astype(jnp.int32)
    nact = (pad_end[-1:] // MOE_BLOCK).astype(jnp.int32)
    fill_start = (pad_start + counts).astype(jnp.int32)
    return dest, block_expert, nact, fill_start, pad_end.astype(jnp.int32), n_rows


def _dispatch_kernel(dest_ref, fs_ref, fe_ref, nact_ref, x_hbm, xs_hbm, xbuf, zbuf, lsem, sem,
                     zsem):
    i = pl.program_id(0)
    nsteps = pl.num_programs(0)
    slot = i % DISPATCH_BUFS
    ct = DISPATCH_TOKENS
    gt = ct // SUBLANES
    n_blocks = xs_hbm.shape[0] // MOE_BLOCK

    def load(step, sl):
        start = pl.multiple_of(step * gt, gt)
        return pltpu.make_async_copy(x_hbm.at[pl.ds(start, gt)], xbuf.at[sl], lsem.at[sl])

    def wait_step(sl):
        for _ in range(MOE_TOPK):
            pltpu.make_async_copy(x_hbm.at[pl.ds(0, gt)], xbuf.at[sl], sem.at[sl]).wait()

    @pl.when(i == 0)
    def _():
        for s in range(DISPATCH_BUFS - 1):
            load(s, s).start()

        zbuf[...] = jnp.zeros_like(zbuf)

        def block_copy(b):
            start = pl.multiple_of(b * MOE_BLOCK, MOE_BLOCK)
            return pltpu.make_async_copy(zbuf, xs_hbm.at[pl.ds(start, MOE_BLOCK)], zsem.at[0])

        def fill(wait):
            def per_expert(e, c):
                @pl.when(fs_ref[e] < fe_ref[e])
                def _():
                    cp = block_copy(fe_ref[e] // MOE_BLOCK - 1)
                    cp.wait() if wait else cp.start()
                return c
            lax.fori_loop(0, MOE_EXPERTS, per_expert, 0)

            def per_block(b, c):
                cp = block_copy(b)
                cp.wait() if wait else cp.start()
                return c
            lax.fori_loop(nact_ref[0], n_blocks, per_block, 0)

        fill(False)
        fill(True)

    load(i, slot).wait()

    n_tok = nsteps * ct

    def body(g, c):
        base = i * ct + g * SUBLANES
        for t in range(SUBLANES):
            for k in range(MOE_TOPK):
                d = dest_ref[k * n_tok + base + t]
                pltpu.make_async_copy(xbuf.at[slot, g, pl.ds(t, 1)], xs_hbm.at[pl.ds(d, 1)],
                                      sem.at[slot]).start(priority=k)
        return c
    lax.fori_loop(0, gt, body, 0, unroll=2)

    @pl.when(i >= 1)
    def _():
        wait_step((i - 1) % DISPATCH_BUFS)

    @pl.when(i + DISPATCH_BUFS - 1 < nsteps)
    def _():
        load(i + DISPATCH_BUFS - 1, (i + DISPATCH_BUFS - 1) % DISPATCH_BUFS).start()

    @pl.when(i == nsteps - 1)
    def _():
        wait_step(slot)


def _dispatch(x, dest, fill_start, fill_end, nact, n_rows):
    n, width = x.shape
    grid_spec = pltpu.PrefetchScalarGridSpec(
        num_scalar_prefetch=4,
        grid=(n // DISPATCH_TOKENS,),
        in_specs=[pl.BlockSpec(memory_space=pl.ANY)],
        out_specs=pl.BlockSpec(memory_space=pl.ANY),
        scratch_shapes=[pltpu.VMEM((DISPATCH_BUFS, DISPATCH_TOKENS // SUBLANES, SUBLANES, width),
                                   x.dtype),
                        pltpu.VMEM((MOE_BLOCK, width), x.dtype),
                        pltpu.SemaphoreType.DMA((DISPATCH_BUFS,)),
                        pltpu.SemaphoreType.DMA((DISPATCH_BUFS,)),
                        pltpu.SemaphoreType.DMA((2,))],
    )
    return pl.pallas_call(
        _dispatch_kernel,
        grid_spec=grid_spec,
        out_shape=jax.ShapeDtypeStruct((n_rows, width), x.dtype),
        compiler_params=_cparams("arbitrary"),
        name="dispatch",
    )(dest, fill_start, fill_end, nact, x.reshape(n // SUBLANES, SUBLANES, width))


def _moe_ffn_kernel(e0, be_ref, nact_ref, ord_ref, nxt_ref, nxt2_ref, xs_ref, wup_hbm, wdn_hbm,
                    ys_ref, wup_f, wdn_f, wup_bf, wdn_bf, wsem):
    i = pl.program_id(0)
    active = i < nact_ref[0]
    new_expert = (i == 0) | (be_ref[i] != be_ref[jnp.maximum(i - 1, 0)])

    def fetch(e, sl):
        return (pltpu.make_async_copy(wup_hbm.at[e0 + e], wup_f.at[sl], wsem.at[0, sl]),
                pltpu.make_async_copy(wdn_hbm.at[e0 + e], wdn_f.at[sl], wsem.at[1, sl]))

    @pl.when(active & (i == 0))
    def _():
        for cp in fetch(be_ref[0], 0):
            cp.start()

        @pl.when(nxt_ref[be_ref[0]] >= 0)
        def _():
            for cp in fetch(nxt_ref[be_ref[0]], 1):
                cp.start()

    @pl.when(active & new_expert)
    def _():
        k = ord_ref[i]
        sl = k % WEIGHT_SLOTS
        for cp in fetch(be_ref[i], sl):
            cp.wait()

        @pl.when(nxt2_ref[be_ref[i]] >= 0)
        def _():
            for cp in fetch(nxt2_ref[be_ref[i]], (k + 2) % WEIGHT_SLOTS):
                cp.start()

        wup_bf[...] = wup_f[sl].astype(BF16)
        wdn_bf[...] = wdn_f[sl].astype(BF16)

    @pl.when(active)
    def _():
        x_lo, x_hi = _unpack_rows(xs_ref[...])
        xb = jnp.concatenate([x_lo.astype(BF16), x_hi.astype(BF16)], axis=1)
        hcat = jnp.dot(xb, wup_bf[...], preferred_element_type=F32)
        hg = hcat[:, :MOE_D_EXPERT]
        hu = hcat[:, MOE_D_EXPERT:]
        act = (_silu(hg) * hu).astype(BF16)
        ys_ref[...] = _pack_rows(jnp.dot(act, wdn_bf[...], preferred_element_type=F32))

    @pl.when(jnp.logical_not(active))
    def _():
        ys_ref[...] = jnp.zeros_like(ys_ref)


def _moe_ffn(xs, block_expert, nact, counts, w_up, w_down, layer):
    n_rows = xs.shape[0]
    n_blocks = n_rows // MOE_BLOCK
    first = jnp.concatenate([jnp.ones((1,), jnp.int32),
                             (block_expert[1:] != block_expert[:-1]).astype(jnp.int32)])
    ordinal = (jnp.cumsum(first) - 1).astype(jnp.int32)
    ids = jnp.arange(MOE_EXPERTS, dtype=jnp.int32)
    has_rows = counts[None, :] > 0
    later = (ids[None, :] > ids[:, None]) & has_rows
    nxt = jnp.min(jnp.where(later, ids[None, :], MOE_EXPERTS), axis=1)
    later2 = (ids[None, :] > nxt[:, None]) & has_rows
    nxt2 = jnp.min(jnp.where(later2, ids[None, :], MOE_EXPERTS), axis=1)
    next_expert = jnp.where(nxt == MOE_EXPERTS, -1, nxt).astype(jnp.int32)
    next2_expert = jnp.where(nxt2 == MOE_EXPERTS, -1, nxt2).astype(jnp.int32)
    row = pl.BlockSpec((MOE_BLOCK, PACKED), lambda i, *_: (i, 0))
    grid_spec = pltpu.PrefetchScalarGridSpec(
        num_scalar_prefetch=5,
        grid=(n_blocks,),
        in_specs=[row, pl.BlockSpec(memory_space=pl.ANY), pl.BlockSpec(memory_space=pl.ANY)],
        out_specs=row,
        scratch_shapes=[pltpu.VMEM((WEIGHT_SLOTS, D_MODEL, 2 * MOE_D_EXPERT), F32),
                        pltpu.VMEM((WEIGHT_SLOTS, MOE_D_EXPERT, D_MODEL), F32),
                        pltpu.VMEM((D_MODEL, 2 * MOE_D_EXPERT), BF16),
                        pltpu.VMEM((MOE_D_EXPERT, D_MODEL), BF16),
                        pltpu.SemaphoreType.DMA((2, WEIGHT_SLOTS))],
    )
    return pl.pallas_call(
        functools.partial(_moe_ffn_kernel, layer * MOE_EXPERTS),
        grid_spec=grid_spec,
        out_shape=jax.ShapeDtypeStruct((n_rows, PACKED), jnp.int32),
        compiler_params=_cparams("arbitrary"),
        name="moe_ffn",
    )(block_expert, nact, ordinal, next_expert, next2_expert, xs, w_up, w_down)


def _combine_ln_kernel(dest_ref, x_ref, route_ref, ys_hbm, g_ref, b_ref, o_ref, ob_ref, ybuf,
                       sem):
    i = pl.program_id(0)
    nsteps = pl.num_programs(0)
    slot = i % 2
    tm = x_ref.shape[0]

    def start_gather(step, sl):
        def body(g, c):
            base = step * tm + g * SUBLANES
            for t in range(SUBLANES):
                for k in range(MOE_TOPK):
                    d = dest_ref[k * (nsteps * tm) + base + t]
                    pltpu.make_async_copy(ys_hbm.at[pl.ds(d, 1)], ybuf.at[sl, k, g, pl.ds(t, 1)],
                                          sem.at[sl]).start(priority=k)
            return c
        lax.fori_loop(0, tm // SUBLANES, body, 0, unroll=2)

    @pl.when(i == 0)
    def _():
        start_gather(0, 0)

    @pl.when(i + 1 < nsteps)
    def _():
        start_gather(i + 1, 1 - slot)

    for k in range(MOE_TOPK):
        pltpu.make_async_copy(ybuf.at[1 - slot, k], ybuf.at[slot, k], sem.at[slot]).wait()
    route = route_ref[...]
    y0_lo, y0_hi = _unpack_rows(ybuf[slot, 0].reshape(tm, PACKED))
    y1_lo, y1_hi = _unpack_rows(ybuf[slot, 1].reshape(tm, PACKED))
    g0 = route[:, 0:1]
    g1 = route[:, 1:2]
    ffn = jnp.concatenate([g0 * y0_lo + g1 * y1_lo, g0 * y0_hi + g1 * y1_hi], axis=1)
    out = _layer_norm_rows(DN_ALPHA * x_ref[...] + ffn, g_ref[...], b_ref[...])
    o_ref[...] = out
    ob_ref[...] = out.astype(BF16)


def _combine_ln(x, route, ys, dest, g, b):
    n = x.shape[0]
    tm = COMBINE_TILE
    vec = pl.BlockSpec((1, D_MODEL), lambda i, d: (0, 0))
    row = pl.BlockSpec((tm, D_MODEL), lambda i, d: (i, 0))
    grid_spec = pltpu.PrefetchScalarGridSpec(
        num_scalar_prefetch=1,
        grid=(n // tm,),
        in_specs=[pl.BlockSpec((tm, D_MODEL), lambda i, d: (i, 0)),
                  pl.BlockSpec((tm, MOE_TOPK), lambda i, d: (i, 0)),
                  pl.BlockSpec(memory_space=pl.ANY), vec, vec],
        out_specs=[row, row],
        scratch_shapes=[pltpu.VMEM((2, MOE_TOPK, tm // SUBLANES, SUBLANES, PACKED), jnp.int32),
                        pltpu.SemaphoreType.DMA((2,))],
    )
    return pl.pallas_call(
        _combine_ln_kernel,
        grid_spec=grid_spec,
        out_shape=[jax.ShapeDtypeStruct((n, D_MODEL), F32),
                   jax.ShapeDtypeStruct((n, D_MODEL), BF16)],
        compiler_params=_cparams("arbitrary"),
        name="combine_ln",
    )(dest, x, route[0:MOE_TOPK].T, ys, g.reshape(1, D_MODEL), b.reshape(1, D_MODEL))


def kernel(x, hg_w_in, hg_lb_logits, hg_norm_w, hg_w_out, cv_w_in, cv_w, cv_w_out, ln_g, ln_b,
           moe_w_group, moe_b_group, moe_w_expert, moe_b_expert, moe_w_up, moe_w_down):
    batch, seq, d = x.shape
    n = batch * seq
    xf = x.reshape(n, d)
    w_up_all = moe_w_up.reshape(DEPTH * MOE_EXPERTS, D_MODEL, 2 * MOE_D_EXPERT)
    w_down_all = moe_w_down.reshape(DEPTH * MOE_EXPERTS, MOE_D_EXPERT, D_MODEL)
    xin = xf
    for layer in range(DEPTH):
        j = layer // 2
        router_ops = _router_operands(moe_w_group[layer], moe_b_group[layer],
                                      moe_w_expert[layer], moe_b_expert[layer])
        if layer % 2 == 0:
            w = hg_w_in[j]
            w_a = jnp.concatenate([w[:, :D_MODEL], w[:, 3 * D_MODEL:]], axis=1).astype(BF16)
            w_f = w[:, D_MODEL:3 * D_MODEL].astype(BF16)
            proj_a = _matmul(xin, w_a, BF16)
            proj_f = _matmul(xin, w_f, F32)
            o_f, o_b = _gla(proj_a, proj_f, hg_lb_logits, layer, batch, seq)
            xf, xpk, route, counts = _hgrn_out(
                o_f, o_b, proj_a, hg_norm_w[j], hg_w_out[j].astype(BF16), xf,
                ln_g[layer, 0], ln_b[layer, 0], router_ops)
        else:
            proj = _matmul(xin, cv_w_in[j].astype(BF16), BF16)
            xf, xpk, route, counts = _conv_out(
                proj, cv_w[j], cv_w_out[j].astype(BF16), xf,
                ln_g[layer, 0], ln_b[layer, 0], seq, router_ops)
        dest, block_expert, nact, fill_start, fill_end, n_rows = _moe_plan(route, counts, n)
        xs = _dispatch(xpk, dest, fill_start, fill_end, nact, n_rows)
        ys = _moe_ffn(xs, block_expert, nact, counts, w_up_all, w_down_all, layer)
        xf, xin = _combine_ln(xf, route, ys, dest, ln_g[layer, 1], ln_b[layer, 1])
    return xf.reshape(batch, seq, d)
```
